```python
import math
import jax, jax.numpy as jnp
from jax import lax
import numpy as np

D_MODEL = 1024
BATCH = 4
SEQ = 4096
DEPTH = 4
DEC_BATCH = 8
DEC_SEQ = 16
PAST_LEN = 4096

CHUNK = 64
N_EVEN = (DEPTH + 1) // 2
N_ODD = DEPTH // 2
D_FF = 2816
NORM_EPS = 1e-6
POOL_WINDOWS = (2, 4, 8, 16)
N_POOL_GROUPS = len(POOL_WINDOWS)
C_POOL = D_MODEL // 2
POOL_GC = C_POOL // N_POOL_GROUPS
POOL_HIST = max(POOL_WINDOWS) - 1
HEAD_DIM = 64
N_Q_HEADS = (D_MODEL // 2) // HEAD_DIM
N_KV_HEADS = N_Q_HEADS // 4
GQA_GROUP = N_Q_HEADS // N_KV_HEADS
WINDOW = 128
BACK_CHUNKS = WINDOW // CHUNK
SWA_ROWS = WINDOW
Q_W = N_Q_HEADS * HEAD_DIM
KV_W = N_KV_HEADS * HEAD_DIM
IN_EVEN = C_POOL + Q_W + 2 * KV_W
OUT_EVEN = C_POOL + Q_W
N_BUCKETS = 32
MAX_DISTANCE = 128
RWKV_HEAD = 64
RWKV_H = D_MODEL // RWKV_HEAD
DECAY_LORA = 64
AAA_LORA = 64
GATE_LORA = 128
LNX_EPS = 64e-5

kernel_name = 'hybrid_stream_pool_swa_rwkv7_step'


def rmsnorm(x, g):
    x32 = x.astype(jnp.float32)
    y = x32 * lax.rsqrt(jnp.mean(x32 * x32, axis=-1, keepdims=True) + NORM_EPS)
    return (y * g.astype(jnp.float32)).astype(x.dtype)


def swiglu(h, wg, wu, wd):
    return (jax.nn.silu(h @ wg) * (h @ wu)) @ wd


def t5_bucket(rel):
    half = N_BUCKETS // 2
    max_exact = half // 2
    side = jnp.where(rel > 0, half, 0)
    n = jnp.abs(rel)
    nf = jnp.maximum(n, max_exact).astype(jnp.float32)
    large = max_exact + (jnp.log(nf / max_exact) / math.log(MAX_DISTANCE / max_exact)
                         * (half - max_exact)).astype(jnp.int32)
    large = jnp.minimum(large, half - 1)
    return side + jnp.where(n < max_exact, n, large)


def rel_bias(table, lq, lk, offset):
    rel = jnp.arange(lk)[None, :] - offset - jnp.arange(lq)[:, None]
    b = table.astype(jnp.float32)[t5_bucket(rel)]
    return jnp.transpose(b, (2, 0, 1)).reshape(N_KV_HEADS, GQA_GROUP, lq, lk)


def sink_attention(q, k, v, bias, sinks, valid):
    s = jnp.einsum('...qhgd,...khd->...hgqk', q, k).astype(jnp.float32) * (HEAD_DIM ** -0.5) + bias
    if valid is not None:
        s = jnp.where(valid, s, -1e30)
    sink = sinks.astype(jnp.float32)[:, :, None, None]
    m = jnp.maximum(jnp.max(s, axis=-1, keepdims=True), sink)
    p = jnp.exp(s - m)
    p = p / (jnp.sum(p, axis=-1, keepdims=True) + jnp.exp(sink - m))
    return jnp.einsum('...hgqk,...khd->...qhgd', p.astype(v.dtype), v)


def pool_mix(u, hist, pos0, pool_w, pool_scale):
    B, L, _ = u.shape
    full = jnp.concatenate([hist.astype(u.dtype), u], axis=1).astype(jnp.float32)
    cs = jnp.concatenate([jnp.zeros((B, 1, C_POOL), jnp.float32), jnp.cumsum(full, axis=1)], axis=1)
    end = cs[:, POOL_HIST + 1:]
    t = pos0 + jnp.arange(L)
    means = []
    for gi, w in enumerate(POOL_WINDOWS):
        sl = slice(gi * POOL_GC, (gi + 1) * POOL_GC)
        start = cs[:, POOL_HIST + 1 - w: POOL_HIST + 1 - w + L, sl]
        cnt = jnp.minimum(w, t + 1).astype(jnp.float32)[None, :, None]
        means.append((end[..., sl] - start) / cnt)
    pooled = jnp.concatenate(means, axis=-1) - full[:, POOL_HIST:]
    pooled = pooled.astype(u.dtype).reshape(B, L, N_POOL_GROUPS, POOL_GC)
    out = jnp.einsum('blgc,gcd->blgd', pooled, pool_w).reshape(B, L, C_POOL) * pool_scale
    return out, full[:, -POOL_HIST:].astype(u.dtype)


def even_mixer(h, pos0, pool_hist, k_hist, v_hist, w_in, pool_w, pool_scale, sinks, t5_table, w_out):
    B, L, _ = h.shape
    z = h @ w_in
    u = z[..., :C_POOL]
    q = z[..., C_POOL:C_POOL + Q_W].reshape(B, L, N_KV_HEADS, GQA_GROUP, HEAD_DIM)
    k = z[..., C_POOL + Q_W:C_POOL + Q_W + KV_W].reshape(B, L, N_KV_HEADS, HEAD_DIM)
    v = z[..., C_POOL + Q_W + KV_W:].reshape(B, L, N_KV_HEADS, HEAD_DIM)
    pool_out, new_pool = pool_mix(u, pool_hist, pos0, pool_w, pool_scale)
    sink = sinks.reshape(N_KV_HEADS, GQA_GROUP)
    if k_hist is None:
        n_c = L // CHUNK
        band = (BACK_CHUNKS + 1) * CHUNK
        qc = q.reshape(B, n_c, CHUNK, N_KV_HEADS, GQA_GROUP, HEAD_DIM)
        pad = ((0, 0), (BACK_CHUNKS, 0), (0, 0), (0, 0), (0, 0))
        kc = jnp.pad(k.reshape(B, n_c, CHUNK, N_KV_HEADS, HEAD_DIM), pad)
        vc = jnp.pad(v.reshape(B, n_c, CHUNK, N_KV_HEADS, HEAD_DIM), pad)
        kb = jnp.concatenate([kc[:, j:j + n_c] for j in range(BACK_CHUNKS + 1)], axis=2)
        vb = jnp.concatenate([vc[:, j:j + n_c] for j in range(BACK_CHUNKS + 1)], axis=2)
        kpos = (jnp.arange(n_c) * CHUNK)[:, None] - BACK_CHUNKS * CHUNK + jnp.arange(band)[None, :]
        valid = (kpos >= 0)[None, :, None, None, None, :]
        bias = rel_bias(t5_table, CHUNK, band, BACK_CHUNKS * CHUNK)
        att = sink_attention(qc, kb, vb, bias, sink, valid)
        k_all, v_all = k, v
    else:
        k_all = jnp.concatenate([k_hist.astype(k.dtype), k], axis=1)
        v_all = jnp.concatenate([v_hist.astype(v.dtype), v], axis=1)
        bias = rel_bias(t5_table, L, SWA_ROWS + L, SWA_ROWS)
        att = sink_attention(q, k_all, v_all, bias, sink, None)
    att = att.reshape(B, L, Q_W)
    out = jnp.concatenate([pool_out, att], axis=-1) @ w_out
    return out, new_pool, k_all[:, -SWA_ROWS:], v_all[:, -SWA_ROWS:]


def rwkv7_step(S, inp):
    r_t, w_t, k_t, v_t, kk_t, a_t = inp
    sa = jnp.einsum('bhij,bhj->bhi', S, -kk_t)
    S = S * w_t[:, :, None, :] + sa[..., :, None] * (kk_t * a_t)[..., None, :] + v_t[..., :, None] * k_t[..., None, :]
    return S, jnp.einsum('bhij,bhj->bhi', S, r_t)


def rwkv7_time_mix(h, shift_prev, S0, mu, wr, wk, wv, w0, w1, w2, a0, a1, a2, g1, g2,
                   k_k, k_a, r_k, lnx_w, lnx_b, wo):
    B, L, D = h.shape
    f32 = jnp.float32
    h_prev = jnp.concatenate([shift_prev[:, None, :].astype(h.dtype), h[:, :-1]], axis=1)
    xx = h_prev - h
    xr, xw, xk, xv, xa, xg = (h + xx * mu[j] for j in range(6))
    r = xr @ wr
    k = xk @ wk
    v = xv @ wv
    w = -jax.nn.softplus(-(w0 + jnp.tanh(xw @ w1) @ w2).astype(f32)) - 0.5
    a = jax.nn.sigmoid((a0 + (xa @ a1) @ a2).astype(f32))
    g = jax.nn.sigmoid(xg @ g1) @ g2
    heads = lambda t: t.astype(f32).reshape(B, L, RWKV_H, RWKV_HEAD)
    r, k, v, w, a = heads(r), heads(k), heads(v), heads(w), heads(a)
    kk = k * k_k.astype(f32).reshape(RWKV_H, RWKV_HEAD)
    kk = kk * lax.rsqrt(jnp.maximum(jnp.sum(kk * kk, axis=-1, keepdims=True), 1e-24))
    k = k * (1.0 + (a - 1.0) * k_a.astype(f32).reshape(RWKV_H, RWKV_HEAD))
    decay = jnp.exp(-jnp.exp(w))
    xs = tuple(jnp.moveaxis(t, 1, 0) for t in (r, decay, k, v, kk, a))
    S_fin, ys = lax.scan(rwkv7_step, S0.astype(f32), xs)
    y = jnp.moveaxis(ys, 0, 1)
    mean = jnp.mean(y, axis=-1, keepdims=True)
    var = jnp.mean(jnp.square(y - mean), axis=-1, keepdims=True)
    yn = ((y - mean) * lax.rsqrt(var + LNX_EPS)).reshape(B, L, D) * lnx_w.astype(f32) + lnx_b.astype(f32)
    bonus = (jnp.sum(r * k * r_k.astype(f32), axis=-1, keepdims=True) * v).reshape(B, L, D)
    out = ((yn + bonus).astype(h.dtype) * g) @ wo
    return out, h[:, -1], S_fin.astype(h.dtype)


def trunk(x, pos0, caches, p):
    pool_c, k_c, v_c, shift_c, wkv_c = caches
    B = x.shape[0]
    new_pool, new_k, new_v, new_shift, new_wkv = [], [], [], [], []
    for l in range(DEPTH):
        x = x + 0.5 * swiglu(rmsnorm(x, p['norm_ffn'][l, 0]), p['ffn_w_gate'][l, 0],
                             p['ffn_w_up'][l, 0], p['ffn_w_down'][l, 0])
        h = rmsnorm(x, p['norm_mix'][l])
        i = l // 2
        if l % 2 == 0:
            hist = jnp.zeros((B, POOL_HIST, C_POOL), x.dtype) if pool_c is None else pool_c[i]
            out, pc, kc, vc = even_mixer(h, pos0, hist,
                                         None if k_c is None else k_c[i],
                                         None if v_c is None else v_c[i],
                                         p['w_in_even'][i], p['pool_w'][i], p['pool_scale'][i],
                                         p['attn_sinks'][i], p['t5_table'], p['w_out_even'][i])
            new_pool.append(pc)
            new_k.append(kc)
            new_v.append(vc)
        else:
            sh = jnp.zeros((B, D_MODEL), x.dtype) if shift_c is None else shift_c[i]
            S0 = jnp.zeros((B, RWKV_H, RWKV_HEAD, RWKV_HEAD), jnp.float32) if wkv_c is None else wkv_c[i]
            out, sh_new, S_new = rwkv7_time_mix(
                h, sh, S0, p['rwkv_mu'][i], p['rwkv_wr'][i], p['rwkv_wk'][i], p['rwkv_wv'][i],
                p['rwkv_w0'][i], p['rwkv_w1'][i], p['rwkv_w2'][i], p['rwkv_a0'][i], p['rwkv_a1'][i],
                p['rwkv_a2'][i], p['rwkv_g1'][i], p['rwkv_g2'][i], p['rwkv_k_k'][i], p['rwkv_k_a'][i],
                p['rwkv_r_k'][i], p['rwkv_lnx_w'][i], p['rwkv_lnx_b'][i], p['rwkv_wo'][i])
            new_shift.append(sh_new)
            new_wkv.append(S_new)
        x = x + out
        x = x + 0.5 * swiglu(rmsnorm(x, p['norm_ffn'][l, 1]), p['ffn_w_gate'][l, 1],
                             p['ffn_w_up'][l, 1], p['ffn_w_down'][l, 1])
    y = rmsnorm(x, p['norm_final'])
    return (y, jnp.stack(new_pool), jnp.stack(new_k), jnp.stack(new_v),
            jnp.stack(new_shift), jnp.stack(new_wkv))


def setup_inputs(seed: int = 0) -> dict:
    key = jax.random.key(seed)
    ks = iter(jax.random.split(key, 40))
    f32 = jnp.float32
    D = D_MODEL

    def nrm(shape, scale):
        return jax.random.normal(next(ks), shape, f32) * scale

    def unif(shape, lo, hi):
        return jax.random.uniform(next(ks), shape, f32, lo, hi)

    return {
        'x_prompt': nrm((BATCH, SEQ, D), 1.0),
        'x_sample': nrm((DEC_BATCH, DEC_SEQ, D), 1.0),
        'cache_pool': nrm((N_EVEN, DEC_BATCH, POOL_HIST, C_POOL), 1.0),
        'cache_swa_k': nrm((N_EVEN, DEC_BATCH, SWA_ROWS, N_KV_HEADS, HEAD_DIM), 1.0),
        'cache_swa_v': nrm((N_EVEN, DEC_BATCH, SWA_ROWS, N_KV_HEADS, HEAD_DIM), 1.0),
        'state_shift': nrm((N_ODD, DEC_BATCH, D), 1.0),
        'state_wkv': nrm((N_ODD, DEC_BATCH, RWKV_H, RWKV_HEAD, RWKV_HEAD), 1.0),
        't5_table': nrm((N_BUCKETS, N_Q_HEADS), 0.5),
        'norm_ffn': 1.0 + nrm((DEPTH, 2, D), 0.02),
        'ffn_w_gate': nrm((DEPTH, 2, D, D_FF), D ** -0.5),
        'ffn_w_up': nrm((DEPTH, 2, D, D_FF), D ** -0.5),
        'ffn_w_down': nrm((DEPTH, 2, D_FF, D), D_FF ** -0.5),
        'norm_mix': 1.0 + nrm((DEPTH, D), 0.02),
        'w_in_even': nrm((N_EVEN, D, IN_EVEN), D ** -0.5),
        'pool_w': nrm((N_EVEN, N_POOL_GROUPS, POOL_GC, POOL_GC), POOL_GC ** -0.5),
        'pool_scale': 1.0 + nrm((N_EVEN, C_POOL), 0.1),
        'attn_sinks': nrm((N_EVEN, N_Q_HEADS), 0.5),
        'w_out_even': nrm((N_EVEN, OUT_EVEN, D), OUT_EVEN ** -0.5),
        'rwkv_mu': unif((N_ODD, 6, D), 0.0, 1.0),
        'rwkv_wr': nrm((N_ODD, D, D), D ** -0.5),
        'rwkv_wk': nrm((N_ODD, D, D), D ** -0.5),
        'rwkv_wv': nrm((N_ODD, D, D), D ** -0.5),
        'rwkv_w0': unif((N_ODD, D), -5.0, 0.5),
        'rwkv_w1': nrm((N_ODD, D, DECAY_LORA), D ** -0.5),
        'rwkv_w2': nrm((N_ODD, DECAY_LORA, D), 0.1 * DECAY_LORA ** -0.5),
        'rwkv_a0': nrm((N_ODD, D), 0.1),
        'rwkv_a1': nrm((N_ODD, D, AAA_LORA), D ** -0.5),
        'rwkv_a2': nrm((N_ODD, AAA_LORA, D), 0.1 * AAA_LORA ** -0.5),
        'rwkv_g1': nrm((N_ODD, D, GATE_LORA), D ** -0.5),
        'rwkv_g2': nrm((N_ODD, GATE_LORA, D), GATE_LORA ** -0.5),
        'rwkv_k_k': 0.85 + nrm((N_ODD, D), 0.05),
        'rwkv_k_a': 1.0 + nrm((N_ODD, D), 0.05),
        'rwkv_r_k': nrm((N_ODD, RWKV_H, RWKV_HEAD), 0.1),
        'rwkv_lnx_w': 1.0 + nrm((N_ODD, D), 0.02),
        'rwkv_lnx_b': nrm((N_ODD, D), 0.02),
        'rwkv_wo': nrm((N_ODD, D, D), D ** -0.5),
        'norm_final': 1.0 + nrm((D,), 0.02),
    }


def reference(x_prompt, x_sample, cache_pool, cache_swa_k, cache_swa_v, state_shift, state_wkv,
              t5_table, norm_ffn, ffn_w_gate, ffn_w_up, ffn_w_down, norm_mix,
              w_in_even, pool_w, pool_scale, attn_sinks, w_out_even,
              rwkv_mu, rwkv_wr, rwkv_wk, rwkv_wv, rwkv_w0, rwkv_w1, rwkv_w2,
              rwkv_a0, rwkv_a1, rwkv_a2, rwkv_g1, rwkv_g2, rwkv_k_k, rwkv_k_a, rwkv_r_k,
              rwkv_lnx_w, rwkv_lnx_b, rwkv_wo, norm_final):
    p = dict(t5_table=t5_table, norm_ffn=norm_ffn, ffn_w_gate=ffn_w_gate, ffn_w_up=ffn_w_up,
             ffn_w_down=ffn_w_down, norm_mix=norm_mix, w_in_even=w_in_even, pool_w=pool_w,
             pool_scale=pool_scale, attn_sinks=attn_sinks, w_out_even=w_out_even,
             rwkv_mu=rwkv_mu, rwkv_wr=rwkv_wr, rwkv_wk=rwkv_wk, rwkv_wv=rwkv_wv,
             rwkv_w0=rwkv_w0, rwkv_w1=rwkv_w1, rwkv_w2=rwkv_w2, rwkv_a0=rwkv_a0, rwkv_a1=rwkv_a1,
             rwkv_a2=rwkv_a2, rwkv_g1=rwkv_g1, rwkv_g2=rwkv_g2, rwkv_k_k=rwkv_k_k, rwkv_k_a=rwkv_k_a,
             rwkv_r_k=rwkv_r_k, rwkv_lnx_w=rwkv_lnx_w, rwkv_lnx_b=rwkv_lnx_b, rwkv_wo=rwkv_wo,
             norm_final=norm_final)
    y_prompt, pool_p, k_p, v_p, shift_p, wkv_p = trunk(x_prompt, 0, (None, None, None, None, None), p)
    y_sample, pool_s, k_s, v_s, shift_s, wkv_s = trunk(
        x_sample, PAST_LEN, (cache_pool, cache_swa_k, cache_swa_v, state_shift, state_wkv), p)
    return (y_prompt, y_sample, pool_p, pool_s, k_p, k_s, v_p, v_s, shift_p, shift_s, wkv_p, wkv_s)
```

```python
import functools
import math

import jax
import jax.numpy as jnp
from jax import lax
from jax.experimental import pallas as pl
from jax.experimental.pallas import tpu as pltpu

F32 = jnp.float32
BF16 = jnp.bfloat16

D_MODEL = 1024
D_FF = 2816
NORM_EPS = 1e-6
CHUNK = 64
POOL_WINDOWS = (2, 4, 8, 16)
C_POOL = 512
POOL_GC = 128
POOL_HIST = 15
POOL_HALO = 16
HEAD_DIM = 64
N_Q_HEADS = 8
N_KV_HEADS = 2
GQA_GROUP = 4
WINDOW = 128
SWA_ROWS = 128
Q_W = 512
KV_W = 128
IN_EVEN = C_POOL + Q_W + 2 * KV_W
N_BUCKETS = 32
MAX_DISTANCE = 128
RWKV_HEAD = 64
RWKV_H = 16
N_PAIRS = RWKV_H // 2
LANES = 128
LNX_EPS = 64e-5
PAST_LEN = 4096
VMEM_LIMIT_BYTES = 56 * 1024 * 1024


def _params(sem):
    return pltpu.CompilerParams(dimension_semantics=sem, vmem_limit_bytes=VMEM_LIMIT_BYTES)


def _dot(a, b):
    return jnp.dot(a, b, preferred_element_type=F32)


def _dot_nt(a, b):
    return lax.dot_general(a, b, (((1,), (1,)), ((), ())), preferred_element_type=F32)


def _dot_tn(a, b):
    return lax.dot_general(a, b, (((0,), (0,)), ((), ())), preferred_element_type=F32)


def _rms(x, g):
    return x * lax.rsqrt(jnp.mean(x * x, axis=-1, keepdims=True) + NORM_EPS) * g


def _split(x):
    hi = x.astype(BF16)
    lo = (x - hi.astype(F32)).astype(BF16)
    return hi, lo


def _head_sum(x, e_ref, et_ref):
    hi, lo = _split(x)
    s = _dot(hi, e_ref[...]) + _dot(lo, e_ref[...])
    shi, slo = _split(s)
    return _dot(shi, et_ref[...]) + _dot(slo, et_ref[...])


def _ffn_kernel(x_ref, g_ref, wg_ref, wu_ref, wd_ref, *rest, f_chunk, final):
    if final:
        gf_ref, o_ref, acc_ref = rest
    else:
        o_ref, acc_ref = rest
    x = x_ref[...]
    h = _rms(x, g_ref[...]).astype(BF16)
    for j in range(D_FF // f_chunk):
        sl = slice(j * f_chunk, (j + 1) * f_chunk)
        gate = _dot(h, wg_ref[:, sl])
        up = _dot(h, wu_ref[:, sl])
        act = (gate * jax.nn.sigmoid(gate) * up).astype(BF16)
        part = _dot(act, wd_ref[sl, :])
        if j == 0:
            acc_ref[...] = part
        else:
            acc_ref[...] += part
    y = x + 0.5 * acc_ref[...]
    if final:
        y = _rms(y, gf_ref[...])
    o_ref[...] = y


def _resident(shape, index_map):
    return pl.BlockSpec(shape, index_map, pipeline_mode=pl.Buffered(1))


def _ffn(x, g, wg, wu, wd, l, j, final_g=None):
    T = x.shape[0]
    tm = min(512, T)
    final = final_g is not None
    in_specs = [
        pl.BlockSpec((tm, D_MODEL), lambda i: (i, 0)),
        _resident((1, D_MODEL), lambda i: (0, 0)),
        _resident((None, None, D_MODEL, D_FF), lambda i: (l, j, 0, 0)),
        _resident((None, None, D_MODEL, D_FF), lambda i: (l, j, 0, 0)),
        _resident((None, None, D_FF, D_MODEL), lambda i: (l, j, 0, 0)),
    ]
    args = [x, g, wg, wu, wd]
    if final:
        in_specs.append(_resident((1, D_MODEL), lambda i: (0, 0)))
        args.append(final_g)
    return pl.pallas_call(
        functools.partial(_ffn_kernel, f_chunk=256, final=final),
        grid=(T // tm,),
        in_specs=in_specs,
        out_specs=pl.BlockSpec((tm, D_MODEL), lambda i: (i, 0)),
        out_shape=jax.ShapeDtypeStruct((T, D_MODEL), F32),
        scratch_shapes=[pltpu.VMEM((tm, D_MODEL), F32)],
        compiler_params=_params(("parallel",)),
    )(*args)


def _even_in_kernel(x_ref, g_ref, w_ref, u_ref, q_ref, k_ref, v_ref):
    h = _rms(x_ref[...], g_ref[...]).astype(BF16)
    z = _dot(h, w_ref[...])
    u_ref[...] = z[:, :C_POOL]
    q_ref[...] = z[:, C_POOL:C_POOL + Q_W].astype(BF16)
    k_ref[...] = z[:, C_POOL + Q_W:C_POOL + Q_W + KV_W]
    v_ref[...] = z[:, C_POOL + Q_W + KV_W:]


def _even_in(x, g, w_in, i):
    T = x.shape[0]
    tm = min(512, T)
    row = lambda n: pl.BlockSpec((tm, n), lambda t: (t, 0))
    return pl.pallas_call(
        _even_in_kernel,
        grid=(T // tm,),
        in_specs=[row(D_MODEL), _resident((1, D_MODEL), lambda t: (0, 0)),
                  _resident((None, D_MODEL, IN_EVEN), lambda t: (i, 0, 0))],
        out_specs=[row(C_POOL), row(Q_W), row(KV_W), row(KV_W)],
        out_shape=[jax.ShapeDtypeStruct((T, C_POOL), F32), jax.ShapeDtypeStruct((T, Q_W), BF16),
                   jax.ShapeDtypeStruct((T, KV_W), F32), jax.ShapeDtypeStruct((T, KV_W), F32)],
        compiler_params=_params(("parallel",)),
    )(x, g, w_in)


def _pool_kernel(u_ref, halo_ref, hist_ref, pw_ref, ps_ref, o_ref, *, tt, pos0):
    t = pl.program_id(1)
    halo = jnp.where(t == 0, hist_ref[0], halo_ref[...])
    u = u_ref[...]
    ext = jnp.concatenate([halo, u], axis=0)
    s2 = ext + pltpu.roll(ext, 1, 0)
    s4 = s2 + pltpu.roll(s2, 2, 0)
    s8 = s4 + pltpu.roll(s4, 4, 0)
    s16 = s8 + pltpu.roll(s8, 8, 0)
    pos = pos0 + t * tt + lax.broadcasted_iota(jnp.int32, (tt, 1), 0)
    outs = []
    for gi, (w, s) in enumerate(zip(POOL_WINDOWS, (s2, s4, s8, s16))):
        sl = slice(gi * POOL_GC, (gi + 1) * POOL_GC)
        cnt = jnp.minimum(w, pos + 1).astype(F32)
        pooled = s[POOL_HALO:, sl] / cnt - u[:, sl]
        outs.append(_dot(pooled.astype(BF16), pw_ref[gi]))
    o_ref[...] = (jnp.concatenate(outs, axis=-1) * ps_ref[...]).astype(BF16)


def _pool(u, hist, pool_w, pool_scale, B, L, pos0):
    tt = min(512, L)
    nt = L // tt
    hb = tt // POOL_HALO
    return pl.pallas_call(
        functools.partial(_pool_kernel, tt=tt, pos0=pos0),
        grid=(B, nt),
        in_specs=[
            pl.BlockSpec((tt, C_POOL), lambda b, t: (b * nt + t, 0)),
            pl.BlockSpec((POOL_HALO, C_POOL), lambda b, t: (jnp.maximum((b * nt + t) * hb - 1, 0), 0)),
            pl.BlockSpec((1, POOL_HALO, C_POOL), lambda b, t: (b, 0, 0)),
            _resident((len(POOL_WINDOWS), POOL_GC, POOL_GC), lambda b, t: (0, 0, 0)),
            _resident((1, C_POOL), lambda b, t: (0, 0)),
        ],
        out_specs=pl.BlockSpec((tt, C_POOL), lambda b, t: (b * nt + t, 0)),
        out_shape=jax.ShapeDtypeStruct((B * L, C_POOL), BF16),
        compiler_params=_params(("parallel", "arbitrary")),
    )(u, u, hist, pool_w, pool_scale)


def _attn_kernel(*refs, nkb, lq, lk, banded):
    q_ref = refs[0]
    k_refs = refs[1:1 + nkb]
    v_refs = refs[1 + nkb:1 + 2 * nkb]
    bkt_ref, tab_ref, sink_ref, o_ref, bias_ref = refs[1 + 2 * nkb:]

    first = pl.program_id(0) == 0
    if banded:
        first = jnp.logical_and(first, pl.program_id(1) == 0)

    @pl.when(first)
    def _():
        bkt = bkt_ref[...]
        for h in range(N_Q_HEADS):
            b = jnp.zeros(bkt.shape, F32)
            for n in range(N_BUCKETS):
                b = jnp.where(bkt == n, tab_ref[n, h], b)
            bias_ref[h] = b

    if banded:
        k = jnp.concatenate([r[...] for r in k_refs], axis=0)
        v = jnp.concatenate([r[...] for r in v_refs], axis=0)
        kpos = pl.program_id(1) * CHUNK - WINDOW + lax.broadcasted_iota(jnp.int32, (1, lk), 1)
        valid = kpos >= 0
    else:
        k = k_refs[0][0]
        v = v_refs[0][0]
    q = q_ref[...]
    outs = []
    for g in range(N_KV_HEADS):
        kg = k[:, g * HEAD_DIM:(g + 1) * HEAD_DIM].astype(BF16)
        vg = v[:, g * HEAD_DIM:(g + 1) * HEAD_DIM].astype(BF16)
        heads = [g * GQA_GROUP + i for i in range(GQA_GROUP)]
        qg = jnp.concatenate([q[:, h * HEAD_DIM:(h + 1) * HEAD_DIM] for h in heads], axis=0)
        s = _dot_nt(qg, kg) * (HEAD_DIM ** -0.5) + jnp.concatenate([bias_ref[h] for h in heads], axis=0)
        if banded:
            s = jnp.where(valid, s, -1e30)
        sink = jnp.concatenate([jnp.full((lq, 1), sink_ref[h], F32) for h in heads], axis=0)
        m = jnp.maximum(jnp.max(s, axis=-1, keepdims=True), sink)
        p = jnp.exp(s - m)
        denom = jnp.sum(p, axis=-1, keepdims=True) + jnp.exp(sink - m)
        o = _dot(p.astype(BF16), vg) / denom
        outs.extend(o[i * lq:(i + 1) * lq] for i in range(GQA_GROUP))
    o_ref[...] = jnp.concatenate(outs, axis=-1).astype(BF16)


def _smem():
    return pl.BlockSpec(memory_space=pltpu.SMEM)


def _attn_prompt(q, k, v, bkt, table, sinks, B, L):
    nc = L // CHUNK
    nkb = WINDOW // CHUNK + 1
    lk = nkb * CHUNK
    kspec = lambda j: pl.BlockSpec((CHUNK, KV_W), lambda b, c: (b * nc + jnp.maximum(c - (nkb - 1) + j, 0), 0))
    return pl.pallas_call(
        functools.partial(_attn_kernel, nkb=nkb, lq=CHUNK, lk=lk, banded=True),
        grid=(B, nc),
        in_specs=[pl.BlockSpec((CHUNK, Q_W), lambda b, c: (b * nc + c, 0))]
        + [kspec(j) for j in range(nkb)] + [kspec(j) for j in range(nkb)]
        + [_resident((CHUNK, lk), lambda b, c: (0, 0)), _smem(), _smem()],
        out_specs=pl.BlockSpec((CHUNK, Q_W), lambda b, c: (b * nc + c, 0)),
        out_shape=jax.ShapeDtypeStruct((B * L, Q_W), BF16),
        scratch_shapes=[pltpu.VMEM((N_Q_HEADS, CHUNK, lk), F32)],
        compiler_params=_params(("arbitrary", "arbitrary")),
    )(q, *([k] * nkb), *([v] * nkb), bkt, table, sinks)


def _attn_step(q, k_all, v_all, bkt, table, sinks, B, L):
    lk = k_all.shape[1]
    return pl.pallas_call(
        functools.partial(_attn_kernel, nkb=1, lq=L, lk=lk, banded=False),
        grid=(B,),
        in_specs=[pl.BlockSpec((L, Q_W), lambda b: (b, 0)),
                  pl.BlockSpec((1, lk, KV_W), lambda b: (b, 0, 0)),
                  pl.BlockSpec((1, lk, KV_W), lambda b: (b, 0, 0)),
                  _resident((L, lk), lambda b: (0, 0)), _smem(), _smem()],
        out_specs=pl.BlockSpec((L, Q_W), lambda b: (b, 0)),
        out_shape=jax.ShapeDtypeStruct((B * L, Q_W), BF16),
        scratch_shapes=[pltpu.VMEM((N_Q_HEADS, L, lk), F32)],
        compiler_params=_params(("arbitrary",)),
    )(q, k_all, v_all, bkt, table, sinks)


def _even_out_kernel(x_ref, p_ref, a_ref, wp_ref, wa_ref, o_ref):
    o_ref[...] = x_ref[...] + _dot(p_ref[...], wp_ref[...]) + _dot(a_ref[...], wa_ref[...])


def _even_out(x, pool_out, att, w_out, i):
    T = x.shape[0]
    tm = min(512, T)
    row = lambda n: pl.BlockSpec((tm, n), lambda t: (t, 0))
    return pl.pallas_call(
        _even_out_kernel,
        grid=(T // tm,),
        in_specs=[row(D_MODEL), row(C_POOL), row(Q_W),
                  _resident((None, C_POOL, D_MODEL), lambda t: (i, 0, 0)),
                  _resident((None, Q_W, D_MODEL), lambda t: (i, 1, 0))],
        out_specs=row(D_MODEL),
        out_shape=jax.ShapeDtypeStruct((T, D_MODEL), F32),
        compiler_params=_params(("parallel",)),
    )(x, pool_out, att, w_out, w_out)


def _t5_bucket(rel):
    half = N_BUCKETS // 2
    max_exact = half // 2
    side = jnp.where(rel > 0, half, 0)
    n = jnp.abs(rel)
    nf = jnp.maximum(n, max_exact).astype(F32)
    large = max_exact + (jnp.log(nf / max_exact) / math.log(MAX_DISTANCE / max_exact)
                         * (half - max_exact)).astype(jnp.int32)
    large = jnp.minimum(large, half - 1)
    return side + jnp.where(n < max_exact, n, large)


def _bucket_index(lq, lk, offset):
    rel = jnp.arange(lk)[None, :] - offset - jnp.arange(lq)[:, None]
    return _t5_bucket(rel).astype(jnp.int32)


def _rwkv_pre_kernel(x_ref, xp_ref, sh_ref, gn_ref, mu_ref, wr_ref, wk_ref, wv_ref, w1_ref, w2_ref,
                     a1_ref, a2_ref, g1_ref, g2_ref, w0_ref, a0_ref, kk_ref, ka_ref, e_ref, et_ref,
                     r_o, lw_o, k_o, v_o, kk_o, b_o, g_o, hs_o, *, tm):
    t = pl.program_id(1)
    gn = gn_ref[...]
    h = _rms(x_ref[...], gn)
    prev_tile_last = _rms(xp_ref[...], gn)[7:8]
    first_prev = jnp.where(t == 0, sh_ref[0], prev_tile_last)
    row = lax.broadcasted_iota(jnp.int32, (tm, 1), 0)
    h_prev = jnp.where(row == 0, first_prev, pltpu.roll(h, 1, 0))
    xx = h_prev - h

    def mix(j):
        return (h + xx * mu_ref[j:j + 1]).astype(BF16)

    r = _dot(mix(0), wr_ref[...])
    k = _dot(mix(2), wk_ref[...])
    v = _dot(mix(3), wv_ref[...])
    zw = w0_ref[...] + _dot(jnp.tanh(_dot(mix(1), w1_ref[...])).astype(BF16), w2_ref[...])
    w = -(jnp.maximum(-zw, 0.0) + jnp.log(1.0 + jnp.exp(-jnp.abs(zw)))) - 0.5
    lw = -jnp.exp(w)
    a = jax.nn.sigmoid(a0_ref[...] + _dot(_dot(mix(4), a1_ref[...]).astype(BF16), a2_ref[...]))
    g = _dot(jax.nn.sigmoid(_dot(mix(5), g1_ref[...])).astype(BF16), g2_ref[...])
    kk = k * kk_ref[...]
    kk = kk * lax.rsqrt(jnp.maximum(_head_sum(kk * kk, e_ref, et_ref), 1e-24))
    k = k * (1.0 + (a - 1.0) * ka_ref[...])
    b = kk * a
    for hp in range(N_PAIRS):
        sl = slice(hp * LANES, (hp + 1) * LANES)
        r_o[0, hp] = r[:, sl]
        lw_o[0, hp] = lw[:, sl]
        k_o[0, hp] = k[:, sl]
        v_o[0, hp] = v[:, sl]
        kk_o[0, hp] = kk[:, sl]
        b_o[0, hp] = b[:, sl]
    g_o[...] = g
    hs_o[0] = h[tm - 8:]


def _rwkv_pre(x, shift_prev, gn, p, i, B, L):
    tm = min(512, L)
    nt = L // tm
    vec = lambda: _resident((1, D_MODEL), lambda b, t: (0, 0))
    mat = lambda r, c: _resident((None, r, c), lambda b, t: (i, 0, 0))
    lora = p['rwkv_w1'].shape[-1], p['rwkv_a1'].shape[-1], p['rwkv_g1'].shape[-1]
    head_spec = pl.BlockSpec((1, N_PAIRS, tm, LANES), lambda b, t: (b, 0, t, 0))
    head_shape = jax.ShapeDtypeStruct((B, N_PAIRS, L, LANES), F32)
    return pl.pallas_call(
        functools.partial(_rwkv_pre_kernel, tm=tm),
        grid=(B, nt),
        in_specs=[
            pl.BlockSpec((tm, D_MODEL), lambda b, t: (b * nt + t, 0)),
            pl.BlockSpec((8, D_MODEL), lambda b, t: (jnp.maximum((b * nt + t) * (tm // 8) - 1, 0), 0)),
            pl.BlockSpec((1, 1, D_MODEL), lambda b, t: (b, 0, 0)),
            vec(),
            _resident((None, 6, D_MODEL), lambda b, t: (i, 0, 0)),
            mat(D_MODEL, D_MODEL), mat(D_MODEL, D_MODEL), mat(D_MODEL, D_MODEL),
            mat(D_MODEL, lora[0]), mat(lora[0], D_MODEL),
            mat(D_MODEL, lora[1]), mat(lora[1], D_MODEL),
            mat(D_MODEL, lora[2]), mat(lora[2], D_MODEL),
            vec(), vec(), vec(), vec(),
            _resident((D_MODEL, LANES), lambda b, t: (0, 0)),
            _resident((LANES, D_MODEL), lambda b, t: (0, 0)),
        ],
        out_specs=[head_spec] * 6 + [
            pl.BlockSpec((tm, D_MODEL), lambda b, t: (b * nt + t, 0)),
            pl.BlockSpec((1, 8, D_MODEL), lambda b, t: (b, 0, 0)),
        ],
        out_shape=[head_shape] * 6 + [
            jax.ShapeDtypeStruct((B * L, D_MODEL), F32),
            jax.ShapeDtypeStruct((B, 8, D_MODEL), F32),
        ],
        compiler_params=_params(("parallel", "arbitrary")),
    )(x, x, shift_prev, gn, p['rwkv_mu'], p['rwkv_wr'], p['rwkv_wk'], p['rwkv_wv'],
      p['rwkv_w1'], p['rwkv_w2'], p['rwkv_a1'], p['rwkv_a2'], p['rwkv_g1'], p['rwkv_g2'],
      p['rwkv_w0'][i][None], p['rwkv_a0'][i][None], p['rwkv_k_k'][i][None], p['rwkv_k_a'][i][None],
      p['head_onehot'], p['head_onehot_t'])


def _unit_lower_inverse(low, c):
    ti = lax.broadcasted_iota(jnp.int32, (c, c), 0)
    si = lax.broadcasted_iota(jnp.int32, (c, c), 1)
    inv = jnp.where(ti == si, 1.0, 0.0) + jnp.where(jnp.logical_and(ti == si + 1, (ti & 1) == 1), low, 0.0)
    m = 2
    while m < c:
        sh = m.bit_length() - 1
        sub = jnp.logical_and(jnp.logical_and((ti >> (sh + 1)) == (si >> (sh + 1)), ((ti >> sh) & 1) == 1),
                              ((si >> sh) & 1) == 0)
        e = jnp.where(sub, low, 0.0).astype(BF16)
        tb = inv.astype(BF16)
        inv = inv + _dot(_dot(tb, e).astype(BF16), tb)
        m *= 2
    return inv


def _rwkv_scan_kernel(r_ref, lw_ref, k_ref, v_ref, kk_ref, b_ref, s0_ref, y_ref, so_ref, s_ref, *, c):
    ci = pl.program_id(2)

    @pl.when(ci == 0)
    def _():
        s_ref[...] = s0_ref[0, 0]

    r, lw, k, v, kk, b = (ref[0, 0] for ref in (r_ref, lw_ref, k_ref, v_ref, kk_ref, b_ref))
    row = lax.broadcasted_iota(jnp.int32, (c, LANES), 0)
    lane = lax.broadcasted_iota(jnp.int32, (c, LANES), 1)
    cum = lw
    sh = 1
    while sh < c:
        cum = cum + jnp.where(row >= sh, pltpu.roll(cum, sh, 0), 0.0)
        sh *= 2
    tot = cum[c - 1:c]
    grow = jnp.exp(-cum)
    a_t = -kk * jnp.exp(cum - lw)
    r_t = r * jnp.exp(cum)
    tail = jnp.exp(tot - cum)
    ar = jnp.concatenate([a_t, r_t], axis=0)
    bk = jnp.concatenate([b * grow, k * grow], axis=0).astype(BF16)
    bk_end = jnp.concatenate([b * tail, k * tail], axis=0).astype(BF16)
    state = s_ref[...]
    from_state = _dot_nt(ar.astype(BF16), state.astype(BF16))
    ti = lax.broadcasted_iota(jnp.int32, (c, c), 0)
    si = lax.broadcasted_iota(jnp.int32, (c, c), 1)
    strict = ti > si
    incl = ti >= si
    vb = v.astype(BF16)
    sa = jnp.zeros((c, LANES), F32)
    y = from_state[c:]
    lane2 = jnp.concatenate([lane, lane], axis=0)
    for hd in range(2):
        own = (lane >= RWKV_HEAD) == bool(hd)
        own2 = (lane2 >= RWKV_HEAD) == bool(hd)
        gram = _dot_nt(jnp.where(own2, ar, 0.0).astype(BF16), bk)
        low = jnp.where(strict, gram[:c, :c], 0.0)
        m_ak = jnp.where(strict, gram[:c, c:], 0.0)
        m_r = jnp.concatenate([jnp.where(incl, gram[c:, :c], 0.0), jnp.where(incl, gram[c:, c:], 0.0)], axis=1)
        inv = _unit_lower_inverse(low, c)
        rhs = from_state[:c] + _dot(m_ak.astype(BF16), vb)
        sa_h = _dot(inv.astype(BF16), rhs.astype(BF16))
        sa = jnp.where(own, sa_h, sa)
        y_h = _dot(m_r.astype(BF16), jnp.concatenate([sa_h.astype(BF16), vb], axis=0))
        y = y + jnp.where(own, y_h, 0.0)
    y_ref[0, 0] = y
    sav = jnp.concatenate([sa.astype(BF16), vb], axis=0)
    upd = _dot_tn(sav, bk_end)
    pi = lax.broadcasted_iota(jnp.int32, (LANES, LANES), 0)
    pj = lax.broadcasted_iota(jnp.int32, (LANES, LANES), 1)
    same_head = (pi >= RWKV_HEAD) == (pj >= RWKV_HEAD)
    new_state = state * jnp.exp(tot) + jnp.where(same_head, upd, 0.0)
    s_ref[...] = new_state

    @pl.when(ci == pl.num_programs(2) - 1)
    def _():
        so_ref[0, 0] = new_state


def _rwkv_scan(r, lw, k, v, kk, b, s0, B, L):
    c = min(CHUNK, L)
    nc = L // c
    seq = pl.BlockSpec((1, 1, c, LANES), lambda bi, hp, ci: (bi, hp, ci, 0))
    st = pl.BlockSpec((1, 1, LANES, LANES), lambda bi, hp, ci: (bi, hp, 0, 0))
    return pl.pallas_call(
        functools.partial(_rwkv_scan_kernel, c=c),
        grid=(B, N_PAIRS, nc),
        in_specs=[seq] * 6 + [st],
        out_specs=[seq, st],
        out_shape=[jax.ShapeDtypeStruct((B, N_PAIRS, L, LANES), F32),
                   jax.ShapeDtypeStruct((B, N_PAIRS, LANES, LANES), F32)],
        scratch_shapes=[pltpu.VMEM((LANES, LANES), F32)],
        compiler_params=_params(("parallel", "parallel", "arbitrary")),
    )(r, lw, k, v, kk, b, s0)


def _rwkv_post_kernel(x_ref, y_ref, r_ref, k_ref, v_ref, g_ref, rk_ref, lw_ref, lb_ref, e_ref, et_ref,
                      wo_ref, o_ref):
    cat = lambda ref: jnp.concatenate([ref[0, hp] for hp in range(N_PAIRS)], axis=-1)
    y, r, k, v = cat(y_ref), cat(r_ref), cat(k_ref), cat(v_ref)
    mean = _head_sum(y, e_ref, et_ref) * (1.0 / RWKV_HEAD)
    d = y - mean
    var = _head_sum(d * d, e_ref, et_ref) * (1.0 / RWKV_HEAD)
    yn = d * lax.rsqrt(var + LNX_EPS) * lw_ref[...] + lb_ref[...]
    bonus = _head_sum(r * k * rk_ref[...], e_ref, et_ref) * v
    o_ref[...] = x_ref[...] + _dot(((yn + bonus) * g_ref[...]).astype(BF16), wo_ref[...])


def _rwkv_post(x, y, r, k, v, g, p, i, B, L):
    tm = min(512, L)
    nt = L // tm
    vec = lambda: _resident((1, D_MODEL), lambda b, t: (0, 0))
    head_spec = pl.BlockSpec((1, N_PAIRS, tm, LANES), lambda b, t: (b, 0, t, 0))
    row = pl.BlockSpec((tm, D_MODEL), lambda b, t: (b * nt + t, 0))
    return pl.pallas_call(
        _rwkv_post_kernel,
        grid=(B, nt),
        in_specs=[row, head_spec, head_spec, head_spec, head_spec, row, vec(), vec(), vec(),
                  _resident((D_MODEL, LANES), lambda b, t: (0, 0)),
                  _resident((LANES, D_MODEL), lambda b, t: (0, 0)),
                  _resident((None, D_MODEL, D_MODEL), lambda b, t: (i, 0, 0))],
        out_specs=row,
        out_shape=jax.ShapeDtypeStruct((B * L, D_MODEL), F32),
        compiler_params=_params(("parallel", "parallel")),
    )(x, y, r, k, v, g, p['rwkv_r_k'][i].reshape(1, D_MODEL), p['rwkv_lnx_w'][i][None], p['rwkv_lnx_b'][i][None],
      p['head_onehot'], p['head_onehot_t'], p['rwkv_wo'])


def _pack_state(s):
    B = s.shape[0]
    s = s.reshape(B, N_PAIRS, 2, RWKV_HEAD, RWKV_HEAD)
    z = jnp.zeros_like(s[:, :, 0])
    top = jnp.concatenate([s[:, :, 0], z], axis=-1)
    bot = jnp.concatenate([z, s[:, :, 1]], axis=-1)
    return jnp.concatenate([top, bot], axis=-2)


def _unpack_state(s):
    B = s.shape[0]
    return jnp.stack([s[:, :, :RWKV_HEAD, :RWKV_HEAD], s[:, :, RWKV_HEAD:, RWKV_HEAD:]], axis=2).reshape(
        B, RWKV_H, RWKV_HEAD, RWKV_HEAD)


def _trunk(x, pos0, caches, p):
    pool_c, k_c, v_c, shift_c, wkv_c = caches
    B, L, _ = x.shape
    depth = p['norm_mix'].shape[0]
    stepping = k_c is not None
    x = x.reshape(B * L, D_MODEL)
    if stepping:
        bkt = _bucket_index(L, SWA_ROWS + L, SWA_ROWS)
    else:
        bkt = _bucket_index(CHUNK, WINDOW + CHUNK, WINDOW)
    new_pool, new_k, new_v, new_shift, new_wkv = [], [], [], [], []
    for l in range(depth):
        i = l // 2
        x = _ffn(x, p['norm_ffn'][l, 0][None], p['ffn_w_gate'], p['ffn_w_up'], p['ffn_w_down'], l, 0)
        gn = p['norm_mix'][l][None]
        if l % 2 == 0:
            u, q, k, v = _even_in(x, gn, p['w_in_even'], i)
            if stepping:
                hist = jnp.pad(pool_c[i], ((0, 0), (POOL_HALO - POOL_HIST, 0), (0, 0)))
            else:
                hist = jnp.zeros((B, POOL_HALO, C_POOL), F32)
            pool_out = _pool(u, hist, p['pool_w'][i], p['pool_scale'][i][None], B, L, pos0)
            k3, v3 = k.reshape(B, L, KV_W), v.reshape(B, L, KV_W)
            if stepping:
                k3 = jnp.concatenate([k_c[i].reshape(B, SWA_ROWS, KV_W), k3], axis=1)
                v3 = jnp.concatenate([v_c[i].reshape(B, SWA_ROWS, KV_W), v3], axis=1)
                att = _attn_step(q, k3, v3, bkt, p['t5_table'], p['attn_sinks'][i], B, L)
            else:
                att = _attn_prompt(q, k, v, bkt, p['t5_table'], p['attn_sinks'][i], B, L)
            x = _even_out(x, pool_out, att, p['w_out_even'], i)
            full = jnp.concatenate([hist, u.reshape(B, L, C_POOL)], axis=1)
            new_pool.append(full[:, -POOL_HIST:])
            new_k.append(k3[:, -SWA_ROWS:].reshape(B, SWA_ROWS, N_KV_HEADS, HEAD_DIM))
            new_v.append(v3[:, -SWA_ROWS:].reshape(B, SWA_ROWS, N_KV_HEADS, HEAD_DIM))
        else:
            if stepping:
                shift_prev = shift_c[i][:, None, :]
                s0 = _pack_state(wkv_c[i])
            else:
                shift_prev = jnp.zeros((B, 1, D_MODEL), F32)
                s0 = jnp.zeros((B, N_PAIRS, LANES, LANES), F32)
            r, lw, k, v, kk, b, g, hs = _rwkv_pre(x, shift_prev, gn, p, i, B, L)
            y, s_new = _rwkv_scan(r, lw, k, v, kk, b, s0, B, L)
            x = _rwkv_post(x, y, r, k, v, g, p, i, B, L)
            new_shift.append(hs[:, 7])
            new_wkv.append(_unpack_state(s_new))
        final_g = p['norm_final'][None] if l == depth - 1 else None
        x = _ffn(x, p['norm_ffn'][l, 1][None], p['ffn_w_gate'], p['ffn_w_up'], p['ffn_w_down'], l, 1, final_g)
    return (x.reshape(B, L, D_MODEL), jnp.stack(new_pool), jnp.stack(new_k), jnp.stack(new_v),
            jnp.stack(new_shift), jnp.stack(new_wkv))


_MATMUL_WEIGHTS = ('ffn_w_gate', 'ffn_w_up', 'ffn_w_down', 'w_in_even', 'pool_w', 'w_out_even', 'rwkv_wr',
                   'rwkv_wk', 'rwkv_wv', 'rwkv_w1', 'rwkv_w2', 'rwkv_a1', 'rwkv_a2', 'rwkv_g1', 'rwkv_g2', 'rwkv_wo')


def _prepare(p):
    p = dict(p)
    for name in _MATMUL_WEIGHTS:
        p[name] = p[name].astype(BF16)
    onehot = (jnp.arange(D_MODEL)[:, None] // RWKV_HEAD == jnp.arange(LANES)[None, :]).astype(BF16)
    p['head_onehot'] = onehot
    p['head_onehot_t'] = onehot.T
    return p


def kernel(x_prompt, x_sample, cache_pool, cache_swa_k, cache_swa_v, state_shift, state_wkv, t5_table, norm_ffn, ffn_w_gate, ffn_w_up, ffn_w_down, norm_mix, w_in_even, pool_w, pool_scale, attn_sinks, w_out_even, rwkv_mu, rwkv_wr, rwkv_wk, rwkv_wv, rwkv_w0, rwkv_w1, rwkv_w2, rwkv_a0, rwkv_a1, rwkv_a2, rwkv_g1, rwkv_g2, rwkv_k_k, rwkv_k_a, rwkv_r_k, rwkv_lnx_w, rwkv_lnx_b, rwkv_wo, norm_final):
    p = _prepare(dict(
        t5_table=t5_table, norm_ffn=norm_ffn, ffn_w_gate=ffn_w_gate, ffn_w_up=ffn_w_up, ffn_w_down=ffn_w_down,
        norm_mix=norm_mix, w_in_even=w_in_even, pool_w=pool_w, pool_scale=pool_scale, attn_sinks=attn_sinks,
        w_out_even=w_out_even, rwkv_mu=rwkv_mu, rwkv_wr=rwkv_wr, rwkv_wk=rwkv_wk, rwkv_wv=rwkv_wv,
        rwkv_w0=rwkv_w0, rwkv_w1=rwkv_w1, rwkv_w2=rwkv_w2, rwkv_a0=rwkv_a0, rwkv_a1=rwkv_a1, rwkv_a2=rwkv_a2,
        rwkv_g1=rwkv_g1, rwkv_g2=rwkv_g2, rwkv_k_k=rwkv_k_k, rwkv_k_a=rwkv_k_a, rwkv_r_k=rwkv_r_k,
        rwkv_lnx_w=rwkv_lnx_w, rwkv_lnx_b=rwkv_lnx_b, rwkv_wo=rwkv_wo, norm_final=norm_final))
    y_p, pool_p, k_p, v_p, shift_p, wkv_p = _trunk(x_prompt, 0, (None, None, None, None, None), p)
    y_s, pool_s, k_s, v_s, shift_s, wkv_s = _trunk(
        x_sample, PAST_LEN, (cache_pool, cache_swa_k, cache_swa_v, state_shift, state_wkv), p)
    return (y_p, y_s, pool_p, pool_s, k_p, k_s, v_p, v_s, shift_p, shift_s, wkv_p, wkv_s)
```

```python
import functools
import math

import jax
import jax.numpy as jnp
from jax import lax
from jax.experimental import pallas as pl
from jax.experimental.pallas import tpu as pltpu

F32 = jnp.float32
BF16 = jnp.bfloat16

D_MODEL = 1024
D_FF = 2816
NORM_EPS = 1e-6
CHUNK = 64
POOL_WINDOWS = (2, 4, 8, 16)
C_POOL = 512
POOL_GC = 128
POOL_HIST = 15
POOL_HALO = 16
HEAD_DIM = 64
N_Q_HEADS = 8
N_KV_HEADS = 2
GQA_GROUP = 4
WINDOW = 128
SWA_ROWS = 128
Q_W = 512
KV_W = 128
IN_EVEN = C_POOL + Q_W + 2 * KV_W
N_BUCKETS = 32
MAX_DISTANCE = 128
RWKV_HEAD = 64
RWKV_H = 16
N_PAIRS = RWKV_H // 2
LANES = 128
LNX_EPS = 64e-5
PAST_LEN = 4096
VMEM_LIMIT_BYTES = 56 * 1024 * 1024


def _params(sem):
    return pltpu.CompilerParams(dimension_semantics=sem, vmem_limit_bytes=VMEM_LIMIT_BYTES)


def _dot(a, b):
    return jnp.dot(a, b, preferred_element_type=F32)


def _dot_nt(a, b):
    return lax.dot_general(a, b, (((1,), (1,)), ((), ())), preferred_element_type=F32)


def _dot_tn(a, b):
    return lax.dot_general(a, b, (((0,), (0,)), ((), ())), preferred_element_type=F32)


def _rms(x, g):
    return x * lax.rsqrt(jnp.mean(x * x, axis=-1, keepdims=True) + NORM_EPS) * g


def _split(x):
    hi = x.astype(BF16)
    lo = (x - hi.astype(F32)).astype(BF16)
    return hi, lo


def _head_sum(x, e_ref, et_ref):
    hi, lo = _split(x)
    s = _dot(hi, e_ref[...]) + _dot(lo, e_ref[...])
    shi, slo = _split(s)
    return _dot(shi, et_ref[...]) + _dot(slo, et_ref[...])


def _ffn_kernel(x_ref, g_ref, wg_ref, wu_ref, wd_ref, *rest, f_chunk, final):
    if final:
        gf_ref, o_ref, acc_ref = rest
    else:
        o_ref, acc_ref = rest
    x = x_ref[...]
    h = _rms(x, g_ref[...]).astype(BF16)
    for j in range(D_FF // f_chunk):
        sl = slice(j * f_chunk, (j + 1) * f_chunk)
        gate = _dot(h, wg_ref[:, sl])
        up = _dot(h, wu_ref[:, sl])
        act = (gate * jax.nn.sigmoid(gate) * up).astype(BF16)
        part = _dot(act, wd_ref[sl, :])
        if j == 0:
            acc_ref[...] = part
        else:
            acc_ref[...] += part
    y = x + 0.5 * acc_ref[...]
    if final:
        y = _rms(y, gf_ref[...])
    o_ref[...] = y


def _resident(shape, index_map):
    return pl.BlockSpec(shape, index_map, pipeline_mode=pl.Buffered(1))


def _ffn(x, g, wg, wu, wd, l, j, final_g=None):
    T = x.shape[0]
    tm = min(512, T)
    final = final_g is not None
    in_specs = [
        pl.BlockSpec((tm, D_MODEL), lambda i: (i, 0)),
        _resident((1, D_MODEL), lambda i: (0, 0)),
        _resident((None, None, D_MODEL, D_FF), lambda i: (l, j, 0, 0)),
        _resident((None, None, D_MODEL, D_FF), lambda i: (l, j, 0, 0)),
        _resident((None, None, D_FF, D_MODEL), lambda i: (l, j, 0, 0)),
    ]
    args = [x, g, wg, wu, wd]
    if final:
        in_specs.append(_resident((1, D_MODEL), lambda i: (0, 0)))
        args.append(final_g)
    return pl.pallas_call(
        functools.partial(_ffn_kernel, f_chunk=256, final=final),
        grid=(T // tm,),
        in_specs=in_specs,
        out_specs=pl.BlockSpec((tm, D_MODEL), lambda i: (i, 0)),
        out_shape=jax.ShapeDtypeStruct((T, D_MODEL), F32),
        scratch_shapes=[pltpu.VMEM((tm, D_MODEL), F32)],
        compiler_params=_params(("parallel",)),
    )(*args)


def _even_in_kernel(x_ref, g_ref, w_ref, u_ref, q_ref, k_ref, v_ref):
    h = _rms(x_ref[...], g_ref[...]).astype(BF16)
    z = _dot(h, w_ref[...])
    u_ref[...] = z[:, :C_POOL]
    q_ref[...] = z[:, C_POOL:C_POOL + Q_W].astype(BF16)
    k_ref[...] = z[:, C_POOL + Q_W:C_POOL + Q_W + KV_W]
    v_ref[...] = z[:, C_POOL + Q_W + KV_W:]


def _even_in(x, g, w_in, i):
    T = x.shape[0]
    tm = min(512, T)
    row = lambda n: pl.BlockSpec((tm, n), lambda t: (t, 0))
    return pl.pallas_call(
        _even_in_kernel,
        grid=(T // tm,),
        in_specs=[row(D_MODEL), _resident((1, D_MODEL), lambda t: (0, 0)),
                  _resident((None, D_MODEL, IN_EVEN), lambda t: (i, 0, 0))],
        out_specs=[row(C_POOL), row(Q_W), row(KV_W), row(KV_W)],
        out_shape=[jax.ShapeDtypeStruct((T, C_POOL), F32), jax.ShapeDtypeStruct((T, Q_W), BF16),
                   jax.ShapeDtypeStruct((T, KV_W), F32), jax.ShapeDtypeStruct((T, KV_W), F32)],
        compiler_params=_params(("parallel",)),
    )(x, g, w_in)


def _pool_kernel(u_ref, halo_ref, hist_ref, pw_ref, ps_ref, o_ref, *, tt, pos0):
    t = pl.program_id(1)
    halo = jnp.where(t == 0, hist_ref[0], halo_ref[...])
    u = u_ref[...]
    ext = jnp.concatenate([halo, u], axis=0)
    s2 = ext + pltpu.roll(ext, 1, 0)
    s4 = s2 + pltpu.roll(s2, 2, 0)
    s8 = s4 + pltpu.roll(s4, 4, 0)
    s16 = s8 + pltpu.roll(s8, 8, 0)
    pos = pos0 + t * tt + lax.broadcasted_iota(jnp.int32, (tt, 1), 0)
    outs = []
    for gi, (w, s) in enumerate(zip(POOL_WINDOWS, (s2, s4, s8, s16))):
        sl = slice(gi * POOL_GC, (gi + 1) * POOL_GC)
        cnt = jnp.minimum(w, pos + 1).astype(F32)
        pooled = s[POOL_HALO:, sl] / cnt - u[:, sl]
        outs.append(_dot(pooled.astype(BF16), pw_ref[gi]))
    o_ref[...] = (jnp.concatenate(outs, axis=-1) * ps_ref[...]).astype(BF16)


def _pool(u, hist, pool_w, pool_scale, B, L, pos0):
    tt = min(512, L)
    nt = L // tt
    hb = tt // POOL_HALO
    return pl.pallas_call(
        functools.partial(_pool_kernel, tt=tt, pos0=pos0),
        grid=(B, nt),
        in_specs=[
            pl.BlockSpec((tt, C_POOL), lambda b, t: (b * nt + t, 0)),
            pl.BlockSpec((POOL_HALO, C_POOL), lambda b, t: (jnp.maximum((b * nt + t) * hb - 1, 0), 0)),
            pl.BlockSpec((1, POOL_HALO, C_POOL), lambda b, t: (b, 0, 0)),
            _resident((len(POOL_WINDOWS), POOL_GC, POOL_GC), lambda b, t: (0, 0, 0)),
            _resident((1, C_POOL), lambda b, t: (0, 0)),
        ],
        out_specs=pl.BlockSpec((tt, C_POOL), lambda b, t: (b * nt + t, 0)),
        out_shape=jax.ShapeDtypeStruct((B * L, C_POOL), BF16),
        compiler_params=_params(("parallel", "arbitrary")),
    )(u, u, hist, pool_w, pool_scale)


def _attn_kernel(*refs, nkb, lq, lk, banded):
    q_ref = refs[0]
    k_refs = refs[1:1 + nkb]
    v_refs = refs[1 + nkb:1 + 2 * nkb]
    bkt_ref, tab_ref, sink_ref, o_ref, bias_ref = refs[1 + 2 * nkb:]

    first = pl.program_id(0) == 0
    if banded:
        first = jnp.logical_and(first, pl.program_id(1) == 0)

    @pl.when(first)
    def _():
        bkt = bkt_ref[...]
        for h in range(N_Q_HEADS):
            b = jnp.zeros(bkt.shape, F32)
            for n in range(N_BUCKETS):
                b = jnp.where(bkt == n, tab_ref[n, h], b)
            bias_ref[h] = b

    if banded:
        k = jnp.concatenate([r[...] for r in k_refs], axis=0)
        v = jnp.concatenate([r[...] for r in v_refs], axis=0)
        kpos = pl.program_id(1) * CHUNK - WINDOW + lax.broadcasted_iota(jnp.int32, (1, lk), 1)
        valid = kpos >= 0
    else:
        k = k_refs[0][0]
        v = v_refs[0][0]
    q = q_ref[...]
    outs = []
    for g in range(N_KV_HEADS):
        kg = k[:, g * HEAD_DIM:(g + 1) * HEAD_DIM].astype(BF16)
        vg = v[:, g * HEAD_DIM:(g + 1) * HEAD_DIM].astype(BF16)
        heads = [g * GQA_GROUP + i for i in range(GQA_GROUP)]
        qg = jnp.concatenate([q[:, h * HEAD_DIM:(h + 1) * HEAD_DIM] for h in heads], axis=0)
        s = _dot_nt(qg, kg) * (HEAD_DIM ** -0.5) + jnp.concatenate([bias_ref[h] for h in heads], axis=0)
        if banded:
            s = jnp.where(valid, s, -1e30)
        sink = jnp.concatenate([jnp.full((lq, 1), sink_ref[h], F32) for h in heads], axis=0)
        m = jnp.maximum(jnp.max(s, axis=-1, keepdims=True), sink)
        p = jnp.exp(s - m)
        denom = jnp.sum(p, axis=-1, keepdims=True) + jnp.exp(sink - m)
        o = _dot(p.astype(BF16), vg) / denom
        outs.extend(o[i * lq:(i + 1) * lq] for i in range(GQA_GROUP))
    o_ref[...] = jnp.concatenate(outs, axis=-1).astype(BF16)


def _smem():
    return pl.BlockSpec(memory_space=pltpu.SMEM)


def _attn_prompt(q, k, v, bkt, table, sinks, B, L):
    nc = L // CHUNK
    nkb = WINDOW // CHUNK + 1
    lk = nkb * CHUNK
    kspec = lambda j: pl.BlockSpec((CHUNK, KV_W), lambda b, c: (b * nc + jnp.maximum(c - (nkb - 1) + j, 0), 0))
    return pl.pallas_call(
        functools.partial(_attn_kernel, nkb=nkb, lq=CHUNK, lk=lk, banded=True),
        grid=(B, nc),
        in_specs=[pl.BlockSpec((CHUNK, Q_W), lambda b, c: (b * nc + c, 0))]
        + [kspec(j) for j in range(nkb)] + [kspec(j) for j in range(nkb)]
        + [_resident((CHUNK, lk), lambda b, c: (0, 0)), _smem(), _smem()],
        out_specs=pl.BlockSpec((CHUNK, Q_W), lambda b, c: (b * nc + c, 0)),
        out_shape=jax.ShapeDtypeStruct((B * L, Q_W), BF16),
        scratch_shapes=[pltpu.VMEM((N_Q_HEADS, CHUNK, lk), F32)],
        compiler_params=_params(("arbitrary", "arbitrary")),
    )(q, *([k] * nkb), *([v] * nkb), bkt, table, sinks)


def _attn_step(q, k_all, v_all, bkt, table, sinks, B, L):
    lk = k_all.shape[1]
    return pl.pallas_call(
        functools.partial(_attn_kernel, nkb=1, lq=L, lk=lk, banded=False),
        grid=(B,),
        in_specs=[pl.BlockSpec((L, Q_W), lambda b: (b, 0)),
                  pl.BlockSpec((1, lk, KV_W), lambda b: (b, 0, 0)),
                  pl.BlockSpec((1, lk, KV_W), lambda b: (b, 0, 0)),
                  _resident((L, lk), lambda b: (0, 0)), _smem(), _smem()],
        out_specs=pl.BlockSpec((L, Q_W), lambda b: (b, 0)),
        out_shape=jax.ShapeDtypeStruct((B * L, Q_W), BF16),
        scratch_shapes=[pltpu.VMEM((N_Q_HEADS, L, lk), F32)],
        compiler_params=_params(("arbitrary",)),
    )(q, k_all, v_all, bkt, table, sinks)


def _even_out_kernel(x_ref, p_ref, a_ref, wp_ref, wa_ref, o_ref):
    o_ref[...] = x_ref[...] + _dot(p_ref[...], wp_ref[...]) + _dot(a_ref[...], wa_ref[...])


def _even_out(x, pool_out, att, w_out, i):
    T = x.shape[0]
    tm = min(512, T)
    row = lambda n: pl.BlockSpec((tm, n), lambda t: (t, 0))
    return pl.pallas_call(
        _even_out_kernel,
        grid=(T // tm,),
        in_specs=[row(D_MODEL), row(C_POOL), row(Q_W),
                  _resident((None, C_POOL, D_MODEL), lambda t: (i, 0, 0)),
                  _resident((None, Q_W, D_MODEL), lambda t: (i, 1, 0))],
        out_specs=row(D_MODEL),
        out_shape=jax.ShapeDtypeStruct((T, D_MODEL), F32),
        compiler_params=_params(("parallel",)),
    )(x, pool_out, att, w_out, w_out)


def _t5_bucket(rel):
    half = N_BUCKETS // 2
    max_exact = half // 2
    side = jnp.where(rel > 0, half, 0)
    n = jnp.abs(rel)
    nf = jnp.maximum(n, max_exact).astype(F32)
    large = max_exact + (jnp.log(nf / max_exact) / math.log(MAX_DISTANCE / max_exact)
                         * (half - max_exact)).astype(jnp.int32)
    large = jnp.minimum(large, half - 1)
    return side + jnp.where(n < max_exact, n, large)


def _bucket_index(lq, lk, offset):
    rel = jnp.arange(lk)[None, :] - offset - jnp.arange(lq)[:, None]
    return _t5_bucket(rel).astype(jnp.int32)


def _rwkv_pre_kernel(x_ref, xp_ref, sh_ref, gn_ref, mu_ref, wr_ref, wk_ref, wv_ref, w1_ref, w2_ref,
                     a1_ref, a2_ref, g1_ref, g2_ref, w0_ref, a0_ref, kk_ref, ka_ref, e_ref, et_ref,
                     r_o, lw_o, k_o, v_o, kk_o, b_o, g_o, hs_o, *, tm):
    t = pl.program_id(1)
    gn = gn_ref[...]
    h = _rms(x_ref[...], gn)
    prev_tile_last = _rms(xp_ref[...], gn)[7:8]
    first_prev = jnp.where(t == 0, sh_ref[0], prev_tile_last)
    row = lax.broadcasted_iota(jnp.int32, (tm, 1), 0)
    h_prev = jnp.where(row == 0, first_prev, pltpu.roll(h, 1, 0))
    xx = h_prev - h

    def mix(j):
        return (h + xx * mu_ref[j:j + 1]).astype(BF16)

    r = _dot(mix(0), wr_ref[...])
    k = _dot(mix(2), wk_ref[...])
    v = _dot(mix(3), wv_ref[...])
    zw = w0_ref[...] + _dot(jnp.tanh(_dot(mix(1), w1_ref[...])).astype(BF16), w2_ref[...])
    w = -(jnp.maximum(-zw, 0.0) + jnp.log(1.0 + jnp.exp(-jnp.abs(zw)))) - 0.5
    lw = -jnp.exp(w)
    a = jax.nn.sigmoid(a0_ref[...] + _dot(_dot(mix(4), a1_ref[...]).astype(BF16), a2_ref[...]))
    g = _dot(jax.nn.sigmoid(_dot(mix(5), g1_ref[...])).astype(BF16), g2_ref[...])
    kk = k * kk_ref[...]
    kk = kk * lax.rsqrt(jnp.maximum(_head_sum(kk * kk, e_ref, et_ref), 1e-24))
    k = k * (1.0 + (a - 1.0) * ka_ref[...])
    b = kk * a
    for hp in range(N_PAIRS):
        sl = slice(hp * LANES, (hp + 1) * LANES)
        r_o[0, hp] = r[:, sl]
        lw_o[0, hp] = lw[:, sl]
        k_o[0, hp] = k[:, sl]
        v_o[0, hp] = v[:, sl]
        kk_o[0, hp] = kk[:, sl]
        b_o[0, hp] = b[:, sl]
    g_o[...] = g
    hs_o[0] = h[tm - 8:]


def _rwkv_pre(x, shift_prev, gn, p, i, B, L):
    tm = min(512, L)
    nt = L // tm
    vec = lambda: _resident((1, D_MODEL), lambda b, t: (0, 0))
    mat = lambda r, c: _resident((None, r, c), lambda b, t: (i, 0, 0))
    lora = p['rwkv_w1'].shape[-1], p['rwkv_a1'].shape[-1], p['rwkv_g1'].shape[-1]
    head_spec = pl.BlockSpec((1, N_PAIRS, tm, LANES), lambda b, t: (b, 0, t, 0))
    head_shape = jax.ShapeDtypeStruct((B, N_PAIRS, L, LANES), F32)
    return pl.pallas_call(
        functools.partial(_rwkv_pre_kernel, tm=tm),
        grid=(B, nt),
        in_specs=[
            pl.BlockSpec((tm, D_MODEL), lambda b, t: (b * nt + t, 0)),
            pl.BlockSpec((8, D_MODEL), lambda b, t: (jnp.maximum((b * nt + t) * (tm // 8) - 1, 0), 0)),
            pl.BlockSpec((1, 1, D_MODEL), lambda b, t: (b, 0, 0)),
            vec(),
            _resident((None, 6, D_MODEL), lambda b, t: (i, 0, 0)),
            mat(D_MODEL, D_MODEL), mat(D_MODEL, D_MODEL), mat(D_MODEL, D_MODEL),
            mat(D_MODEL, lora[0]), mat(lora[0], D_MODEL),
            mat(D_MODEL, lora[1]), mat(lora[1], D_MODEL),
            mat(D_MODEL, lora[2]), mat(lora[2], D_MODEL),
            vec(), vec(), vec(), vec(),
            _resident((D_MODEL, LANES), lambda b, t: (0, 0)),
            _resident((LANES, D_MODEL), lambda b, t: (0, 0)),
        ],
        out_specs=[head_spec] * 6 + [
            pl.BlockSpec((tm, D_MODEL), lambda b, t: (b * nt + t, 0)),
            pl.BlockSpec((1, 8, D_MODEL), lambda b, t: (b, 0, 0)),
        ],
        out_shape=[head_shape] * 6 + [
            jax.ShapeDtypeStruct((B * L, D_MODEL), F32),
            jax.ShapeDtypeStruct((B, 8, D_MODEL), F32),
        ],
        compiler_params=_params(("parallel", "arbitrary")),
    )(x, x, shift_prev, gn, p['rwkv_mu'], p['rwkv_wr'], p['rwkv_wk'], p['rwkv_wv'],
      p['rwkv_w1'], p['rwkv_w2'], p['rwkv_a1'], p['rwkv_a2'], p['rwkv_g1'], p['rwkv_g2'],
      p['rwkv_w0'][i][None], p['rwkv_a0'][i][None], p['rwkv_k_k'][i][None], p['rwkv_k_a'][i][None],
      p['head_onehot'], p['head_onehot_t'])


def _unit_lower_inverses(lows, n):
    ri = lax.broadcasted_iota(jnp.int32, (n, n), 0)
    qi = lax.broadcasted_iota(jnp.int32, (n, n), 1)
    eye = jnp.where(ri == qi, 1.0, 0.0)
    first = jnp.logical_and(ri == qi + 1, (ri & 1) == 1)
    invs = [eye + jnp.where(first, low, 0.0) for low in lows]
    m = 2
    while m < n:
        sh = m.bit_length() - 1
        sub = jnp.logical_and(jnp.logical_and((ri >> (sh + 1)) == (qi >> (sh + 1)), ((ri >> sh) & 1) == 1),
                              ((qi >> sh) & 1) == 0)
        tbs = [inv.astype(BF16) for inv in invs]
        half = [_dot(tb, jnp.where(sub, low, 0.0).astype(BF16)).astype(BF16) for tb, low in zip(tbs, lows)]
        invs = [inv + _dot(h, tb) for inv, h, tb in zip(invs, half, tbs)]
        m *= 2
    return invs


def _scan_chunks(seqs, states, c):
    n = 2 * c
    row = lax.broadcasted_iota(jnp.int32, (c, LANES), 0)
    head1 = lax.broadcasted_iota(jnp.int32, (c, LANES), 1) >= RWKV_HEAD
    ri = lax.broadcasted_iota(jnp.int32, (n, n), 0)
    qi = lax.broadcasted_iota(jnp.int32, (n, n), 1)
    strict = ri > qi
    incl = ri >= qi
    pi = lax.broadcasted_iota(jnp.int32, (LANES, LANES), 0)
    pj = lax.broadcasted_iota(jnp.int32, (LANES, LANES), 1)
    same_head = (pi >= RWKV_HEAD) == (pj >= RWKV_HEAD)

    def stack(x):
        return jnp.concatenate([jnp.where(head1, 0.0, x), jnp.where(head1, x, 0.0)], axis=0)

    def unstack(x):
        return x[:c] + x[c:]

    lhs, rhs, v_st, v_bf, bk_end, decay = [], [], [], [], [], []
    for r, lw, k, v, kk, b in seqs:
        cum = lw
        sh = 1
        while sh < c:
            cum = cum + jnp.where(row >= sh, pltpu.roll(cum, sh, 0), 0.0)
            sh *= 2
        tot = cum[c - 1:c]
        grow = jnp.exp(-cum)
        tail = jnp.exp(tot - cum)
        a_t = -kk * jnp.exp(cum - lw)
        r_t = r * jnp.exp(cum)
        lhs.append(jnp.concatenate([stack(a_t), stack(r_t)], axis=0).astype(BF16))
        rhs.append(jnp.concatenate([stack(b * grow), stack(k * grow)], axis=0).astype(BF16))
        v_st.append(stack(v).astype(BF16))
        v_bf.append(v.astype(BF16))
        bk_end.append(jnp.concatenate([b * tail, k * tail], axis=0).astype(BF16))
        decay.append(jnp.exp(tot))
    grams = [_dot_nt(x, y) for x, y in zip(lhs, rhs)]
    from_state = [_dot_nt(x, s.astype(BF16)) for x, s in zip(lhs, states)]
    lows = [jnp.where(strict, g[:n, :n], 0.0) for g in grams]
    m_ak = [jnp.where(strict, g[:n, n:], 0.0).astype(BF16) for g in grams]
    m_r = [jnp.concatenate([jnp.where(incl, g[n:, :n], 0.0), jnp.where(incl, g[n:, n:], 0.0)], axis=1).astype(BF16)
           for g in grams]
    invs = _unit_lower_inverses(lows, n)
    rhs_sa = [(f[:n] + _dot(m, vs)).astype(BF16) for f, m, vs in zip(from_state, m_ak, v_st)]
    sa_st = [_dot(inv.astype(BF16), x) for inv, x in zip(invs, rhs_sa)]
    ys = [unstack(f[n:] + _dot(m, jnp.concatenate([sa.astype(BF16), vs], axis=0)))
          for f, m, sa, vs in zip(from_state, m_r, sa_st, v_st)]
    upd = [_dot_tn(jnp.concatenate([unstack(sa).astype(BF16), vb], axis=0), be)
           for sa, vb, be in zip(sa_st, v_bf, bk_end)]
    new_states = [s * d + jnp.where(same_head, u, 0.0) for s, d, u in zip(states, decay, upd)]
    return ys, new_states


def _rwkv_scan_kernel(r_ref, lw_ref, k_ref, v_ref, kk_ref, b_ref, s0_ref, y_ref, so_ref, s_ref, *, c):
    ci = pl.program_id(1)

    @pl.when(ci == 0)
    def _():
        s_ref[...] = s0_ref[0]

    seqs = [tuple(ref[0, hp] for ref in (r_ref, lw_ref, k_ref, v_ref, kk_ref, b_ref)) for hp in range(N_PAIRS)]
    ys, new_states = _scan_chunks(seqs, [s_ref[hp] for hp in range(N_PAIRS)], c)
    for hp in range(N_PAIRS):
        y_ref[0, hp] = ys[hp]
        s_ref[hp] = new_states[hp]

    @pl.when(ci == pl.num_programs(1) - 1)
    def _():
        so_ref[0] = s_ref[...]


def _rwkv_scan(r, lw, k, v, kk, b, s0, B, L):
    c = min(CHUNK, L)
    nc = L // c
    seq = pl.BlockSpec((1, N_PAIRS, c, LANES), lambda bi, ci: (bi, 0, ci, 0))
    st = pl.BlockSpec((1, N_PAIRS, LANES, LANES), lambda bi, ci: (bi, 0, 0, 0))
    return pl.pallas_call(
        functools.partial(_rwkv_scan_kernel, c=c),
        grid=(B, nc),
        in_specs=[seq] * 6 + [st],
        out_specs=[seq, st],
        out_shape=[jax.ShapeDtypeStruct((B, N_PAIRS, L, LANES), F32),
                   jax.ShapeDtypeStruct((B, N_PAIRS, LANES, LANES), F32)],
        scratch_shapes=[pltpu.VMEM((N_PAIRS, LANES, LANES), F32)],
        compiler_params=_params(("parallel", "arbitrary")),
    )(r, lw, k, v, kk, b, s0)


def _rwkv_post_kernel(x_ref, y_ref, r_ref, k_ref, v_ref, g_ref, rk_ref, lw_ref, lb_ref, e_ref, et_ref,
                      wo_ref, o_ref):
    cat = lambda ref: jnp.concatenate([ref[0, hp] for hp in range(N_PAIRS)], axis=-1)
    y, r, k, v = cat(y_ref), cat(r_ref), cat(k_ref), cat(v_ref)
    mean = _head_sum(y, e_ref, et_ref) * (1.0 / RWKV_HEAD)
    d = y - mean
    var = _head_sum(d * d, e_ref, et_ref) * (1.0 / RWKV_HEAD)
    yn = d * lax.rsqrt(var + LNX_EPS) * lw_ref[...] + lb_ref[...]
    bonus = _head_sum(r * k * rk_ref[...], e_ref, et_ref) * v
    o_ref[...] = x_ref[...] + _dot(((yn + bonus) * g_ref[...]).astype(BF16), wo_ref[...])


def _rwkv_post(x, y, r, k, v, g, p, i, B, L):
    tm = min(512, L)
    nt = L // tm
    vec = lambda: _resident((1, D_MODEL), lambda b, t: (0, 0))
    head_spec = pl.BlockSpec((1, N_PAIRS, tm, LANES), lambda b, t: (b, 0, t, 0))
    row = pl.BlockSpec((tm, D_MODEL), lambda b, t: (b * nt + t, 0))
    return pl.pallas_call(
        _rwkv_post_kernel,
        grid=(B, nt),
        in_specs=[row, head_spec, head_spec, head_spec, head_spec, row, vec(), vec(), vec(),
                  _resident((D_MODEL, LANES), lambda b, t: (0, 0)),
                  _resident((LANES, D_MODEL), lambda b, t: (0, 0)),
                  _resident((None, D_MODEL, D_MODEL), lambda b, t: (i, 0, 0))],
        out_specs=row,
        out_shape=jax.ShapeDtypeStruct((B * L, D_MODEL), F32),
        compiler_params=_params(("parallel", "parallel")),
    )(x, y, r, k, v, g, p['rwkv_r_k'][i].reshape(1, D_MODEL), p['rwkv_lnx_w'][i][None], p['rwkv_lnx_b'][i][None],
      p['head_onehot'], p['head_onehot_t'], p['rwkv_wo'])


def _pack_state(s):
    B = s.shape[0]
    s = s.reshape(B, N_PAIRS, 2, RWKV_HEAD, RWKV_HEAD)
    z = jnp.zeros_like(s[:, :, 0])
    top = jnp.concatenate([s[:, :, 0], z], axis=-1)
    bot = jnp.concatenate([z, s[:, :, 1]], axis=-1)
    return jnp.concatenate([top, bot], axis=-2)


def _unpack_state(s):
    B = s.shape[0]
    return jnp.stack([s[:, :, :RWKV_HEAD, :RWKV_HEAD], s[:, :, RWKV_HEAD:, RWKV_HEAD:]], axis=2).reshape(
        B, RWKV_H, RWKV_HEAD, RWKV_HEAD)


def _trunk(x, pos0, caches, p):
    pool_c, k_c, v_c, shift_c, wkv_c = caches
    B, L, _ = x.shape
    depth = p['norm_mix'].shape[0]
    stepping = k_c is not None
    x = x.reshape(B * L, D_MODEL)
    if stepping:
        bkt = _bucket_index(L, SWA_ROWS + L, SWA_ROWS)
    else:
        bkt = _bucket_index(CHUNK, WINDOW + CHUNK, WINDOW)
    new_pool, new_k, new_v, new_shift, new_wkv = [], [], [], [], []
    for l in range(depth):
        i = l // 2
        x = _ffn(x, p['norm_ffn'][l, 0][None], p['ffn_w_gate'], p['ffn_w_up'], p['ffn_w_down'], l, 0)
        gn = p['norm_mix'][l][None]
        if l % 2 == 0:
            u, q, k, v = _even_in(x, gn, p['w_in_even'], i)
            if stepping:
                hist = jnp.pad(pool_c[i], ((0, 0), (POOL_HALO - POOL_HIST, 0), (0, 0)))
            else:
                hist = jnp.zeros((B, POOL_HALO, C_POOL), F32)
            pool_out = _pool(u, hist, p['pool_w'][i], p['pool_scale'][i][None], B, L, pos0)
            k3, v3 = k.reshape(B, L, KV_W), v.reshape(B, L, KV_W)
            if stepping:
                k3 = jnp.concatenate([k_c[i].reshape(B, SWA_ROWS, KV_W), k3], axis=1)
                v3 = jnp.concatenate([v_c[i].reshape(B, SWA_ROWS, KV_W), v3], axis=1)
                att = _attn_step(q, k3, v3, bkt, p['t5_table'], p['attn_sinks'][i], B, L)
            else:
                att = _attn_prompt(q, k, v, bkt, p['t5_table'], p['attn_sinks'][i], B, L)
            x = _even_out(x, pool_out, att, p['w_out_even'], i)
            full = jnp.concatenate([hist, u.reshape(B, L, C_POOL)], axis=1)
            new_pool.append(full[:, -POOL_HIST:])
            new_k.append(k3[:, -SWA_ROWS:].reshape(B, SWA_ROWS, N_KV_HEADS, HEAD_DIM))
            new_v.append(v3[:, -SWA_ROWS:].reshape(B, SWA_ROWS, N_KV_HEADS, HEAD_DIM))
        else:
            if stepping:
                shift_prev = shift_c[i][:, None, :]
                s0 = _pack_state(wkv_c[i])
            else:
                shift_prev = jnp.zeros((B, 1, D_MODEL), F32)
                s0 = jnp.zeros((B, N_PAIRS, LANES, LANES), F32)
            r, lw, k, v, kk, b, g, hs = _rwkv_pre(x, shift_prev, gn, p, i, B, L)
            y, s_new = _rwkv_scan(r, lw, k, v, kk, b, s0, B, L)
            x = _rwkv_post(x, y, r, k, v, g, p, i, B, L)
            new_shift.append(hs[:, 7])
            new_wkv.append(_unpack_state(s_new))
        final_g = p['norm_final'][None] if l == depth - 1 else None
        x = _ffn(x, p['norm_ffn'][l, 1][None], p['ffn_w_gate'], p['ffn_w_up'], p['ffn_w_down'], l, 1, final_g)
    return (x.reshape(B, L, D_MODEL), jnp.stack(new_pool), jnp.stack(new_k), jnp.stack(new_v),
            jnp.stack(new_shift), jnp.stack(new_wkv))


_MATMUL_WEIGHTS = ('ffn_w_gate', 'ffn_w_up', 'ffn_w_down', 'w_in_even', 'pool_w', 'w_out_even', 'rwkv_wr',
                   'rwkv_wk', 'rwkv_wv', 'rwkv_w1', 'rwkv_w2', 'rwkv_a1', 'rwkv_a2', 'rwkv_g1', 'rwkv_g2', 'rwkv_wo')


def _prepare(p):
    p = dict(p)
    for name in _MATMUL_WEIGHTS:
        p[name] = p[name].astype(BF16)
    onehot = (jnp.arange(D_MODEL)[:, None] // RWKV_HEAD == jnp.arange(LANES)[None, :]).astype(BF16)
    p['head_onehot'] = onehot
    p['head_onehot_t'] = onehot.T
    return p


def kernel(x_prompt, x_sample, cache_pool, cache_swa_k, cache_swa_v, state_shift, state_wkv, t5_table, norm_ffn, ffn_w_gate, ffn_w_up, ffn_w_down, norm_mix, w_in_even, pool_w, pool_scale, attn_sinks, w_out_even, rwkv_mu, rwkv_wr, rwkv_wk, rwkv_wv, rwkv_w0, rwkv_w1, rwkv_w2, rwkv_a0, rwkv_a1, rwkv_a2, rwkv_g1, rwkv_g2, rwkv_k_k, rwkv_k_a, rwkv_r_k, rwkv_lnx_w, rwkv_lnx_b, rwkv_wo, norm_final):
    p = _prepare(dict(
        t5_table=t5_table, norm_ffn=norm_ffn, ffn_w_gate=ffn_w_gate, ffn_w_up=ffn_w_up, ffn_w_down=ffn_w_down,
        norm_mix=norm_mix, w_in_even=w_in_even, pool_w=pool_w, pool_scale=pool_scale, attn_sinks=attn_sinks,
        w_out_even=w_out_even, rwkv_mu=rwkv_mu, rwkv_wr=rwkv_wr, rwkv_wk=rwkv_wk, rwkv_wv=rwkv_wv,
        rwkv_w0=rwkv_w0, rwkv_w1=rwkv_w1, rwkv_w2=rwkv_w2, rwkv_a0=rwkv_a0, rwkv_a1=rwkv_a1, rwkv_a2=rwkv_a2,
        rwkv_g1=rwkv_g1, rwkv_g2=rwkv_g2, rwkv_k_k=rwkv_k_k, rwkv_k_a=rwkv_k_a, rwkv_r_k=rwkv_r_k,
        rwkv_lnx_w=rwkv_lnx_w, rwkv_lnx_b=rwkv_lnx_b, rwkv_wo=rwkv_wo, norm_final=norm_final))
    y_p, pool_p, k_p, v_p, shift_p, wkv_p = _trunk(x_prompt, 0, (None, None, None, None, None), p)
    y_s, pool_s, k_s, v_s, shift_s, wkv_s = _trunk(
        x_sample, PAST_LEN, (cache_pool, cache_swa_k, cache_swa_v, state_shift, state_wkv), p)
    return (y_p, y_s, pool_p, pool_s, k_p, k_s, v_p, v_s, shift_p, shift_s, wkv_p, wkv_s)
```

```python
import functools
import math

import jax
import jax.numpy as jnp
from jax import lax
from jax.experimental import pallas as pl
from jax.experimental.pallas import tpu as pltpu

F32 = jnp.float32
BF16 = jnp.bfloat16

D_MODEL = 1024
D_FF = 2816
NORM_EPS = 1e-6
CHUNK = 64
POOL_WINDOWS = (2, 4, 8, 16)
C_POOL = 512
POOL_GC = 128
POOL_HIST = 15
POOL_HALO = 16
HEAD_DIM = 64
N_Q_HEADS = 8
N_KV_HEADS = 2
GQA_GROUP = 4
WINDOW = 128
SWA_ROWS = 128
Q_W = 512
KV_W = 128
IN_EVEN = C_POOL + Q_W + 2 * KV_W
N_BUCKETS = 32
MAX_DISTANCE = 128
RWKV_HEAD = 64
RWKV_H = 16
N_PAIRS = RWKV_H // 2
LANES = 128
LNX_EPS = 64e-5
PAST_LEN = 4096
VMEM_LIMIT_BYTES = 56 * 1024 * 1024


def _params(sem):
    return pltpu.CompilerParams(dimension_semantics=sem, vmem_limit_bytes=VMEM_LIMIT_BYTES)


def _dot(a, b):
    return jnp.dot(a, b, preferred_element_type=F32)


def _dot_nt(a, b):
    return lax.dot_general(a, b, (((1,), (1,)), ((), ())), preferred_element_type=F32)


def _dot_tn(a, b):
    return lax.dot_general(a, b, (((0,), (0,)), ((), ())), preferred_element_type=F32)


def _rms(x, g):
    return x * lax.rsqrt(jnp.mean(x * x, axis=-1, keepdims=True) + NORM_EPS) * g


def _split(x):
    hi = x.astype(BF16)
    lo = (x - hi.astype(F32)).astype(BF16)
    return hi, lo


def _head_sum(x, e_ref, et_ref):
    hi, lo = _split(x)
    s = _dot(hi, e_ref[...]) + _dot(lo, e_ref[...])
    shi, slo = _split(s)
    return _dot(shi, et_ref[...]) + _dot(slo, et_ref[...])


def _ffn_kernel(x_ref, g_ref, wg_ref, wu_ref, wd_ref, *rest, f_chunk, final):
    if final:
        gf_ref, o_ref, acc_ref = rest
    else:
        o_ref, acc_ref = rest
    x = x_ref[...]
    h = _rms(x, g_ref[...]).astype(BF16)
    for j in range(D_FF // f_chunk):
        sl = slice(j * f_chunk, (j + 1) * f_chunk)
        gate = _dot(h, wg_ref[:, sl])
        up = _dot(h, wu_ref[:, sl])
        act = (gate * jax.nn.sigmoid(gate) * up).astype(BF16)
        part = _dot(act, wd_ref[sl, :])
        if j == 0:
            acc_ref[...] = part
        else:
            acc_ref[...] += part
    y = x + 0.5 * acc_ref[...]
    if final:
        y = _rms(y, gf_ref[...])
    o_ref[...] = y


def _resident(shape, index_map):
    return pl.BlockSpec(shape, index_map, pipeline_mode=pl.Buffered(1))


def _ffn(x, g, wg, wu, wd, l, j, final_g=None):
    T = x.shape[0]
    tm = min(512, T)
    final = final_g is not None
    in_specs = [
        pl.BlockSpec((tm, D_MODEL), lambda i: (i, 0)),
        _resident((1, D_MODEL), lambda i: (0, 0)),
        _resident((None, None, D_MODEL, D_FF), lambda i: (l, j, 0, 0)),
        _resident((None, None, D_MODEL, D_FF), lambda i: (l, j, 0, 0)),
        _resident((None, None, D_FF, D_MODEL), lambda i: (l, j, 0, 0)),
    ]
    args = [x, g, wg, wu, wd]
    if final:
        in_specs.append(_resident((1, D_MODEL), lambda i: (0, 0)))
        args.append(final_g)
    return pl.pallas_call(
        functools.partial(_ffn_kernel, f_chunk=256, final=final),
        grid=(T // tm,),
        in_specs=in_specs,
        out_specs=pl.BlockSpec((tm, D_MODEL), lambda i: (i, 0)),
        out_shape=jax.ShapeDtypeStruct((T, D_MODEL), F32),
        scratch_shapes=[pltpu.VMEM((tm, D_MODEL), F32)],
        compiler_params=_params(("parallel",)),
    )(*args)


def _even_in_kernel(x_ref, g_ref, w_ref, u_ref, q_ref, k_ref, v_ref):
    h = _rms(x_ref[...], g_ref[...]).astype(BF16)
    z = _dot(h, w_ref[...])
    u_ref[...] = z[:, :C_POOL]
    q_ref[...] = z[:, C_POOL:C_POOL + Q_W].astype(BF16)
    k_ref[...] = z[:, C_POOL + Q_W:C_POOL + Q_W + KV_W]
    v_ref[...] = z[:, C_POOL + Q_W + KV_W:]


def _even_in(x, g, w_in, i):
    T = x.shape[0]
    tm = min(512, T)
    row = lambda n: pl.BlockSpec((tm, n), lambda t: (t, 0))
    return pl.pallas_call(
        _even_in_kernel,
        grid=(T // tm,),
        in_specs=[row(D_MODEL), _resident((1, D_MODEL), lambda t: (0, 0)),
                  _resident((None, D_MODEL, IN_EVEN), lambda t: (i, 0, 0))],
        out_specs=[row(C_POOL), row(Q_W), row(KV_W), row(KV_W)],
        out_shape=[jax.ShapeDtypeStruct((T, C_POOL), F32), jax.ShapeDtypeStruct((T, Q_W), BF16),
                   jax.ShapeDtypeStruct((T, KV_W), F32), jax.ShapeDtypeStruct((T, KV_W), F32)],
        compiler_params=_params(("parallel",)),
    )(x, g, w_in)


def _pool_kernel(u_ref, halo_ref, hist_ref, pw_ref, ps_ref, o_ref, *, tt, pos0):
    t = pl.program_id(1)
    halo = jnp.where(t == 0, hist_ref[0], halo_ref[...])
    u = u_ref[...]
    ext = jnp.concatenate([halo, u], axis=0)
    s2 = ext + pltpu.roll(ext, 1, 0)
    s4 = s2 + pltpu.roll(s2, 2, 0)
    s8 = s4 + pltpu.roll(s4, 4, 0)
    s16 = s8 + pltpu.roll(s8, 8, 0)
    pos = pos0 + t * tt + lax.broadcasted_iota(jnp.int32, (tt, 1), 0)
    outs = []
    for gi, (w, s) in enumerate(zip(POOL_WINDOWS, (s2, s4, s8, s16))):
        sl = slice(gi * POOL_GC, (gi + 1) * POOL_GC)
        cnt = jnp.minimum(w, pos + 1).astype(F32)
        pooled = s[POOL_HALO:, sl] / cnt - u[:, sl]
        outs.append(_dot(pooled.astype(BF16), pw_ref[gi]))
    o_ref[...] = (jnp.concatenate(outs, axis=-1) * ps_ref[...]).astype(BF16)


def _pool(u, hist, pool_w, pool_scale, B, L, pos0):
    tt = min(512, L)
    nt = L // tt
    hb = tt // POOL_HALO
    return pl.pallas_call(
        functools.partial(_pool_kernel, tt=tt, pos0=pos0),
        grid=(B, nt),
        in_specs=[
            pl.BlockSpec((tt, C_POOL), lambda b, t: (b * nt + t, 0)),
            pl.BlockSpec((POOL_HALO, C_POOL), lambda b, t: (jnp.maximum((b * nt + t) * hb - 1, 0), 0)),
            pl.BlockSpec((1, POOL_HALO, C_POOL), lambda b, t: (b, 0, 0)),
            _resident((len(POOL_WINDOWS), POOL_GC, POOL_GC), lambda b, t: (0, 0, 0)),
            _resident((1, C_POOL), lambda b, t: (0, 0)),
        ],
        out_specs=pl.BlockSpec((tt, C_POOL), lambda b, t: (b * nt + t, 0)),
        out_shape=jax.ShapeDtypeStruct((B * L, C_POOL), BF16),
        compiler_params=_params(("parallel", "arbitrary")),
    )(u, u, hist, pool_w, pool_scale)


def _build_bias(bkt_ref, tab_ref, bias_ref, lq):
    bkt = bkt_ref[...]
    for h in range(N_Q_HEADS):
        b = jnp.zeros(bkt.shape, F32)
        for n in range(N_BUCKETS):
            b = jnp.where(bkt == n, tab_ref[n, h], b)
        g, i = divmod(h, GQA_GROUP)
        bias_ref[g, i * lq:(i + 1) * lq] = b


def _group_queries(q, g):
    return jnp.concatenate([q[:, h * HEAD_DIM:(h + 1) * HEAD_DIM]
                            for h in range(g * GQA_GROUP, (g + 1) * GQA_GROUP)], axis=0)


def _sink_columns(sink_ref, lq):
    return [jnp.concatenate([jnp.full((lq, 1), sink_ref[g * GQA_GROUP + i], F32) for i in range(GQA_GROUP)], axis=0)
            for g in range(N_KV_HEADS)]


def _attn_core(qs, ks, vs, biases, sinks, valids):
    ss = [_dot_nt(q, k) * (HEAD_DIM ** -0.5) + b for q, k, b in zip(qs, ks, biases)]
    ss = [s if ok is None else jnp.where(ok, s, -1e30) for s, ok in zip(ss, valids)]
    ms = [jnp.maximum(jnp.max(s, axis=-1, keepdims=True), sk) for s, sk in zip(ss, sinks)]
    ps = [jnp.exp(s - m) for s, m in zip(ss, ms)]
    dens = [jnp.sum(p, axis=-1, keepdims=True) + jnp.exp(sk - m) for p, sk, m in zip(ps, sinks, ms)]
    return [_dot(p.astype(BF16), v) / d for p, v, d in zip(ps, vs, dens)]


def _attn_band_kernel(q_ref, kp_ref, ko_ref, vp_ref, vo_ref, bkt_ref, tab_ref, sink_ref, o_ref, bias_ref, *, cps):
    @pl.when(jnp.logical_and(pl.program_id(0) == 0, pl.program_id(1) == 0))
    def _():
        _build_bias(bkt_ref, tab_ref, bias_ref, CHUNK)

    lk = WINDOW + CHUNK
    k = jnp.concatenate([kp_ref[...], ko_ref[...]], axis=0).astype(BF16)
    v = jnp.concatenate([vp_ref[...], vo_ref[...]], axis=0).astype(BF16)
    q = q_ref[...]
    sink_cols = _sink_columns(sink_ref, CHUNK)
    first_pos = pl.program_id(1) * (cps * CHUNK) - WINDOW + lax.broadcasted_iota(jnp.int32, (1, lk), 1)
    qs, ks, vs, biases, sinks, valids = [], [], [], [], [], []
    for j in range(cps):
        qj = q[j * CHUNK:(j + 1) * CHUNK]
        for g in range(N_KV_HEADS):
            qs.append(_group_queries(qj, g))
            ks.append(k[j * CHUNK:j * CHUNK + lk, g * HEAD_DIM:(g + 1) * HEAD_DIM])
            vs.append(v[j * CHUNK:j * CHUNK + lk, g * HEAD_DIM:(g + 1) * HEAD_DIM])
            biases.append(bias_ref[g])
            sinks.append(sink_cols[g])
            valids.append(first_pos + j * CHUNK >= 0)
    outs = _attn_core(qs, ks, vs, biases, sinks, valids)
    for j in range(cps):
        heads = [outs[j * N_KV_HEADS + g][i * CHUNK:(i + 1) * CHUNK]
                 for g in range(N_KV_HEADS) for i in range(GQA_GROUP)]
        o_ref[j * CHUNK:(j + 1) * CHUNK, :] = jnp.concatenate(heads, axis=-1).astype(BF16)


def _attn_step_kernel(q_ref, k_ref, v_ref, bkt_ref, tab_ref, sink_ref, o_ref, bias_ref, *, lq):
    @pl.when(pl.program_id(0) == 0)
    def _():
        _build_bias(bkt_ref, tab_ref, bias_ref, lq)

    k = k_ref[0].astype(BF16)
    v = v_ref[0].astype(BF16)
    q = q_ref[...]
    groups = range(N_KV_HEADS)
    outs = _attn_core([_group_queries(q, g) for g in groups],
                      [k[:, g * HEAD_DIM:(g + 1) * HEAD_DIM] for g in groups],
                      [v[:, g * HEAD_DIM:(g + 1) * HEAD_DIM] for g in groups],
                      [bias_ref[g] for g in groups], _sink_columns(sink_ref, lq), [None] * N_KV_HEADS)
    heads = [outs[g][i * lq:(i + 1) * lq] for g in groups for i in range(GQA_GROUP)]
    o_ref[...] = jnp.concatenate(heads, axis=-1).astype(BF16)


def _smem():
    return pl.BlockSpec(memory_space=pltpu.SMEM)


def _attn_prompt(q, k, v, bkt, table, sinks, B, L):
    nc = L // CHUNK
    cps = 4 if nc % 4 == 0 else 2
    assert nc % cps == 0
    rows = cps * CHUNK
    ns = nc // cps
    lk = WINDOW + CHUNK
    own = lambda n: pl.BlockSpec((rows, n), lambda b, s: (b * ns + s, 0))
    prev = pl.BlockSpec((WINDOW, KV_W), lambda b, s: (jnp.maximum((b * ns + s) * (rows // WINDOW) - 1, 0), 0))
    return pl.pallas_call(
        functools.partial(_attn_band_kernel, cps=cps),
        grid=(B, ns),
        in_specs=[own(Q_W), prev, own(KV_W), prev, own(KV_W),
                  _resident((CHUNK, lk), lambda b, s: (0, 0)), _smem(), _smem()],
        out_specs=own(Q_W),
        out_shape=jax.ShapeDtypeStruct((B * L, Q_W), BF16),
        scratch_shapes=[pltpu.VMEM((N_KV_HEADS, GQA_GROUP * CHUNK, lk), F32)],
        compiler_params=_params(("arbitrary", "arbitrary")),
    )(q, k, k, v, v, bkt, table, sinks)


def _attn_step(q, k_all, v_all, bkt, table, sinks, B, L):
    lk = k_all.shape[1]
    return pl.pallas_call(
        functools.partial(_attn_step_kernel, lq=L),
        grid=(B,),
        in_specs=[pl.BlockSpec((L, Q_W), lambda b: (b, 0)),
                  pl.BlockSpec((1, lk, KV_W), lambda b: (b, 0, 0)),
                  pl.BlockSpec((1, lk, KV_W), lambda b: (b, 0, 0)),
                  _resident((L, lk), lambda b: (0, 0)), _smem(), _smem()],
        out_specs=pl.BlockSpec((L, Q_W), lambda b: (b, 0)),
        out_shape=jax.ShapeDtypeStruct((B * L, Q_W), BF16),
        scratch_shapes=[pltpu.VMEM((N_KV_HEADS, GQA_GROUP * L, lk), F32)],
        compiler_params=_params(("arbitrary",)),
    )(q, k_all, v_all, bkt, table, sinks)


def _even_out_kernel(x_ref, p_ref, a_ref, wp_ref, wa_ref, o_ref):
    o_ref[...] = x_ref[...] + _dot(p_ref[...], wp_ref[...]) + _dot(a_ref[...], wa_ref[...])


def _even_out(x, pool_out, att, w_out, i):
    T = x.shape[0]
    tm = min(512, T)
    row = lambda n: pl.BlockSpec((tm, n), lambda t: (t, 0))
    return pl.pallas_call(
        _even_out_kernel,
        grid=(T // tm,),
        in_specs=[row(D_MODEL), row(C_POOL), row(Q_W),
                  _resident((None, C_POOL, D_MODEL), lambda t: (i, 0, 0)),
                  _resident((None, Q_W, D_MODEL), lambda t: (i, 1, 0))],
        out_specs=row(D_MODEL),
        out_shape=jax.ShapeDtypeStruct((T, D_MODEL), F32),
        compiler_params=_params(("parallel",)),
    )(x, pool_out, att, w_out, w_out)


def _t5_bucket(rel):
    half = N_BUCKETS // 2
    max_exact = half // 2
    side = jnp.where(rel > 0, half, 0)
    n = jnp.abs(rel)
    nf = jnp.maximum(n, max_exact).astype(F32)
    large = max_exact + (jnp.log(nf / max_exact) / math.log(MAX_DISTANCE / max_exact)
                         * (half - max_exact)).astype(jnp.int32)
    large = jnp.minimum(large, half - 1)
    return side + jnp.where(n < max_exact, n, large)


def _bucket_index(lq, lk, offset):
    rel = jnp.arange(lk)[None, :] - offset - jnp.arange(lq)[:, None]
    return _t5_bucket(rel).astype(jnp.int32)


def _rwkv_pre_kernel(x_ref, xp_ref, sh_ref, gn_ref, mu_ref, wr_ref, wk_ref, wv_ref, w1_ref, w2_ref,
                     a1_ref, a2_ref, g1_ref, g2_ref, w0_ref, a0_ref, kk_ref, ka_ref, e_ref, et_ref,
                     r_o, lw_o, k_o, v_o, kk_o, b_o, g_o, hs_o, *, tm):
    t = pl.program_id(1)
    gn = gn_ref[...]
    h = _rms(x_ref[...], gn)
    prev_tile_last = _rms(xp_ref[...], gn)[7:8]
    first_prev = jnp.where(t == 0, sh_ref[0], prev_tile_last)
    row = lax.broadcasted_iota(jnp.int32, (tm, 1), 0)
    h_prev = jnp.where(row == 0, first_prev, pltpu.roll(h, 1, 0))
    xx = h_prev - h

    def mix(j):
        return (h + xx * mu_ref[j:j + 1]).astype(BF16)

    r = _dot(mix(0), wr_ref[...])
    k = _dot(mix(2), wk_ref[...])
    v = _dot(mix(3), wv_ref[...])
    zw = w0_ref[...] + _dot(jnp.tanh(_dot(mix(1), w1_ref[...])).astype(BF16), w2_ref[...])
    w = -(jnp.maximum(-zw, 0.0) + jnp.log(1.0 + jnp.exp(-jnp.abs(zw)))) - 0.5
    lw = -jnp.exp(w)
    a = jax.nn.sigmoid(a0_ref[...] + _dot(_dot(mix(4), a1_ref[...]).astype(BF16), a2_ref[...]))
    g = _dot(jax.nn.sigmoid(_dot(mix(5), g1_ref[...])).astype(BF16), g2_ref[...])
    kk = k * kk_ref[...]
    kk = kk * lax.rsqrt(jnp.maximum(_head_sum(kk * kk, e_ref, et_ref), 1e-24))
    k = k * (1.0 + (a - 1.0) * ka_ref[...])
    b = kk * a
    for hp in range(N_PAIRS):
        sl = slice(hp * LANES, (hp + 1) * LANES)
        r_o[0, hp] = r[:, sl]
        lw_o[0, hp] = lw[:, sl]
        k_o[0, hp] = k[:, sl]
        v_o[0, hp] = v[:, sl]
        kk_o[0, hp] = kk[:, sl]
        b_o[0, hp] = b[:, sl]
    g_o[...] = g
    hs_o[0] = h[tm - 8:]


def _rwkv_pre(x, shift_prev, gn, p, i, B, L):
    tm = min(512, L)
    nt = L // tm
    vec = lambda: _resident((1, D_MODEL), lambda b, t: (0, 0))
    mat = lambda r, c: _resident((None, r, c), lambda b, t: (i, 0, 0))
    lora = p['rwkv_w1'].shape[-1], p['rwkv_a1'].shape[-1], p['rwkv_g1'].shape[-1]
    head_spec = pl.BlockSpec((1, N_PAIRS, tm, LANES), lambda b, t: (b, 0, t, 0))
    head_shape = jax.ShapeDtypeStruct((B, N_PAIRS, L, LANES), F32)
    return pl.pallas_call(
        functools.partial(_rwkv_pre_kernel, tm=tm),
        grid=(B, nt),
        in_specs=[
            pl.BlockSpec((tm, D_MODEL), lambda b, t: (b * nt + t, 0)),
            pl.BlockSpec((8, D_MODEL), lambda b, t: (jnp.maximum((b * nt + t) * (tm // 8) - 1, 0), 0)),
            pl.BlockSpec((1, 1, D_MODEL), lambda b, t: (b, 0, 0)),
            vec(),
            _resident((None, 6, D_MODEL), lambda b, t: (i, 0, 0)),
            mat(D_MODEL, D_MODEL), mat(D_MODEL, D_MODEL), mat(D_MODEL, D_MODEL),
            mat(D_MODEL, lora[0]), mat(lora[0], D_MODEL),
            mat(D_MODEL, lora[1]), mat(lora[1], D_MODEL),
            mat(D_MODEL, lora[2]), mat(lora[2], D_MODEL),
            vec(), vec(), vec(), vec(),
            _resident((D_MODEL, LANES), lambda b, t: (0, 0)),
            _resident((LANES, D_MODEL), lambda b, t: (0, 0)),
        ],
        out_specs=[head_spec] * 6 + [
            pl.BlockSpec((tm, D_MODEL), lambda b, t: (b * nt + t, 0)),
            pl.BlockSpec((1, 8, D_MODEL), lambda b, t: (b, 0, 0)),
        ],
        out_shape=[head_shape] * 6 + [
            jax.ShapeDtypeStruct((B * L, D_MODEL), F32),
            jax.ShapeDtypeStruct((B, 8, D_MODEL), F32),
        ],
        compiler_params=_params(("parallel", "arbitrary")),
    )(x, x, shift_prev, gn, p['rwkv_mu'], p['rwkv_wr'], p['rwkv_wk'], p['rwkv_wv'],
      p['rwkv_w1'], p['rwkv_w2'], p['rwkv_a1'], p['rwkv_a2'], p['rwkv_g1'], p['rwkv_g2'],
      p['rwkv_w0'][i][None], p['rwkv_a0'][i][None], p['rwkv_k_k'][i][None], p['rwkv_k_a'][i][None],
      p['head_onehot'], p['head_onehot_t'])


DIAG = 8


def _replication_matrix(n):
    src = jnp.arange(n)[:, None]
    dst = jnp.arange(n)[None, :]
    return jnp.concatenate([(src == (dst // DIAG) * DIAG + s) for s in range(DIAG - 1)], axis=1).astype(BF16)


def _unit_lower_inverses(lows, c, rep_ref):
    n = 2 * c
    ti = lax.broadcasted_iota(jnp.int32, (c, n), 0)
    li = lax.broadcasted_iota(jnp.int32, (c, n), 1)
    si = li & (c - 1)
    head1 = li >= c

    def block_diag(x):
        return jnp.concatenate([jnp.where(head1, 0.0, x), jnp.where(head1, x, 0.0)], axis=0)

    if rep_ref is None:
        invs = [jnp.where(ti == si, 1.0, 0.0) + jnp.where(jnp.logical_and(ti == si + 1, (ti & 1) == 1), low, 0.0)
                for low in lows]
        m = 2
    else:
        pt = lax.broadcasted_iota(jnp.int32, (DIAG, n), 0)
        pl_ = lax.broadcasted_iota(jnp.int32, (DIAG, n), 1)
        blk = (pl_ & (c - 1)) // DIAG
        packed = []
        for low in lows:
            d = jnp.zeros((DIAG, n), F32)
            for i in range(c // DIAG):
                d = jnp.where(blk == i, low[i * DIAG:(i + 1) * DIAG], d)
            packed.append(d)
        rep = _dot(jnp.concatenate(packed, axis=0).astype(BF16), rep_ref[...])
        sols = [jnp.where(pt == (pl_ & (DIAG - 1)), 1.0, 0.0) for _ in lows]
        for s in range(DIAG - 1):
            sols = [sol + rep[p * DIAG:(p + 1) * DIAG, s * n:(s + 1) * n] * sol[s:s + 1]
                    for p, sol in enumerate(sols)]
        invs = [jnp.concatenate([jnp.where(blk == i, sol, 0.0) for i in range(c // DIAG)], axis=0) for sol in sols]
        m = DIAG
    while m < c:
        sh = m.bit_length() - 1
        sub = jnp.logical_and(jnp.logical_and((ti >> (sh + 1)) == (si >> (sh + 1)), ((ti >> sh) & 1) == 1),
                              ((si >> sh) & 1) == 0)
        diag = [block_diag(inv).astype(BF16) for inv in invs]
        half = [_dot(inv.astype(BF16), block_diag(jnp.where(sub, low, 0.0)).astype(BF16)).astype(BF16)
                for inv, low in zip(invs, lows)]
        invs = [inv + _dot(h, d) for inv, h, d in zip(invs, half, diag)]
        m *= 2
    return invs


def _scan_chunks(seqs, states, c, rep_ref):
    n = 2 * c
    row = lax.broadcasted_iota(jnp.int32, (c, LANES), 0)
    head1 = lax.broadcasted_iota(jnp.int32, (c, LANES), 1) >= RWKV_HEAD
    ti = lax.broadcasted_iota(jnp.int32, (c, n), 0)
    si = lax.broadcasted_iota(jnp.int32, (c, n), 1) & (c - 1)
    strict = ti > si
    incl = ti >= si
    pi = lax.broadcasted_iota(jnp.int32, (LANES, LANES), 0)
    pj = lax.broadcasted_iota(jnp.int32, (LANES, LANES), 1)
    same_head = (pi >= RWKV_HEAD) == (pj >= RWKV_HEAD)

    def stack(x):
        return jnp.concatenate([jnp.where(head1, 0.0, x), jnp.where(head1, x, 0.0)], axis=0)

    lhs, rhs, v_st, v_bf, bk_end, decay = [], [], [], [], [], []
    for r, lw, k, v, kk, b in seqs:
        cum = lw
        sh = 1
        while sh < c:
            cum = cum + jnp.where(row >= sh, pltpu.roll(cum, sh, 0), 0.0)
            sh *= 2
        tot = cum[c - 1:c]
        grow = jnp.exp(-cum)
        tail = jnp.exp(tot - cum)
        a_t = -kk * jnp.exp(cum - lw)
        r_t = r * jnp.exp(cum)
        lhs.append(jnp.concatenate([a_t, r_t], axis=0).astype(BF16))
        rhs.append(jnp.concatenate([stack(b * grow), stack(k * grow)], axis=0).astype(BF16))
        v_st.append(stack(v).astype(BF16))
        v_bf.append(v.astype(BF16))
        bk_end.append(jnp.concatenate([b * tail, k * tail], axis=0).astype(BF16))
        decay.append(jnp.exp(tot))
    grams = [_dot_nt(x, y) for x, y in zip(lhs, rhs)]
    lows = [jnp.where(strict, g[:c, :n], 0.0) for g in grams]
    m_ak = [jnp.where(strict, g[:c, n:], 0.0).astype(BF16) for g in grams]
    m_r = [jnp.concatenate([jnp.where(incl, g[c:, :n], 0.0), jnp.where(incl, g[c:, n:], 0.0)], axis=1).astype(BF16)
           for g in grams]
    invs = [inv.astype(BF16) for inv in _unit_lower_inverses(lows, c, rep_ref)]
    from_v = [_dot(m, vs) for m, vs in zip(m_ak, v_st)]

    ys = []
    states = list(states)
    npair = len(states)
    for j in range(len(seqs) // npair):
        sl = slice(j * npair, (j + 1) * npair)
        from_state = [_dot_nt(x, s.astype(BF16)) for x, s in zip(lhs[sl], states)]
        rhs_sa = [stack(f[:c] + fv).astype(BF16) for f, fv in zip(from_state, from_v[sl])]
        sas = [_dot(inv, x) for inv, x in zip(invs[sl], rhs_sa)]
        ys += [f[c:] + _dot(m, jnp.concatenate([stack(sa).astype(BF16), vs], axis=0))
               for f, m, sa, vs in zip(from_state, m_r[sl], sas, v_st[sl])]
        upd = [_dot_tn(jnp.concatenate([sa.astype(BF16), vb], axis=0), be)
               for sa, vb, be in zip(sas, v_bf[sl], bk_end[sl])]
        states = [s * d + jnp.where(same_head, u, 0.0) for s, d, u in zip(states, decay[sl], upd)]
    return ys, states


def _rwkv_scan_kernel(r_ref, lw_ref, k_ref, v_ref, kk_ref, b_ref, s0_ref, *rest, c, sub, vpu_diag):
    if vpu_diag:
        rep_ref, y_ref, so_ref, s_ref = rest
    else:
        rep_ref = None
        y_ref, so_ref, s_ref = rest
    ci = pl.program_id(1)

    @pl.when(ci == 0)
    def _():
        s_ref[...] = s0_ref[0]

    seqs = [tuple(ref[0, hp, j * c:(j + 1) * c] for ref in (r_ref, lw_ref, k_ref, v_ref, kk_ref, b_ref))
            for j in range(sub) for hp in range(N_PAIRS)]
    ys, new_states = _scan_chunks(seqs, [s_ref[hp] for hp in range(N_PAIRS)], c, rep_ref)
    for j in range(sub):
        for hp in range(N_PAIRS):
            y_ref[0, hp, j * c:(j + 1) * c] = ys[j * N_PAIRS + hp]
    for hp in range(N_PAIRS):
        s_ref[hp] = new_states[hp]

    @pl.when(ci == pl.num_programs(1) - 1)
    def _():
        so_ref[0] = s_ref[...]


def _rwkv_scan(r, lw, k, v, kk, b, s0, B, L):
    c = min(CHUNK, L)
    nc = L // c
    sub = 2 if nc % 2 == 0 else 1
    nc //= sub
    seq = pl.BlockSpec((1, N_PAIRS, sub * c, LANES), lambda bi, ci: (bi, 0, ci, 0))
    st = pl.BlockSpec((1, N_PAIRS, LANES, LANES), lambda bi, ci: (bi, 0, 0, 0))
    vpu_diag = 2 * c == LANES
    extra_specs, extra_args = [], []
    if vpu_diag:
        extra_specs = [_resident((LANES, (DIAG - 1) * LANES), lambda bi, ci: (0, 0))]
        extra_args = [_replication_matrix(LANES)]
    return pl.pallas_call(
        functools.partial(_rwkv_scan_kernel, c=c, sub=sub, vpu_diag=vpu_diag),
        grid=(B, nc),
        in_specs=[seq] * 6 + [st] + extra_specs,
        out_specs=[seq, st],
        out_shape=[jax.ShapeDtypeStruct((B, N_PAIRS, L, LANES), F32),
                   jax.ShapeDtypeStruct((B, N_PAIRS, LANES, LANES), F32)],
        scratch_shapes=[pltpu.VMEM((N_PAIRS, LANES, LANES), F32)],
        compiler_params=_params(("parallel", "arbitrary")),
    )(r, lw, k, v, kk, b, s0, *extra_args)


def _rwkv_post_kernel(x_ref, y_ref, r_ref, k_ref, v_ref, g_ref, rk_ref, lw_ref, lb_ref, e_ref, et_ref,
                      wo_ref, o_ref):
    cat = lambda ref: jnp.concatenate([ref[0, hp] for hp in range(N_PAIRS)], axis=-1)
    y, r, k, v = cat(y_ref), cat(r_ref), cat(k_ref), cat(v_ref)
    mean = _head_sum(y, e_ref, et_ref) * (1.0 / RWKV_HEAD)
    d = y - mean
    var = _head_sum(d * d, e_ref, et_ref) * (1.0 / RWKV_HEAD)
    yn = d * lax.rsqrt(var + LNX_EPS) * lw_ref[...] + lb_ref[...]
    bonus = _head_sum(r * k * rk_ref[...], e_ref, et_ref) * v
    o_ref[...] = x_ref[...] + _dot(((yn + bonus) * g_ref[...]).astype(BF16), wo_ref[...])


def _rwkv_post(x, y, r, k, v, g, p, i, B, L):
    tm = min(512, L)
    nt = L // tm
    vec = lambda: _resident((1, D_MODEL), lambda b, t: (0, 0))
    head_spec = pl.BlockSpec((1, N_PAIRS, tm, LANES), lambda b, t: (b, 0, t, 0))
    row = pl.BlockSpec((tm, D_MODEL), lambda b, t: (b * nt + t, 0))
    return pl.pallas_call(
        _rwkv_post_kernel,
        grid=(B, nt),
        in_specs=[row, head_spec, head_spec, head_spec, head_spec, row, vec(), vec(), vec(),
                  _resident((D_MODEL, LANES), lambda b, t: (0, 0)),
                  _resident((LANES, D_MODEL), lambda b, t: (0, 0)),
                  _resident((None, D_MODEL, D_MODEL), lambda b, t: (i, 0, 0))],
        out_specs=row,
        out_shape=jax.ShapeDtypeStruct((B * L, D_MODEL), F32),
        compiler_params=_params(("parallel", "parallel")),
    )(x, y, r, k, v, g, p['rwkv_r_k'][i].reshape(1, D_MODEL), p['rwkv_lnx_w'][i][None], p['rwkv_lnx_b'][i][None],
      p['head_onehot'], p['head_onehot_t'], p['rwkv_wo'])


def _pack_state(s):
    B = s.shape[0]
    s = s.reshape(B, N_PAIRS, 2, RWKV_HEAD, RWKV_HEAD)
    z = jnp.zeros_like(s[:, :, 0])
    top = jnp.concatenate([s[:, :, 0], z], axis=-1)
    bot = jnp.concatenate([z, s[:, :, 1]], axis=-1)
    return jnp.concatenate([top, bot], axis=-2)


def _unpack_state(s):
    B = s.shape[0]
    return jnp.stack([s[:, :, :RWKV_HEAD, :RWKV_HEAD], s[:, :, RWKV_HEAD:, RWKV_HEAD:]], axis=2).reshape(
        B, RWKV_H, RWKV_HEAD, RWKV_HEAD)


def _trunk(x, pos0, caches, p):
    pool_c, k_c, v_c, shift_c, wkv_c = caches
    B, L, _ = x.shape
    depth = p['norm_mix'].shape[0]
    stepping = k_c is not None
    x = x.reshape(B * L, D_MODEL)
    if stepping:
        bkt = _bucket_index(L, SWA_ROWS + L, SWA_ROWS)
    else:
        bkt = _bucket_index(CHUNK, WINDOW + CHUNK, WINDOW)
    new_pool, new_k, new_v, new_shift, new_wkv = [], [], [], [], []
    for l in range(depth):
        i = l // 2
        x = _ffn(x, p['norm_ffn'][l, 0][None], p['ffn_w_gate'], p['ffn_w_up'], p['ffn_w_down'], l, 0)
        gn = p['norm_mix'][l][None]
        if l % 2 == 0:
            u, q, k, v = _even_in(x, gn, p['w_in_even'], i)
            if stepping:
                hist = jnp.pad(pool_c[i], ((0, 0), (POOL_HALO - POOL_HIST, 0), (0, 0)))
            else:
                hist = jnp.zeros((B, POOL_HALO, C_POOL), F32)
            pool_out = _pool(u, hist, p['pool_w'][i], p['pool_scale'][i][None], B, L, pos0)
            k3, v3 = k.reshape(B, L, KV_W), v.reshape(B, L, KV_W)
            if stepping:
                k3 = jnp.concatenate([k_c[i].reshape(B, SWA_ROWS, KV_W), k3], axis=1)
                v3 = jnp.concatenate([v_c[i].reshape(B, SWA_ROWS, KV_W), v3], axis=1)
                att = _attn_step(q, k3, v3, bkt, p['t5_table'], p['attn_sinks'][i], B, L)
            else:
                att = _attn_prompt(q, k, v, bkt, p['t5_table'], p['attn_sinks'][i], B, L)
            x = _even_out(x, pool_out, att, p['w_out_even'], i)
            full = jnp.concatenate([hist, u.reshape(B, L, C_POOL)], axis=1)
            new_pool.append(full[:, -POOL_HIST:])
            new_k.append(k3[:, -SWA_ROWS:].reshape(B, SWA_ROWS, N_KV_HEADS, HEAD_DIM))
            new_v.append(v3[:, -SWA_ROWS:].reshape(B, SWA_ROWS, N_KV_HEADS, HEAD_DIM))
        else:
            if stepping:
                shift_prev = shift_c[i][:, None, :]
                s0 = _pack_state(wkv_c[i])
            else:
                shift_prev = jnp.zeros((B, 1, D_MODEL), F32)
                s0 = jnp.zeros((B, N_PAIRS, LANES, LANES), F32)
            r, lw, k, v, kk, b, g, hs = _rwkv_pre(x, shift_prev, gn, p, i, B, L)
            y, s_new = _rwkv_scan(r, lw, k, v, kk, b, s0, B, L)
            x = _rwkv_post(x, y, r, k, v, g, p, i, B, L)
            new_shift.append(hs[:, 7])
            new_wkv.append(_unpack_state(s_new))
        final_g = p['norm_final'][None] if l == depth - 1 else None
        x = _ffn(x, p['norm_ffn'][l, 1][None], p['ffn_w_gate'], p['ffn_w_up'], p['ffn_w_down'], l, 1, final_g)
    return (x.reshape(B, L, D_MODEL), jnp.stack(new_pool), jnp.stack(new_k), jnp.stack(new_v),
            jnp.stack(new_shift), jnp.stack(new_wkv))


_MATMUL_WEIGHTS = ('ffn_w_gate', 'ffn_w_up', 'ffn_w_down', 'w_in_even', 'pool_w', 'w_out_even', 'rwkv_wr',
                   'rwkv_wk', 'rwkv_wv', 'rwkv_w1', 'rwkv_w2', 'rwkv_a1', 'rwkv_a2', 'rwkv_g1', 'rwkv_g2', 'rwkv_wo')


def _prepare(p):
    p = dict(p)
    for name in _MATMUL_WEIGHTS:
        p[name] = p[name].astype(BF16)
    onehot = (jnp.arange(D_MODEL)[:, None] // RWKV_HEAD == jnp.arange(LANES)[None, :]).astype(BF16)
    p['head_onehot'] = onehot
    p['head_onehot_t'] = onehot.T
    return p


def kernel(x_prompt, x_sample, cache_pool, cache_swa_k, cache_swa_v, state_shift, state_wkv, t5_table, norm_ffn, ffn_w_gate, ffn_w_up, ffn_w_down, norm_mix, w_in_even, pool_w, pool_scale, attn_sinks, w_out_even, rwkv_mu, rwkv_wr, rwkv_wk, rwkv_wv, rwkv_w0, rwkv_w1, rwkv_w2, rwkv_a0, rwkv_a1, rwkv_a2, rwkv_g1, rwkv_g2, rwkv_k_k, rwkv_k_a, rwkv_r_k, rwkv_lnx_w, rwkv_lnx_b, rwkv_wo, norm_final):
    p = _prepare(dict(
        t5_table=t5_table, norm_ffn=norm_ffn, ffn_w_gate=ffn_w_gate, ffn_w_up=ffn_w_up, ffn_w_down=ffn_w_down,
        norm_mix=norm_mix, w_in_even=w_in_even, pool_w=pool_w, pool_scale=pool_scale, attn_sinks=attn_sinks,
        w_out_even=w_out_even, rwkv_mu=rwkv_mu, rwkv_wr=rwkv_wr, rwkv_wk=rwkv_wk, rwkv_wv=rwkv_wv,
        rwkv_w0=rwkv_w0, rwkv_w1=rwkv_w1, rwkv_w2=rwkv_w2, rwkv_a0=rwkv_a0, rwkv_a1=rwkv_a1, rwkv_a2=rwkv_a2,
        rwkv_g1=rwkv_g1, rwkv_g2=rwkv_g2, rwkv_k_k=rwkv_k_k, rwkv_k_a=rwkv_k_a, rwkv_r_k=rwkv_r_k,
        rwkv_lnx_w=rwkv_lnx_w, rwkv_lnx_b=rwkv_lnx_b, rwkv_wo=rwkv_wo, norm_final=norm_final))
    y_p, pool_p, k_p, v_p, shift_p, wkv_p = _trunk(x_prompt, 0, (None, None, None, None, None), p)
    y_s, pool_s, k_s, v_s, shift_s, wkv_s = _trunk(
        x_sample, PAST_LEN, (cache_pool, cache_swa_k, cache_swa_v, state_shift, state_wkv), p)
    return (y_p, y_s, pool_p, pool_s, k_p, k_s, v_p, v_s, shift_p, shift_s, wkv_p, wkv_s)
```

```python
import functools
import math

import jax
import jax.numpy as jnp
from jax import lax
from jax.experimental import pallas as pl
from jax.experimental.pallas import tpu as pltpu

F32 = jnp.float32
BF16 = jnp.bfloat16

D_MODEL = 1024
D_FF = 2816
NORM_EPS = 1e-6
CHUNK = 64
POOL_WINDOWS = (2, 4, 8, 16)
C_POOL = 512
POOL_GC = 128
POOL_HIST = 15
POOL_HALO = 16
HEAD_DIM = 64
N_Q_HEADS = 8
N_KV_HEADS = 2
GQA_GROUP = 4
WINDOW = 128
SWA_ROWS = 128
Q_W = 512
KV_W = 128
IN_EVEN = C_POOL + Q_W + 2 * KV_W
N_BUCKETS = 32
MAX_DISTANCE = 128
RWKV_HEAD = 64
RWKV_H = 16
N_PAIRS = RWKV_H // 2
LANES = 128
LNX_EPS = 64e-5
PAST_LEN = 4096
VMEM_LIMIT_BYTES = 56 * 1024 * 1024


def _params(sem):
    return pltpu.CompilerParams(dimension_semantics=sem, vmem_limit_bytes=VMEM_LIMIT_BYTES)


def _dot(a, b):
    return jnp.dot(a, b, preferred_element_type=F32)


def _dot_nt(a, b):
    return lax.dot_general(a, b, (((1,), (1,)), ((), ())), preferred_element_type=F32)


def _dot_tn(a, b):
    return lax.dot_general(a, b, (((0,), (0,)), ((), ())), preferred_element_type=F32)


def _rms(x, g):
    return x * lax.rsqrt(jnp.mean(x * x, axis=-1, keepdims=True) + NORM_EPS) * g


def _split(x):
    hi = x.astype(BF16)
    lo = (x - hi.astype(F32)).astype(BF16)
    return hi, lo


def _head_sum(x, e_ref, et_ref):
    hi, lo = _split(x)
    s = _dot(hi, e_ref[...]) + _dot(lo, e_ref[...])
    shi, slo = _split(s)
    return _dot(shi, et_ref[...]) + _dot(slo, et_ref[...])


def _ffn_kernel(x_ref, g_ref, wg_ref, wu_ref, wd_ref, *rest, f_chunk, final):
    if final:
        gf_ref, o_ref, acc_ref = rest
    else:
        o_ref, acc_ref = rest
    x = x_ref[...]
    h = _rms(x, g_ref[...]).astype(BF16)
    for j in range(D_FF // f_chunk):
        sl = slice(j * f_chunk, (j + 1) * f_chunk)
        gate = _dot(h, wg_ref[:, sl])
        up = _dot(h, wu_ref[:, sl])
        act = (gate * jax.nn.sigmoid(gate) * up).astype(BF16)
        part = _dot(act, wd_ref[sl, :])
        if j == 0:
            acc_ref[...] = part
        else:
            acc_ref[...] += part
    y = x + 0.5 * acc_ref[...]
    if final:
        y = _rms(y, gf_ref[...])
    o_ref[...] = y


def _resident(shape, index_map):
    return pl.BlockSpec(shape, index_map, pipeline_mode=pl.Buffered(1))


def _ffn(x, g, wg, wu, wd, l, j, final_g=None):
    T = x.shape[0]
    tm = min(512, T)
    final = final_g is not None
    in_specs = [
        pl.BlockSpec((tm, D_MODEL), lambda i: (i, 0)),
        _resident((1, D_MODEL), lambda i: (0, 0)),
        _resident((None, None, D_MODEL, D_FF), lambda i: (l, j, 0, 0)),
        _resident((None, None, D_MODEL, D_FF), lambda i: (l, j, 0, 0)),
        _resident((None, None, D_FF, D_MODEL), lambda i: (l, j, 0, 0)),
    ]
    args = [x, g, wg, wu, wd]
    if final:
        in_specs.append(_resident((1, D_MODEL), lambda i: (0, 0)))
        args.append(final_g)
    return pl.pallas_call(
        functools.partial(_ffn_kernel, f_chunk=256, final=final),
        grid=(T // tm,),
        in_specs=in_specs,
        out_specs=pl.BlockSpec((tm, D_MODEL), lambda i: (i, 0)),
        out_shape=jax.ShapeDtypeStruct((T, D_MODEL), F32),
        scratch_shapes=[pltpu.VMEM((tm, D_MODEL), F32)],
        compiler_params=_params(("parallel",)),
    )(*args)


def _even_in_kernel(x_ref, g_ref, w_ref, u_ref, q_ref, k_ref, v_ref):
    h = _rms(x_ref[...], g_ref[...]).astype(BF16)
    z = _dot(h, w_ref[...])
    u_ref[...] = z[:, :C_POOL]
    q_ref[...] = z[:, C_POOL:C_POOL + Q_W].astype(BF16)
    k_ref[...] = z[:, C_POOL + Q_W:C_POOL + Q_W + KV_W]
    v_ref[...] = z[:, C_POOL + Q_W + KV_W:]


def _even_in(x, g, w_in, i):
    T = x.shape[0]
    tm = min(512, T)
    row = lambda n: pl.BlockSpec((tm, n), lambda t: (t, 0))
    return pl.pallas_call(
        _even_in_kernel,
        grid=(T // tm,),
        in_specs=[row(D_MODEL), _resident((1, D_MODEL), lambda t: (0, 0)),
                  _resident((None, D_MODEL, IN_EVEN), lambda t: (i, 0, 0))],
        out_specs=[row(C_POOL), row(Q_W), row(KV_W), row(KV_W)],
        out_shape=[jax.ShapeDtypeStruct((T, C_POOL), F32), jax.ShapeDtypeStruct((T, Q_W), BF16),
                   jax.ShapeDtypeStruct((T, KV_W), F32), jax.ShapeDtypeStruct((T, KV_W), F32)],
        compiler_params=_params(("parallel",)),
    )(x, g, w_in)


def _pool_kernel(u_ref, halo_ref, hist_ref, pw_ref, ps_ref, o_ref, *, tt, pos0):
    t = pl.program_id(1)
    halo = jnp.where(t == 0, hist_ref[0], halo_ref[...])
    u = u_ref[...]
    ext = jnp.concatenate([halo, u], axis=0)
    s2 = ext + pltpu.roll(ext, 1, 0)
    s4 = s2 + pltpu.roll(s2, 2, 0)
    s8 = s4 + pltpu.roll(s4, 4, 0)
    s16 = s8 + pltpu.roll(s8, 8, 0)
    pos = pos0 + t * tt + lax.broadcasted_iota(jnp.int32, (tt, 1), 0)
    outs = []
    for gi, (w, s) in enumerate(zip(POOL_WINDOWS, (s2, s4, s8, s16))):
        sl = slice(gi * POOL_GC, (gi + 1) * POOL_GC)
        cnt = jnp.minimum(w, pos + 1).astype(F32)
        pooled = s[POOL_HALO:, sl] / cnt - u[:, sl]
        outs.append(_dot(pooled.astype(BF16), pw_ref[gi]))
    o_ref[...] = (jnp.concatenate(outs, axis=-1) * ps_ref[...]).astype(BF16)


def _pool(u, hist, pool_w, pool_scale, B, L, pos0):
    tt = min(512, L)
    nt = L // tt
    hb = tt // POOL_HALO
    return pl.pallas_call(
        functools.partial(_pool_kernel, tt=tt, pos0=pos0),
        grid=(B, nt),
        in_specs=[
            pl.BlockSpec((tt, C_POOL), lambda b, t: (b * nt + t, 0)),
            pl.BlockSpec((POOL_HALO, C_POOL), lambda b, t: (jnp.maximum((b * nt + t) * hb - 1, 0), 0)),
            pl.BlockSpec((1, POOL_HALO, C_POOL), lambda b, t: (b, 0, 0)),
            _resident((len(POOL_WINDOWS), POOL_GC, POOL_GC), lambda b, t: (0, 0, 0)),
            _resident((1, C_POOL), lambda b, t: (0, 0)),
        ],
        out_specs=pl.BlockSpec((tt, C_POOL), lambda b, t: (b * nt + t, 0)),
        out_shape=jax.ShapeDtypeStruct((B * L, C_POOL), BF16),
        compiler_params=_params(("parallel", "arbitrary")),
    )(u, u, hist, pool_w, pool_scale)


def _build_bias(bkt_ref, tab_ref, bias_ref, lq):
    bkt = bkt_ref[...]
    for h in range(N_Q_HEADS):
        b = jnp.zeros(bkt.shape, F32)
        for n in range(N_BUCKETS):
            b = jnp.where(bkt == n, tab_ref[n, h], b)
        g, i = divmod(h, GQA_GROUP)
        bias_ref[g, :, i * lq:(i + 1) * lq] = b


def _group_queries(q, g):
    return jnp.concatenate([q[:, h * HEAD_DIM:(h + 1) * HEAD_DIM]
                            for h in range(g * GQA_GROUP, (g + 1) * GQA_GROUP)], axis=0) * (HEAD_DIM ** -0.5)


def _sink_rows(sink_ref, lq):
    lane = lax.broadcasted_iota(jnp.int32, (1, GQA_GROUP * lq), 1)
    rows = []
    for g in range(N_KV_HEADS):
        r = jnp.zeros((1, GQA_GROUP * lq), F32)
        for i in range(GQA_GROUP):
            r = jnp.where(lane // lq == i, sink_ref[g * GQA_GROUP + i], r)
        rows.append(r)
    return rows


def _attn_core(qs, ks, vs, biases, sinks, valids):
    ss = [_dot_nt(k, q) + b for q, k, b in zip(qs, ks, biases)]
    ss = [s if ok is None else jnp.where(ok, s, -1e30) for s, ok in zip(ss, valids)]
    ms = [jnp.maximum(jnp.max(s, axis=0, keepdims=True), sk) for s, sk in zip(ss, sinks)]
    ps = [jnp.exp(s - m) for s, m in zip(ss, ms)]
    invs = [1.0 / (jnp.sum(p, axis=0, keepdims=True) + jnp.exp(sk - m)) for p, sk, m in zip(ps, sinks, ms)]
    return [_dot_tn((p * r).astype(BF16), v) for p, r, v in zip(ps, invs, vs)]


def _attn_band_kernel(q_ref, kp_ref, ko_ref, vp_ref, vo_ref, bkt_ref, tab_ref, sink_ref, o_ref, bias_ref, *, cps):
    @pl.when(jnp.logical_and(pl.program_id(0) == 0, pl.program_id(1) == 0))
    def _():
        _build_bias(bkt_ref, tab_ref, bias_ref, CHUNK)

    lk = WINDOW + CHUNK
    k = jnp.concatenate([kp_ref[...], ko_ref[...]], axis=0).astype(BF16)
    v = jnp.concatenate([vp_ref[...], vo_ref[...]], axis=0).astype(BF16)
    q = q_ref[...]
    sink_cols = _sink_rows(sink_ref, CHUNK)
    first_pos = pl.program_id(1) * (cps * CHUNK) - WINDOW + lax.broadcasted_iota(jnp.int32, (lk, 1), 0)
    qs, ks, vs, biases, sinks, valids = [], [], [], [], [], []
    for j in range(cps):
        qj = q[j * CHUNK:(j + 1) * CHUNK]
        for g in range(N_KV_HEADS):
            qs.append(_group_queries(qj, g))
            ks.append(k[j * CHUNK:j * CHUNK + lk, g * HEAD_DIM:(g + 1) * HEAD_DIM])
            vs.append(v[j * CHUNK:j * CHUNK + lk, g * HEAD_DIM:(g + 1) * HEAD_DIM])
            biases.append(bias_ref[g])
            sinks.append(sink_cols[g])
            valids.append(first_pos + j * CHUNK >= 0)
    outs = _attn_core(qs, ks, vs, biases, sinks, valids)
    for j in range(cps):
        heads = [outs[j * N_KV_HEADS + g][i * CHUNK:(i + 1) * CHUNK]
                 for g in range(N_KV_HEADS) for i in range(GQA_GROUP)]
        o_ref[j * CHUNK:(j + 1) * CHUNK, :] = jnp.concatenate(heads, axis=-1).astype(BF16)


def _attn_step_kernel(q_ref, k_ref, v_ref, bkt_ref, tab_ref, sink_ref, o_ref, bias_ref, *, lq):
    @pl.when(pl.program_id(0) == 0)
    def _():
        _build_bias(bkt_ref, tab_ref, bias_ref, lq)

    k = k_ref[0].astype(BF16)
    v = v_ref[0].astype(BF16)
    q = q_ref[...]
    groups = range(N_KV_HEADS)
    outs = _attn_core([_group_queries(q, g) for g in groups],
                      [k[:, g * HEAD_DIM:(g + 1) * HEAD_DIM] for g in groups],
                      [v[:, g * HEAD_DIM:(g + 1) * HEAD_DIM] for g in groups],
                      [bias_ref[g] for g in groups], _sink_rows(sink_ref, lq), [None] * N_KV_HEADS)
    heads = [outs[g][i * lq:(i + 1) * lq] for g in groups for i in range(GQA_GROUP)]
    o_ref[...] = jnp.concatenate(heads, axis=-1).astype(BF16)


def _smem():
    return pl.BlockSpec(memory_space=pltpu.SMEM)


def _attn_prompt(q, k, v, bkt, table, sinks, B, L):
    nc = L // CHUNK
    cps = 4 if nc % 4 == 0 else 2
    assert nc % cps == 0
    rows = cps * CHUNK
    ns = nc // cps
    lk = WINDOW + CHUNK
    own = lambda n: pl.BlockSpec((rows, n), lambda b, s: (b * ns + s, 0))
    prev = pl.BlockSpec((WINDOW, KV_W), lambda b, s: (jnp.maximum((b * ns + s) * (rows // WINDOW) - 1, 0), 0))
    return pl.pallas_call(
        functools.partial(_attn_band_kernel, cps=cps),
        grid=(B, ns),
        in_specs=[own(Q_W), prev, own(KV_W), prev, own(KV_W),
                  _resident((lk, CHUNK), lambda b, s: (0, 0)), _smem(), _smem()],
        out_specs=own(Q_W),
        out_shape=jax.ShapeDtypeStruct((B * L, Q_W), BF16),
        scratch_shapes=[pltpu.VMEM((N_KV_HEADS, lk, GQA_GROUP * CHUNK), F32)],
        compiler_params=_params(("arbitrary", "arbitrary")),
    )(q, k, k, v, v, bkt, table, sinks)


def _attn_step(q, k_all, v_all, bkt, table, sinks, B, L):
    lk = k_all.shape[1]
    return pl.pallas_call(
        functools.partial(_attn_step_kernel, lq=L),
        grid=(B,),
        in_specs=[pl.BlockSpec((L, Q_W), lambda b: (b, 0)),
                  pl.BlockSpec((1, lk, KV_W), lambda b: (b, 0, 0)),
                  pl.BlockSpec((1, lk, KV_W), lambda b: (b, 0, 0)),
                  _resident((lk, L), lambda b: (0, 0)), _smem(), _smem()],
        out_specs=pl.BlockSpec((L, Q_W), lambda b: (b, 0)),
        out_shape=jax.ShapeDtypeStruct((B * L, Q_W), BF16),
        scratch_shapes=[pltpu.VMEM((N_KV_HEADS, lk, GQA_GROUP * L), F32)],
        compiler_params=_params(("arbitrary",)),
    )(q, k_all, v_all, bkt, table, sinks)


def _even_out_kernel(x_ref, p_ref, a_ref, wp_ref, wa_ref, o_ref):
    o_ref[...] = x_ref[...] + _dot(p_ref[...], wp_ref[...]) + _dot(a_ref[...], wa_ref[...])


def _even_out(x, pool_out, att, w_out, i):
    T = x.shape[0]
    tm = min(512, T)
    row = lambda n: pl.BlockSpec((tm, n), lambda t: (t, 0))
    return pl.pallas_call(
        _even_out_kernel,
        grid=(T // tm,),
        in_specs=[row(D_MODEL), row(C_POOL), row(Q_W),
                  _resident((None, C_POOL, D_MODEL), lambda t: (i, 0, 0)),
                  _resident((None, Q_W, D_MODEL), lambda t: (i, 1, 0))],
        out_specs=row(D_MODEL),
        out_shape=jax.ShapeDtypeStruct((T, D_MODEL), F32),
        compiler_params=_params(("parallel",)),
    )(x, pool_out, att, w_out, w_out)


def _t5_bucket(rel):
    half = N_BUCKETS // 2
    max_exact = half // 2
    side = jnp.where(rel > 0, half, 0)
    n = jnp.abs(rel)
    nf = jnp.maximum(n, max_exact).astype(F32)
    large = max_exact + (jnp.log(nf / max_exact) / math.log(MAX_DISTANCE / max_exact)
                         * (half - max_exact)).astype(jnp.int32)
    large = jnp.minimum(large, half - 1)
    return side + jnp.where(n < max_exact, n, large)


def _bucket_index(lq, lk, offset):
    rel = jnp.arange(lk)[:, None] - offset - jnp.arange(lq)[None, :]
    return _t5_bucket(rel).astype(jnp.int32)


def _rwkv_pre_kernel(x_ref, xp_ref, sh_ref, gn_ref, mu_ref, wr_ref, wk_ref, wv_ref, w1_ref, w2_ref,
                     a1_ref, a2_ref, g1_ref, g2_ref, w0_ref, a0_ref, kk_ref, ka_ref, e_ref, et_ref,
                     r_o, lw_o, k_o, v_o, kk_o, b_o, g_o, hs_o, *, tm):
    t = pl.program_id(1)
    gn = gn_ref[...]
    h = _rms(x_ref[...], gn)
    prev_tile_last = _rms(xp_ref[...], gn)[7:8]
    first_prev = jnp.where(t == 0, sh_ref[0], prev_tile_last)
    row = lax.broadcasted_iota(jnp.int32, (tm, 1), 0)
    h_prev = jnp.where(row == 0, first_prev, pltpu.roll(h, 1, 0))
    xx = h_prev - h

    def mix(j):
        return (h + xx * mu_ref[j:j + 1]).astype(BF16)

    r = _dot(mix(0), wr_ref[...])
    k = _dot(mix(2), wk_ref[...])
    v = _dot(mix(3), wv_ref[...])
    zw = w0_ref[...] + _dot(jnp.tanh(_dot(mix(1), w1_ref[...])).astype(BF16), w2_ref[...])
    w = -(jnp.maximum(-zw, 0.0) + jnp.log(1.0 + jnp.exp(-jnp.abs(zw)))) - 0.5
    lw = -jnp.exp(w)
    a = jax.nn.sigmoid(a0_ref[...] + _dot(_dot(mix(4), a1_ref[...]).astype(BF16), a2_ref[...]))
    g = _dot(jax.nn.sigmoid(_dot(mix(5), g1_ref[...])).astype(BF16), g2_ref[...])
    kk = k * kk_ref[...]
    kk = kk * lax.rsqrt(jnp.maximum(_head_sum(kk * kk, e_ref, et_ref), 1e-24))
    k = k * (1.0 + (a - 1.0) * ka_ref[...])
    b = kk * a
    for hp in range(N_PAIRS):
        sl = slice(hp * LANES, (hp + 1) * LANES)
        r_o[0, hp] = r[:, sl]
        lw_o[0, hp] = lw[:, sl]
        k_o[0, hp] = k[:, sl]
        v_o[0, hp] = v[:, sl]
        kk_o[0, hp] = kk[:, sl]
        b_o[0, hp] = b[:, sl]
    g_o[...] = g
    hs_o[0] = h[tm - 8:]


def _rwkv_pre(x, shift_prev, gn, p, i, B, L):
    tm = min(512, L)
    nt = L // tm
    vec = lambda: _resident((1, D_MODEL), lambda b, t: (0, 0))
    mat = lambda r, c: _resident((None, r, c), lambda b, t: (i, 0, 0))
    lora = p['rwkv_w1'].shape[-1], p['rwkv_a1'].shape[-1], p['rwkv_g1'].shape[-1]
    head_spec = pl.BlockSpec((1, N_PAIRS, tm, LANES), lambda b, t: (b, 0, t, 0))
    head_shape = jax.ShapeDtypeStruct((B, N_PAIRS, L, LANES), F32)
    return pl.pallas_call(
        functools.partial(_rwkv_pre_kernel, tm=tm),
        grid=(B, nt),
        in_specs=[
            pl.BlockSpec((tm, D_MODEL), lambda b, t: (b * nt + t, 0)),
            pl.BlockSpec((8, D_MODEL), lambda b, t: (jnp.maximum((b * nt + t) * (tm // 8) - 1, 0), 0)),
            pl.BlockSpec((1, 1, D_MODEL), lambda b, t: (b, 0, 0)),
            vec(),
            _resident((None, 6, D_MODEL), lambda b, t: (i, 0, 0)),
            mat(D_MODEL, D_MODEL), mat(D_MODEL, D_MODEL), mat(D_MODEL, D_MODEL),
            mat(D_MODEL, lora[0]), mat(lora[0], D_MODEL),
            mat(D_MODEL, lora[1]), mat(lora[1], D_MODEL),
            mat(D_MODEL, lora[2]), mat(lora[2], D_MODEL),
            vec(), vec(), vec(), vec(),
            _resident((D_MODEL, LANES), lambda b, t: (0, 0)),
            _resident((LANES, D_MODEL), lambda b, t: (0, 0)),
        ],
        out_specs=[head_spec] * 6 + [
            pl.BlockSpec((tm, D_MODEL), lambda b, t: (b * nt + t, 0)),
            pl.BlockSpec((1, 8, D_MODEL), lambda b, t: (b, 0, 0)),
        ],
        out_shape=[head_shape] * 6 + [
            jax.ShapeDtypeStruct((B * L, D_MODEL), F32),
            jax.ShapeDtypeStruct((B, 8, D_MODEL), F32),
        ],
        compiler_params=_params(("parallel", "arbitrary")),
    )(x, x, shift_prev, gn, p['rwkv_mu'], p['rwkv_wr'], p['rwkv_wk'], p['rwkv_wv'],
      p['rwkv_w1'], p['rwkv_w2'], p['rwkv_a1'], p['rwkv_a2'], p['rwkv_g1'], p['rwkv_g2'],
      p['rwkv_w0'][i][None], p['rwkv_a0'][i][None], p['rwkv_k_k'][i][None], p['rwkv_k_a'][i][None],
      p['head_onehot'], p['head_onehot_t'])


DIAG = 8


def _replication_matrix(n):
    src = jnp.arange(n)[:, None]
    dst = jnp.arange(n)[None, :]
    return jnp.concatenate([(src == (dst // DIAG) * DIAG + s) for s in range(DIAG - 1)], axis=1).astype(BF16)


def _unit_lower_inverses(lows, c, rep_ref):
    n = 2 * c
    ti = lax.broadcasted_iota(jnp.int32, (c, n), 0)
    li = lax.broadcasted_iota(jnp.int32, (c, n), 1)
    si = li & (c - 1)
    head1 = li >= c

    def block_diag(x):
        return jnp.concatenate([jnp.where(head1, 0.0, x), jnp.where(head1, x, 0.0)], axis=0)

    if rep_ref is None:
        invs = [jnp.where(ti == si, 1.0, 0.0) + jnp.where(jnp.logical_and(ti == si + 1, (ti & 1) == 1), low, 0.0)
                for low in lows]
        m = 2
    else:
        pt = lax.broadcasted_iota(jnp.int32, (DIAG, n), 0)
        pl_ = lax.broadcasted_iota(jnp.int32, (DIAG, n), 1)
        blk = (pl_ & (c - 1)) // DIAG
        packed = []
        for low in lows:
            d = jnp.zeros((DIAG, n), F32)
            for i in range(c // DIAG):
                d = jnp.where(blk == i, low[i * DIAG:(i + 1) * DIAG], d)
            packed.append(d)
        rep = _dot(jnp.concatenate(packed, axis=0).astype(BF16), rep_ref[...])
        sols = [jnp.where(pt == (pl_ & (DIAG - 1)), 1.0, 0.0) for _ in lows]
        for s in range(DIAG - 1):
            sols = [sol + rep[p * DIAG:(p + 1) * DIAG, s * n:(s + 1) * n] * sol[s:s + 1]
                    for p, sol in enumerate(sols)]
        invs = [jnp.concatenate([jnp.where(blk == i, sol, 0.0) for i in range(c // DIAG)], axis=0) for sol in sols]
        m = DIAG
    while m < c:
        sh = m.bit_length() - 1
        sub = jnp.logical_and(jnp.logical_and((ti >> (sh + 1)) == (si >> (sh + 1)), ((ti >> sh) & 1) == 1),
                              ((si >> sh) & 1) == 0)
        diag = [block_diag(inv).astype(BF16) for inv in invs]
        half = [_dot(inv.astype(BF16), block_diag(jnp.where(sub, low, 0.0)).astype(BF16)).astype(BF16)
                for inv, low in zip(invs, lows)]
        invs = [inv + _dot(h, d) for inv, h, d in zip(invs, half, diag)]
        m *= 2
    return invs


def _scan_chunks(seqs, states, c, rep_ref):
    n = 2 * c
    row = lax.broadcasted_iota(jnp.int32, (c, LANES), 0)
    head1 = lax.broadcasted_iota(jnp.int32, (c, LANES), 1) >= RWKV_HEAD
    ti = lax.broadcasted_iota(jnp.int32, (c, n), 0)
    si = lax.broadcasted_iota(jnp.int32, (c, n), 1) & (c - 1)
    strict = ti > si
    incl = ti >= si
    pi = lax.broadcasted_iota(jnp.int32, (LANES, LANES), 0)
    pj = lax.broadcasted_iota(jnp.int32, (LANES, LANES), 1)
    same_head = (pi >= RWKV_HEAD) == (pj >= RWKV_HEAD)

    def stack(x):
        return jnp.concatenate([jnp.where(head1, 0.0, x), jnp.where(head1, x, 0.0)], axis=0)

    lhs, rhs, v_st, v_bf, bk_end, decay = [], [], [], [], [], []
    for r, lw, k, v, kk, b in seqs:
        cum = lw
        sh = 1
        while sh < c:
            cum = cum + jnp.where(row >= sh, pltpu.roll(cum, sh, 0), 0.0)
            sh *= 2
        tot = cum[c - 1:c]
        grow = jnp.exp(-cum)
        tail = jnp.exp(tot - cum)
        a_t = -kk * jnp.exp(cum - lw)
        r_t = r * jnp.exp(cum)
        lhs.append(jnp.concatenate([a_t, r_t], axis=0).astype(BF16))
        rhs.append(jnp.concatenate([stack(b * grow), stack(k * grow)], axis=0).astype(BF16))
        v_st.append(stack(v).astype(BF16))
        v_bf.append(v.astype(BF16))
        bk_end.append(jnp.concatenate([b * tail, k * tail], axis=0).astype(BF16))
        decay.append(jnp.exp(tot))
    grams = [_dot_nt(x, y) for x, y in zip(lhs, rhs)]
    lows = [jnp.where(strict, g[:c, :n], 0.0) for g in grams]
    m_ak = [jnp.where(strict, g[:c, n:], 0.0).astype(BF16) for g in grams]
    m_r = [jnp.concatenate([jnp.where(incl, g[c:, :n], 0.0), jnp.where(incl, g[c:, n:], 0.0)], axis=1).astype(BF16)
           for g in grams]
    invs = [inv.astype(BF16) for inv in _unit_lower_inverses(lows, c, rep_ref)]
    from_v = [_dot(m, vs) for m, vs in zip(m_ak, v_st)]

    ys = []
    states = list(states)
    npair = len(states)
    for j in range(len(seqs) // npair):
        sl = slice(j * npair, (j + 1) * npair)
        from_state = [_dot_nt(x, s.astype(BF16)) for x, s in zip(lhs[sl], states)]
        rhs_sa = [stack(f[:c] + fv).astype(BF16) for f, fv in zip(from_state, from_v[sl])]
        sas = [_dot(inv, x) for inv, x in zip(invs[sl], rhs_sa)]
        ys += [f[c:] + _dot(m, jnp.concatenate([stack(sa).astype(BF16), vs], axis=0))
               for f, m, sa, vs in zip(from_state, m_r[sl], sas, v_st[sl])]
        upd = [_dot_tn(jnp.concatenate([sa.astype(BF16), vb], axis=0), be)
               for sa, vb, be in zip(sas, v_bf[sl], bk_end[sl])]
        states = [s * d + jnp.where(same_head, u, 0.0) for s, d, u in zip(states, decay[sl], upd)]
    return ys, states


def _rwkv_scan_kernel(r_ref, lw_ref, k_ref, v_ref, kk_ref, b_ref, s0_ref, *rest, c, sub, vpu_diag):
    if vpu_diag:
        rep_ref, y_ref, so_ref, s_ref = rest
    else:
        rep_ref = None
        y_ref, so_ref, s_ref = rest
    ci = pl.program_id(1)

    @pl.when(ci == 0)
    def _():
        s_ref[...] = s0_ref[0]

    seqs = [tuple(ref[0, hp, j * c:(j + 1) * c] for ref in (r_ref, lw_ref, k_ref, v_ref, kk_ref, b_ref))
            for j in range(sub) for hp in range(N_PAIRS)]
    ys, new_states = _scan_chunks(seqs, [s_ref[hp] for hp in range(N_PAIRS)], c, rep_ref)
    for j in range(sub):
        for hp in range(N_PAIRS):
            y_ref[0, hp, j * c:(j + 1) * c] = ys[j * N_PAIRS + hp]
    for hp in range(N_PAIRS):
        s_ref[hp] = new_states[hp]

    @pl.when(ci == pl.num_programs(1) - 1)
    def _():
        so_ref[0] = s_ref[...]


def _rwkv_scan(r, lw, k, v, kk, b, s0, B, L):
    c = min(CHUNK, L)
    nc = L // c
    sub = 2 if nc % 2 == 0 else 1
    nc //= sub
    seq = pl.BlockSpec((1, N_PAIRS, sub * c, LANES), lambda bi, ci: (bi, 0, ci, 0))
    st = pl.BlockSpec((1, N_PAIRS, LANES, LANES), lambda bi, ci: (bi, 0, 0, 0))
    vpu_diag = 2 * c == LANES
    extra_specs, extra_args = [], []
    if vpu_diag:
        extra_specs = [_resident((LANES, (DIAG - 1) * LANES), lambda bi, ci: (0, 0))]
        extra_args = [_replication_matrix(LANES)]
    return pl.pallas_call(
        functools.partial(_rwkv_scan_kernel, c=c, sub=sub, vpu_diag=vpu_diag),
        grid=(B, nc),
        in_specs=[seq] * 6 + [st] + extra_specs,
        out_specs=[seq, st],
        out_shape=[jax.ShapeDtypeStruct((B, N_PAIRS, L, LANES), F32),
                   jax.ShapeDtypeStruct((B, N_PAIRS, LANES, LANES), F32)],
        scratch_shapes=[pltpu.VMEM((N_PAIRS, LANES, LANES), F32)],
        compiler_params=_params(("parallel", "arbitrary")),
    )(r, lw, k, v, kk, b, s0, *extra_args)


def _rwkv_post_kernel(x_ref, y_ref, r_ref, k_ref, v_ref, g_ref, rk_ref, lw_ref, lb_ref, e_ref, et_ref,
                      wo_ref, o_ref):
    cat = lambda ref: jnp.concatenate([ref[0, hp] for hp in range(N_PAIRS)], axis=-1)
    y, r, k, v = cat(y_ref), cat(r_ref), cat(k_ref), cat(v_ref)
    mean = _head_sum(y, e_ref, et_ref) * (1.0 / RWKV_HEAD)
    d = y - mean
    var = _head_sum(d * d, e_ref, et_ref) * (1.0 / RWKV_HEAD)
    yn = d * lax.rsqrt(var + LNX_EPS) * lw_ref[...] + lb_ref[...]
    bonus = _head_sum(r * k * rk_ref[...], e_ref, et_ref) * v
    o_ref[...] = x_ref[...] + _dot(((yn + bonus) * g_ref[...]).astype(BF16), wo_ref[...])


def _rwkv_post(x, y, r, k, v, g, p, i, B, L):
    tm = min(512, L)
    nt = L // tm
    vec = lambda: _resident((1, D_MODEL), lambda b, t: (0, 0))
    head_spec = pl.BlockSpec((1, N_PAIRS, tm, LANES), lambda b, t: (b, 0, t, 0))
    row = pl.BlockSpec((tm, D_MODEL), lambda b, t: (b * nt + t, 0))
    return pl.pallas_call(
        _rwkv_post_kernel,
        grid=(B, nt),
        in_specs=[row, head_spec, head_spec, head_spec, head_spec, row, vec(), vec(), vec(),
                  _resident((D_MODEL, LANES), lambda b, t: (0, 0)),
                  _resident((LANES, D_MODEL), lambda b, t: (0, 0)),
                  _resident((None, D_MODEL, D_MODEL), lambda b, t: (i, 0, 0))],
        out_specs=row,
        out_shape=jax.ShapeDtypeStruct((B * L, D_MODEL), F32),
        compiler_params=_params(("parallel", "parallel")),
    )(x, y, r, k, v, g, p['rwkv_r_k'][i].reshape(1, D_MODEL), p['rwkv_lnx_w'][i][None], p['rwkv_lnx_b'][i][None],
      p['head_onehot'], p['head_onehot_t'], p['rwkv_wo'])


def _pack_state(s):
    B = s.shape[0]
    s = s.reshape(B, N_PAIRS, 2, RWKV_HEAD, RWKV_HEAD)
    z = jnp.zeros_like(s[:, :, 0])
    top = jnp.concatenate([s[:, :, 0], z], axis=-1)
    bot = jnp.concatenate([z, s[:, :, 1]], axis=-1)
    return jnp.concatenate([top, bot], axis=-2)


def _unpack_state(s):
    B = s.shape[0]
    return jnp.stack([s[:, :, :RWKV_HEAD, :RWKV_HEAD], s[:, :, RWKV_HEAD:, RWKV_HEAD:]], axis=2).reshape(
        B, RWKV_H, RWKV_HEAD, RWKV_HEAD)


def _trunk(x, pos0, caches, p):
    pool_c, k_c, v_c, shift_c, wkv_c = caches
    B, L, _ = x.shape
    depth = p['norm_mix'].shape[0]
    stepping = k_c is not None
    x = x.reshape(B * L, D_MODEL)
    if stepping:
        bkt = _bucket_index(L, SWA_ROWS + L, SWA_ROWS)
    else:
        bkt = _bucket_index(CHUNK, WINDOW + CHUNK, WINDOW)
    new_pool, new_k, new_v, new_shift, new_wkv = [], [], [], [], []
    for l in range(depth):
        i = l // 2
        x = _ffn(x, p['norm_ffn'][l, 0][None], p['ffn_w_gate'], p['ffn_w_up'], p['ffn_w_down'], l, 0)
        gn = p['norm_mix'][l][None]
        if l % 2 == 0:
            u, q, k, v = _even_in(x, gn, p['w_in_even'], i)
            if stepping:
                hist = jnp.pad(pool_c[i], ((0, 0), (POOL_HALO - POOL_HIST, 0), (0, 0)))
            else:
                hist = jnp.zeros((B, POOL_HALO, C_POOL), F32)
            pool_out = _pool(u, hist, p['pool_w'][i], p['pool_scale'][i][None], B, L, pos0)
            k3, v3 = k.reshape(B, L, KV_W), v.reshape(B, L, KV_W)
            if stepping:
                k3 = jnp.concatenate([k_c[i].reshape(B, SWA_ROWS, KV_W), k3], axis=1)
                v3 = jnp.concatenate([v_c[i].reshape(B, SWA_ROWS, KV_W), v3], axis=1)
                att = _attn_step(q, k3, v3, bkt, p['t5_table'], p['attn_sinks'][i], B, L)
            else:
                att = _attn_prompt(q, k, v, bkt, p['t5_table'], p['attn_sinks'][i], B, L)
            x = _even_out(x, pool_out, att, p['w_out_even'], i)
            full = jnp.concatenate([hist, u.reshape(B, L, C_POOL)], axis=1)
            new_pool.append(full[:, -POOL_HIST:])
            new_k.append(k3[:, -SWA_ROWS:].reshape(B, SWA_ROWS, N_KV_HEADS, HEAD_DIM))
            new_v.append(v3[:, -SWA_ROWS:].reshape(B, SWA_ROWS, N_KV_HEADS, HEAD_DIM))
        else:
            if stepping:
                shift_prev = shift_c[i][:, None, :]
                s0 = _pack_state(wkv_c[i])
            else:
                shift_prev = jnp.zeros((B, 1, D_MODEL), F32)
                s0 = jnp.zeros((B, N_PAIRS, LANES, LANES), F32)
            r, lw, k, v, kk, b, g, hs = _rwkv_pre(x, shift_prev, gn, p, i, B, L)
            y, s_new = _rwkv_scan(r, lw, k, v, kk, b, s0, B, L)
            x = _rwkv_post(x, y, r, k, v, g, p, i, B, L)
            new_shift.append(hs[:, 7])
            new_wkv.append(_unpack_state(s_new))
        final_g = p['norm_final'][None] if l == depth - 1 else None
        x = _ffn(x, p['norm_ffn'][l, 1][None], p['ffn_w_gate'], p['ffn_w_up'], p['ffn_w_down'], l, 1, final_g)
    return (x.reshape(B, L, D_MODEL), jnp.stack(new_pool), jnp.stack(new_k), jnp.stack(new_v),
            jnp.stack(new_shift), jnp.stack(new_wkv))


_MATMUL_WEIGHTS = ('ffn_w_gate', 'ffn_w_up', 'ffn_w_down', 'w_in_even', 'pool_w', 'w_out_even', 'rwkv_wr',
                   'rwkv_wk', 'rwkv_wv', 'rwkv_w1', 'rwkv_w2', 'rwkv_a1', 'rwkv_a2', 'rwkv_g1', 'rwkv_g2', 'rwkv_wo')


def _prepare(p):
    p = dict(p)
    for name in _MATMUL_WEIGHTS:
        p[name] = p[name].astype(BF16)
    onehot = (jnp.arange(D_MODEL)[:, None] // RWKV_HEAD == jnp.arange(LANES)[None, :]).astype(BF16)
    p['head_onehot'] = onehot
    p['head_onehot_t'] = onehot.T
    return p


def kernel(x_prompt, x_sample, cache_pool, cache_swa_k, cache_swa_v, state_shift, state_wkv, t5_table, norm_ffn, ffn_w_gate, ffn_w_up, ffn_w_down, norm_mix, w_in_even, pool_w, pool_scale, attn_sinks, w_out_even, rwkv_mu, rwkv_wr, rwkv_wk, rwkv_wv, rwkv_w0, rwkv_w1, rwkv_w2, rwkv_a0, rwkv_a1, rwkv_a2, rwkv_g1, rwkv_g2, rwkv_k_k, rwkv_k_a, rwkv_r_k, rwkv_lnx_w, rwkv_lnx_b, rwkv_wo, norm_final):
    p = _prepare(dict(
        t5_table=t5_table, norm_ffn=norm_ffn, ffn_w_gate=ffn_w_gate, ffn_w_up=ffn_w_up, ffn_w_down=ffn_w_down,
        norm_mix=norm_mix, w_in_even=w_in_even, pool_w=pool_w, pool_scale=pool_scale, attn_sinks=attn_sinks,
        w_out_even=w_out_even, rwkv_mu=rwkv_mu, rwkv_wr=rwkv_wr, rwkv_wk=rwkv_wk, rwkv_wv=rwkv_wv,
        rwkv_w0=rwkv_w0, rwkv_w1=rwkv_w1, rwkv_w2=rwkv_w2, rwkv_a0=rwkv_a0, rwkv_a1=rwkv_a1, rwkv_a2=rwkv_a2,
        rwkv_g1=rwkv_g1, rwkv_g2=rwkv_g2, rwkv_k_k=rwkv_k_k, rwkv_k_a=rwkv_k_a, rwkv_r_k=rwkv_r_k,
        rwkv_lnx_w=rwkv_lnx_w, rwkv_lnx_b=rwkv_lnx_b, rwkv_wo=rwkv_wo, norm_final=norm_final))
    y_p, pool_p, k_p, v_p, shift_p, wkv_p = _trunk(x_prompt, 0, (None, None, None, None, None), p)
    y_s, pool_s, k_s, v_s, shift_s, wkv_s = _trunk(
        x_sample, PAST_LEN, (cache_pool, cache_swa_k, cache_swa_v, state_shift, state_wkv), p)
    return (y_p, y_s, pool_p, pool_s, k_p, k_s, v_p, v_s, shift_p, shift_s, wkv_p, wkv_s)
```

```python
import functools
import math

import jax
import jax.numpy as jnp
from jax import lax
from jax.experimental import pallas as pl
from jax.experimental.pallas import tpu as pltpu

F32 = jnp.float32
BF16 = jnp.bfloat16

D_MODEL = 1024
D_FF = 2816
NORM_EPS = 1e-6
CHUNK = 64
POOL_WINDOWS = (2, 4, 8, 16)
C_POOL = 512
POOL_GC = 128
POOL_HIST = 15
POOL_HALO = 16
HEAD_DIM = 64
N_Q_HEADS = 8
N_KV_HEADS = 2
GQA_GROUP = 4
WINDOW = 128
SWA_ROWS = 128
Q_W = 512
KV_W = 128
IN_EVEN = C_POOL + Q_W + 2 * KV_W
N_BUCKETS = 32
MAX_DISTANCE = 128
RWKV_HEAD = 64
RWKV_H = 16
N_PAIRS = RWKV_H // 2
LANES = 128
LNX_EPS = 64e-5
PAST_LEN = 4096
VMEM_LIMIT_BYTES = 56 * 1024 * 1024


def _params(sem):
    return pltpu.CompilerParams(dimension_semantics=sem, vmem_limit_bytes=VMEM_LIMIT_BYTES)


def _dot(a, b):
    return jnp.dot(a, b, preferred_element_type=F32)


def _dot_nt(a, b):
    return lax.dot_general(a, b, (((1,), (1,)), ((), ())), preferred_element_type=F32)


def _dot_tn(a, b):
    return lax.dot_general(a, b, (((0,), (0,)), ((), ())), preferred_element_type=F32)


def _rms(x, g):
    return x * lax.rsqrt(jnp.mean(x * x, axis=-1, keepdims=True) + NORM_EPS) * g


def _split(x):
    hi = x.astype(BF16)
    lo = (x - hi.astype(F32)).astype(BF16)
    return hi, lo


def _head_sum(x, e_ref, et_ref, split=True):
    if not split:
        return _dot(_dot(x.astype(BF16), e_ref[...]).astype(BF16), et_ref[...])
    hi, lo = _split(x)
    s = _dot(hi, e_ref[...]) + _dot(lo, e_ref[...])
    shi, slo = _split(s)
    return _dot(shi, et_ref[...]) + _dot(slo, et_ref[...])


def _ffn_kernel(x_ref, g_ref, wg_ref, wu_ref, wd_ref, *rest, f_chunk, final):
    if final:
        gf_ref, o_ref, acc_ref = rest
    else:
        o_ref, acc_ref = rest
    x = x_ref[...]
    h = _rms(x, g_ref[...]).astype(BF16)
    for j in range(D_FF // f_chunk):
        sl = slice(j * f_chunk, (j + 1) * f_chunk)
        gate = _dot(h, wg_ref[:, sl])
        up = _dot(h, wu_ref[:, sl])
        act = (gate * jax.nn.sigmoid(gate) * up).astype(BF16)
        part = _dot(act, wd_ref[sl, :])
        if j == 0:
            acc_ref[...] = part
        else:
            acc_ref[...] += part
    y = x + 0.5 * acc_ref[...]
    if final:
        y = _rms(y, gf_ref[...])
    o_ref[...] = y


def _resident(shape, index_map):
    return pl.BlockSpec(shape, index_map, pipeline_mode=pl.Buffered(1))


def _ffn(x, g, wg, wu, wd, l, j, final_g=None):
    T = x.shape[0]
    tm = min(512, T)
    final = final_g is not None
    in_specs = [
        pl.BlockSpec((tm, D_MODEL), lambda i: (i, 0)),
        _resident((1, D_MODEL), lambda i: (0, 0)),
        _resident((None, None, D_MODEL, D_FF), lambda i: (l, j, 0, 0)),
        _resident((None, None, D_MODEL, D_FF), lambda i: (l, j, 0, 0)),
        _resident((None, None, D_FF, D_MODEL), lambda i: (l, j, 0, 0)),
    ]
    args = [x, g, wg, wu, wd]
    if final:
        in_specs.append(_resident((1, D_MODEL), lambda i: (0, 0)))
        args.append(final_g)
    return pl.pallas_call(
        functools.partial(_ffn_kernel, f_chunk=256, final=final),
        grid=(T // tm,),
        in_specs=in_specs,
        out_specs=pl.BlockSpec((tm, D_MODEL), lambda i: (i, 0)),
        out_shape=jax.ShapeDtypeStruct((T, D_MODEL), F32),
        scratch_shapes=[pltpu.VMEM((tm, D_MODEL), F32)],
        compiler_params=_params(("parallel",)),
    )(*args)


def _even_in_kernel(x_ref, g_ref, w_ref, u_ref, q_ref, k_ref, v_ref):
    h = _rms(x_ref[...], g_ref[...]).astype(BF16)
    z = _dot(h, w_ref[...])
    u_ref[...] = z[:, :C_POOL]
    q_ref[...] = z[:, C_POOL:C_POOL + Q_W].astype(BF16)
    k_ref[...] = z[:, C_POOL + Q_W:C_POOL + Q_W + KV_W]
    v_ref[...] = z[:, C_POOL + Q_W + KV_W:]


def _even_in(x, g, w_in, i):
    T = x.shape[0]
    tm = min(512, T)
    row = lambda n: pl.BlockSpec((tm, n), lambda t: (t, 0))
    return pl.pallas_call(
        _even_in_kernel,
        grid=(T // tm,),
        in_specs=[row(D_MODEL), _resident((1, D_MODEL), lambda t: (0, 0)),
                  _resident((None, D_MODEL, IN_EVEN), lambda t: (i, 0, 0))],
        out_specs=[row(C_POOL), row(Q_W), row(KV_W), row(KV_W)],
        out_shape=[jax.ShapeDtypeStruct((T, C_POOL), F32), jax.ShapeDtypeStruct((T, Q_W), BF16),
                   jax.ShapeDtypeStruct((T, KV_W), F32), jax.ShapeDtypeStruct((T, KV_W), F32)],
        compiler_params=_params(("parallel",)),
    )(x, g, w_in)


def _pool_kernel(u_ref, halo_ref, hist_ref, pw_ref, ps_ref, o_ref, *, tt, pos0):
    t = pl.program_id(1)
    halo = jnp.where(t == 0, hist_ref[0], halo_ref[...])
    u = u_ref[...]
    ext = jnp.concatenate([halo, u], axis=0)
    s2 = ext + pltpu.roll(ext, 1, 0)
    s4 = s2 + pltpu.roll(s2, 2, 0)
    s8 = s4 + pltpu.roll(s4, 4, 0)
    s16 = s8 + pltpu.roll(s8, 8, 0)
    pos = pos0 + t * tt + lax.broadcasted_iota(jnp.int32, (tt, 1), 0)
    outs = []
    for gi, (w, s) in enumerate(zip(POOL_WINDOWS, (s2, s4, s8, s16))):
        sl = slice(gi * POOL_GC, (gi + 1) * POOL_GC)
        cnt = jnp.minimum(w, pos + 1).astype(F32)
        pooled = s[POOL_HALO:, sl] / cnt - u[:, sl]
        outs.append(_dot(pooled.astype(BF16), pw_ref[gi]))
    o_ref[...] = (jnp.concatenate(outs, axis=-1) * ps_ref[...]).astype(BF16)


def _pool(u, hist, pool_w, pool_scale, B, L, pos0):
    tt = min(512, L)
    nt = L // tt
    hb = tt // POOL_HALO
    return pl.pallas_call(
        functools.partial(_pool_kernel, tt=tt, pos0=pos0),
        grid=(B, nt),
        in_specs=[
            pl.BlockSpec((tt, C_POOL), lambda b, t: (b * nt + t, 0)),
            pl.BlockSpec((POOL_HALO, C_POOL), lambda b, t: (jnp.maximum((b * nt + t) * hb - 1, 0), 0)),
            pl.BlockSpec((1, POOL_HALO, C_POOL), lambda b, t: (b, 0, 0)),
            _resident((len(POOL_WINDOWS), POOL_GC, POOL_GC), lambda b, t: (0, 0, 0)),
            _resident((1, C_POOL), lambda b, t: (0, 0)),
        ],
        out_specs=pl.BlockSpec((tt, C_POOL), lambda b, t: (b * nt + t, 0)),
        out_shape=jax.ShapeDtypeStruct((B * L, C_POOL), BF16),
        compiler_params=_params(("parallel", "arbitrary")),
    )(u, u, hist, pool_w, pool_scale)


def _build_bias(bkt_ref, tab_ref, bias_ref, lq):
    bkt = bkt_ref[...]
    for h in range(N_Q_HEADS):
        b = jnp.zeros(bkt.shape, F32)
        for n in range(N_BUCKETS):
            b = jnp.where(bkt == n, tab_ref[n, h], b)
        g, i = divmod(h, GQA_GROUP)
        bias_ref[g, :, i * lq:(i + 1) * lq] = b


def _group_queries(q, g):
    return jnp.concatenate([q[:, h * HEAD_DIM:(h + 1) * HEAD_DIM]
                            for h in range(g * GQA_GROUP, (g + 1) * GQA_GROUP)], axis=0) * (HEAD_DIM ** -0.5)


def _sink_rows(sink_ref, lq):
    lane = lax.broadcasted_iota(jnp.int32, (1, GQA_GROUP * lq), 1)
    rows = []
    for g in range(N_KV_HEADS):
        r = jnp.zeros((1, GQA_GROUP * lq), F32)
        for i in range(GQA_GROUP):
            r = jnp.where(lane // lq == i, sink_ref[g * GQA_GROUP + i], r)
        rows.append(r)
    return rows


def _attn_core(qs, ks, vs, biases, sinks, valids):
    ss = [_dot_nt(k, q) + b for q, k, b in zip(qs, ks, biases)]
    ss = [s if ok is None else jnp.where(ok, s, -1e30) for s, ok in zip(ss, valids)]
    ms = [jnp.maximum(jnp.max(s, axis=0, keepdims=True), sk) for s, sk in zip(ss, sinks)]
    ps = [jnp.exp(s - m) for s, m in zip(ss, ms)]
    invs = [1.0 / (jnp.sum(p, axis=0, keepdims=True) + jnp.exp(sk - m)) for p, sk, m in zip(ps, sinks, ms)]
    return [_dot_tn((p * r).astype(BF16), v) for p, r, v in zip(ps, invs, vs)]


def _attn_band_kernel(q_ref, kp_ref, ko_ref, vp_ref, vo_ref, bkt_ref, tab_ref, sink_ref, o_ref, bias_ref, *, cps):
    @pl.when(jnp.logical_and(pl.program_id(0) == 0, pl.program_id(1) == 0))
    def _():
        _build_bias(bkt_ref, tab_ref, bias_ref, CHUNK)

    lk = WINDOW + CHUNK
    k = jnp.concatenate([kp_ref[...], ko_ref[...]], axis=0).astype(BF16)
    v = jnp.concatenate([vp_ref[...], vo_ref[...]], axis=0).astype(BF16)
    q = q_ref[...]
    sink_cols = _sink_rows(sink_ref, CHUNK)
    first_pos = pl.program_id(1) * (cps * CHUNK) - WINDOW + lax.broadcasted_iota(jnp.int32, (lk, 1), 0)
    qs, ks, vs, biases, sinks, valids = [], [], [], [], [], []
    for j in range(cps):
        qj = q[j * CHUNK:(j + 1) * CHUNK]
        for g in range(N_KV_HEADS):
            qs.append(_group_queries(qj, g))
            ks.append(k[j * CHUNK:j * CHUNK + lk, g * HEAD_DIM:(g + 1) * HEAD_DIM])
            vs.append(v[j * CHUNK:j * CHUNK + lk, g * HEAD_DIM:(g + 1) * HEAD_DIM])
            biases.append(bias_ref[g])
            sinks.append(sink_cols[g])
            valids.append(first_pos + j * CHUNK >= 0)
    outs = _attn_core(qs, ks, vs, biases, sinks, valids)
    for j in range(cps):
        heads = [outs[j * N_KV_HEADS + g][i * CHUNK:(i + 1) * CHUNK]
                 for g in range(N_KV_HEADS) for i in range(GQA_GROUP)]
        o_ref[j * CHUNK:(j + 1) * CHUNK, :] = jnp.concatenate(heads, axis=-1).astype(BF16)


def _attn_step_kernel(q_ref, k_ref, v_ref, bkt_ref, tab_ref, sink_ref, o_ref, bias_ref, *, lq):
    @pl.when(pl.program_id(0) == 0)
    def _():
        _build_bias(bkt_ref, tab_ref, bias_ref, lq)

    k = k_ref[0].astype(BF16)
    v = v_ref[0].astype(BF16)
    q = q_ref[...]
    groups = range(N_KV_HEADS)
    outs = _attn_core([_group_queries(q, g) for g in groups],
                      [k[:, g * HEAD_DIM:(g + 1) * HEAD_DIM] for g in groups],
                      [v[:, g * HEAD_DIM:(g + 1) * HEAD_DIM] for g in groups],
                      [bias_ref[g] for g in groups], _sink_rows(sink_ref, lq), [None] * N_KV_HEADS)
    heads = [outs[g][i * lq:(i + 1) * lq] for g in groups for i in range(GQA_GROUP)]
    o_ref[...] = jnp.concatenate(heads, axis=-1).astype(BF16)


def _smem():
    return pl.BlockSpec(memory_space=pltpu.SMEM)


def _attn_prompt(q, k, v, bkt, table, sinks, B, L):
    nc = L // CHUNK
    cps = 4 if nc % 4 == 0 else 2
    assert nc % cps == 0
    rows = cps * CHUNK
    ns = nc // cps
    lk = WINDOW + CHUNK
    own = lambda n: pl.BlockSpec((rows, n), lambda b, s: (b * ns + s, 0))
    prev = pl.BlockSpec((WINDOW, KV_W), lambda b, s: (jnp.maximum((b * ns + s) * (rows // WINDOW) - 1, 0), 0))
    return pl.pallas_call(
        functools.partial(_attn_band_kernel, cps=cps),
        grid=(B, ns),
        in_specs=[own(Q_W), prev, own(KV_W), prev, own(KV_W),
                  _resident((lk, CHUNK), lambda b, s: (0, 0)), _smem(), _smem()],
        out_specs=own(Q_W),
        out_shape=jax.ShapeDtypeStruct((B * L, Q_W), BF16),
        scratch_shapes=[pltpu.VMEM((N_KV_HEADS, lk, GQA_GROUP * CHUNK), F32)],
        compiler_params=_params(("arbitrary", "arbitrary")),
    )(q, k, k, v, v, bkt, table, sinks)


def _attn_step(q, k_all, v_all, bkt, table, sinks, B, L):
    lk = k_all.shape[1]
    return pl.pallas_call(
        functools.partial(_attn_step_kernel, lq=L),
        grid=(B,),
        in_specs=[pl.BlockSpec((L, Q_W), lambda b: (b, 0)),
                  pl.BlockSpec((1, lk, KV_W), lambda b: (b, 0, 0)),
                  pl.BlockSpec((1, lk, KV_W), lambda b: (b, 0, 0)),
                  _resident((lk, L), lambda b: (0, 0)), _smem(), _smem()],
        out_specs=pl.BlockSpec((L, Q_W), lambda b: (b, 0)),
        out_shape=jax.ShapeDtypeStruct((B * L, Q_W), BF16),
        scratch_shapes=[pltpu.VMEM((N_KV_HEADS, lk, GQA_GROUP * L), F32)],
        compiler_params=_params(("arbitrary",)),
    )(q, k_all, v_all, bkt, table, sinks)


def _even_out_kernel(x_ref, p_ref, a_ref, wp_ref, wa_ref, o_ref):
    o_ref[...] = x_ref[...] + _dot(p_ref[...], wp_ref[...]) + _dot(a_ref[...], wa_ref[...])


def _even_out(x, pool_out, att, w_out, i):
    T = x.shape[0]
    tm = min(512, T)
    row = lambda n: pl.BlockSpec((tm, n), lambda t: (t, 0))
    return pl.pallas_call(
        _even_out_kernel,
        grid=(T // tm,),
        in_specs=[row(D_MODEL), row(C_POOL), row(Q_W),
                  _resident((None, C_POOL, D_MODEL), lambda t: (i, 0, 0)),
                  _resident((None, Q_W, D_MODEL), lambda t: (i, 1, 0))],
        out_specs=row(D_MODEL),
        out_shape=jax.ShapeDtypeStruct((T, D_MODEL), F32),
        compiler_params=_params(("parallel",)),
    )(x, pool_out, att, w_out, w_out)


def _t5_bucket(rel):
    half = N_BUCKETS // 2
    max_exact = half // 2
    side = jnp.where(rel > 0, half, 0)
    n = jnp.abs(rel)
    nf = jnp.maximum(n, max_exact).astype(F32)
    large = max_exact + (jnp.log(nf / max_exact) / math.log(MAX_DISTANCE / max_exact)
                         * (half - max_exact)).astype(jnp.int32)
    large = jnp.minimum(large, half - 1)
    return side + jnp.where(n < max_exact, n, large)


def _bucket_index(lq, lk, offset):
    rel = jnp.arange(lk)[:, None] - offset - jnp.arange(lq)[None, :]
    return _t5_bucket(rel).astype(jnp.int32)


def _rwkv_pre_kernel(x_ref, xp_ref, sh_ref, gn_ref, mu_ref, wr_ref, wk_ref, wv_ref, w1_ref, w2_ref,
                     a1_ref, a2_ref, g1_ref, g2_ref, w0_ref, a0_ref, kk_ref, ka_ref, e_ref, et_ref,
                     r_o, lw_o, k_o, v_o, kk_o, b_o, g_o, hs_o, *, tm):
    t = pl.program_id(1)
    gn = gn_ref[...]
    h = _rms(x_ref[...], gn)
    prev_tile_last = _rms(xp_ref[...], gn)[7:8]
    first_prev = jnp.where(t == 0, sh_ref[0], prev_tile_last)
    row = lax.broadcasted_iota(jnp.int32, (tm, 1), 0)
    h_prev = jnp.where(row == 0, first_prev, pltpu.roll(h, 1, 0))
    xx = h_prev - h

    hb = h.astype(BF16)
    xb = xx.astype(BF16)
    mu = mu_ref[...].astype(BF16)

    def mix(j):
        return hb + xb * mu[j:j + 1]

    zw = w0_ref[...] + _dot(jnp.tanh(_dot(mix(1), w1_ref[...])).astype(BF16), w2_ref[...])
    za = a0_ref[...] + _dot(_dot(mix(4), a1_ref[...]).astype(BF16), a2_ref[...])
    g = _dot(jax.nn.sigmoid(_dot(mix(5), g1_ref[...])).astype(BF16), g2_ref[...])
    r = _dot(mix(0), wr_ref[...])
    lw = -math.exp(-0.5) * jax.nn.sigmoid(zw)
    k = _dot(mix(2), wk_ref[...])
    a = jax.nn.sigmoid(za)
    v = _dot(mix(3), wv_ref[...])
    kk = k * kk_ref[...]
    kk = kk * lax.rsqrt(jnp.maximum(_head_sum(kk * kk, e_ref, et_ref, split=False), 1e-24))
    k = k * (1.0 + (a - 1.0) * ka_ref[...])
    b = kk * a
    for hp in range(N_PAIRS):
        sl = slice(hp * LANES, (hp + 1) * LANES)
        r_o[0, hp] = r[:, sl]
        lw_o[0, hp] = lw[:, sl]
        k_o[0, hp] = k[:, sl]
        v_o[0, hp] = v[:, sl]
        kk_o[0, hp] = kk[:, sl]
        b_o[0, hp] = b[:, sl]
    g_o[...] = g
    hs_o[0] = h[tm - 8:]


def _rwkv_pre(x, shift_prev, gn, p, i, B, L):
    tm = min(512, L)
    nt = L // tm
    vec = lambda: _resident((1, D_MODEL), lambda b, t: (0, 0))
    mat = lambda r, c: _resident((None, r, c), lambda b, t: (i, 0, 0))
    lora = p['rwkv_w1'].shape[-1], p['rwkv_a1'].shape[-1], p['rwkv_g1'].shape[-1]
    head_spec = pl.BlockSpec((1, N_PAIRS, tm, LANES), lambda b, t: (b, 0, t, 0))
    head_shape = jax.ShapeDtypeStruct((B, N_PAIRS, L, LANES), F32)
    return pl.pallas_call(
        functools.partial(_rwkv_pre_kernel, tm=tm),
        grid=(B, nt),
        in_specs=[
            pl.BlockSpec((tm, D_MODEL), lambda b, t: (b * nt + t, 0)),
            pl.BlockSpec((8, D_MODEL), lambda b, t: (jnp.maximum((b * nt + t) * (tm // 8) - 1, 0), 0)),
            pl.BlockSpec((1, 1, D_MODEL), lambda b, t: (b, 0, 0)),
            vec(),
            _resident((None, 6, D_MODEL), lambda b, t: (i, 0, 0)),
            mat(D_MODEL, D_MODEL), mat(D_MODEL, D_MODEL), mat(D_MODEL, D_MODEL),
            mat(D_MODEL, lora[0]), mat(lora[0], D_MODEL),
            mat(D_MODEL, lora[1]), mat(lora[1], D_MODEL),
            mat(D_MODEL, lora[2]), mat(lora[2], D_MODEL),
            vec(), vec(), vec(), vec(),
            _resident((D_MODEL, LANES), lambda b, t: (0, 0)),
            _resident((LANES, D_MODEL), lambda b, t: (0, 0)),
        ],
        out_specs=[head_spec] * 6 + [
            pl.BlockSpec((tm, D_MODEL), lambda b, t: (b * nt + t, 0)),
            pl.BlockSpec((1, 8, D_MODEL), lambda b, t: (b, 0, 0)),
        ],
        out_shape=[head_shape] * 6 + [
            jax.ShapeDtypeStruct((B * L, D_MODEL), F32),
            jax.ShapeDtypeStruct((B, 8, D_MODEL), F32),
        ],
        compiler_params=_params(("parallel", "arbitrary")),
    )(x, x, shift_prev, gn, p['rwkv_mu'], p['rwkv_wr'], p['rwkv_wk'], p['rwkv_wv'],
      p['rwkv_w1'], p['rwkv_w2'], p['rwkv_a1'], p['rwkv_a2'], p['rwkv_g1'], p['rwkv_g2'],
      p['rwkv_w0'][i][None], p['rwkv_a0'][i][None], p['rwkv_k_k'][i][None], p['rwkv_k_a'][i][None],
      p['head_onehot'], p['head_onehot_t'])


DIAG = 8


def _replication_matrix(n):
    src = jnp.arange(n)[:, None]
    dst = jnp.arange(n)[None, :]
    return jnp.concatenate([(src == (dst // DIAG) * DIAG + s) for s in range(DIAG - 1)], axis=1).astype(BF16)


def _unit_lower_inverses(lows, c, rep_ref):
    n = 2 * c
    ti = lax.broadcasted_iota(jnp.int32, (c, n), 0)
    li = lax.broadcasted_iota(jnp.int32, (c, n), 1)
    si = li & (c - 1)
    head1 = li >= c

    def block_diag(x):
        return jnp.concatenate([jnp.where(head1, 0.0, x), jnp.where(head1, x, 0.0)], axis=0)

    if rep_ref is None:
        invs = [jnp.where(ti == si, 1.0, 0.0) + jnp.where(jnp.logical_and(ti == si + 1, (ti & 1) == 1), low, 0.0)
                for low in lows]
        m = 2
    else:
        pt = lax.broadcasted_iota(jnp.int32, (DIAG, n), 0)
        pl_ = lax.broadcasted_iota(jnp.int32, (DIAG, n), 1)
        blk = (pl_ & (c - 1)) // DIAG
        packed = []
        for low in lows:
            d = jnp.zeros((DIAG, n), F32)
            for i in range(c // DIAG):
                d = jnp.where(blk == i, low[i * DIAG:(i + 1) * DIAG], d)
            packed.append(d)
        rep = _dot(jnp.concatenate(packed, axis=0).astype(BF16), rep_ref[...])
        sols = [jnp.where(pt == (pl_ & (DIAG - 1)), 1.0, 0.0) for _ in lows]
        for s in range(DIAG - 1):
            sols = [sol + rep[p * DIAG:(p + 1) * DIAG, s * n:(s + 1) * n] * sol[s:s + 1]
                    for p, sol in enumerate(sols)]
        invs = [jnp.concatenate([jnp.where(blk == i, sol, 0.0) for i in range(c // DIAG)], axis=0) for sol in sols]
        m = DIAG
    while m < c:
        sh = m.bit_length() - 1
        sub = jnp.logical_and(jnp.logical_and((ti >> (sh + 1)) == (si >> (sh + 1)), ((ti >> sh) & 1) == 1),
                              ((si >> sh) & 1) == 0)
        diag = [block_diag(inv).astype(BF16) for inv in invs]
        half = [_dot(inv.astype(BF16), block_diag(jnp.where(sub, low, 0.0)).astype(BF16)).astype(BF16)
                for inv, low in zip(invs, lows)]
        invs = [inv + _dot(h, d) for inv, h, d in zip(invs, half, diag)]
        m *= 2
    return invs


def _scan_chunks(seqs, states, c, rep_ref):
    n = 2 * c
    row = lax.broadcasted_iota(jnp.int32, (c, LANES), 0)
    head1 = lax.broadcasted_iota(jnp.int32, (c, LANES), 1) >= RWKV_HEAD
    ti = lax.broadcasted_iota(jnp.int32, (c, n), 0)
    si = lax.broadcasted_iota(jnp.int32, (c, n), 1) & (c - 1)
    strict = ti > si
    incl = ti >= si
    pi = lax.broadcasted_iota(jnp.int32, (LANES, LANES), 0)
    pj = lax.broadcasted_iota(jnp.int32, (LANES, LANES), 1)
    same_head = (pi >= RWKV_HEAD) == (pj >= RWKV_HEAD)

    def stack(x):
        return jnp.concatenate([jnp.where(head1, 0.0, x), jnp.where(head1, x, 0.0)], axis=0)

    lhs, rhs, v_st, v_bf, bk_end, decay = [], [], [], [], [], []
    for r, lw, k, v, kk, b in seqs:
        cum = lw
        sh = 1
        while sh < c:
            cum = cum + jnp.where(row >= sh, pltpu.roll(cum, sh, 0), 0.0)
            sh *= 2
        tot = cum[c - 1:c]
        grow = jnp.exp(-cum)
        tail = jnp.exp(tot - cum)
        a_t = -kk * jnp.exp(cum - lw)
        r_t = r * jnp.exp(cum)
        lhs.append(jnp.concatenate([a_t, r_t], axis=0).astype(BF16))
        rhs.append(jnp.concatenate([stack(b * grow), stack(k * grow)], axis=0).astype(BF16))
        v_st.append(stack(v).astype(BF16))
        v_bf.append(v.astype(BF16))
        bk_end.append(jnp.concatenate([b * tail, k * tail], axis=0).astype(BF16))
        decay.append(jnp.exp(tot))
    grams = [_dot_nt(x, y) for x, y in zip(lhs, rhs)]
    lows = [jnp.where(strict, g[:c, :n], 0.0) for g in grams]
    m_ak = [jnp.where(strict, g[:c, n:], 0.0).astype(BF16) for g in grams]
    m_r = [jnp.concatenate([jnp.where(incl, g[c:, :n], 0.0), jnp.where(incl, g[c:, n:], 0.0)], axis=1).astype(BF16)
           for g in grams]
    invs = [inv.astype(BF16) for inv in _unit_lower_inverses(lows, c, rep_ref)]
    from_v = [_dot(m, vs) for m, vs in zip(m_ak, v_st)]

    ys = []
    states = list(states)
    npair = len(states)
    for j in range(len(seqs) // npair):
        sl = slice(j * npair, (j + 1) * npair)
        from_state = [_dot_nt(x, s.astype(BF16)) for x, s in zip(lhs[sl], states)]
        rhs_sa = [stack(f[:c] + fv).astype(BF16) for f, fv in zip(from_state, from_v[sl])]
        sas = [_dot(inv, x) for inv, x in zip(invs[sl], rhs_sa)]
        ys += [f[c:] + _dot(m, jnp.concatenate([stack(sa).astype(BF16), vs], axis=0))
               for f, m, sa, vs in zip(from_state, m_r[sl], sas, v_st[sl])]
        upd = [_dot_tn(jnp.concatenate([sa.astype(BF16), vb], axis=0), be)
               for sa, vb, be in zip(sas, v_bf[sl], bk_end[sl])]
        states = [s * d + jnp.where(same_head, u, 0.0) for s, d, u in zip(states, decay[sl], upd)]
    return ys, states


def _rwkv_scan_kernel(r_ref, lw_ref, k_ref, v_ref, kk_ref, b_ref, s0_ref, *rest, c, sub, vpu_diag):
    if vpu_diag:
        rep_ref, y_ref, so_ref, s_ref = rest
    else:
        rep_ref = None
        y_ref, so_ref, s_ref = rest
    ci = pl.program_id(1)

    @pl.when(ci == 0)
    def _():
        s_ref[...] = s0_ref[0]

    seqs = [tuple(ref[0, hp, j * c:(j + 1) * c] for ref in (r_ref, lw_ref, k_ref, v_ref, kk_ref, b_ref))
            for j in range(sub) for hp in range(N_PAIRS)]
    ys, new_states = _scan_chunks(seqs, [s_ref[hp] for hp in range(N_PAIRS)], c, rep_ref)
    for j in range(sub):
        for hp in range(N_PAIRS):
            y_ref[0, hp, j * c:(j + 1) * c] = ys[j * N_PAIRS + hp]
    for hp in range(N_PAIRS):
        s_ref[hp] = new_states[hp]

    @pl.when(ci == pl.num_programs(1) - 1)
    def _():
        so_ref[0] = s_ref[...]


def _rwkv_scan(r, lw, k, v, kk, b, s0, B, L):
    c = min(CHUNK, L)
    nc = L // c
    sub = 2 if nc % 2 == 0 else 1
    nc //= sub
    seq = pl.BlockSpec((1, N_PAIRS, sub * c, LANES), lambda bi, ci: (bi, 0, ci, 0))
    st = pl.BlockSpec((1, N_PAIRS, LANES, LANES), lambda bi, ci: (bi, 0, 0, 0))
    vpu_diag = 2 * c == LANES
    extra_specs, extra_args = [], []
    if vpu_diag:
        extra_specs = [_resident((LANES, (DIAG - 1) * LANES), lambda bi, ci: (0, 0))]
        extra_args = [_replication_matrix(LANES)]
    return pl.pallas_call(
        functools.partial(_rwkv_scan_kernel, c=c, sub=sub, vpu_diag=vpu_diag),
        grid=(B, nc),
        in_specs=[seq] * 6 + [st] + extra_specs,
        out_specs=[seq, st],
        out_shape=[jax.ShapeDtypeStruct((B, N_PAIRS, L, LANES), F32),
                   jax.ShapeDtypeStruct((B, N_PAIRS, LANES, LANES), F32)],
        scratch_shapes=[pltpu.VMEM((N_PAIRS, LANES, LANES), F32)],
        compiler_params=_params(("parallel", "arbitrary")),
    )(r, lw, k, v, kk, b, s0, *extra_args)


def _rwkv_post_kernel(x_ref, y_ref, r_ref, k_ref, v_ref, g_ref, rk_ref, lw_ref, lb_ref, e_ref, et_ref,
                      wo_ref, o_ref, *, parts):
    rows = x_ref.shape[0] // parts
    sls = [slice(i * rows, (i + 1) * rows) for i in range(parts)]
    cat = lambda ref, sl: jnp.concatenate([ref[0, hp, sl] for hp in range(N_PAIRS)], axis=-1)
    head_sum = lambda t, split: _head_sum(t, e_ref, et_ref, split)
    ys = [cat(y_ref, sl) for sl in sls]
    ds = [y - head_sum(y, True) * (1.0 / RWKV_HEAD) for y in ys]
    bonus = [head_sum(cat(r_ref, sl) * cat(k_ref, sl) * rk_ref[...], False) * cat(v_ref, sl) for sl in sls]
    var = [head_sum(d * d, False) * (1.0 / RWKV_HEAD) for d in ds]
    yn = [d * lax.rsqrt(vr + LNX_EPS) * lw_ref[...] + lb_ref[...] for d, vr in zip(ds, var)]
    gated = [((n + bo) * g_ref[sl, :]).astype(BF16) for n, bo, sl in zip(yn, bonus, sls)]
    for gt, sl in zip(gated, sls):
        o_ref[sl, :] = x_ref[sl, :] + _dot(gt, wo_ref[...])


def _rwkv_post(x, y, r, k, v, g, p, i, B, L):
    tm = min(512, L)
    nt = L // tm
    vec = lambda: _resident((1, D_MODEL), lambda b, t: (0, 0))
    head_spec = pl.BlockSpec((1, N_PAIRS, tm, LANES), lambda b, t: (b, 0, t, 0))
    row = pl.BlockSpec((tm, D_MODEL), lambda b, t: (b * nt + t, 0))
    return pl.pallas_call(
        functools.partial(_rwkv_post_kernel, parts=2 if tm >= 256 else 1),
        grid=(B, nt),
        in_specs=[row, head_spec, head_spec, head_spec, head_spec, row, vec(), vec(), vec(),
                  _resident((D_MODEL, LANES), lambda b, t: (0, 0)),
                  _resident((LANES, D_MODEL), lambda b, t: (0, 0)),
                  _resident((None, D_MODEL, D_MODEL), lambda b, t: (i, 0, 0))],
        out_specs=row,
        out_shape=jax.ShapeDtypeStruct((B * L, D_MODEL), F32),
        compiler_params=_params(("parallel", "parallel")),
    )(x, y, r, k, v, g, p['rwkv_r_k'][i].reshape(1, D_MODEL), p['rwkv_lnx_w'][i][None], p['rwkv_lnx_b'][i][None],
      p['head_onehot'], p['head_onehot_t'], p['rwkv_wo'])


def _pack_state(s):
    B = s.shape[0]
    s = s.reshape(B, N_PAIRS, 2, RWKV_HEAD, RWKV_HEAD)
    z = jnp.zeros_like(s[:, :, 0])
    top = jnp.concatenate([s[:, :, 0], z], axis=-1)
    bot = jnp.concatenate([z, s[:, :, 1]], axis=-1)
    return jnp.concatenate([top, bot], axis=-2)


def _unpack_state(s):
    B = s.shape[0]
    return jnp.stack([s[:, :, :RWKV_HEAD, :RWKV_HEAD], s[:, :, RWKV_HEAD:, RWKV_HEAD:]], axis=2).reshape(
        B, RWKV_H, RWKV_HEAD, RWKV_HEAD)


def _trunk(x, pos0, caches, p):
    pool_c, k_c, v_c, shift_c, wkv_c = caches
    B, L, _ = x.shape
    depth = p['norm_mix'].shape[0]
    stepping = k_c is not None
    x = x.reshape(B * L, D_MODEL)
    if stepping:
        bkt = _bucket_index(L, SWA_ROWS + L, SWA_ROWS)
    else:
        bkt = _bucket_index(CHUNK, WINDOW + CHUNK, WINDOW)
    new_pool, new_k, new_v, new_shift, new_wkv = [], [], [], [], []
    for l in range(depth):
        i = l // 2
        x = _ffn(x, p['norm_ffn'][l, 0][None], p['ffn_w_gate'], p['ffn_w_up'], p['ffn_w_down'], l, 0)
        gn = p['norm_mix'][l][None]
        if l % 2 == 0:
            u, q, k, v = _even_in(x, gn, p['w_in_even'], i)
            if stepping:
                hist = jnp.pad(pool_c[i], ((0, 0), (POOL_HALO - POOL_HIST, 0), (0, 0)))
            else:
                hist = jnp.zeros((B, POOL_HALO, C_POOL), F32)
            pool_out = _pool(u, hist, p['pool_w'][i], p['pool_scale'][i][None], B, L, pos0)
            k3, v3 = k.reshape(B, L, KV_W), v.reshape(B, L, KV_W)
            if stepping:
                k3 = jnp.concatenate([k_c[i].reshape(B, SWA_ROWS, KV_W), k3], axis=1)
                v3 = jnp.concatenate([v_c[i].reshape(B, SWA_ROWS, KV_W), v3], axis=1)
                att = _attn_step(q, k3, v3, bkt, p['t5_table'], p['attn_sinks'][i], B, L)
            else:
                att = _attn_prompt(q, k, v, bkt, p['t5_table'], p['attn_sinks'][i], B, L)
            x = _even_out(x, pool_out, att, p['w_out_even'], i)
            full = jnp.concatenate([hist, u.reshape(B, L, C_POOL)], axis=1)
            new_pool.append(full[:, -POOL_HIST:])
            new_k.append(k3[:, -SWA_ROWS:].reshape(B, SWA_ROWS, N_KV_HEADS, HEAD_DIM))
            new_v.append(v3[:, -SWA_ROWS:].reshape(B, SWA_ROWS, N_KV_HEADS, HEAD_DIM))
        else:
            if stepping:
                shift_prev = shift_c[i][:, None, :]
                s0 = _pack_state(wkv_c[i])
            else:
                shift_prev = jnp.zeros((B, 1, D_MODEL), F32)
                s0 = jnp.zeros((B, N_PAIRS, LANES, LANES), F32)
            r, lw, k, v, kk, b, g, hs = _rwkv_pre(x, shift_prev, gn, p, i, B, L)
            y, s_new = _rwkv_scan(r, lw, k, v, kk, b, s0, B, L)
            x = _rwkv_post(x, y, r, k, v, g, p, i, B, L)
            new_shift.append(hs[:, 7])
            new_wkv.append(_unpack_state(s_new))
        final_g = p['norm_final'][None] if l == depth - 1 else None
        x = _ffn(x, p['norm_ffn'][l, 1][None], p['ffn_w_gate'], p['ffn_w_up'], p['ffn_w_down'], l, 1, final_g)
    return (x.reshape(B, L, D_MODEL), jnp.stack(new_pool), jnp.stack(new_k), jnp.stack(new_v),
            jnp.stack(new_shift), jnp.stack(new_wkv))


_MATMUL_WEIGHTS = ('ffn_w_gate', 'ffn_w_up', 'ffn_w_down', 'w_in_even', 'pool_w', 'w_out_even', 'rwkv_wr',
                   'rwkv_wk', 'rwkv_wv', 'rwkv_w1', 'rwkv_w2', 'rwkv_a1', 'rwkv_a2', 'rwkv_g1', 'rwkv_g2', 'rwkv_wo')


def _prepare(p):
    p = dict(p)
    for name in _MATMUL_WEIGHTS:
        p[name] = p[name].astype(BF16)
    onehot = (jnp.arange(D_MODEL)[:, None] // RWKV_HEAD == jnp.arange(LANES)[None, :]).astype(BF16)
    p['head_onehot'] = onehot
    p['head_onehot_t'] = onehot.T
    return p


def kernel(x_prompt, x_sample, cache_pool, cache_swa_k, cache_swa_v, state_shift, state_wkv, t5_table, norm_ffn, ffn_w_gate, ffn_w_up, ffn_w_down, norm_mix, w_in_even, pool_w, pool_scale, attn_sinks, w_out_even, rwkv_mu, rwkv_wr, rwkv_wk, rwkv_wv, rwkv_w0, rwkv_w1, rwkv_w2, rwkv_a0, rwkv_a1, rwkv_a2, rwkv_g1, rwkv_g2, rwkv_k_k, rwkv_k_a, rwkv_r_k, rwkv_lnx_w, rwkv_lnx_b, rwkv_wo, norm_final):
    p = _prepare(dict(
        t5_table=t5_table, norm_ffn=norm_ffn, ffn_w_gate=ffn_w_gate, ffn_w_up=ffn_w_up, ffn_w_down=ffn_w_down,
        norm_mix=norm_mix, w_in_even=w_in_even, pool_w=pool_w, pool_scale=pool_scale, attn_sinks=attn_sinks,
        w_out_even=w_out_even, rwkv_mu=rwkv_mu, rwkv_wr=rwkv_wr, rwkv_wk=rwkv_wk, rwkv_wv=rwkv_wv,
        rwkv_w0=rwkv_w0, rwkv_w1=rwkv_w1, rwkv_w2=rwkv_w2, rwkv_a0=rwkv_a0, rwkv_a1=rwkv_a1, rwkv_a2=rwkv_a2,
        rwkv_g1=rwkv_g1, rwkv_g2=rwkv_g2, rwkv_k_k=rwkv_k_k, rwkv_k_a=rwkv_k_a, rwkv_r_k=rwkv_r_k,
        rwkv_lnx_w=rwkv_lnx_w, rwkv_lnx_b=rwkv_lnx_b, rwkv_wo=rwkv_wo, norm_final=norm_final))
    y_p, pool_p, k_p, v_p, shift_p, wkv_p = _trunk(x_prompt, 0, (None, None, None, None, None), p)
    y_s, pool_s, k_s, v_s, shift_s, wkv_s = _trunk(
        x_sample, PAST_LEN, (cache_pool, cache_swa_k, cache_swa_v, state_shift, state_wkv), p)
    return (y_p, y_s, pool_p, pool_s, k_p, k_s, v_p, v_s, shift_p, shift_s, wkv_p, wkv_s)
```

```python
import functools
import math

import jax
import jax.numpy as jnp
from jax import lax
from jax.experimental import pallas as pl
from jax.experimental.pallas import tpu as pltpu

F32 = jnp.float32
BF16 = jnp.bfloat16

D_MODEL = 1024
D_FF = 2816
NORM_EPS = 1e-6
CHUNK = 64
POOL_WINDOWS = (2, 4, 8, 16)
C_POOL = 512
POOL_GC = 128
POOL_HIST = 15
POOL_HALO = 16
HEAD_DIM = 64
N_Q_HEADS = 8
N_KV_HEADS = 2
GQA_GROUP = 4
WINDOW = 128
SWA_ROWS = 128
Q_W = 512
KV_W = 128
IN_EVEN = C_POOL + Q_W + 2 * KV_W
N_BUCKETS = 32
MAX_DISTANCE = 128
RWKV_HEAD = 64
RWKV_H = 16
N_PAIRS = RWKV_H // 2
LANES = 128
LNX_EPS = 64e-5
PAST_LEN = 4096
VMEM_LIMIT_BYTES = 56 * 1024 * 1024


def _params(sem):
    return pltpu.CompilerParams(dimension_semantics=sem, vmem_limit_bytes=VMEM_LIMIT_BYTES)


def _dot(a, b):
    return jnp.dot(a, b, preferred_element_type=F32)


def _dot_nt(a, b):
    return lax.dot_general(a, b, (((1,), (1,)), ((), ())), preferred_element_type=F32)


def _dot_tn(a, b):
    return lax.dot_general(a, b, (((0,), (0,)), ((), ())), preferred_element_type=F32)


def _rms(x, g):
    return x * lax.rsqrt(jnp.mean(x * x, axis=-1, keepdims=True) + NORM_EPS) * g


def _split(x):
    hi = x.astype(BF16)
    lo = (x - hi.astype(F32)).astype(BF16)
    return hi, lo


def _head_sum(x, e_ref, et_ref, split=True):
    if not split:
        return _dot(_dot(x.astype(BF16), e_ref[...]).astype(BF16), et_ref[...])
    hi, lo = _split(x)
    s = _dot(hi, e_ref[...]) + _dot(lo, e_ref[...])
    shi, slo = _split(s)
    return _dot(shi, et_ref[...]) + _dot(slo, et_ref[...])


def _ffn_kernel(x_ref, g_ref, wg_ref, wu_ref, wd_ref, *rest, f_chunk, final, convert):
    rest = list(rest)
    gf_ref = rest.pop(0) if final else None
    src_refs = [rest.pop(0) for _ in range(3)] if convert else []
    o_ref = rest.pop(0)
    dst_refs = [rest.pop(0) for _ in range(3)] if convert else []
    acc_ref, = rest
    for src, dst in zip(src_refs, dst_refs):
        dst[...] = src[...].astype(BF16)
    x = x_ref[...]
    h = _rms(x, g_ref[...]).astype(BF16)
    for j in range(D_FF // f_chunk):
        sl = slice(j * f_chunk, (j + 1) * f_chunk)
        gate = _dot(h, wg_ref[:, sl])
        up = _dot(h, wu_ref[:, sl])
        act = (gate * jax.nn.sigmoid(gate) * up).astype(BF16)
        part = _dot(act, wd_ref[sl, :])
        if j == 0:
            acc_ref[...] = part
        else:
            acc_ref[...] += part
    y = x + 0.5 * acc_ref[...]
    if final:
        y = _rms(y, gf_ref[...])
    o_ref[...] = y


def _resident(shape, index_map):
    return pl.BlockSpec(shape, index_map, pipeline_mode=pl.Buffered(1))


BF16_SUBLANES = 16


def _slab_steps(rows, steps):
    return next(k for k in range(steps, 0, -1) if rows % (k * BF16_SUBLANES) == 0)


def _ffn(x, g, weights, final_g=None, convert=None):
    T = x.shape[0]
    tm = min(512, T)
    steps = T // tm
    final = final_g is not None
    in_specs = [pl.BlockSpec((tm, D_MODEL), lambda i: (i, 0)), _resident((1, D_MODEL), lambda i: (0, 0))]
    in_specs += [_resident(w.shape, lambda i: (0, 0)) for w in weights]
    args = [x, g, *weights]
    out_specs = [pl.BlockSpec((tm, D_MODEL), lambda i: (i, 0))]
    out_shape = [jax.ShapeDtypeStruct((T, D_MODEL), F32)]
    if final:
        in_specs.append(_resident((1, D_MODEL), lambda i: (0, 0)))
        args.append(final_g)
    if convert is not None:
        stacks, l, j = convert
        for w in stacks:
            rows, cols = w.shape[2:]
            k = _slab_steps(rows, steps)
            in_specs.append(pl.BlockSpec((None, None, rows // k, cols),
                                         lambda i, k=k: (l, j, jnp.minimum(i, k - 1), 0)))
            out_specs.append(pl.BlockSpec((rows // k, cols), lambda i, k=k: (jnp.minimum(i, k - 1), 0)))
            out_shape.append(jax.ShapeDtypeStruct((rows, cols), BF16))
            args.append(w)
    outs = pl.pallas_call(
        functools.partial(_ffn_kernel, f_chunk=256, final=final, convert=convert is not None),
        grid=(steps,),
        in_specs=in_specs,
        out_specs=out_specs,
        out_shape=out_shape,
        scratch_shapes=[pltpu.VMEM((tm, D_MODEL), F32)],
        compiler_params=_params(("arbitrary",)),
    )(*args)
    return outs[0], tuple(outs[1:])


def _even_in_kernel(x_ref, g_ref, w_ref, u_ref, q_ref, k_ref, v_ref):
    h = _rms(x_ref[...], g_ref[...]).astype(BF16)
    z = _dot(h, w_ref[...])
    u_ref[...] = z[:, :C_POOL]
    q_ref[...] = z[:, C_POOL:C_POOL + Q_W].astype(BF16)
    k_ref[...] = z[:, C_POOL + Q_W:C_POOL + Q_W + KV_W]
    v_ref[...] = z[:, C_POOL + Q_W + KV_W:]


def _even_in(x, g, w_in, i):
    T = x.shape[0]
    tm = min(512, T)
    row = lambda n: pl.BlockSpec((tm, n), lambda t: (t, 0))
    return pl.pallas_call(
        _even_in_kernel,
        grid=(T // tm,),
        in_specs=[row(D_MODEL), _resident((1, D_MODEL), lambda t: (0, 0)),
                  _resident((None, D_MODEL, IN_EVEN), lambda t: (i, 0, 0))],
        out_specs=[row(C_POOL), row(Q_W), row(KV_W), row(KV_W)],
        out_shape=[jax.ShapeDtypeStruct((T, C_POOL), F32), jax.ShapeDtypeStruct((T, Q_W), BF16),
                   jax.ShapeDtypeStruct((T, KV_W), F32), jax.ShapeDtypeStruct((T, KV_W), F32)],
        compiler_params=_params(("parallel",)),
    )(x, g, w_in)


def _pool_kernel(u_ref, halo_ref, hist_ref, pw_ref, ps_ref, o_ref, *, tt, pos0):
    t = pl.program_id(1)
    halo = jnp.where(t == 0, hist_ref[0], halo_ref[...])
    u = u_ref[...]
    ext = jnp.concatenate([halo, u], axis=0)
    s2 = ext + pltpu.roll(ext, 1, 0)
    s4 = s2 + pltpu.roll(s2, 2, 0)
    s8 = s4 + pltpu.roll(s4, 4, 0)
    s16 = s8 + pltpu.roll(s8, 8, 0)
    pos = pos0 + t * tt + lax.broadcasted_iota(jnp.int32, (tt, 1), 0)
    outs = []
    for gi, (w, s) in enumerate(zip(POOL_WINDOWS, (s2, s4, s8, s16))):
        sl = slice(gi * POOL_GC, (gi + 1) * POOL_GC)
        cnt = jnp.minimum(w, pos + 1).astype(F32)
        pooled = s[POOL_HALO:, sl] / cnt - u[:, sl]
        outs.append(_dot(pooled.astype(BF16), pw_ref[gi]))
    o_ref[...] = (jnp.concatenate(outs, axis=-1) * ps_ref[...]).astype(BF16)


def _pool(u, hist, pool_w, pool_scale, B, L, pos0):
    tt = min(512, L)
    nt = L // tt
    hb = tt // POOL_HALO
    return pl.pallas_call(
        functools.partial(_pool_kernel, tt=tt, pos0=pos0),
        grid=(B, nt),
        in_specs=[
            pl.BlockSpec((tt, C_POOL), lambda b, t: (b * nt + t, 0)),
            pl.BlockSpec((POOL_HALO, C_POOL), lambda b, t: (jnp.maximum((b * nt + t) * hb - 1, 0), 0)),
            pl.BlockSpec((1, POOL_HALO, C_POOL), lambda b, t: (b, 0, 0)),
            _resident((len(POOL_WINDOWS), POOL_GC, POOL_GC), lambda b, t: (0, 0, 0)),
            _resident((1, C_POOL), lambda b, t: (0, 0)),
        ],
        out_specs=pl.BlockSpec((tt, C_POOL), lambda b, t: (b * nt + t, 0)),
        out_shape=jax.ShapeDtypeStruct((B * L, C_POOL), BF16),
        compiler_params=_params(("parallel", "arbitrary")),
    )(u, u, hist, pool_w, pool_scale)


def _build_bias(bkt_ref, tab_ref, bias_ref, lq):
    bkt = bkt_ref[...]
    for h in range(N_Q_HEADS):
        b = jnp.zeros(bkt.shape, F32)
        for n in range(N_BUCKETS):
            b = jnp.where(bkt == n, tab_ref[n, h], b)
        g, i = divmod(h, GQA_GROUP)
        bias_ref[g, :, i * lq:(i + 1) * lq] = b


def _group_queries(q, g):
    return jnp.concatenate([q[:, h * HEAD_DIM:(h + 1) * HEAD_DIM]
                            for h in range(g * GQA_GROUP, (g + 1) * GQA_GROUP)], axis=0) * (HEAD_DIM ** -0.5)


def _sink_rows(sink_ref, lq):
    lane = lax.broadcasted_iota(jnp.int32, (1, GQA_GROUP * lq), 1)
    rows = []
    for g in range(N_KV_HEADS):
        r = jnp.zeros((1, GQA_GROUP * lq), F32)
        for i in range(GQA_GROUP):
            r = jnp.where(lane // lq == i, sink_ref[g * GQA_GROUP + i], r)
        rows.append(r)
    return rows


def _attn_core(qs, ks, vs, biases, sinks, valids):
    ss = [_dot_nt(k, q) + b for q, k, b in zip(qs, ks, biases)]
    ss = [s if ok is None else jnp.where(ok, s, -1e30) for s, ok in zip(ss, valids)]
    ms = [jnp.maximum(jnp.max(s, axis=0, keepdims=True), sk) for s, sk in zip(ss, sinks)]
    ps = [jnp.exp(s - m) for s, m in zip(ss, ms)]
    invs = [1.0 / (jnp.sum(p, axis=0, keepdims=True) + jnp.exp(sk - m)) for p, sk, m in zip(ps, sinks, ms)]
    return [_dot_tn((p * r).astype(BF16), v) for p, r, v in zip(ps, invs, vs)]


def _attn_band_kernel(q_ref, kp_ref, ko_ref, vp_ref, vo_ref, bkt_ref, tab_ref, sink_ref, o_ref, bias_ref, *, cps):
    @pl.when(jnp.logical_and(pl.program_id(0) == 0, pl.program_id(1) == 0))
    def _():
        _build_bias(bkt_ref, tab_ref, bias_ref, CHUNK)

    lk = WINDOW + CHUNK
    k = jnp.concatenate([kp_ref[...], ko_ref[...]], axis=0).astype(BF16)
    v = jnp.concatenate([vp_ref[...], vo_ref[...]], axis=0).astype(BF16)
    q = q_ref[...]
    sink_cols = _sink_rows(sink_ref, CHUNK)
    first_pos = pl.program_id(1) * (cps * CHUNK) - WINDOW + lax.broadcasted_iota(jnp.int32, (lk, 1), 0)
    qs, ks, vs, biases, sinks, valids = [], [], [], [], [], []
    for j in range(cps):
        qj = q[j * CHUNK:(j + 1) * CHUNK]
        for g in range(N_KV_HEADS):
            qs.append(_group_queries(qj, g))
            ks.append(k[j * CHUNK:j * CHUNK + lk, g * HEAD_DIM:(g + 1) * HEAD_DIM])
            vs.append(v[j * CHUNK:j * CHUNK + lk, g * HEAD_DIM:(g + 1) * HEAD_DIM])
            biases.append(bias_ref[g])
            sinks.append(sink_cols[g])
            valids.append(first_pos + j * CHUNK >= 0)
    outs = _attn_core(qs, ks, vs, biases, sinks, valids)
    for j in range(cps):
        heads = [outs[j * N_KV_HEADS + g][i * CHUNK:(i + 1) * CHUNK]
                 for g in range(N_KV_HEADS) for i in range(GQA_GROUP)]
        o_ref[j * CHUNK:(j + 1) * CHUNK, :] = jnp.concatenate(heads, axis=-1).astype(BF16)


def _attn_step_kernel(q_ref, k_ref, v_ref, bkt_ref, tab_ref, sink_ref, o_ref, bias_ref, *, lq):
    @pl.when(pl.program_id(0) == 0)
    def _():
        _build_bias(bkt_ref, tab_ref, bias_ref, lq)

    k = k_ref[0].astype(BF16)
    v = v_ref[0].astype(BF16)
    q = q_ref[...]
    groups = range(N_KV_HEADS)
    outs = _attn_core([_group_queries(q, g) for g in groups],
                      [k[:, g * HEAD_DIM:(g + 1) * HEAD_DIM] for g in groups],
                      [v[:, g * HEAD_DIM:(g + 1) * HEAD_DIM] for g in groups],
                      [bias_ref[g] for g in groups], _sink_rows(sink_ref, lq), [None] * N_KV_HEADS)
    heads = [outs[g][i * lq:(i + 1) * lq] for g in groups for i in range(GQA_GROUP)]
    o_ref[...] = jnp.concatenate(heads, axis=-1).astype(BF16)


def _smem():
    return pl.BlockSpec(memory_space=pltpu.SMEM)


def _attn_prompt(q, k, v, bkt, table, sinks, B, L):
    nc = L // CHUNK
    cps = 4 if nc % 4 == 0 else 2
    assert nc % cps == 0
    rows = cps * CHUNK
    ns = nc // cps
    lk = WINDOW + CHUNK
    own = lambda n: pl.BlockSpec((rows, n), lambda b, s: (b * ns + s, 0))
    prev = pl.BlockSpec((WINDOW, KV_W), lambda b, s: (jnp.maximum((b * ns + s) * (rows // WINDOW) - 1, 0), 0))
    return pl.pallas_call(
        functools.partial(_attn_band_kernel, cps=cps),
        grid=(B, ns),
        in_specs=[own(Q_W), prev, own(KV_W), prev, own(KV_W),
                  _resident((lk, CHUNK), lambda b, s: (0, 0)), _smem(), _smem()],
        out_specs=own(Q_W),
        out_shape=jax.ShapeDtypeStruct((B * L, Q_W), BF16),
        scratch_shapes=[pltpu.VMEM((N_KV_HEADS, lk, GQA_GROUP * CHUNK), F32)],
        compiler_params=_params(("arbitrary", "arbitrary")),
    )(q, k, k, v, v, bkt, table, sinks)


def _attn_step(q, k_all, v_all, bkt, table, sinks, B, L):
    lk = k_all.shape[1]
    return pl.pallas_call(
        functools.partial(_attn_step_kernel, lq=L),
        grid=(B,),
        in_specs=[pl.BlockSpec((L, Q_W), lambda b: (b, 0)),
                  pl.BlockSpec((1, lk, KV_W), lambda b: (b, 0, 0)),
                  pl.BlockSpec((1, lk, KV_W), lambda b: (b, 0, 0)),
                  _resident((lk, L), lambda b: (0, 0)), _smem(), _smem()],
        out_specs=pl.BlockSpec((L, Q_W), lambda b: (b, 0)),
        out_shape=jax.ShapeDtypeStruct((B * L, Q_W), BF16),
        scratch_shapes=[pltpu.VMEM((N_KV_HEADS, lk, GQA_GROUP * L), F32)],
        compiler_params=_params(("arbitrary",)),
    )(q, k_all, v_all, bkt, table, sinks)


def _even_out_kernel(x_ref, p_ref, a_ref, wp_ref, wa_ref, o_ref):
    o_ref[...] = x_ref[...] + _dot(p_ref[...], wp_ref[...]) + _dot(a_ref[...], wa_ref[...])


def _even_out(x, pool_out, att, w_out, i):
    T = x.shape[0]
    tm = min(512, T)
    row = lambda n: pl.BlockSpec((tm, n), lambda t: (t, 0))
    return pl.pallas_call(
        _even_out_kernel,
        grid=(T // tm,),
        in_specs=[row(D_MODEL), row(C_POOL), row(Q_W),
                  _resident((None, C_POOL, D_MODEL), lambda t: (i, 0, 0)),
                  _resident((None, Q_W, D_MODEL), lambda t: (i, 1, 0))],
        out_specs=row(D_MODEL),
        out_shape=jax.ShapeDtypeStruct((T, D_MODEL), F32),
        compiler_params=_params(("parallel",)),
    )(x, pool_out, att, w_out, w_out)


def _t5_bucket(rel):
    half = N_BUCKETS // 2
    max_exact = half // 2
    side = jnp.where(rel > 0, half, 0)
    n = jnp.abs(rel)
    nf = jnp.maximum(n, max_exact).astype(F32)
    large = max_exact + (jnp.log(nf / max_exact) / math.log(MAX_DISTANCE / max_exact)
                         * (half - max_exact)).astype(jnp.int32)
    large = jnp.minimum(large, half - 1)
    return side + jnp.where(n < max_exact, n, large)


def _bucket_index(lq, lk, offset):
    rel = jnp.arange(lk)[:, None] - offset - jnp.arange(lq)[None, :]
    return _t5_bucket(rel).astype(jnp.int32)


def _rwkv_pre_kernel(x_ref, xp_ref, sh_ref, gn_ref, mu_ref, wr_ref, wk_ref, wv_ref, w1_ref, w2_ref,
                     a1_ref, a2_ref, g1_ref, g2_ref, w0_ref, a0_ref, kk_ref, ka_ref, rk_ref, e_ref, et_ref,
                     r_o, lw_o, k_o, v_o, kk_o, b_o, g_o, bonus_o, hs_o, *, tm):
    t = pl.program_id(1)
    gn = gn_ref[...]
    h = _rms(x_ref[...], gn)
    prev_tile_last = _rms(xp_ref[...], gn)[7:8]
    first_prev = jnp.where(t == 0, sh_ref[0], prev_tile_last)
    row = lax.broadcasted_iota(jnp.int32, (tm, 1), 0)
    h_prev = jnp.where(row == 0, first_prev, pltpu.roll(h, 1, 0))
    xx = h_prev - h

    hb = h.astype(BF16)
    xb = xx.astype(BF16)
    mu = mu_ref[...].astype(BF16)

    def mix(j):
        return hb + xb * mu[j:j + 1]

    zw = w0_ref[...] + _dot(jnp.tanh(_dot(mix(1), w1_ref[...])).astype(BF16), w2_ref[...])
    za = a0_ref[...] + _dot(_dot(mix(4), a1_ref[...]).astype(BF16), a2_ref[...])
    g = _dot(jax.nn.sigmoid(_dot(mix(5), g1_ref[...])).astype(BF16), g2_ref[...])
    r = _dot(mix(0), wr_ref[...])
    lw = -math.exp(-0.5) * jax.nn.sigmoid(zw)
    k = _dot(mix(2), wk_ref[...])
    a = jax.nn.sigmoid(za)
    v = _dot(mix(3), wv_ref[...])
    kk = k * kk_ref[...]
    kk = kk * lax.rsqrt(jnp.maximum(_head_sum(kk * kk, e_ref, et_ref, split=False), 1e-24))
    k = k * (1.0 + (a - 1.0) * ka_ref[...])
    b = kk * a
    for hp in range(N_PAIRS):
        sl = slice(hp * LANES, (hp + 1) * LANES)
        r_o[0, hp] = r[:, sl]
        lw_o[0, hp] = lw[:, sl]
        k_o[0, hp] = k[:, sl]
        v_o[0, hp] = v[:, sl]
        kk_o[0, hp] = kk[:, sl]
        b_o[0, hp] = b[:, sl]
    g_o[...] = g.astype(BF16)
    bonus_o[...] = (_head_sum(r * k * rk_ref[...], e_ref, et_ref, split=False) * v).astype(BF16)
    hs_o[0] = h[tm - 8:]


def _rwkv_pre(x, shift_prev, gn, p, i, B, L):
    tm = min(512, L)
    nt = L // tm
    vec = lambda: _resident((1, D_MODEL), lambda b, t: (0, 0))
    mat = lambda r, c: _resident((None, r, c), lambda b, t: (i, 0, 0))
    lora = p['rwkv_w1'].shape[-1], p['rwkv_a1'].shape[-1], p['rwkv_g1'].shape[-1]
    head_spec = pl.BlockSpec((1, N_PAIRS, tm, LANES), lambda b, t: (b, 0, t, 0))
    head_shape = jax.ShapeDtypeStruct((B, N_PAIRS, L, LANES), F32)
    return pl.pallas_call(
        functools.partial(_rwkv_pre_kernel, tm=tm),
        grid=(B, nt),
        in_specs=[
            pl.BlockSpec((tm, D_MODEL), lambda b, t: (b * nt + t, 0)),
            pl.BlockSpec((8, D_MODEL), lambda b, t: (jnp.maximum((b * nt + t) * (tm // 8) - 1, 0), 0)),
            pl.BlockSpec((1, 1, D_MODEL), lambda b, t: (b, 0, 0)),
            vec(),
            _resident((None, 6, D_MODEL), lambda b, t: (i, 0, 0)),
            mat(D_MODEL, D_MODEL), mat(D_MODEL, D_MODEL), mat(D_MODEL, D_MODEL),
            mat(D_MODEL, lora[0]), mat(lora[0], D_MODEL),
            mat(D_MODEL, lora[1]), mat(lora[1], D_MODEL),
            mat(D_MODEL, lora[2]), mat(lora[2], D_MODEL),
            vec(), vec(), vec(), vec(), vec(),
            _resident((D_MODEL, LANES), lambda b, t: (0, 0)),
            _resident((LANES, D_MODEL), lambda b, t: (0, 0)),
        ],
        out_specs=[head_spec] * 6 + [
            pl.BlockSpec((tm, D_MODEL), lambda b, t: (b * nt + t, 0)),
            pl.BlockSpec((tm, D_MODEL), lambda b, t: (b * nt + t, 0)),
            pl.BlockSpec((1, 8, D_MODEL), lambda b, t: (b, 0, 0)),
        ],
        out_shape=[head_shape] * 6 + [
            jax.ShapeDtypeStruct((B * L, D_MODEL), BF16),
            jax.ShapeDtypeStruct((B * L, D_MODEL), BF16),
            jax.ShapeDtypeStruct((B, 8, D_MODEL), F32),
        ],
        compiler_params=_params(("parallel", "arbitrary")),
    )(x, x, shift_prev, gn, p['rwkv_mu'], p['rwkv_wr'], p['rwkv_wk'], p['rwkv_wv'],
      p['rwkv_w1'], p['rwkv_w2'], p['rwkv_a1'], p['rwkv_a2'], p['rwkv_g1'], p['rwkv_g2'],
      p['rwkv_w0'][i][None], p['rwkv_a0'][i][None], p['rwkv_k_k'][i][None], p['rwkv_k_a'][i][None],
      p['rwkv_r_k'][i].reshape(1, D_MODEL), p['head_onehot'], p['head_onehot_t'])


DIAG = 8


def _replication_matrix(n):
    src = jnp.arange(n)[:, None]
    dst = jnp.arange(n)[None, :]
    return jnp.concatenate([(src == (dst // DIAG) * DIAG + s) for s in range(DIAG - 1)], axis=1).astype(BF16)


def _unit_lower_inverses(lows, c, rep_ref):
    n = 2 * c
    ti = lax.broadcasted_iota(jnp.int32, (c, n), 0)
    li = lax.broadcasted_iota(jnp.int32, (c, n), 1)
    si = li & (c - 1)
    head1 = li >= c

    def block_diag(x):
        return jnp.concatenate([jnp.where(head1, 0.0, x), jnp.where(head1, x, 0.0)], axis=0)

    if rep_ref is None:
        invs = [jnp.where(ti == si, 1.0, 0.0) + jnp.where(jnp.logical_and(ti == si + 1, (ti & 1) == 1), low, 0.0)
                for low in lows]
        m = 2
    else:
        pt = lax.broadcasted_iota(jnp.int32, (DIAG, n), 0)
        pl_ = lax.broadcasted_iota(jnp.int32, (DIAG, n), 1)
        blk = (pl_ & (c - 1)) // DIAG
        packed = []
        for low in lows:
            d = jnp.zeros((DIAG, n), F32)
            for i in range(c // DIAG):
                d = jnp.where(blk == i, low[i * DIAG:(i + 1) * DIAG], d)
            packed.append(d)
        rep = _dot(jnp.concatenate(packed, axis=0).astype(BF16), rep_ref[...])
        sols = [jnp.where(pt == (pl_ & (DIAG - 1)), 1.0, 0.0) for _ in lows]
        for s in range(DIAG - 1):
            sols = [sol + rep[p * DIAG:(p + 1) * DIAG, s * n:(s + 1) * n] * sol[s:s + 1]
                    for p, sol in enumerate(sols)]
        invs = [jnp.concatenate([jnp.where(blk == i, sol, 0.0) for i in range(c // DIAG)], axis=0) for sol in sols]
        m = DIAG
    while m < c:
        sh = m.bit_length() - 1
        sub = jnp.logical_and(jnp.logical_and((ti >> (sh + 1)) == (si >> (sh + 1)), ((ti >> sh) & 1) == 1),
                              ((si >> sh) & 1) == 0)
        diag = [block_diag(inv).astype(BF16) for inv in invs]
        half = [_dot(inv.astype(BF16), block_diag(jnp.where(sub, low, 0.0)).astype(BF16)).astype(BF16)
                for inv, low in zip(invs, lows)]
        invs = [inv + _dot(h, d) for inv, h, d in zip(invs, half, diag)]
        m *= 2
    return invs


def _scan_chunks(seqs, states, c, rep_ref):
    n = 2 * c
    row = lax.broadcasted_iota(jnp.int32, (c, LANES), 0)
    head1 = lax.broadcasted_iota(jnp.int32, (c, LANES), 1) >= RWKV_HEAD
    ti = lax.broadcasted_iota(jnp.int32, (c, n), 0)
    si = lax.broadcasted_iota(jnp.int32, (c, n), 1) & (c - 1)
    strict = ti > si
    incl = ti >= si
    pi = lax.broadcasted_iota(jnp.int32, (LANES, LANES), 0)
    pj = lax.broadcasted_iota(jnp.int32, (LANES, LANES), 1)
    same_head = (pi >= RWKV_HEAD) == (pj >= RWKV_HEAD)

    def stack(x):
        return jnp.concatenate([jnp.where(head1, 0.0, x), jnp.where(head1, x, 0.0)], axis=0)

    lhs, rhs, v_st, v_bf, bk_end, decay = [], [], [], [], [], []
    for r, lw, k, v, kk, b in seqs:
        cum = lw
        sh = 1
        while sh < c:
            cum = cum + jnp.where(row >= sh, pltpu.roll(cum, sh, 0), 0.0)
            sh *= 2
        tot = cum[c - 1:c]
        grow = jnp.exp(-cum)
        tail = jnp.exp(tot - cum)
        a_t = -kk * jnp.exp(cum - lw)
        r_t = r * jnp.exp(cum)
        lhs.append(jnp.concatenate([a_t, r_t], axis=0).astype(BF16))
        rhs.append(jnp.concatenate([stack(b * grow), stack(k * grow)], axis=0).astype(BF16))
        v_st.append(stack(v).astype(BF16))
        v_bf.append(v.astype(BF16))
        bk_end.append(jnp.concatenate([b * tail, k * tail], axis=0).astype(BF16))
        decay.append(jnp.exp(tot))
    grams = [_dot_nt(x, y) for x, y in zip(lhs, rhs)]
    lows = [jnp.where(strict, g[:c, :n], 0.0) for g in grams]
    m_ak = [jnp.where(strict, g[:c, n:], 0.0).astype(BF16) for g in grams]
    m_r = [jnp.concatenate([jnp.where(incl, g[c:, :n], 0.0), jnp.where(incl, g[c:, n:], 0.0)], axis=1).astype(BF16)
           for g in grams]
    invs = [inv.astype(BF16) for inv in _unit_lower_inverses(lows, c, rep_ref)]
    from_v = [_dot(m, vs) for m, vs in zip(m_ak, v_st)]

    ys = []
    states = list(states)
    npair = len(states)
    for j in range(len(seqs) // npair):
        sl = slice(j * npair, (j + 1) * npair)
        from_state = [_dot_nt(x, s.astype(BF16)) for x, s in zip(lhs[sl], states)]
        rhs_sa = [stack(f[:c] + fv).astype(BF16) for f, fv in zip(from_state, from_v[sl])]
        sas = [_dot(inv, x) for inv, x in zip(invs[sl], rhs_sa)]
        ys += [f[c:] + _dot(m, jnp.concatenate([stack(sa).astype(BF16), vs], axis=0))
               for f, m, sa, vs in zip(from_state, m_r[sl], sas, v_st[sl])]
        upd = [_dot_tn(jnp.concatenate([sa.astype(BF16), vb], axis=0), be)
               for sa, vb, be in zip(sas, v_bf[sl], bk_end[sl])]
        states = [s * d + jnp.where(same_head, u, 0.0) for s, d, u in zip(states, decay[sl], upd)]
    return ys, states


def _rwkv_scan_kernel(r_ref, lw_ref, k_ref, v_ref, kk_ref, b_ref, s0_ref, *rest, c, sub, vpu_diag):
    if vpu_diag:
        rep_ref, y_ref, so_ref, s_ref = rest
    else:
        rep_ref = None
        y_ref, so_ref, s_ref = rest
    ci = pl.program_id(1)

    @pl.when(ci == 0)
    def _():
        s_ref[...] = s0_ref[0]

    seqs = [tuple(ref[0, hp, j * c:(j + 1) * c] for ref in (r_ref, lw_ref, k_ref, v_ref, kk_ref, b_ref))
            for j in range(sub) for hp in range(N_PAIRS)]
    ys, new_states = _scan_chunks(seqs, [s_ref[hp] for hp in range(N_PAIRS)], c, rep_ref)
    for j in range(sub):
        for hp in range(N_PAIRS):
            y_ref[0, hp, j * c:(j + 1) * c] = ys[j * N_PAIRS + hp]
    for hp in range(N_PAIRS):
        s_ref[hp] = new_states[hp]

    @pl.when(ci == pl.num_programs(1) - 1)
    def _():
        so_ref[0] = s_ref[...]


def _rwkv_scan(r, lw, k, v, kk, b, s0, B, L):
    c = min(CHUNK, L)
    nc = L // c
    sub = 2 if nc % 2 == 0 else 1
    nc //= sub
    seq = pl.BlockSpec((1, N_PAIRS, sub * c, LANES), lambda bi, ci: (bi, 0, ci, 0))
    st = pl.BlockSpec((1, N_PAIRS, LANES, LANES), lambda bi, ci: (bi, 0, 0, 0))
    vpu_diag = 2 * c == LANES
    extra_specs, extra_args = [], []
    if vpu_diag:
        extra_specs = [_resident((LANES, (DIAG - 1) * LANES), lambda bi, ci: (0, 0))]
        extra_args = [_replication_matrix(LANES)]
    return pl.pallas_call(
        functools.partial(_rwkv_scan_kernel, c=c, sub=sub, vpu_diag=vpu_diag),
        grid=(B, nc),
        in_specs=[seq] * 6 + [st] + extra_specs,
        out_specs=[seq, st],
        out_shape=[jax.ShapeDtypeStruct((B, N_PAIRS, L, LANES), F32),
                   jax.ShapeDtypeStruct((B, N_PAIRS, LANES, LANES), F32)],
        scratch_shapes=[pltpu.VMEM((N_PAIRS, LANES, LANES), F32)],
        compiler_params=_params(("parallel", "arbitrary")),
    )(r, lw, k, v, kk, b, s0, *extra_args)


def _rwkv_post_kernel(x_ref, y_ref, g_ref, bonus_ref, lw_ref, lb_ref, e_ref, et_ref, wo_ref, o_ref, *, parts):
    rows = x_ref.shape[0] // parts
    sls = [slice(i * rows, (i + 1) * rows) for i in range(parts)]
    head_sum = lambda t, split: _head_sum(t, e_ref, et_ref, split)
    ys = [jnp.concatenate([y_ref[0, hp, sl] for hp in range(N_PAIRS)], axis=-1) for sl in sls]
    ds = [y - head_sum(y, True) * (1.0 / RWKV_HEAD) for y in ys]
    var = [head_sum(d * d, False) * (1.0 / RWKV_HEAD) for d in ds]
    yn = [d * lax.rsqrt(vr + LNX_EPS) * lw_ref[...] + lb_ref[...] for d, vr in zip(ds, var)]
    gated = [((n + bonus_ref[sl, :].astype(F32)) * g_ref[sl, :].astype(F32)).astype(BF16) for n, sl in zip(yn, sls)]
    for gt, sl in zip(gated, sls):
        o_ref[sl, :] = x_ref[sl, :] + _dot(gt, wo_ref[...])


def _rwkv_post(x, y, g, bonus, p, i, B, L):
    tm = min(512, L)
    nt = L // tm
    vec = lambda: _resident((1, D_MODEL), lambda b, t: (0, 0))
    head_spec = pl.BlockSpec((1, N_PAIRS, tm, LANES), lambda b, t: (b, 0, t, 0))
    row = pl.BlockSpec((tm, D_MODEL), lambda b, t: (b * nt + t, 0))
    return pl.pallas_call(
        functools.partial(_rwkv_post_kernel, parts=2 if tm >= 256 else 1),
        grid=(B, nt),
        in_specs=[row, head_spec, row, row, vec(), vec(),
                  _resident((D_MODEL, LANES), lambda b, t: (0, 0)),
                  _resident((LANES, D_MODEL), lambda b, t: (0, 0)),
                  _resident((None, D_MODEL, D_MODEL), lambda b, t: (i, 0, 0))],
        out_specs=row,
        out_shape=jax.ShapeDtypeStruct((B * L, D_MODEL), F32),
        compiler_params=_params(("parallel", "parallel")),
    )(x, y, g, bonus, p['rwkv_lnx_w'][i][None], p['rwkv_lnx_b'][i][None],
      p['head_onehot'], p['head_onehot_t'], p['rwkv_wo'])


def _pack_state(s):
    B = s.shape[0]
    s = s.reshape(B, N_PAIRS, 2, RWKV_HEAD, RWKV_HEAD)
    z = jnp.zeros_like(s[:, :, 0])
    top = jnp.concatenate([s[:, :, 0], z], axis=-1)
    bot = jnp.concatenate([z, s[:, :, 1]], axis=-1)
    return jnp.concatenate([top, bot], axis=-2)


def _unpack_state(s):
    B = s.shape[0]
    return jnp.stack([s[:, :, :RWKV_HEAD, :RWKV_HEAD], s[:, :, RWKV_HEAD:, RWKV_HEAD:]], axis=2).reshape(
        B, RWKV_H, RWKV_HEAD, RWKV_HEAD)


def _trunk(x, pos0, caches, p, ffn_weights=None):
    pool_c, k_c, v_c, shift_c, wkv_c = caches
    B, L, _ = x.shape
    depth = p['norm_mix'].shape[0]
    stepping = k_c is not None
    x = x.reshape(B * L, D_MODEL)
    stacks = (p['ffn_w_gate'], p['ffn_w_up'], p['ffn_w_down'])
    chain = ffn_weights is None
    if chain:
        ffn_weights = [tuple(w[0, 0].astype(BF16) for w in stacks)]

    def ffn(x, l, j, final_g=None):
        n = 2 * l + j
        convert = (stacks, *divmod(n + 1, 2)) if chain and n + 1 < 2 * depth else None
        x, converted = _ffn(x, p['norm_ffn'][l, j][None], ffn_weights[n], final_g, convert)
        if convert is not None:
            ffn_weights.append(converted)
        return x

    if stepping:
        bkt = _bucket_index(L, SWA_ROWS + L, SWA_ROWS)
    else:
        bkt = _bucket_index(CHUNK, WINDOW + CHUNK, WINDOW)
    new_pool, new_k, new_v, new_shift, new_wkv = [], [], [], [], []
    for l in range(depth):
        i = l // 2
        x = ffn(x, l, 0)
        gn = p['norm_mix'][l][None]
        if l % 2 == 0:
            u, q, k, v = _even_in(x, gn, p['w_in_even'], i)
            if stepping:
                hist = jnp.pad(pool_c[i], ((0, 0), (POOL_HALO - POOL_HIST, 0), (0, 0)))
            else:
                hist = jnp.zeros((B, POOL_HALO, C_POOL), F32)
            pool_out = _pool(u, hist, p['pool_w'][i], p['pool_scale'][i][None], B, L, pos0)
            k3, v3 = k.reshape(B, L, KV_W), v.reshape(B, L, KV_W)
            if stepping:
                k3 = jnp.concatenate([k_c[i].reshape(B, SWA_ROWS, KV_W), k3], axis=1)
                v3 = jnp.concatenate([v_c[i].reshape(B, SWA_ROWS, KV_W), v3], axis=1)
                att = _attn_step(q, k3, v3, bkt, p['t5_table'], p['attn_sinks'][i], B, L)
            else:
                att = _attn_prompt(q, k, v, bkt, p['t5_table'], p['attn_sinks'][i], B, L)
            x = _even_out(x, pool_out, att, p['w_out_even'], i)
            full = jnp.concatenate([hist, u.reshape(B, L, C_POOL)], axis=1)
            new_pool.append(full[:, -POOL_HIST:])
            new_k.append(k3[:, -SWA_ROWS:].reshape(B, SWA_ROWS, N_KV_HEADS, HEAD_DIM))
            new_v.append(v3[:, -SWA_ROWS:].reshape(B, SWA_ROWS, N_KV_HEADS, HEAD_DIM))
        else:
            if stepping:
                shift_prev = shift_c[i][:, None, :]
                s0 = _pack_state(wkv_c[i])
            else:
                shift_prev = jnp.zeros((B, 1, D_MODEL), F32)
                s0 = jnp.zeros((B, N_PAIRS, LANES, LANES), F32)
            r, lw, k, v, kk, b, g, bonus, hs = _rwkv_pre(x, shift_prev, gn, p, i, B, L)
            y, s_new = _rwkv_scan(r, lw, k, v, kk, b, s0, B, L)
            x = _rwkv_post(x, y, g, bonus, p, i, B, L)
            new_shift.append(hs[:, 7])
            new_wkv.append(_unpack_state(s_new))
        final_g = p['norm_final'][None] if l == depth - 1 else None
        x = ffn(x, l, 1, final_g)
    return (x.reshape(B, L, D_MODEL), jnp.stack(new_pool), jnp.stack(new_k), jnp.stack(new_v),
            jnp.stack(new_shift), jnp.stack(new_wkv), ffn_weights)


_MATMUL_WEIGHTS = ('w_in_even', 'pool_w', 'w_out_even', 'rwkv_wr', 'rwkv_wk', 'rwkv_wv', 'rwkv_w1', 'rwkv_w2',
                   'rwkv_a1', 'rwkv_a2', 'rwkv_g1', 'rwkv_g2', 'rwkv_wo')


def _prepare(p):
    p = dict(p)
    for name in _MATMUL_WEIGHTS:
        p[name] = p[name].astype(BF16)
    onehot = (jnp.arange(D_MODEL)[:, None] // RWKV_HEAD == jnp.arange(LANES)[None, :]).astype(BF16)
    p['head_onehot'] = onehot
    p['head_onehot_t'] = onehot.T
    return p


def kernel(x_prompt, x_sample, cache_pool, cache_swa_k, cache_swa_v, state_shift, state_wkv, t5_table, norm_ffn, ffn_w_gate, ffn_w_up, ffn_w_down, norm_mix, w_in_even, pool_w, pool_scale, attn_sinks, w_out_even, rwkv_mu, rwkv_wr, rwkv_wk, rwkv_wv, rwkv_w0, rwkv_w1, rwkv_w2, rwkv_a0, rwkv_a1, rwkv_a2, rwkv_g1, rwkv_g2, rwkv_k_k, rwkv_k_a, rwkv_r_k, rwkv_lnx_w, rwkv_lnx_b, rwkv_wo, norm_final):
    p = _prepare(dict(
        t5_table=t5_table, norm_ffn=norm_ffn, ffn_w_gate=ffn_w_gate, ffn_w_up=ffn_w_up, ffn_w_down=ffn_w_down,
        norm_mix=norm_mix, w_in_even=w_in_even, pool_w=pool_w, pool_scale=pool_scale, attn_sinks=attn_sinks,
        w_out_even=w_out_even, rwkv_mu=rwkv_mu, rwkv_wr=rwkv_wr, rwkv_wk=rwkv_wk, rwkv_wv=rwkv_wv,
        rwkv_w0=rwkv_w0, rwkv_w1=rwkv_w1, rwkv_w2=rwkv_w2, rwkv_a0=rwkv_a0, rwkv_a1=rwkv_a1, rwkv_a2=rwkv_a2,
        rwkv_g1=rwkv_g1, rwkv_g2=rwkv_g2, rwkv_k_k=rwkv_k_k, rwkv_k_a=rwkv_k_a, rwkv_r_k=rwkv_r_k,
        rwkv_lnx_w=rwkv_lnx_w, rwkv_lnx_b=rwkv_lnx_b, rwkv_wo=rwkv_wo, norm_final=norm_final))
    y_p, pool_p, k_p, v_p, shift_p, wkv_p, ffn_weights = _trunk(x_prompt, 0, (None, None, None, None, None), p)
    y_s, pool_s, k_s, v_s, shift_s, wkv_s, _ = _trunk(
        x_sample, PAST_LEN, (cache_pool, cache_swa_k, cache_swa_v, state_shift, state_wkv), p, ffn_weights)
    return (y_p, y_s, pool_p, pool_s, k_p, k_s, v_p, v_s, shift_p, shift_s, wkv_p, wkv_s)
```

```python
import functools
import math

import jax
import jax.numpy as jnp
from jax import lax
from jax.experimental import pallas as pl
from jax.experimental.pallas import tpu as pltpu

F32 = jnp.float32
BF16 = jnp.bfloat16

D_MODEL = 1024
D_FF = 2816
NORM_EPS = 1e-6
CHUNK = 64
POOL_WINDOWS = (2, 4, 8, 16)
C_POOL = 512
POOL_GC = 128
POOL_HIST = 15
POOL_HALO = 16
HEAD_DIM = 64
N_Q_HEADS = 8
N_KV_HEADS = 2
GQA_GROUP = 4
WINDOW = 128
SWA_ROWS = 128
Q_W = 512
KV_W = 128
IN_EVEN = C_POOL + Q_W + 2 * KV_W
N_BUCKETS = 32
MAX_DISTANCE = 128
RWKV_HEAD = 64
RWKV_H = 16
N_PAIRS = RWKV_H // 2
LANES = 128
LNX_EPS = 64e-5
PAST_LEN = 4096
VMEM_LIMIT_BYTES = 56 * 1024 * 1024


def _params(sem):
    return pltpu.CompilerParams(dimension_semantics=sem, vmem_limit_bytes=VMEM_LIMIT_BYTES)


def _dot(a, b):
    return jnp.dot(a, b, preferred_element_type=F32)


def _dot_nt(a, b):
    return lax.dot_general(a, b, (((1,), (1,)), ((), ())), preferred_element_type=F32)


def _dot_tn(a, b):
    return lax.dot_general(a, b, (((0,), (0,)), ((), ())), preferred_element_type=F32)


def _rms(x, g):
    return x * lax.rsqrt(jnp.mean(x * x, axis=-1, keepdims=True) + NORM_EPS) * g


def _split(x):
    hi = x.astype(BF16)
    lo = (x - hi.astype(F32)).astype(BF16)
    return hi, lo


def _head_sum(x, e_ref, et_ref, split=True):
    if not split:
        return _dot(_dot(x.astype(BF16), e_ref[...]).astype(BF16), et_ref[...])
    hi, lo = _split(x)
    s = _dot(hi, e_ref[...]) + _dot(lo, e_ref[...])
    shi, slo = _split(s)
    return _dot(shi, et_ref[...]) + _dot(slo, et_ref[...])


def _ffn_kernel(x_ref, g_ref, wg_ref, wu_ref, wd_ref, *rest, f_chunk, final, convert):
    rest = list(rest)
    gf_ref = rest.pop(0) if final else None
    src_refs = [rest.pop(0) for _ in range(3)] if convert else []
    o_ref = rest.pop(0)
    dst_refs = [rest.pop(0) for _ in range(3)] if convert else []
    acc_ref, = rest
    for src, dst in zip(src_refs, dst_refs):
        dst[...] = src[...].astype(BF16)
    x = x_ref[...]
    h = _rms(x, g_ref[...]).astype(BF16)
    for j in range(D_FF // f_chunk):
        sl = slice(j * f_chunk, (j + 1) * f_chunk)
        gate = _dot(h, wg_ref[:, sl])
        up = _dot(h, wu_ref[:, sl])
        act = (gate * jax.nn.sigmoid(gate) * up).astype(BF16)
        part = _dot(act, wd_ref[sl, :])
        if j == 0:
            acc_ref[...] = part
        else:
            acc_ref[...] += part
    y = x + 0.5 * acc_ref[...]
    if final:
        y = _rms(y, gf_ref[...])
    o_ref[...] = y


def _resident(shape, index_map):
    return pl.BlockSpec(shape, index_map, pipeline_mode=pl.Buffered(1))


BF16_SUBLANES = 16


def _slab_steps(rows, steps):
    return next(k for k in range(steps, 0, -1) if rows % (k * BF16_SUBLANES) == 0)


def _ffn(x, g, weights, final_g=None, convert=None):
    T = x.shape[0]
    tm = min(512, T)
    steps = T // tm
    final = final_g is not None
    in_specs = [pl.BlockSpec((tm, D_MODEL), lambda i: (i, 0)), _resident((1, D_MODEL), lambda i: (0, 0))]
    in_specs += [_resident(w.shape, lambda i: (0, 0)) for w in weights]
    args = [x, g, *weights]
    out_specs = [pl.BlockSpec((tm, D_MODEL), lambda i: (i, 0))]
    out_shape = [jax.ShapeDtypeStruct((T, D_MODEL), F32)]
    if final:
        in_specs.append(_resident((1, D_MODEL), lambda i: (0, 0)))
        args.append(final_g)
    if convert is not None:
        stacks, l, j = convert
        for w in stacks:
            rows, cols = w.shape[2:]
            k = _slab_steps(rows, steps)
            in_specs.append(pl.BlockSpec((None, None, rows // k, cols),
                                         lambda i, k=k: (l, j, jnp.minimum(i, k - 1), 0)))
            out_specs.append(pl.BlockSpec((rows // k, cols), lambda i, k=k: (jnp.minimum(i, k - 1), 0)))
            out_shape.append(jax.ShapeDtypeStruct((rows, cols), BF16))
            args.append(w)
    outs = pl.pallas_call(
        functools.partial(_ffn_kernel, f_chunk=256, final=final, convert=convert is not None),
        grid=(steps,),
        in_specs=in_specs,
        out_specs=out_specs,
        out_shape=out_shape,
        scratch_shapes=[pltpu.VMEM((tm, D_MODEL), F32)],
        compiler_params=_params(("arbitrary",)),
    )(*args)
    return outs[0], tuple(outs[1:])


def _even_in_kernel(x_ref, g_ref, w_ref, u_ref, q_ref, k_ref, v_ref):
    h = _rms(x_ref[...], g_ref[...]).astype(BF16)
    z = _dot(h, w_ref[...])
    u_ref[...] = z[:, :C_POOL]
    q_ref[...] = z[:, C_POOL:C_POOL + Q_W].astype(BF16)
    k_ref[...] = z[:, C_POOL + Q_W:C_POOL + Q_W + KV_W]
    v_ref[...] = z[:, C_POOL + Q_W + KV_W:]


def _even_in(x, g, w_in, i):
    T = x.shape[0]
    tm = min(512, T)
    row = lambda n: pl.BlockSpec((tm, n), lambda t: (t, 0))
    return pl.pallas_call(
        _even_in_kernel,
        grid=(T // tm,),
        in_specs=[row(D_MODEL), _resident((1, D_MODEL), lambda t: (0, 0)),
                  _resident((None, D_MODEL, IN_EVEN), lambda t: (i, 0, 0))],
        out_specs=[row(C_POOL), row(Q_W), row(KV_W), row(KV_W)],
        out_shape=[jax.ShapeDtypeStruct((T, C_POOL), F32), jax.ShapeDtypeStruct((T, Q_W), BF16),
                   jax.ShapeDtypeStruct((T, KV_W), F32), jax.ShapeDtypeStruct((T, KV_W), F32)],
        compiler_params=_params(("parallel",)),
    )(x, g, w_in)


def _pool_kernel(u_ref, halo_ref, hist_ref, pw_ref, ps_ref, o_ref, *, tt, pos0):
    t = pl.program_id(1)
    halo = jnp.where(t == 0, hist_ref[0], halo_ref[...])
    u = u_ref[...]
    ext = jnp.concatenate([halo, u], axis=0)
    s2 = ext + pltpu.roll(ext, 1, 0)
    s4 = s2 + pltpu.roll(s2, 2, 0)
    s8 = s4 + pltpu.roll(s4, 4, 0)
    s16 = s8 + pltpu.roll(s8, 8, 0)
    pos = pos0 + t * tt + lax.broadcasted_iota(jnp.int32, (tt, 1), 0)
    outs = []
    for gi, (w, s) in enumerate(zip(POOL_WINDOWS, (s2, s4, s8, s16))):
        sl = slice(gi * POOL_GC, (gi + 1) * POOL_GC)
        cnt = jnp.minimum(w, pos + 1).astype(F32)
        pooled = s[POOL_HALO:, sl] / cnt - u[:, sl]
        outs.append(_dot(pooled.astype(BF16), pw_ref[gi]))
    o_ref[...] = (jnp.concatenate(outs, axis=-1) * ps_ref[...]).astype(BF16)


def _pool(u, hist, pool_w, pool_scale, B, L, pos0):
    tt = min(512, L)
    nt = L // tt
    hb = tt // POOL_HALO
    return pl.pallas_call(
        functools.partial(_pool_kernel, tt=tt, pos0=pos0),
        grid=(B, nt),
        in_specs=[
            pl.BlockSpec((tt, C_POOL), lambda b, t: (b * nt + t, 0)),
            pl.BlockSpec((POOL_HALO, C_POOL), lambda b, t: (jnp.maximum((b * nt + t) * hb - 1, 0), 0)),
            pl.BlockSpec((1, POOL_HALO, C_POOL), lambda b, t: (b, 0, 0)),
            _resident((len(POOL_WINDOWS), POOL_GC, POOL_GC), lambda b, t: (0, 0, 0)),
            _resident((1, C_POOL), lambda b, t: (0, 0)),
        ],
        out_specs=pl.BlockSpec((tt, C_POOL), lambda b, t: (b * nt + t, 0)),
        out_shape=jax.ShapeDtypeStruct((B * L, C_POOL), BF16),
        compiler_params=_params(("parallel", "arbitrary")),
    )(u, u, hist, pool_w, pool_scale)


def _build_bias(bkt_ref, tab_ref, bias_ref, lq):
    bkt = bkt_ref[...]
    for h in range(N_Q_HEADS):
        b = jnp.zeros(bkt.shape, F32)
        for n in range(N_BUCKETS):
            b = jnp.where(bkt == n, tab_ref[n, h], b)
        g, i = divmod(h, GQA_GROUP)
        bias_ref[g, :, i * lq:(i + 1) * lq] = b


def _group_queries(q, g):
    return jnp.concatenate([q[:, h * HEAD_DIM:(h + 1) * HEAD_DIM]
                            for h in range(g * GQA_GROUP, (g + 1) * GQA_GROUP)], axis=0) * (HEAD_DIM ** -0.5)


def _sink_rows(sink_ref, lq):
    lane = lax.broadcasted_iota(jnp.int32, (1, GQA_GROUP * lq), 1)
    rows = []
    for g in range(N_KV_HEADS):
        r = jnp.zeros((1, GQA_GROUP * lq), F32)
        for i in range(GQA_GROUP):
            r = jnp.where(lane // lq == i, sink_ref[g * GQA_GROUP + i], r)
        rows.append(r)
    return rows


def _attn_core(qs, ks, vs, biases, sinks, valids):
    ss = [_dot_nt(k, q) + b for q, k, b in zip(qs, ks, biases)]
    ss = [s if ok is None else jnp.where(ok, s, -1e30) for s, ok in zip(ss, valids)]
    ms = [jnp.maximum(jnp.max(s, axis=0, keepdims=True), sk) for s, sk in zip(ss, sinks)]
    ps = [jnp.exp(s - m) for s, m in zip(ss, ms)]
    invs = [1.0 / (jnp.sum(p, axis=0, keepdims=True) + jnp.exp(sk - m)) for p, sk, m in zip(ps, sinks, ms)]
    return [_dot_tn((p * r).astype(BF16), v) for p, r, v in zip(ps, invs, vs)]


def _attn_band_kernel(q_ref, kp_ref, ko_ref, vp_ref, vo_ref, bkt_ref, tab_ref, sink_ref, o_ref, bias_ref, *, cps):
    @pl.when(jnp.logical_and(pl.program_id(0) == 0, pl.program_id(1) == 0))
    def _():
        _build_bias(bkt_ref, tab_ref, bias_ref, CHUNK)

    lk = WINDOW + CHUNK
    k = jnp.concatenate([kp_ref[...], ko_ref[...]], axis=0).astype(BF16)
    v = jnp.concatenate([vp_ref[...], vo_ref[...]], axis=0).astype(BF16)
    q = q_ref[...]
    sink_cols = _sink_rows(sink_ref, CHUNK)
    first_pos = pl.program_id(1) * (cps * CHUNK) - WINDOW + lax.broadcasted_iota(jnp.int32, (lk, 1), 0)
    qs, ks, vs, biases, sinks, valids = [], [], [], [], [], []
    for j in range(cps):
        qj = q[j * CHUNK:(j + 1) * CHUNK]
        for g in range(N_KV_HEADS):
            qs.append(_group_queries(qj, g))
            ks.append(k[j * CHUNK:j * CHUNK + lk, g * HEAD_DIM:(g + 1) * HEAD_DIM])
            vs.append(v[j * CHUNK:j * CHUNK + lk, g * HEAD_DIM:(g + 1) * HEAD_DIM])
            biases.append(bias_ref[g])
            sinks.append(sink_cols[g])
            valids.append(first_pos + j * CHUNK >= 0)
    outs = _attn_core(qs, ks, vs, biases, sinks, valids)
    for j in range(cps):
        heads = [outs[j * N_KV_HEADS + g][i * CHUNK:(i + 1) * CHUNK]
                 for g in range(N_KV_HEADS) for i in range(GQA_GROUP)]
        o_ref[j * CHUNK:(j + 1) * CHUNK, :] = jnp.concatenate(heads, axis=-1).astype(BF16)


def _attn_step_kernel(q_ref, k_ref, v_ref, bkt_ref, tab_ref, sink_ref, o_ref, bias_ref, *, lq):
    @pl.when(pl.program_id(0) == 0)
    def _():
        _build_bias(bkt_ref, tab_ref, bias_ref, lq)

    k = k_ref[0].astype(BF16)
    v = v_ref[0].astype(BF16)
    q = q_ref[...]
    groups = range(N_KV_HEADS)
    outs = _attn_core([_group_queries(q, g) for g in groups],
                      [k[:, g * HEAD_DIM:(g + 1) * HEAD_DIM] for g in groups],
                      [v[:, g * HEAD_DIM:(g + 1) * HEAD_DIM] for g in groups],
                      [bias_ref[g] for g in groups], _sink_rows(sink_ref, lq), [None] * N_KV_HEADS)
    heads = [outs[g][i * lq:(i + 1) * lq] for g in groups for i in range(GQA_GROUP)]
    o_ref[...] = jnp.concatenate(heads, axis=-1).astype(BF16)


def _smem():
    return pl.BlockSpec(memory_space=pltpu.SMEM)


def _attn_prompt(q, k, v, bkt, table, sinks, B, L):
    nc = L // CHUNK
    cps = 4 if nc % 4 == 0 else 2
    assert nc % cps == 0
    rows = cps * CHUNK
    ns = nc // cps
    lk = WINDOW + CHUNK
    own = lambda n: pl.BlockSpec((rows, n), lambda b, s: (b * ns + s, 0))
    prev = pl.BlockSpec((WINDOW, KV_W), lambda b, s: (jnp.maximum((b * ns + s) * (rows // WINDOW) - 1, 0), 0))
    return pl.pallas_call(
        functools.partial(_attn_band_kernel, cps=cps),
        grid=(B, ns),
        in_specs=[own(Q_W), prev, own(KV_W), prev, own(KV_W),
                  _resident((lk, CHUNK), lambda b, s: (0, 0)), _smem(), _smem()],
        out_specs=own(Q_W),
        out_shape=jax.ShapeDtypeStruct((B * L, Q_W), BF16),
        scratch_shapes=[pltpu.VMEM((N_KV_HEADS, lk, GQA_GROUP * CHUNK), F32)],
        compiler_params=_params(("arbitrary", "arbitrary")),
    )(q, k, k, v, v, bkt, table, sinks)


def _attn_step(q, k_all, v_all, bkt, table, sinks, B, L):
    lk = k_all.shape[1]
    return pl.pallas_call(
        functools.partial(_attn_step_kernel, lq=L),
        grid=(B,),
        in_specs=[pl.BlockSpec((L, Q_W), lambda b: (b, 0)),
                  pl.BlockSpec((1, lk, KV_W), lambda b: (b, 0, 0)),
                  pl.BlockSpec((1, lk, KV_W), lambda b: (b, 0, 0)),
                  _resident((lk, L), lambda b: (0, 0)), _smem(), _smem()],
        out_specs=pl.BlockSpec((L, Q_W), lambda b: (b, 0)),
        out_shape=jax.ShapeDtypeStruct((B * L, Q_W), BF16),
        scratch_shapes=[pltpu.VMEM((N_KV_HEADS, lk, GQA_GROUP * L), F32)],
        compiler_params=_params(("arbitrary",)),
    )(q, k_all, v_all, bkt, table, sinks)


def _even_out_kernel(x_ref, p_ref, a_ref, wp_ref, wa_ref, o_ref):
    o_ref[...] = x_ref[...] + _dot(p_ref[...], wp_ref[...]) + _dot(a_ref[...], wa_ref[...])


def _even_out(x, pool_out, att, w_out, i):
    T = x.shape[0]
    tm = min(512, T)
    row = lambda n: pl.BlockSpec((tm, n), lambda t: (t, 0))
    return pl.pallas_call(
        _even_out_kernel,
        grid=(T // tm,),
        in_specs=[row(D_MODEL), row(C_POOL), row(Q_W),
                  _resident((None, C_POOL, D_MODEL), lambda t: (i, 0, 0)),
                  _resident((None, Q_W, D_MODEL), lambda t: (i, 1, 0))],
        out_specs=row(D_MODEL),
        out_shape=jax.ShapeDtypeStruct((T, D_MODEL), F32),
        compiler_params=_params(("parallel",)),
    )(x, pool_out, att, w_out, w_out)


def _even_layer_kernel(x_ref, gn_ref, win_ref, pw_ref, ps_ref, hist_ref, kc_ref, vc_ref, bkt_ref, tab_ref, sink_ref,
                       wop_ref, woa_ref, o_ref, utail_o, ktail_o, vtail_o, halo_ref, kprev_ref, vprev_ref, bias_ref,
                       *, rows, chunk, pos0, masked):
    t = pl.program_id(1)

    @pl.when(jnp.logical_and(pl.program_id(0) == 0, t == 0))
    def _():
        _build_bias(bkt_ref, tab_ref, bias_ref, chunk)

    @pl.when(t == 0)
    def _():
        halo_ref[...] = hist_ref[0]
        kprev_ref[...] = kc_ref[0]
        vprev_ref[...] = vc_ref[0]

    x = x_ref[...]
    z = _dot(_rms(x, gn_ref[...]).astype(BF16), win_ref[...])
    u = z[:, :C_POOL]
    q = z[:, C_POOL:C_POOL + Q_W].astype(BF16)
    k_all = jnp.concatenate([kprev_ref[...], z[:, C_POOL + Q_W:C_POOL + Q_W + KV_W]], axis=0)
    v_all = jnp.concatenate([vprev_ref[...], z[:, C_POOL + Q_W + KV_W:]], axis=0)

    ext = jnp.concatenate([halo_ref[...], u], axis=0)
    pos = pos0 + t * rows + lax.broadcasted_iota(jnp.int32, (rows, POOL_GC), 0)
    pooled = []
    for gi, w in enumerate(POOL_WINDOWS):
        sl = slice(gi * POOL_GC, (gi + 1) * POOL_GC)
        s = ext[:, sl]
        span = 1
        while span < w:
            s = s + pltpu.roll(s, span, 0)
            span *= 2
        cnt = jnp.minimum(w, pos + 1).astype(F32)
        pooled.append(_dot((s[POOL_HALO:] / cnt - u[:, sl]).astype(BF16), pw_ref[gi]))
    pool_out = (jnp.concatenate(pooled, axis=-1) * ps_ref[...]).astype(BF16)

    lk = WINDOW + chunk
    kb = k_all.astype(BF16)
    vb = v_all.astype(BF16)
    sink_rows = _sink_rows(sink_ref, chunk)
    first_pos = t * rows - WINDOW + lax.broadcasted_iota(jnp.int32, (lk, 1), 0)
    nchunk = rows // chunk
    qs, ks, vs, biases, sinks, valids = [], [], [], [], [], []
    for j in range(nchunk):
        qj = q[j * chunk:(j + 1) * chunk]
        for g in range(N_KV_HEADS):
            qs.append(_group_queries(qj, g))
            ks.append(kb[j * chunk:j * chunk + lk, g * HEAD_DIM:(g + 1) * HEAD_DIM])
            vs.append(vb[j * chunk:j * chunk + lk, g * HEAD_DIM:(g + 1) * HEAD_DIM])
            biases.append(bias_ref[g])
            sinks.append(sink_rows[g])
            valids.append(first_pos + j * chunk >= 0 if masked else None)
    outs = _attn_core(qs, ks, vs, biases, sinks, valids)
    att = jnp.concatenate(
        [jnp.concatenate([outs[j * N_KV_HEADS + g][i * chunk:(i + 1) * chunk]
                          for g in range(N_KV_HEADS) for i in range(GQA_GROUP)], axis=-1) for j in range(nchunk)],
        axis=0).astype(BF16)

    o_ref[...] = x + _dot(pool_out, wop_ref[...]) + _dot(att, woa_ref[...])

    halo_ref[...] = ext[rows:]
    kprev_ref[...] = k_all[rows:]
    vprev_ref[...] = v_all[rows:]
    utail_o[0] = ext[rows:]
    ktail_o[0] = k_all[rows:]
    vtail_o[0] = v_all[rows:]


def _even_layer(x, gn, p, i, hist, k_cache, v_cache, bkt, B, L, pos0, chunk, masked):
    rows = min(4 * chunk, L)
    nt = L // rows
    lk = WINDOW + chunk
    row = pl.BlockSpec((rows, D_MODEL), lambda b, t: (b * nt + t, 0))
    per_seq = lambda n, m: pl.BlockSpec((1, n, m), lambda b, t: (b, 0, 0))
    return pl.pallas_call(
        functools.partial(_even_layer_kernel, rows=rows, chunk=chunk, pos0=pos0, masked=masked),
        grid=(B, nt),
        in_specs=[row, _resident((1, D_MODEL), lambda b, t: (0, 0)),
                  _resident((None, D_MODEL, IN_EVEN), lambda b, t: (i, 0, 0)),
                  _resident((None, len(POOL_WINDOWS), POOL_GC, POOL_GC), lambda b, t: (i, 0, 0, 0)),
                  _resident((1, C_POOL), lambda b, t: (0, 0)),
                  per_seq(POOL_HALO, C_POOL), per_seq(WINDOW, KV_W), per_seq(WINDOW, KV_W),
                  _resident((lk, chunk), lambda b, t: (0, 0)), _smem(), _smem(),
                  _resident((None, C_POOL, D_MODEL), lambda b, t: (i, 0, 0)),
                  _resident((None, Q_W, D_MODEL), lambda b, t: (i, 1, 0))],
        out_specs=[row, per_seq(POOL_HALO, C_POOL), per_seq(WINDOW, KV_W), per_seq(WINDOW, KV_W)],
        out_shape=[jax.ShapeDtypeStruct((B * L, D_MODEL), F32), jax.ShapeDtypeStruct((B, POOL_HALO, C_POOL), F32),
                   jax.ShapeDtypeStruct((B, WINDOW, KV_W), F32), jax.ShapeDtypeStruct((B, WINDOW, KV_W), F32)],
        scratch_shapes=[pltpu.VMEM((POOL_HALO, C_POOL), F32), pltpu.VMEM((WINDOW, KV_W), F32),
                        pltpu.VMEM((WINDOW, KV_W), F32), pltpu.VMEM((N_KV_HEADS, lk, GQA_GROUP * chunk), F32)],
        compiler_params=_params(("arbitrary", "arbitrary")),
    )(x, gn, p['w_in_even'], p['pool_w'], p['pool_scale'][i][None], hist, k_cache, v_cache, bkt,
      p['t5_table'], p['attn_sinks'][i], p['w_out_even'], p['w_out_even'])


def _t5_bucket(rel):
    half = N_BUCKETS // 2
    max_exact = half // 2
    side = jnp.where(rel > 0, half, 0)
    n = jnp.abs(rel)
    nf = jnp.maximum(n, max_exact).astype(F32)
    large = max_exact + (jnp.log(nf / max_exact) / math.log(MAX_DISTANCE / max_exact)
                         * (half - max_exact)).astype(jnp.int32)
    large = jnp.minimum(large, half - 1)
    return side + jnp.where(n < max_exact, n, large)


def _bucket_index(lq, lk, offset):
    rel = jnp.arange(lk)[:, None] - offset - jnp.arange(lq)[None, :]
    return _t5_bucket(rel).astype(jnp.int32)


def _rwkv_pre_kernel(x_ref, xp_ref, sh_ref, gn_ref, mu_ref, wr_ref, wk_ref, wv_ref, w1_ref, w2_ref,
                     a1_ref, a2_ref, g1_ref, g2_ref, w0_ref, a0_ref, kk_ref, ka_ref, rk_ref, e_ref, et_ref,
                     r_o, lw_o, k_o, v_o, kk_o, b_o, g_o, bonus_o, hs_o, *, tm):
    t = pl.program_id(1)
    gn = gn_ref[...]
    h = _rms(x_ref[...], gn)
    prev_tile_last = _rms(xp_ref[...], gn)[7:8]
    first_prev = jnp.where(t == 0, sh_ref[0], prev_tile_last)
    row = lax.broadcasted_iota(jnp.int32, (tm, 1), 0)
    h_prev = jnp.where(row == 0, first_prev, pltpu.roll(h, 1, 0))
    xx = h_prev - h

    hb = h.astype(BF16)
    xb = xx.astype(BF16)
    mu = mu_ref[...].astype(BF16)

    def mix(j):
        return hb + xb * mu[j:j + 1]

    zw = w0_ref[...] + _dot(jnp.tanh(_dot(mix(1), w1_ref[...])).astype(BF16), w2_ref[...])
    za = a0_ref[...] + _dot(_dot(mix(4), a1_ref[...]).astype(BF16), a2_ref[...])
    g = _dot(jax.nn.sigmoid(_dot(mix(5), g1_ref[...])).astype(BF16), g2_ref[...])
    r = _dot(mix(0), wr_ref[...])
    lw = -math.exp(-0.5) * jax.nn.sigmoid(zw)
    k = _dot(mix(2), wk_ref[...])
    a = jax.nn.sigmoid(za)
    v = _dot(mix(3), wv_ref[...])
    kk = k * kk_ref[...]
    kk = kk * lax.rsqrt(jnp.maximum(_head_sum(kk * kk, e_ref, et_ref, split=False), 1e-24))
    k = k * (1.0 + (a - 1.0) * ka_ref[...])
    b = kk * a
    for hp in range(N_PAIRS):
        sl = slice(hp * LANES, (hp + 1) * LANES)
        r_o[0, hp] = r[:, sl]
        lw_o[0, hp] = lw[:, sl]
        k_o[0, hp] = k[:, sl]
        v_o[0, hp] = v[:, sl]
        kk_o[0, hp] = kk[:, sl]
        b_o[0, hp] = b[:, sl]
    g_o[...] = g.astype(BF16)
    bonus_o[...] = (_head_sum(r * k * rk_ref[...], e_ref, et_ref, split=False) * v).astype(BF16)
    hs_o[0] = h[tm - 8:]


def _rwkv_pre(x, shift_prev, gn, p, i, B, L):
    tm = min(512, L)
    nt = L // tm
    vec = lambda: _resident((1, D_MODEL), lambda b, t: (0, 0))
    mat = lambda r, c: _resident((None, r, c), lambda b, t: (i, 0, 0))
    lora = p['rwkv_w1'].shape[-1], p['rwkv_a1'].shape[-1], p['rwkv_g1'].shape[-1]
    head_spec = pl.BlockSpec((1, N_PAIRS, tm, LANES), lambda b, t: (b, 0, t, 0))
    head_shape = jax.ShapeDtypeStruct((B, N_PAIRS, L, LANES), F32)
    return pl.pallas_call(
        functools.partial(_rwkv_pre_kernel, tm=tm),
        grid=(B, nt),
        in_specs=[
            pl.BlockSpec((tm, D_MODEL), lambda b, t: (b * nt + t, 0)),
            pl.BlockSpec((8, D_MODEL), lambda b, t: (jnp.maximum((b * nt + t) * (tm // 8) - 1, 0), 0)),
            pl.BlockSpec((1, 1, D_MODEL), lambda b, t: (b, 0, 0)),
            vec(),
            _resident((None, 6, D_MODEL), lambda b, t: (i, 0, 0)),
            mat(D_MODEL, D_MODEL), mat(D_MODEL, D_MODEL), mat(D_MODEL, D_MODEL),
            mat(D_MODEL, lora[0]), mat(lora[0], D_MODEL),
            mat(D_MODEL, lora[1]), mat(lora[1], D_MODEL),
            mat(D_MODEL, lora[2]), mat(lora[2], D_MODEL),
            vec(), vec(), vec(), vec(), vec(),
            _resident((D_MODEL, LANES), lambda b, t: (0, 0)),
            _resident((LANES, D_MODEL), lambda b, t: (0, 0)),
        ],
        out_specs=[head_spec] * 6 + [
            pl.BlockSpec((tm, D_MODEL), lambda b, t: (b * nt + t, 0)),
            pl.BlockSpec((tm, D_MODEL), lambda b, t: (b * nt + t, 0)),
            pl.BlockSpec((1, 8, D_MODEL), lambda b, t: (b, 0, 0)),
        ],
        out_shape=[head_shape] * 6 + [
            jax.ShapeDtypeStruct((B * L, D_MODEL), BF16),
            jax.ShapeDtypeStruct((B * L, D_MODEL), BF16),
            jax.ShapeDtypeStruct((B, 8, D_MODEL), F32),
        ],
        compiler_params=_params(("parallel", "arbitrary")),
    )(x, x, shift_prev, gn, p['rwkv_mu'], p['rwkv_wr'], p['rwkv_wk'], p['rwkv_wv'],
      p['rwkv_w1'], p['rwkv_w2'], p['rwkv_a1'], p['rwkv_a2'], p['rwkv_g1'], p['rwkv_g2'],
      p['rwkv_w0'][i][None], p['rwkv_a0'][i][None], p['rwkv_k_k'][i][None], p['rwkv_k_a'][i][None],
      p['rwkv_r_k'][i].reshape(1, D_MODEL), p['head_onehot'], p['head_onehot_t'])


DIAG = 8


def _replication_matrix(n):
    src = jnp.arange(n)[:, None]
    dst = jnp.arange(n)[None, :]
    return jnp.concatenate([(src == (dst // DIAG) * DIAG + s) for s in range(DIAG - 1)], axis=1).astype(BF16)


def _unit_lower_inverses(lows, c, rep_ref):
    n = 2 * c
    ti = lax.broadcasted_iota(jnp.int32, (c, n), 0)
    li = lax.broadcasted_iota(jnp.int32, (c, n), 1)
    si = li & (c - 1)
    head1 = li >= c

    def block_diag(x):
        return jnp.concatenate([jnp.where(head1, 0.0, x), jnp.where(head1, x, 0.0)], axis=0)

    if rep_ref is None:
        invs = [jnp.where(ti == si, 1.0, 0.0) + jnp.where(jnp.logical_and(ti == si + 1, (ti & 1) == 1), low, 0.0)
                for low in lows]
        m = 2
    else:
        pt = lax.broadcasted_iota(jnp.int32, (DIAG, n), 0)
        pl_ = lax.broadcasted_iota(jnp.int32, (DIAG, n), 1)
        blk = (pl_ & (c - 1)) // DIAG
        packed = []
        for low in lows:
            d = jnp.zeros((DIAG, n), F32)
            for i in range(c // DIAG):
                d = jnp.where(blk == i, low[i * DIAG:(i + 1) * DIAG], d)
            packed.append(d)
        rep = _dot(jnp.concatenate(packed, axis=0).astype(BF16), rep_ref[...])
        sols = [jnp.where(pt == (pl_ & (DIAG - 1)), 1.0, 0.0) for _ in lows]
        for s in range(DIAG - 1):
            sols = [sol + rep[p * DIAG:(p + 1) * DIAG, s * n:(s + 1) * n] * sol[s:s + 1]
                    for p, sol in enumerate(sols)]
        invs = [jnp.concatenate([jnp.where(blk == i, sol, 0.0) for i in range(c // DIAG)], axis=0) for sol in sols]
        m = DIAG
    while m < c:
        sh = m.bit_length() - 1
        sub = jnp.logical_and(jnp.logical_and((ti >> (sh + 1)) == (si >> (sh + 1)), ((ti >> sh) & 1) == 1),
                              ((si >> sh) & 1) == 0)
        diag = [block_diag(inv).astype(BF16) for inv in invs]
        half = [_dot(inv.astype(BF16), block_diag(jnp.where(sub, low, 0.0)).astype(BF16)).astype(BF16)
                for inv, low in zip(invs, lows)]
        invs = [inv + _dot(h, d) for inv, h, d in zip(invs, half, diag)]
        m *= 2
    return invs


def _scan_chunks(seqs, states, c, rep_ref):
    n = 2 * c
    row = lax.broadcasted_iota(jnp.int32, (c, LANES), 0)
    head1 = lax.broadcasted_iota(jnp.int32, (c, LANES), 1) >= RWKV_HEAD
    ti = lax.broadcasted_iota(jnp.int32, (c, n), 0)
    si = lax.broadcasted_iota(jnp.int32, (c, n), 1) & (c - 1)
    strict = ti > si
    incl = ti >= si
    pi = lax.broadcasted_iota(jnp.int32, (LANES, LANES), 0)
    pj = lax.broadcasted_iota(jnp.int32, (LANES, LANES), 1)
    same_head = (pi >= RWKV_HEAD) == (pj >= RWKV_HEAD)

    def stack(x):
        return jnp.concatenate([jnp.where(head1, 0.0, x), jnp.where(head1, x, 0.0)], axis=0)

    lhs, rhs, v_st, v_bf, bk_end, decay = [], [], [], [], [], []
    for r, lw, k, v, kk, b in seqs:
        cum = lw
        sh = 1
        while sh < c:
            cum = cum + jnp.where(row >= sh, pltpu.roll(cum, sh, 0), 0.0)
            sh *= 2
        tot = cum[c - 1:c]
        grow = jnp.exp(-cum)
        tail = jnp.exp(tot - cum)
        a_t = -kk * jnp.exp(cum - lw)
        r_t = r * jnp.exp(cum)
        lhs.append(jnp.concatenate([a_t, r_t], axis=0).astype(BF16))
        rhs.append(jnp.concatenate([stack(b * grow), stack(k * grow)], axis=0).astype(BF16))
        v_st.append(stack(v).astype(BF16))
        v_bf.append(v.astype(BF16))
        bk_end.append(jnp.concatenate([b * tail, k * tail], axis=0).astype(BF16))
        decay.append(jnp.exp(tot))
    grams = [_dot_nt(x, y) for x, y in zip(lhs, rhs)]
    lows = [jnp.where(strict, g[:c, :n], 0.0) for g in grams]
    m_ak = [jnp.where(strict, g[:c, n:], 0.0).astype(BF16) for g in grams]
    m_r = [jnp.concatenate([jnp.where(incl, g[c:, :n], 0.0), jnp.where(incl, g[c:, n:], 0.0)], axis=1).astype(BF16)
           for g in grams]
    invs = [inv.astype(BF16) for inv in _unit_lower_inverses(lows, c, rep_ref)]
    from_v = [_dot(m, vs) for m, vs in zip(m_ak, v_st)]

    ys = []
    states = list(states)
    npair = len(states)
    for j in range(len(seqs) // npair):
        sl = slice(j * npair, (j + 1) * npair)
        from_state = [_dot_nt(x, s.astype(BF16)) for x, s in zip(lhs[sl], states)]
        rhs_sa = [stack(f[:c] + fv).astype(BF16) for f, fv in zip(from_state, from_v[sl])]
        sas = [_dot(inv, x) for inv, x in zip(invs[sl], rhs_sa)]
        ys += [f[c:] + _dot(m, jnp.concatenate([stack(sa).astype(BF16), vs], axis=0))
               for f, m, sa, vs in zip(from_state, m_r[sl], sas, v_st[sl])]
        upd = [_dot_tn(jnp.concatenate([sa.astype(BF16), vb], axis=0), be)
               for sa, vb, be in zip(sas, v_bf[sl], bk_end[sl])]
        states = [s * d + jnp.where(same_head, u, 0.0) for s, d, u in zip(states, decay[sl], upd)]
    return ys, states


def _rwkv_scan_kernel(r_ref, lw_ref, k_ref, v_ref, kk_ref, b_ref, s0_ref, *rest, c, sub, vpu_diag):
    if vpu_diag:
        rep_ref, y_ref, so_ref, s_ref = rest
    else:
        rep_ref = None
        y_ref, so_ref, s_ref = rest
    ci = pl.program_id(1)

    @pl.when(ci == 0)
    def _():
        s_ref[...] = s0_ref[0]

    seqs = [tuple(ref[0, hp, j * c:(j + 1) * c] for ref in (r_ref, lw_ref, k_ref, v_ref, kk_ref, b_ref))
            for j in range(sub) for hp in range(N_PAIRS)]
    ys, new_states = _scan_chunks(seqs, [s_ref[hp] for hp in range(N_PAIRS)], c, rep_ref)
    for j in range(sub):
        for hp in range(N_PAIRS):
            y_ref[0, hp, j * c:(j + 1) * c] = ys[j * N_PAIRS + hp]
    for hp in range(N_PAIRS):
        s_ref[hp] = new_states[hp]

    @pl.when(ci == pl.num_programs(1) - 1)
    def _():
        so_ref[0] = s_ref[...]


def _rwkv_scan(r, lw, k, v, kk, b, s0, B, L):
    c = min(CHUNK, L)
    nc = L // c
    sub = 2 if nc % 2 == 0 else 1
    nc //= sub
    seq = pl.BlockSpec((1, N_PAIRS, sub * c, LANES), lambda bi, ci: (bi, 0, ci, 0))
    st = pl.BlockSpec((1, N_PAIRS, LANES, LANES), lambda bi, ci: (bi, 0, 0, 0))
    vpu_diag = 2 * c == LANES
    extra_specs, extra_args = [], []
    if vpu_diag:
        extra_specs = [_resident((LANES, (DIAG - 1) * LANES), lambda bi, ci: (0, 0))]
        extra_args = [_replication_matrix(LANES)]
    return pl.pallas_call(
        functools.partial(_rwkv_scan_kernel, c=c, sub=sub, vpu_diag=vpu_diag),
        grid=(B, nc),
        in_specs=[seq] * 6 + [st] + extra_specs,
        out_specs=[seq, st],
        out_shape=[jax.ShapeDtypeStruct((B, N_PAIRS, L, LANES), F32),
                   jax.ShapeDtypeStruct((B, N_PAIRS, LANES, LANES), F32)],
        scratch_shapes=[pltpu.VMEM((N_PAIRS, LANES, LANES), F32)],
        compiler_params=_params(("parallel", "arbitrary")),
    )(r, lw, k, v, kk, b, s0, *extra_args)


def _rwkv_post_kernel(x_ref, y_ref, g_ref, bonus_ref, lw_ref, lb_ref, e_ref, et_ref, wo_ref, o_ref, *, parts):
    rows = x_ref.shape[0] // parts
    sls = [slice(i * rows, (i + 1) * rows) for i in range(parts)]
    head_sum = lambda t, split: _head_sum(t, e_ref, et_ref, split)
    ys = [jnp.concatenate([y_ref[0, hp, sl] for hp in range(N_PAIRS)], axis=-1) for sl in sls]
    ds = [y - head_sum(y, True) * (1.0 / RWKV_HEAD) for y in ys]
    var = [head_sum(d * d, False) * (1.0 / RWKV_HEAD) for d in ds]
    yn = [d * lax.rsqrt(vr + LNX_EPS) * lw_ref[...] + lb_ref[...] for d, vr in zip(ds, var)]
    gated = [((n + bonus_ref[sl, :].astype(F32)) * g_ref[sl, :].astype(F32)).astype(BF16) for n, sl in zip(yn, sls)]
    for gt, sl in zip(gated, sls):
        o_ref[sl, :] = x_ref[sl, :] + _dot(gt, wo_ref[...])


def _rwkv_post(x, y, g, bonus, p, i, B, L):
    tm = min(512, L)
    nt = L // tm
    vec = lambda: _resident((1, D_MODEL), lambda b, t: (0, 0))
    head_spec = pl.BlockSpec((1, N_PAIRS, tm, LANES), lambda b, t: (b, 0, t, 0))
    row = pl.BlockSpec((tm, D_MODEL), lambda b, t: (b * nt + t, 0))
    return pl.pallas_call(
        functools.partial(_rwkv_post_kernel, parts=2 if tm >= 256 else 1),
        grid=(B, nt),
        in_specs=[row, head_spec, row, row, vec(), vec(),
                  _resident((D_MODEL, LANES), lambda b, t: (0, 0)),
                  _resident((LANES, D_MODEL), lambda b, t: (0, 0)),
                  _resident((None, D_MODEL, D_MODEL), lambda b, t: (i, 0, 0))],
        out_specs=row,
        out_shape=jax.ShapeDtypeStruct((B * L, D_MODEL), F32),
        compiler_params=_params(("parallel", "parallel")),
    )(x, y, g, bonus, p['rwkv_lnx_w'][i][None], p['rwkv_lnx_b'][i][None],
      p['head_onehot'], p['head_onehot_t'], p['rwkv_wo'])


def _pack_state(s):
    B = s.shape[0]
    s = s.reshape(B, N_PAIRS, 2, RWKV_HEAD, RWKV_HEAD)
    z = jnp.zeros_like(s[:, :, 0])
    top = jnp.concatenate([s[:, :, 0], z], axis=-1)
    bot = jnp.concatenate([z, s[:, :, 1]], axis=-1)
    return jnp.concatenate([top, bot], axis=-2)


def _unpack_state(s):
    B = s.shape[0]
    return jnp.stack([s[:, :, :RWKV_HEAD, :RWKV_HEAD], s[:, :, RWKV_HEAD:, RWKV_HEAD:]], axis=2).reshape(
        B, RWKV_H, RWKV_HEAD, RWKV_HEAD)


def _trunk(x, pos0, caches, p, ffn_weights=None):
    pool_c, k_c, v_c, shift_c, wkv_c = caches
    B, L, _ = x.shape
    depth = p['norm_mix'].shape[0]
    stepping = k_c is not None
    x = x.reshape(B * L, D_MODEL)
    stacks = (p['ffn_w_gate'], p['ffn_w_up'], p['ffn_w_down'])
    chain = ffn_weights is None
    if chain:
        ffn_weights = [tuple(w[0, 0].astype(BF16) for w in stacks)]

    def ffn(x, l, j, final_g=None):
        n = 2 * l + j
        convert = (stacks, *divmod(n + 1, 2)) if chain and n + 1 < 2 * depth else None
        x, converted = _ffn(x, p['norm_ffn'][l, j][None], ffn_weights[n], final_g, convert)
        if convert is not None:
            ffn_weights.append(converted)
        return x

    if stepping:
        bkt = _bucket_index(L, SWA_ROWS + L, SWA_ROWS)
    else:
        bkt = _bucket_index(CHUNK, WINDOW + CHUNK, WINDOW)
    new_pool, new_k, new_v, new_shift, new_wkv = [], [], [], [], []
    for l in range(depth):
        i = l // 2
        x = ffn(x, l, 0)
        gn = p['norm_mix'][l][None]
        if l % 2 == 0:
            if stepping:
                hist = jnp.pad(pool_c[i], ((0, 0), (POOL_HALO - POOL_HIST, 0), (0, 0)))
                k_cache = k_c[i].reshape(B, SWA_ROWS, KV_W)
                v_cache = v_c[i].reshape(B, SWA_ROWS, KV_W)
            else:
                hist = jnp.zeros((B, POOL_HALO, C_POOL), F32)
                k_cache = v_cache = jnp.zeros((B, WINDOW, KV_W), F32)
            x, u_tail, k_tail, v_tail = _even_layer(x, gn, p, i, hist, k_cache, v_cache, bkt, B, L, pos0,
                                                    chunk=L if stepping else CHUNK, masked=not stepping)
            new_pool.append(u_tail[:, -POOL_HIST:])
            new_k.append(k_tail.reshape(B, SWA_ROWS, N_KV_HEADS, HEAD_DIM))
            new_v.append(v_tail.reshape(B, SWA_ROWS, N_KV_HEADS, HEAD_DIM))
        else:
            if stepping:
                shift_prev = shift_c[i][:, None, :]
                s0 = _pack_state(wkv_c[i])
            else:
                shift_prev = jnp.zeros((B, 1, D_MODEL), F32)
                s0 = jnp.zeros((B, N_PAIRS, LANES, LANES), F32)
            r, lw, k, v, kk, b, g, bonus, hs = _rwkv_pre(x, shift_prev, gn, p, i, B, L)
            y, s_new = _rwkv_scan(r, lw, k, v, kk, b, s0, B, L)
            x = _rwkv_post(x, y, g, bonus, p, i, B, L)
            new_shift.append(hs[:, 7])
            new_wkv.append(_unpack_state(s_new))
        final_g = p['norm_final'][None] if l == depth - 1 else None
        x = ffn(x, l, 1, final_g)
    return (x.reshape(B, L, D_MODEL), jnp.stack(new_pool), jnp.stack(new_k), jnp.stack(new_v),
            jnp.stack(new_shift), jnp.stack(new_wkv), ffn_weights)


_MATMUL_WEIGHTS = ('w_in_even', 'pool_w', 'w_out_even', 'rwkv_wr', 'rwkv_wk', 'rwkv_wv', 'rwkv_w1', 'rwkv_w2',
                   'rwkv_a1', 'rwkv_a2', 'rwkv_g1', 'rwkv_g2', 'rwkv_wo')


def _prepare(p):
    p = dict(p)
    for name in _MATMUL_WEIGHTS:
        p[name] = p[name].astype(BF16)
    onehot = (jnp.arange(D_MODEL)[:, None] // RWKV_HEAD == jnp.arange(LANES)[None, :]).astype(BF16)
    p['head_onehot'] = onehot
    p['head_onehot_t'] = onehot.T
    return p


def kernel(x_prompt, x_sample, cache_pool, cache_swa_k, cache_swa_v, state_shift, state_wkv, t5_table, norm_ffn, ffn_w_gate, ffn_w_up, ffn_w_down, norm_mix, w_in_even, pool_w, pool_scale, attn_sinks, w_out_even, rwkv_mu, rwkv_wr, rwkv_wk, rwkv_wv, rwkv_w0, rwkv_w1, rwkv_w2, rwkv_a0, rwkv_a1, rwkv_a2, rwkv_g1, rwkv_g2, rwkv_k_k, rwkv_k_a, rwkv_r_k, rwkv_lnx_w, rwkv_lnx_b, rwkv_wo, norm_final):
    p = _prepare(dict(
        t5_table=t5_table, norm_ffn=norm_ffn, ffn_w_gate=ffn_w_gate, ffn_w_up=ffn_w_up, ffn_w_down=ffn_w_down,
        norm_mix=norm_mix, w_in_even=w_in_even, pool_w=pool_w, pool_scale=pool_scale, attn_sinks=attn_sinks,
        w_out_even=w_out_even, rwkv_mu=rwkv_mu, rwkv_wr=rwkv_wr, rwkv_wk=rwkv_wk, rwkv_wv=rwkv_wv,
        rwkv_w0=rwkv_w0, rwkv_w1=rwkv_w1, rwkv_w2=rwkv_w2, rwkv_a0=rwkv_a0, rwkv_a1=rwkv_a1, rwkv_a2=rwkv_a2,
        rwkv_g1=rwkv_g1, rwkv_g2=rwkv_g2, rwkv_k_k=rwkv_k_k, rwkv_k_a=rwkv_k_a, rwkv_r_k=rwkv_r_k,
        rwkv_lnx_w=rwkv_lnx_w, rwkv_lnx_b=rwkv_lnx_b, rwkv_wo=rwkv_wo, norm_final=norm_final))
    y_p, pool_p, k_p, v_p, shift_p, wkv_p, ffn_weights = _trunk(x_prompt, 0, (None, None, None, None, None), p)
    y_s, pool_s, k_s, v_s, shift_s, wkv_s, _ = _trunk(
        x_sample, PAST_LEN, (cache_pool, cache_swa_k, cache_swa_v, state_shift, state_wkv), p, ffn_weights)
    return (y_p, y_s, pool_p, pool_s, k_p, k_s, v_p, v_s, shift_p, shift_s, wkv_p, wkv_s)
```

```python
import functools
import math

import jax
import jax.numpy as jnp
from jax import lax
from jax.experimental import pallas as pl
from jax.experimental.pallas import tpu as pltpu

F32 = jnp.float32
BF16 = jnp.bfloat16

D_MODEL = 1024
D_FF = 2816
NORM_EPS = 1e-6
CHUNK = 64
POOL_WINDOWS = (2, 4, 8, 16)
C_POOL = 512
POOL_GC = 128
POOL_HIST = 15
POOL_HALO = 16
HEAD_DIM = 64
N_Q_HEADS = 8
N_KV_HEADS = 2
GQA_GROUP = 4
WINDOW = 128
SWA_ROWS = 128
Q_W = 512
KV_W = 128
IN_EVEN = C_POOL + Q_W + 2 * KV_W
N_BUCKETS = 32
MAX_DISTANCE = 128
RWKV_HEAD = 64
RWKV_H = 16
N_PAIRS = RWKV_H // 2
LANES = 128
LNX_EPS = 64e-5
PAST_LEN = 4096
VMEM_LIMIT_BYTES = 56 * 1024 * 1024


def _params(sem):
    return pltpu.CompilerParams(dimension_semantics=sem, vmem_limit_bytes=VMEM_LIMIT_BYTES)


def _dot(a, b):
    return jnp.dot(a, b, preferred_element_type=F32)


def _dot_nt(a, b):
    return lax.dot_general(a, b, (((1,), (1,)), ((), ())), preferred_element_type=F32)


def _dot_tn(a, b):
    return lax.dot_general(a, b, (((0,), (0,)), ((), ())), preferred_element_type=F32)


def _rms(x, g):
    return x * lax.rsqrt(jnp.mean(x * x, axis=-1, keepdims=True) + NORM_EPS) * g


def _split(x):
    hi = x.astype(BF16)
    lo = (x - hi.astype(F32)).astype(BF16)
    return hi, lo


def _head_sum(x, e_ref, et_ref, split=True):
    if not split:
        return _dot(_dot(x.astype(BF16), e_ref[...]).astype(BF16), et_ref[...])
    hi, lo = _split(x)
    s = _dot(hi, e_ref[...]) + _dot(lo, e_ref[...])
    shi, slo = _split(s)
    return _dot(shi, et_ref[...]) + _dot(slo, et_ref[...])


def _ffn_kernel(x_ref, g_ref, wg_ref, wu_ref, wd_ref, *rest, f_chunk, final, convert):
    rest = list(rest)
    gf_ref = rest.pop(0) if final else None
    src_refs = [rest.pop(0) for _ in range(3)] if convert else []
    o_ref = rest.pop(0)
    dst_refs = [rest.pop(0) for _ in range(3)] if convert else []
    acc_ref, = rest
    for src, dst in zip(src_refs, dst_refs):
        dst[...] = src[...].astype(BF16)
    x = x_ref[...]
    h = _rms(x, g_ref[...]).astype(BF16)
    for j in range(D_FF // f_chunk):
        sl = slice(j * f_chunk, (j + 1) * f_chunk)
        gate = _dot(h, wg_ref[:, sl])
        up = _dot(h, wu_ref[:, sl])
        act = (gate * jax.nn.sigmoid(gate) * up).astype(BF16)
        part = _dot(act, wd_ref[sl, :])
        if j == 0:
            acc_ref[...] = part
        else:
            acc_ref[...] += part
    y = x + 0.5 * acc_ref[...]
    if final:
        y = _rms(y, gf_ref[...])
    o_ref[...] = y


def _resident(shape, index_map):
    return pl.BlockSpec(shape, index_map, pipeline_mode=pl.Buffered(1))


BF16_SUBLANES = 16


def _slab_steps(rows, steps):
    return next(k for k in range(steps, 0, -1) if rows % (k * BF16_SUBLANES) == 0)


def _ffn(x, g, weights, final_g=None, convert=None):
    T = x.shape[0]
    tm = min(512, T)
    steps = T // tm
    final = final_g is not None
    in_specs = [pl.BlockSpec((tm, D_MODEL), lambda i: (i, 0)), _resident((1, D_MODEL), lambda i: (0, 0))]
    in_specs += [_resident(w.shape, lambda i: (0, 0)) for w in weights]
    args = [x, g, *weights]
    out_specs = [pl.BlockSpec((tm, D_MODEL), lambda i: (i, 0))]
    out_shape = [jax.ShapeDtypeStruct((T, D_MODEL), F32)]
    if final:
        in_specs.append(_resident((1, D_MODEL), lambda i: (0, 0)))
        args.append(final_g)
    if convert is not None:
        stacks, l, j = convert
        for w in stacks:
            rows, cols = w.shape[2:]
            k = _slab_steps(rows, steps)
            in_specs.append(pl.BlockSpec((None, None, rows // k, cols),
                                         lambda i, k=k: (l, j, jnp.minimum(i, k - 1), 0)))
            out_specs.append(pl.BlockSpec((rows // k, cols), lambda i, k=k: (jnp.minimum(i, k - 1), 0)))
            out_shape.append(jax.ShapeDtypeStruct((rows, cols), BF16))
            args.append(w)
    outs = pl.pallas_call(
        functools.partial(_ffn_kernel, f_chunk=256, final=final, convert=convert is not None),
        grid=(steps,),
        in_specs=in_specs,
        out_specs=out_specs,
        out_shape=out_shape,
        scratch_shapes=[pltpu.VMEM((tm, D_MODEL), F32)],
        compiler_params=_params(("arbitrary",)),
    )(*args)
    return outs[0], tuple(outs[1:])


def _build_bias(bkt_ref, tab_ref, bias_ref, lq):
    bkt = bkt_ref[...]
    for h in range(N_Q_HEADS):
        b = jnp.zeros(bkt.shape, F32)
        for n in range(N_BUCKETS):
            b = jnp.where(bkt == n, tab_ref[n, h], b)
        g, i = divmod(h, GQA_GROUP)
        bias_ref[g, :, i * lq:(i + 1) * lq] = b


def _group_queries(q, g):
    return jnp.concatenate([q[:, h * HEAD_DIM:(h + 1) * HEAD_DIM]
                            for h in range(g * GQA_GROUP, (g + 1) * GQA_GROUP)], axis=0) * (HEAD_DIM ** -0.5)


def _sink_rows(sink_ref, lq):
    lane = lax.broadcasted_iota(jnp.int32, (1, GQA_GROUP * lq), 1)
    rows = []
    for g in range(N_KV_HEADS):
        r = jnp.zeros((1, GQA_GROUP * lq), F32)
        for i in range(GQA_GROUP):
            r = jnp.where(lane // lq == i, sink_ref[g * GQA_GROUP + i], r)
        rows.append(r)
    return rows


def _attn_core(qs, ks, vs, biases, sinks, valids):
    ss = [_dot_nt(k, q) + b for q, k, b in zip(qs, ks, biases)]
    ss = [s if ok is None else jnp.where(ok, s, -1e30) for s, ok in zip(ss, valids)]
    ms = [jnp.maximum(jnp.max(s, axis=0, keepdims=True), sk) for s, sk in zip(ss, sinks)]
    ps = [jnp.exp(s - m) for s, m in zip(ss, ms)]
    invs = [1.0 / (jnp.sum(p, axis=0, keepdims=True) + jnp.exp(sk - m)) for p, sk, m in zip(ps, sinks, ms)]
    return [_dot_tn((p * r).astype(BF16), v) for p, r, v in zip(ps, invs, vs)]


def _smem():
    return pl.BlockSpec(memory_space=pltpu.SMEM)


def _even_layer_kernel(x_ref, gn_ref, win_ref, pw_ref, ps_ref, hist_ref, kc_ref, vc_ref, bkt_ref, tab_ref, sink_ref,
                       wop_ref, woa_ref, o_ref, utail_o, ktail_o, vtail_o, halo_ref, kprev_ref, vprev_ref, bias_ref,
                       *, rows, chunk, pos0, masked):
    t = pl.program_id(1)

    @pl.when(jnp.logical_and(pl.program_id(0) == 0, t == 0))
    def _():
        _build_bias(bkt_ref, tab_ref, bias_ref, chunk)

    @pl.when(t == 0)
    def _():
        halo_ref[...] = hist_ref[0]
        kprev_ref[...] = kc_ref[0]
        vprev_ref[...] = vc_ref[0]

    x = x_ref[...]
    z = _dot(_rms(x, gn_ref[...]).astype(BF16), win_ref[...])
    u = z[:, :C_POOL]
    q = z[:, C_POOL:C_POOL + Q_W].astype(BF16)
    k_all = jnp.concatenate([kprev_ref[...], z[:, C_POOL + Q_W:C_POOL + Q_W + KV_W]], axis=0)
    v_all = jnp.concatenate([vprev_ref[...], z[:, C_POOL + Q_W + KV_W:]], axis=0)

    ext = jnp.concatenate([halo_ref[...], u], axis=0)
    pos = pos0 + t * rows + lax.broadcasted_iota(jnp.int32, (rows, POOL_GC), 0)
    pooled = []
    for gi, w in enumerate(POOL_WINDOWS):
        sl = slice(gi * POOL_GC, (gi + 1) * POOL_GC)
        s = ext[:, sl]
        span = 1
        while span < w:
            s = s + pltpu.roll(s, span, 0)
            span *= 2
        cnt = jnp.minimum(w, pos + 1).astype(F32)
        pooled.append(_dot((s[POOL_HALO:] / cnt - u[:, sl]).astype(BF16), pw_ref[gi]))
    pool_out = (jnp.concatenate(pooled, axis=-1) * ps_ref[...]).astype(BF16)

    lk = WINDOW + chunk
    kb = k_all.astype(BF16)
    vb = v_all.astype(BF16)
    sink_rows = _sink_rows(sink_ref, chunk)
    first_pos = t * rows - WINDOW + lax.broadcasted_iota(jnp.int32, (lk, 1), 0)
    nchunk = rows // chunk
    qs, ks, vs, biases, sinks, valids = [], [], [], [], [], []
    for j in range(nchunk):
        qj = q[j * chunk:(j + 1) * chunk]
        for g in range(N_KV_HEADS):
            qs.append(_group_queries(qj, g))
            ks.append(kb[j * chunk:j * chunk + lk, g * HEAD_DIM:(g + 1) * HEAD_DIM])
            vs.append(vb[j * chunk:j * chunk + lk, g * HEAD_DIM:(g + 1) * HEAD_DIM])
            biases.append(bias_ref[g])
            sinks.append(sink_rows[g])
            valids.append(first_pos + j * chunk >= 0 if masked else None)
    outs = _attn_core(qs, ks, vs, biases, sinks, valids)
    att = jnp.concatenate(
        [jnp.concatenate([outs[j * N_KV_HEADS + g][i * chunk:(i + 1) * chunk]
                          for g in range(N_KV_HEADS) for i in range(GQA_GROUP)], axis=-1) for j in range(nchunk)],
        axis=0).astype(BF16)

    o_ref[...] = x + _dot(pool_out, wop_ref[...]) + _dot(att, woa_ref[...])

    halo_ref[...] = ext[rows:]
    kprev_ref[...] = k_all[rows:]
    vprev_ref[...] = v_all[rows:]
    utail_o[0] = ext[rows:]
    ktail_o[0] = k_all[rows:]
    vtail_o[0] = v_all[rows:]


def _even_layer(x, gn, p, i, hist, k_cache, v_cache, bkt, B, L, pos0, chunk, masked):
    rows = min(4 * chunk, L)
    nt = L // rows
    lk = WINDOW + chunk
    row = pl.BlockSpec((rows, D_MODEL), lambda b, t: (b * nt + t, 0))
    per_seq = lambda n, m: pl.BlockSpec((1, n, m), lambda b, t: (b, 0, 0))
    return pl.pallas_call(
        functools.partial(_even_layer_kernel, rows=rows, chunk=chunk, pos0=pos0, masked=masked),
        grid=(B, nt),
        in_specs=[row, _resident((1, D_MODEL), lambda b, t: (0, 0)),
                  _resident((None, D_MODEL, IN_EVEN), lambda b, t: (i, 0, 0)),
                  _resident((None, len(POOL_WINDOWS), POOL_GC, POOL_GC), lambda b, t: (i, 0, 0, 0)),
                  _resident((1, C_POOL), lambda b, t: (0, 0)),
                  per_seq(POOL_HALO, C_POOL), per_seq(WINDOW, KV_W), per_seq(WINDOW, KV_W),
                  _resident((lk, chunk), lambda b, t: (0, 0)), _smem(), _smem(),
                  _resident((None, C_POOL, D_MODEL), lambda b, t: (i, 0, 0)),
                  _resident((None, Q_W, D_MODEL), lambda b, t: (i, 1, 0))],
        out_specs=[row, per_seq(POOL_HALO, C_POOL), per_seq(WINDOW, KV_W), per_seq(WINDOW, KV_W)],
        out_shape=[jax.ShapeDtypeStruct((B * L, D_MODEL), F32), jax.ShapeDtypeStruct((B, POOL_HALO, C_POOL), F32),
                   jax.ShapeDtypeStruct((B, WINDOW, KV_W), F32), jax.ShapeDtypeStruct((B, WINDOW, KV_W), F32)],
        scratch_shapes=[pltpu.VMEM((POOL_HALO, C_POOL), F32), pltpu.VMEM((WINDOW, KV_W), F32),
                        pltpu.VMEM((WINDOW, KV_W), F32), pltpu.VMEM((N_KV_HEADS, lk, GQA_GROUP * chunk), F32)],
        compiler_params=_params(("arbitrary", "arbitrary")),
    )(x, gn, p['w_in_even'], p['pool_w'], p['pool_scale'][i][None], hist, k_cache, v_cache, bkt,
      p['t5_table'], p['attn_sinks'][i], p['w_out_even'], p['w_out_even'])


def _t5_bucket(rel):
    half = N_BUCKETS // 2
    max_exact = half // 2
    side = jnp.where(rel > 0, half, 0)
    n = jnp.abs(rel)
    nf = jnp.maximum(n, max_exact).astype(F32)
    large = max_exact + (jnp.log(nf / max_exact) / math.log(MAX_DISTANCE / max_exact)
                         * (half - max_exact)).astype(jnp.int32)
    large = jnp.minimum(large, half - 1)
    return side + jnp.where(n < max_exact, n, large)


def _bucket_index(lq, lk, offset):
    rel = jnp.arange(lk)[:, None] - offset - jnp.arange(lq)[None, :]
    return _t5_bucket(rel).astype(jnp.int32)


def _rwkv_pre_kernel(x_ref, xp_ref, sh_ref, gn_ref, mu_ref, wr_ref, wk_ref, wv_ref, w1_ref, w2_ref,
                     a1_ref, a2_ref, g1_ref, g2_ref, w0_ref, a0_ref, kk_ref, ka_ref, rk_ref, e_ref, et_ref,
                     r_o, lw_o, k_o, v_o, kk_o, b_o, g_o, bonus_o, hs_o, *, tm):
    t = pl.program_id(1)
    gn = gn_ref[...]
    h = _rms(x_ref[...], gn)
    prev_tile_last = _rms(xp_ref[...], gn)[7:8]
    first_prev = jnp.where(t == 0, sh_ref[0], prev_tile_last)
    row = lax.broadcasted_iota(jnp.int32, (tm, 1), 0)
    h_prev = jnp.where(row == 0, first_prev, pltpu.roll(h, 1, 0))
    xx = h_prev - h

    hb = h.astype(BF16)
    xb = xx.astype(BF16)
    mu = mu_ref[...].astype(BF16)

    def mix(j):
        return hb + xb * mu[j:j + 1]

    zw = w0_ref[...] + _dot(jnp.tanh(_dot(mix(1), w1_ref[...])).astype(BF16), w2_ref[...])
    za = a0_ref[...] + _dot(_dot(mix(4), a1_ref[...]).astype(BF16), a2_ref[...])
    g = _dot(jax.nn.sigmoid(_dot(mix(5), g1_ref[...])).astype(BF16), g2_ref[...])
    r = _dot(mix(0), wr_ref[...])
    lw = -math.exp(-0.5) * jax.nn.sigmoid(zw)
    k = _dot(mix(2), wk_ref[...])
    a = jax.nn.sigmoid(za)
    v = _dot(mix(3), wv_ref[...])
    kk = k * kk_ref[...]
    kk = kk * lax.rsqrt(jnp.maximum(_head_sum(kk * kk, e_ref, et_ref, split=False), 1e-24))
    k = k * (1.0 + (a - 1.0) * ka_ref[...])
    b = kk * a
    for hp in range(N_PAIRS):
        sl = slice(hp * LANES, (hp + 1) * LANES)
        r_o[0, hp] = r[:, sl]
        lw_o[0, hp] = lw[:, sl]
        k_o[0, hp] = k[:, sl]
        v_o[0, hp] = v[:, sl]
        kk_o[0, hp] = kk[:, sl]
        b_o[0, hp] = b[:, sl]
    g_o[...] = g.astype(BF16)
    bonus_o[...] = (_head_sum(r * k * rk_ref[...], e_ref, et_ref, split=False) * v).astype(BF16)
    hs_o[0] = h[tm - 8:]


def _rwkv_pre(x, shift_prev, gn, p, i, B, L):
    tm = min(512, L)
    nt = L // tm
    vec = lambda: _resident((1, D_MODEL), lambda b, t: (0, 0))
    mat = lambda r, c: _resident((None, r, c), lambda b, t: (i, 0, 0))
    lora = p['rwkv_w1'].shape[-1], p['rwkv_a1'].shape[-1], p['rwkv_g1'].shape[-1]
    head_spec = pl.BlockSpec((1, N_PAIRS, tm, LANES), lambda b, t: (b, 0, t, 0))
    head_shape = jax.ShapeDtypeStruct((B, N_PAIRS, L, LANES), F32)
    return pl.pallas_call(
        functools.partial(_rwkv_pre_kernel, tm=tm),
        grid=(B, nt),
        in_specs=[
            pl.BlockSpec((tm, D_MODEL), lambda b, t: (b * nt + t, 0)),
            pl.BlockSpec((8, D_MODEL), lambda b, t: (jnp.maximum((b * nt + t) * (tm // 8) - 1, 0), 0)),
            pl.BlockSpec((1, 1, D_MODEL), lambda b, t: (b, 0, 0)),
            vec(),
            _resident((None, 6, D_MODEL), lambda b, t: (i, 0, 0)),
            mat(D_MODEL, D_MODEL), mat(D_MODEL, D_MODEL), mat(D_MODEL, D_MODEL),
            mat(D_MODEL, lora[0]), mat(lora[0], D_MODEL),
            mat(D_MODEL, lora[1]), mat(lora[1], D_MODEL),
            mat(D_MODEL, lora[2]), mat(lora[2], D_MODEL),
            vec(), vec(), vec(), vec(), vec(),
            _resident((D_MODEL, LANES), lambda b, t: (0, 0)),
            _resident((LANES, D_MODEL), lambda b, t: (0, 0)),
        ],
        out_specs=[head_spec] * 6 + [
            pl.BlockSpec((tm, D_MODEL), lambda b, t: (b * nt + t, 0)),
            pl.BlockSpec((tm, D_MODEL), lambda b, t: (b * nt + t, 0)),
            pl.BlockSpec((1, 8, D_MODEL), lambda b, t: (b, 0, 0)),
        ],
        out_shape=[head_shape] * 6 + [
            jax.ShapeDtypeStruct((B * L, D_MODEL), BF16),
            jax.ShapeDtypeStruct((B * L, D_MODEL), BF16),
            jax.ShapeDtypeStruct((B, 8, D_MODEL), F32),
        ],
        compiler_params=_params(("parallel", "arbitrary")),
    )(x, x, shift_prev, gn, p['rwkv_mu'], p['rwkv_wr'], p['rwkv_wk'], p['rwkv_wv'],
      p['rwkv_w1'], p['rwkv_w2'], p['rwkv_a1'], p['rwkv_a2'], p['rwkv_g1'], p['rwkv_g2'],
      p['rwkv_w0'][i][None], p['rwkv_a0'][i][None], p['rwkv_k_k'][i][None], p['rwkv_k_a'][i][None],
      p['rwkv_r_k'][i].reshape(1, D_MODEL), p['head_onehot'], p['head_onehot_t'])


DIAG = 16


def _replication_matrix(n):
    src = jnp.arange(n)[:, None]
    dst = jnp.arange(n)[None, :]
    return jnp.concatenate([(src == (dst // DIAG) * DIAG + s) for s in range(DIAG - 1)], axis=1).astype(BF16)


def _unit_lower_inverses(lows, c, rep_ref):
    n = 2 * c
    ti = lax.broadcasted_iota(jnp.int32, (c, n), 0)
    li = lax.broadcasted_iota(jnp.int32, (c, n), 1)
    si = li & (c - 1)
    head1 = li >= c

    def block_diag(x):
        return jnp.concatenate([jnp.where(head1, 0.0, x), jnp.where(head1, x, 0.0)], axis=0)

    if rep_ref is None:
        invs = [jnp.where(ti == si, 1.0, 0.0) + jnp.where(jnp.logical_and(ti == si + 1, (ti & 1) == 1), low, 0.0)
                for low in lows]
        m = 2
    else:
        pt = lax.broadcasted_iota(jnp.int32, (DIAG, n), 0)
        pl_ = lax.broadcasted_iota(jnp.int32, (DIAG, n), 1)
        blk = (pl_ & (c - 1)) // DIAG
        packed = []
        for low in lows:
            d = jnp.zeros((DIAG, n), F32)
            for i in range(c // DIAG):
                d = jnp.where(blk == i, low[i * DIAG:(i + 1) * DIAG], d)
            packed.append(d)
        rep = _dot(jnp.concatenate(packed, axis=0).astype(BF16), rep_ref[...])
        sols = [jnp.where(pt == (pl_ & (DIAG - 1)), 1.0, 0.0) for _ in lows]
        for s in range(DIAG - 1):
            sols = [sol + rep[p * DIAG:(p + 1) * DIAG, s * n:(s + 1) * n] * sol[s:s + 1]
                    for p, sol in enumerate(sols)]
        invs = [jnp.concatenate([jnp.where(blk == i, sol, 0.0) for i in range(c // DIAG)], axis=0) for sol in sols]
        m = DIAG
    while m < c:
        sh = m.bit_length() - 1
        sub = jnp.logical_and(jnp.logical_and((ti >> (sh + 1)) == (si >> (sh + 1)), ((ti >> sh) & 1) == 1),
                              ((si >> sh) & 1) == 0)
        diag = [block_diag(inv).astype(BF16) for inv in invs]
        half = [_dot(inv.astype(BF16), block_diag(jnp.where(sub, low, 0.0)).astype(BF16)).astype(BF16)
                for inv, low in zip(invs, lows)]
        invs = [inv + _dot(h, d) for inv, h, d in zip(invs, half, diag)]
        m *= 2
    return invs


def _scan_chunks(seqs, states, c, rep_ref):
    n = 2 * c
    row = lax.broadcasted_iota(jnp.int32, (c, LANES), 0)
    head1 = lax.broadcasted_iota(jnp.int32, (c, LANES), 1) >= RWKV_HEAD
    ti = lax.broadcasted_iota(jnp.int32, (c, n), 0)
    si = lax.broadcasted_iota(jnp.int32, (c, n), 1) & (c - 1)
    strict = ti > si
    incl = ti >= si
    pi = lax.broadcasted_iota(jnp.int32, (LANES, LANES), 0)
    pj = lax.broadcasted_iota(jnp.int32, (LANES, LANES), 1)
    same_head = (pi >= RWKV_HEAD) == (pj >= RWKV_HEAD)

    def stack(x):
        return jnp.concatenate([jnp.where(head1, 0.0, x), jnp.where(head1, x, 0.0)], axis=0)

    lhs, rhs, v_st, v_bf, bk_end, decay = [], [], [], [], [], []
    for r, lw, k, v, kk, b in seqs:
        cum = lw
        sh = 1
        while sh < c:
            cum = cum + jnp.where(row >= sh, pltpu.roll(cum, sh, 0), 0.0)
            sh *= 2
        tot = cum[c - 1:c]
        grow = jnp.exp(-cum)
        tail = jnp.exp(tot - cum)
        a_t = -kk * jnp.exp(cum - lw)
        r_t = r * jnp.exp(cum)
        lhs.append(jnp.concatenate([a_t, r_t], axis=0).astype(BF16))
        rhs.append(jnp.concatenate([stack(b * grow), stack(k * grow)], axis=0).astype(BF16))
        v_st.append(stack(v).astype(BF16))
        v_bf.append(v.astype(BF16))
        bk_end.append(jnp.concatenate([b * tail, k * tail], axis=0).astype(BF16))
        decay.append(jnp.exp(tot))
    grams = [_dot_nt(x, y) for x, y in zip(lhs, rhs)]
    lows = [jnp.where(strict, g[:c, :n], 0.0) for g in grams]
    m_rb = [jnp.where(incl, g[c:, :n], 0.0).astype(BF16) for g in grams]
    m_v = [jnp.concatenate([jnp.where(strict, g[:c, n:], 0.0), jnp.where(incl, g[c:, n:], 0.0)], axis=0).astype(BF16)
           for g in grams]
    invs = [inv.astype(BF16) for inv in _unit_lower_inverses(lows, c, rep_ref)]
    from_v = [_dot(m, vs) for m, vs in zip(m_v, v_st)]

    ys = []
    states = list(states)
    npair = len(states)
    for j in range(len(seqs) // npair):
        sl = slice(j * npair, (j + 1) * npair)
        from_state = [_dot_nt(x, s.astype(BF16)) for x, s in zip(lhs[sl], states)]
        rhs_sa = [stack(f[:c] + fv[:c]).astype(BF16) for f, fv in zip(from_state, from_v[sl])]
        sas = [_dot(inv, x) for inv, x in zip(invs[sl], rhs_sa)]
        ys += [f[c:] + fv[c:] + _dot(m, stack(sa).astype(BF16))
               for f, fv, m, sa in zip(from_state, from_v[sl], m_rb[sl], sas)]
        upd = [_dot_tn(jnp.concatenate([sa.astype(BF16), vb], axis=0), be)
               for sa, vb, be in zip(sas, v_bf[sl], bk_end[sl])]
        states = [s * d + jnp.where(same_head, u, 0.0) for s, d, u in zip(states, decay[sl], upd)]
    return ys, states


def _rwkv_scan_kernel(r_ref, lw_ref, k_ref, v_ref, kk_ref, b_ref, s0_ref, *rest, c, sub, vpu_diag):
    if vpu_diag:
        rep_ref, y_ref, so_ref, s_ref = rest
    else:
        rep_ref = None
        y_ref, so_ref, s_ref = rest
    ci = pl.program_id(1)

    @pl.when(ci == 0)
    def _():
        s_ref[...] = s0_ref[0]

    seqs = [tuple(ref[0, hp, j * c:(j + 1) * c] for ref in (r_ref, lw_ref, k_ref, v_ref, kk_ref, b_ref))
            for j in range(sub) for hp in range(N_PAIRS)]
    ys, new_states = _scan_chunks(seqs, [s_ref[hp] for hp in range(N_PAIRS)], c, rep_ref)
    for j in range(sub):
        for hp in range(N_PAIRS):
            y_ref[0, hp, j * c:(j + 1) * c] = ys[j * N_PAIRS + hp]
    for hp in range(N_PAIRS):
        s_ref[hp] = new_states[hp]

    @pl.when(ci == pl.num_programs(1) - 1)
    def _():
        so_ref[0] = s_ref[...]


def _rwkv_scan(r, lw, k, v, kk, b, s0, B, L):
    c = min(CHUNK, L)
    nc = L // c
    sub = next(s for s in (4, 2, 1) if nc % s == 0)
    nc //= sub
    seq = pl.BlockSpec((1, N_PAIRS, sub * c, LANES), lambda bi, ci: (bi, 0, ci, 0))
    st = pl.BlockSpec((1, N_PAIRS, LANES, LANES), lambda bi, ci: (bi, 0, 0, 0))
    vpu_diag = 2 * c == LANES
    extra_specs, extra_args = [], []
    if vpu_diag:
        extra_specs = [_resident((LANES, (DIAG - 1) * LANES), lambda bi, ci: (0, 0))]
        extra_args = [_replication_matrix(LANES)]
    return pl.pallas_call(
        functools.partial(_rwkv_scan_kernel, c=c, sub=sub, vpu_diag=vpu_diag),
        grid=(B, nc),
        in_specs=[seq] * 6 + [st] + extra_specs,
        out_specs=[seq, st],
        out_shape=[jax.ShapeDtypeStruct((B, N_PAIRS, L, LANES), F32),
                   jax.ShapeDtypeStruct((B, N_PAIRS, LANES, LANES), F32)],
        scratch_shapes=[pltpu.VMEM((N_PAIRS, LANES, LANES), F32)],
        compiler_params=_params(("parallel", "arbitrary")),
    )(r, lw, k, v, kk, b, s0, *extra_args)


def _rwkv_post_kernel(x_ref, y_ref, g_ref, bonus_ref, lw_ref, lb_ref, e_ref, et_ref, wo_ref, o_ref, *, parts):
    rows = x_ref.shape[0] // parts
    sls = [slice(i * rows, (i + 1) * rows) for i in range(parts)]
    head_sum = lambda t, split: _head_sum(t, e_ref, et_ref, split)
    ys = [jnp.concatenate([y_ref[0, hp, sl] for hp in range(N_PAIRS)], axis=-1) for sl in sls]
    ds = [y - head_sum(y, True) * (1.0 / RWKV_HEAD) for y in ys]
    var = [head_sum(d * d, False) * (1.0 / RWKV_HEAD) for d in ds]
    yn = [d * lax.rsqrt(vr + LNX_EPS) * lw_ref[...] + lb_ref[...] for d, vr in zip(ds, var)]
    gated = [((n + bonus_ref[sl, :].astype(F32)) * g_ref[sl, :].astype(F32)).astype(BF16) for n, sl in zip(yn, sls)]
    for gt, sl in zip(gated, sls):
        o_ref[sl, :] = x_ref[sl, :] + _dot(gt, wo_ref[...])


def _rwkv_post(x, y, g, bonus, p, i, B, L):
    tm = min(512, L)
    nt = L // tm
    vec = lambda: _resident((1, D_MODEL), lambda b, t: (0, 0))
    head_spec = pl.BlockSpec((1, N_PAIRS, tm, LANES), lambda b, t: (b, 0, t, 0))
    row = pl.BlockSpec((tm, D_MODEL), lambda b, t: (b * nt + t, 0))
    return pl.pallas_call(
        functools.partial(_rwkv_post_kernel, parts=2 if tm >= 256 else 1),
        grid=(B, nt),
        in_specs=[row, head_spec, row, row, vec(), vec(),
                  _resident((D_MODEL, LANES), lambda b, t: (0, 0)),
                  _resident((LANES, D_MODEL), lambda b, t: (0, 0)),
                  _resident((None, D_MODEL, D_MODEL), lambda b, t: (i, 0, 0))],
        out_specs=row,
        out_shape=jax.ShapeDtypeStruct((B * L, D_MODEL), F32),
        compiler_params=_params(("parallel", "parallel")),
    )(x, y, g, bonus, p['rwkv_lnx_w'][i][None], p['rwkv_lnx_b'][i][None],
      p['head_onehot'], p['head_onehot_t'], p['rwkv_wo'])


def _pack_state(s):
    B = s.shape[0]
    s = s.reshape(B, N_PAIRS, 2, RWKV_HEAD, RWKV_HEAD)
    z = jnp.zeros_like(s[:, :, 0])
    top = jnp.concatenate([s[:, :, 0], z], axis=-1)
    bot = jnp.concatenate([z, s[:, :, 1]], axis=-1)
    return jnp.concatenate([top, bot], axis=-2)


def _unpack_state(s):
    B = s.shape[0]
    return jnp.stack([s[:, :, :RWKV_HEAD, :RWKV_HEAD], s[:, :, RWKV_HEAD:, RWKV_HEAD:]], axis=2).reshape(
        B, RWKV_H, RWKV_HEAD, RWKV_HEAD)


def _trunk(x, pos0, caches, p, ffn_weights=None):
    pool_c, k_c, v_c, shift_c, wkv_c = caches
    B, L, _ = x.shape
    depth = p['norm_mix'].shape[0]
    stepping = k_c is not None
    x = x.reshape(B * L, D_MODEL)
    stacks = (p['ffn_w_gate'], p['ffn_w_up'], p['ffn_w_down'])
    chain = ffn_weights is None
    if chain:
        ffn_weights = [tuple(w[0, 0].astype(BF16) for w in stacks)]

    def ffn(x, l, j, final_g=None):
        n = 2 * l + j
        convert = (stacks, *divmod(n + 1, 2)) if chain and n + 1 < 2 * depth else None
        x, converted = _ffn(x, p['norm_ffn'][l, j][None], ffn_weights[n], final_g, convert)
        if convert is not None:
            ffn_weights.append(converted)
        return x

    if stepping:
        bkt = _bucket_index(L, SWA_ROWS + L, SWA_ROWS)
    else:
        bkt = _bucket_index(CHUNK, WINDOW + CHUNK, WINDOW)
    new_pool, new_k, new_v, new_shift, new_wkv = [], [], [], [], []
    for l in range(depth):
        i = l // 2
        x = ffn(x, l, 0)
        gn = p['norm_mix'][l][None]
        if l % 2 == 0:
            if stepping:
                hist = jnp.pad(pool_c[i], ((0, 0), (POOL_HALO - POOL_HIST, 0), (0, 0)))
                k_cache = k_c[i].reshape(B, SWA_ROWS, KV_W)
                v_cache = v_c[i].reshape(B, SWA_ROWS, KV_W)
            else:
                hist = jnp.zeros((B, POOL_HALO, C_POOL), F32)
                k_cache = v_cache = jnp.zeros((B, WINDOW, KV_W), F32)
            x, u_tail, k_tail, v_tail = _even_layer(x, gn, p, i, hist, k_cache, v_cache, bkt, B, L, pos0,
                                                    chunk=L if stepping else CHUNK, masked=not stepping)
            new_pool.append(u_tail[:, -POOL_HIST:])
            new_k.append(k_tail.reshape(B, SWA_ROWS, N_KV_HEADS, HEAD_DIM))
            new_v.append(v_tail.reshape(B, SWA_ROWS, N_KV_HEADS, HEAD_DIM))
        else:
            if stepping:
                shift_prev = shift_c[i][:, None, :]
                s0 = _pack_state(wkv_c[i])
            else:
                shift_prev = jnp.zeros((B, 1, D_MODEL), F32)
                s0 = jnp.zeros((B, N_PAIRS, LANES, LANES), F32)
            r, lw, k, v, kk, b, g, bonus, hs = _rwkv_pre(x, shift_prev, gn, p, i, B, L)
            y, s_new = _rwkv_scan(r, lw, k, v, kk, b, s0, B, L)
            x = _rwkv_post(x, y, g, bonus, p, i, B, L)
            new_shift.append(hs[:, 7])
            new_wkv.append(_unpack_state(s_new))
        final_g = p['norm_final'][None] if l == depth - 1 else None
        x = ffn(x, l, 1, final_g)
    return (x.reshape(B, L, D_MODEL), jnp.stack(new_pool), jnp.stack(new_k), jnp.stack(new_v),
            jnp.stack(new_shift), jnp.stack(new_wkv), ffn_weights)


_MATMUL_WEIGHTS = ('w_in_even', 'pool_w', 'w_out_even', 'rwkv_wr', 'rwkv_wk', 'rwkv_wv', 'rwkv_w1', 'rwkv_w2',
                   'rwkv_a1', 'rwkv_a2', 'rwkv_g1', 'rwkv_g2', 'rwkv_wo')


def _prepare(p):
    p = dict(p)
    for name in _MATMUL_WEIGHTS:
        p[name] = p[name].astype(BF16)
    onehot = (jnp.arange(D_MODEL)[:, None] // RWKV_HEAD == jnp.arange(LANES)[None, :]).astype(BF16)
    p['head_onehot'] = onehot
    p['head_onehot_t'] = onehot.T
    return p


def kernel(x_prompt, x_sample, cache_pool, cache_swa_k, cache_swa_v, state_shift, state_wkv, t5_table, norm_ffn, ffn_w_gate, ffn_w_up, ffn_w_down, norm_mix, w_in_even, pool_w, pool_scale, attn_sinks, w_out_even, rwkv_mu, rwkv_wr, rwkv_wk, rwkv_wv, rwkv_w0, rwkv_w1, rwkv_w2, rwkv_a0, rwkv_a1, rwkv_a2, rwkv_g1, rwkv_g2, rwkv_k_k, rwkv_k_a, rwkv_r_k, rwkv_lnx_w, rwkv_lnx_b, rwkv_wo, norm_final):
    p = _prepare(dict(
        t5_table=t5_table, norm_ffn=norm_ffn, ffn_w_gate=ffn_w_gate, ffn_w_up=ffn_w_up, ffn_w_down=ffn_w_down,
        norm_mix=norm_mix, w_in_even=w_in_even, pool_w=pool_w, pool_scale=pool_scale, attn_sinks=attn_sinks,
        w_out_even=w_out_even, rwkv_mu=rwkv_mu, rwkv_wr=rwkv_wr, rwkv_wk=rwkv_wk, rwkv_wv=rwkv_wv,
        rwkv_w0=rwkv_w0, rwkv_w1=rwkv_w1, rwkv_w2=rwkv_w2, rwkv_a0=rwkv_a0, rwkv_a1=rwkv_a1, rwkv_a2=rwkv_a2,
        rwkv_g1=rwkv_g1, rwkv_g2=rwkv_g2, rwkv_k_k=rwkv_k_k, rwkv_k_a=rwkv_k_a, rwkv_r_k=rwkv_r_k,
        rwkv_lnx_w=rwkv_lnx_w, rwkv_lnx_b=rwkv_lnx_b, rwkv_wo=rwkv_wo, norm_final=norm_final))
    y_p, pool_p, k_p, v_p, shift_p, wkv_p, ffn_weights = _trunk(x_prompt, 0, (None, None, None, None, None), p)
    y_s, pool_s, k_s, v_s, shift_s, wkv_s, _ = _trunk(
        x_sample, PAST_LEN, (cache_pool, cache_swa_k, cache_swa_v, state_shift, state_wkv), p, ffn_weights)
    return (y_p, y_s, pool_p, pool_s, k_p, k_s, v_p, v_s, shift_p, shift_s, wkv_p, wkv_s)
```

```python
import functools
import math

import jax
import jax.numpy as jnp
from jax import lax
from jax.experimental import pallas as pl
from jax.experimental.pallas import tpu as pltpu

F32 = jnp.float32
BF16 = jnp.bfloat16

D_MODEL = 1024
D_FF = 2816
NORM_EPS = 1e-6
CHUNK = 64
POOL_WINDOWS = (2, 4, 8, 16)
C_POOL = 512
POOL_GC = 128
POOL_HIST = 15
POOL_HALO = 16
HEAD_DIM = 64
N_Q_HEADS = 8
N_KV_HEADS = 2
GQA_GROUP = 4
WINDOW = 128
SWA_ROWS = 128
Q_W = 512
KV_W = 128
IN_EVEN = C_POOL + Q_W + 2 * KV_W
N_BUCKETS = 32
MAX_DISTANCE = 128
RWKV_HEAD = 64
RWKV_H = 16
N_PAIRS = RWKV_H // 2
LANES = 128
LNX_EPS = 64e-5
PAST_LEN = 4096
VMEM_LIMIT_BYTES = 56 * 1024 * 1024


def _params(sem):
    return pltpu.CompilerParams(dimension_semantics=sem, vmem_limit_bytes=VMEM_LIMIT_BYTES)


def _dot(a, b):
    return jnp.dot(a, b, preferred_element_type=F32)


def _dot_nt(a, b):
    return lax.dot_general(a, b, (((1,), (1,)), ((), ())), preferred_element_type=F32)


def _dot_tn(a, b):
    return lax.dot_general(a, b, (((0,), (0,)), ((), ())), preferred_element_type=F32)


def _rms(x, g):
    return x * lax.rsqrt(jnp.mean(x * x, axis=-1, keepdims=True) + NORM_EPS) * g


def _split(x):
    hi = x.astype(BF16)
    lo = (x - hi.astype(F32)).astype(BF16)
    return hi, lo


def _head_sum(x, e_ref, et_ref, split=True):
    if not split:
        return _dot(_dot(x.astype(BF16), e_ref[...]).astype(BF16), et_ref[...])
    hi, lo = _split(x)
    s = _dot(hi, e_ref[...]) + _dot(lo, e_ref[...])
    shi, slo = _split(s)
    return _dot(shi, et_ref[...]) + _dot(slo, et_ref[...])


def _ffn_kernel(x_ref, xs_ref, g_ref, wg_ref, wu_ref, wd_ref, *rest, f_chunk, final, convert):
    rest = list(rest)
    gf_ref = rest.pop(0) if final else None
    src_refs = [rest.pop(0) for _ in range(3)] if convert else []
    o_ref, os_ref = rest.pop(0), rest.pop(0)
    dst_refs = [rest.pop(0) for _ in range(3)] if convert else []
    acc_ref, = rest
    for src, dst in zip(src_refs, dst_refs):
        dst[...] = src[...].astype(BF16)

    def rows(src_ref, dst_ref):
        n = src_ref.shape[0]
        x = src_ref[...]
        h = _rms(x, g_ref[...]).astype(BF16)
        for j in range(D_FF // f_chunk):
            sl = slice(j * f_chunk, (j + 1) * f_chunk)
            gate = _dot(h, wg_ref[:, sl])
            up = _dot(h, wu_ref[:, sl])
            act = (gate * jax.nn.sigmoid(gate) * up).astype(BF16)
            part = _dot(act, wd_ref[sl, :])
            if j == 0:
                acc_ref[:n] = part
            else:
                acc_ref[:n] += part
        y = x + 0.5 * acc_ref[:n]
        if final:
            y = _rms(y, gf_ref[...])
        dst_ref[...] = y

    rows(x_ref, o_ref)

    @pl.when(pl.program_id(0) == pl.num_programs(0) - 1)
    def _():
        rows(xs_ref, os_ref)


def _resident(shape, index_map):
    return pl.BlockSpec(shape, index_map, pipeline_mode=pl.Buffered(1))


BF16_SUBLANES = 16


def _slab_steps(rows, steps):
    return next(k for k in range(steps, 0, -1) if rows % (k * BF16_SUBLANES) == 0)


def _ffn(x, xs, g, weights, final_g=None, convert=None):
    T = x.shape[0]
    tm = min(512, T)
    steps = T // tm
    Ts = xs.shape[0]
    assert Ts <= tm
    final = final_g is not None
    in_specs = [pl.BlockSpec((tm, D_MODEL), lambda i: (i, 0)), _resident((Ts, D_MODEL), lambda i: (0, 0)),
                _resident((1, D_MODEL), lambda i: (0, 0))]
    in_specs += [_resident(w.shape, lambda i: (0, 0)) for w in weights]
    args = [x, xs, g, *weights]
    out_specs = [pl.BlockSpec((tm, D_MODEL), lambda i: (i, 0)), pl.BlockSpec((Ts, D_MODEL), lambda i: (0, 0))]
    out_shape = [jax.ShapeDtypeStruct((T, D_MODEL), F32), jax.ShapeDtypeStruct((Ts, D_MODEL), F32)]
    if final:
        in_specs.append(_resident((1, D_MODEL), lambda i: (0, 0)))
        args.append(final_g)
    if convert is not None:
        stacks, l, j = convert
        for w in stacks:
            rows, cols = w.shape[2:]
            k = _slab_steps(rows, steps)
            in_specs.append(pl.BlockSpec((None, None, rows // k, cols),
                                         lambda i, k=k: (l, j, jnp.minimum(i, k - 1), 0)))
            out_specs.append(pl.BlockSpec((rows // k, cols), lambda i, k=k: (jnp.minimum(i, k - 1), 0)))
            out_shape.append(jax.ShapeDtypeStruct((rows, cols), BF16))
            args.append(w)
    outs = pl.pallas_call(
        functools.partial(_ffn_kernel, f_chunk=256, final=final, convert=convert is not None),
        grid=(steps,),
        in_specs=in_specs,
        out_specs=out_specs,
        out_shape=out_shape,
        scratch_shapes=[pltpu.VMEM((tm, D_MODEL), F32)],
        compiler_params=_params(("arbitrary",)),
    )(*args)
    return outs[0], outs[1], tuple(outs[2:])


def _build_bias(bkt_ref, tab_ref, bias_ref, lq):
    bkt = bkt_ref[...]
    for h in range(N_Q_HEADS):
        b = jnp.zeros(bkt.shape, F32)
        for n in range(N_BUCKETS):
            b = jnp.where(bkt == n, tab_ref[n, h], b)
        g, i = divmod(h, GQA_GROUP)
        bias_ref[g, :, i * lq:(i + 1) * lq] = b


def _group_queries(q, g):
    return jnp.concatenate([q[:, h * HEAD_DIM:(h + 1) * HEAD_DIM]
                            for h in range(g * GQA_GROUP, (g + 1) * GQA_GROUP)], axis=0) * (HEAD_DIM ** -0.5)


def _sink_rows(sink_ref, lq):
    lane = lax.broadcasted_iota(jnp.int32, (1, GQA_GROUP * lq), 1)
    rows = []
    for g in range(N_KV_HEADS):
        r = jnp.zeros((1, GQA_GROUP * lq), F32)
        for i in range(GQA_GROUP):
            r = jnp.where(lane // lq == i, sink_ref[g * GQA_GROUP + i], r)
        rows.append(r)
    return rows


def _attn_core(qs, ks, vs, biases, sinks, valids):
    ss = [_dot_nt(k, q) + b for q, k, b in zip(qs, ks, biases)]
    ss = [s if ok is None else jnp.where(ok, s, -1e30) for s, ok in zip(ss, valids)]
    ms = [jnp.maximum(jnp.max(s, axis=0, keepdims=True), sk) for s, sk in zip(ss, sinks)]
    ps = [jnp.exp(s - m) for s, m in zip(ss, ms)]
    invs = [1.0 / (jnp.sum(p, axis=0, keepdims=True) + jnp.exp(sk - m)) for p, sk, m in zip(ps, sinks, ms)]
    return [_dot_tn((p * r).astype(BF16), v) for p, r, v in zip(ps, invs, vs)]


def _smem():
    return pl.BlockSpec(memory_space=pltpu.SMEM)


def _even_layer_kernel(x_ref, gn_ref, win_ref, pw_ref, ps_ref, hist_ref, kc_ref, vc_ref, bkt_ref, tab_ref, sink_ref,
                       wop_ref, woa_ref, o_ref, utail_o, ktail_o, vtail_o, halo_ref, kprev_ref, vprev_ref, bias_ref,
                       *, rows, chunk, pos0, masked):
    t = pl.program_id(1)

    @pl.when(jnp.logical_and(pl.program_id(0) == 0, t == 0))
    def _():
        _build_bias(bkt_ref, tab_ref, bias_ref, chunk)

    @pl.when(t == 0)
    def _():
        halo_ref[...] = hist_ref[0]
        kprev_ref[...] = kc_ref[0]
        vprev_ref[...] = vc_ref[0]

    x = x_ref[...]
    z = _dot(_rms(x, gn_ref[...]).astype(BF16), win_ref[...])
    u = z[:, :C_POOL]
    q = z[:, C_POOL:C_POOL + Q_W].astype(BF16)
    k_all = jnp.concatenate([kprev_ref[...], z[:, C_POOL + Q_W:C_POOL + Q_W + KV_W]], axis=0)
    v_all = jnp.concatenate([vprev_ref[...], z[:, C_POOL + Q_W + KV_W:]], axis=0)

    ext = jnp.concatenate([halo_ref[...], u], axis=0)
    pos = pos0 + t * rows + lax.broadcasted_iota(jnp.int32, (rows, POOL_GC), 0)
    pooled = []
    for gi, w in enumerate(POOL_WINDOWS):
        sl = slice(gi * POOL_GC, (gi + 1) * POOL_GC)
        s = ext[:, sl]
        span = 1
        while span < w:
            s = s + pltpu.roll(s, span, 0)
            span *= 2
        cnt = jnp.minimum(w, pos + 1).astype(F32)
        pooled.append(_dot((s[POOL_HALO:] / cnt - u[:, sl]).astype(BF16), pw_ref[gi]))
    pool_out = (jnp.concatenate(pooled, axis=-1) * ps_ref[...]).astype(BF16)

    lk = WINDOW + chunk
    kb = k_all.astype(BF16)
    vb = v_all.astype(BF16)
    sink_rows = _sink_rows(sink_ref, chunk)
    first_pos = t * rows - WINDOW + lax.broadcasted_iota(jnp.int32, (lk, 1), 0)
    nchunk = rows // chunk
    qs, ks, vs, biases, sinks, valids = [], [], [], [], [], []
    for j in range(nchunk):
        qj = q[j * chunk:(j + 1) * chunk]
        for g in range(N_KV_HEADS):
            qs.append(_group_queries(qj, g))
            ks.append(kb[j * chunk:j * chunk + lk, g * HEAD_DIM:(g + 1) * HEAD_DIM])
            vs.append(vb[j * chunk:j * chunk + lk, g * HEAD_DIM:(g + 1) * HEAD_DIM])
            biases.append(bias_ref[g])
            sinks.append(sink_rows[g])
            valids.append(first_pos + j * chunk >= 0 if masked else None)
    outs = _attn_core(qs, ks, vs, biases, sinks, valids)
    att = jnp.concatenate(
        [jnp.concatenate([outs[j * N_KV_HEADS + g][i * chunk:(i + 1) * chunk]
                          for g in range(N_KV_HEADS) for i in range(GQA_GROUP)], axis=-1) for j in range(nchunk)],
        axis=0).astype(BF16)

    o_ref[...] = x + _dot(pool_out, wop_ref[...]) + _dot(att, woa_ref[...])

    halo_ref[...] = ext[rows:]
    kprev_ref[...] = k_all[rows:]
    vprev_ref[...] = v_all[rows:]
    utail_o[0] = ext[rows:]
    ktail_o[0] = k_all[rows:]
    vtail_o[0] = v_all[rows:]


def _even_layer(x, gn, p, i, hist, k_cache, v_cache, bkt, B, L, pos0, chunk, masked):
    rows = min(4 * chunk, L)
    nt = L // rows
    lk = WINDOW + chunk
    row = pl.BlockSpec((rows, D_MODEL), lambda b, t: (b * nt + t, 0))
    per_seq = lambda n, m: pl.BlockSpec((1, n, m), lambda b, t: (b, 0, 0))
    return pl.pallas_call(
        functools.partial(_even_layer_kernel, rows=rows, chunk=chunk, pos0=pos0, masked=masked),
        grid=(B, nt),
        in_specs=[row, _resident((1, D_MODEL), lambda b, t: (0, 0)),
                  _resident((None, D_MODEL, IN_EVEN), lambda b, t: (i, 0, 0)),
                  _resident((None, len(POOL_WINDOWS), POOL_GC, POOL_GC), lambda b, t: (i, 0, 0, 0)),
                  _resident((1, C_POOL), lambda b, t: (0, 0)),
                  per_seq(POOL_HALO, C_POOL), per_seq(WINDOW, KV_W), per_seq(WINDOW, KV_W),
                  _resident((lk, chunk), lambda b, t: (0, 0)), _smem(), _smem(),
                  _resident((None, C_POOL, D_MODEL), lambda b, t: (i, 0, 0)),
                  _resident((None, Q_W, D_MODEL), lambda b, t: (i, 1, 0))],
        out_specs=[row, per_seq(POOL_HALO, C_POOL), per_seq(WINDOW, KV_W), per_seq(WINDOW, KV_W)],
        out_shape=[jax.ShapeDtypeStruct((B * L, D_MODEL), F32), jax.ShapeDtypeStruct((B, POOL_HALO, C_POOL), F32),
                   jax.ShapeDtypeStruct((B, WINDOW, KV_W), F32), jax.ShapeDtypeStruct((B, WINDOW, KV_W), F32)],
        scratch_shapes=[pltpu.VMEM((POOL_HALO, C_POOL), F32), pltpu.VMEM((WINDOW, KV_W), F32),
                        pltpu.VMEM((WINDOW, KV_W), F32), pltpu.VMEM((N_KV_HEADS, lk, GQA_GROUP * chunk), F32)],
        compiler_params=_params(("arbitrary", "arbitrary")),
    )(x, gn, p['w_in_even'], p['pool_w'], p['pool_scale'][i][None], hist, k_cache, v_cache, bkt,
      p['t5_table'], p['attn_sinks'][i], p['w_out_even'], p['w_out_even'])


def _t5_bucket(rel):
    half = N_BUCKETS // 2
    max_exact = half // 2
    side = jnp.where(rel > 0, half, 0)
    n = jnp.abs(rel)
    nf = jnp.maximum(n, max_exact).astype(F32)
    large = max_exact + (jnp.log(nf / max_exact) / math.log(MAX_DISTANCE / max_exact)
                         * (half - max_exact)).astype(jnp.int32)
    large = jnp.minimum(large, half - 1)
    return side + jnp.where(n < max_exact, n, large)


def _bucket_index(lq, lk, offset):
    rel = jnp.arange(lk)[:, None] - offset - jnp.arange(lq)[None, :]
    return _t5_bucket(rel).astype(jnp.int32)


def _rwkv_pre_kernel(x_ref, xp_ref, sh_ref, gn_ref, mu_ref, wr_ref, wk_ref, wv_ref, w1_ref, w2_ref,
                     a1_ref, a2_ref, g1_ref, g2_ref, w0_ref, a0_ref, kk_ref, ka_ref, rk_ref, e_ref, et_ref,
                     r_o, lw_o, k_o, v_o, kk_o, b_o, g_o, bonus_o, hs_o, *, tm):
    t = pl.program_id(1)
    gn = gn_ref[...]
    h = _rms(x_ref[...], gn)
    prev_tile_last = _rms(xp_ref[...], gn)[7:8]
    first_prev = jnp.where(t == 0, sh_ref[0], prev_tile_last)
    row = lax.broadcasted_iota(jnp.int32, (tm, 1), 0)
    h_prev = jnp.where(row == 0, first_prev, pltpu.roll(h, 1, 0))
    xx = h_prev - h

    hb = h.astype(BF16)
    xb = xx.astype(BF16)
    mu = mu_ref[...].astype(BF16)

    def mix(j):
        return hb + xb * mu[j:j + 1]

    zw = w0_ref[...] + _dot(jnp.tanh(_dot(mix(1), w1_ref[...])).astype(BF16), w2_ref[...])
    za = a0_ref[...] + _dot(_dot(mix(4), a1_ref[...]).astype(BF16), a2_ref[...])
    g = _dot(jax.nn.sigmoid(_dot(mix(5), g1_ref[...])).astype(BF16), g2_ref[...])
    r = _dot(mix(0), wr_ref[...])
    lw = -math.exp(-0.5) * jax.nn.sigmoid(zw)
    k = _dot(mix(2), wk_ref[...])
    a = jax.nn.sigmoid(za)
    v = _dot(mix(3), wv_ref[...])
    kk = k * kk_ref[...]
    kk = kk * lax.rsqrt(jnp.maximum(_head_sum(kk * kk, e_ref, et_ref, split=False), 1e-24))
    k = k * (1.0 + (a - 1.0) * ka_ref[...])
    b = kk * a
    for hp in range(N_PAIRS):
        sl = slice(hp * LANES, (hp + 1) * LANES)
        r_o[0, hp] = r[:, sl]
        lw_o[0, hp] = lw[:, sl]
        k_o[0, hp] = k[:, sl]
        v_o[0, hp] = v[:, sl]
        kk_o[0, hp] = kk[:, sl]
        b_o[0, hp] = b[:, sl]
    g_o[...] = g.astype(BF16)
    bonus_o[...] = (_head_sum(r * k * rk_ref[...], e_ref, et_ref, split=False) * v).astype(BF16)
    hs_o[0] = h[tm - 8:]


def _rwkv_pre(x, shift_prev, gn, p, i, B, L):
    tm = min(512, L)
    nt = L // tm
    vec = lambda: _resident((1, D_MODEL), lambda b, t: (0, 0))
    mat = lambda r, c: _resident((None, r, c), lambda b, t: (i, 0, 0))
    lora = p['rwkv_w1'].shape[-1], p['rwkv_a1'].shape[-1], p['rwkv_g1'].shape[-1]
    head_spec = pl.BlockSpec((1, N_PAIRS, tm, LANES), lambda b, t: (b, 0, t, 0))
    head_shape = jax.ShapeDtypeStruct((B, N_PAIRS, L, LANES), F32)
    return pl.pallas_call(
        functools.partial(_rwkv_pre_kernel, tm=tm),
        grid=(B, nt),
        in_specs=[
            pl.BlockSpec((tm, D_MODEL), lambda b, t: (b * nt + t, 0)),
            pl.BlockSpec((8, D_MODEL), lambda b, t: (jnp.maximum((b * nt + t) * (tm // 8) - 1, 0), 0)),
            pl.BlockSpec((1, 1, D_MODEL), lambda b, t: (b, 0, 0)),
            vec(),
            _resident((None, 6, D_MODEL), lambda b, t: (i, 0, 0)),
            mat(D_MODEL, D_MODEL), mat(D_MODEL, D_MODEL), mat(D_MODEL, D_MODEL),
            mat(D_MODEL, lora[0]), mat(lora[0], D_MODEL),
            mat(D_MODEL, lora[1]), mat(lora[1], D_MODEL),
            mat(D_MODEL, lora[2]), mat(lora[2], D_MODEL),
            vec(), vec(), vec(), vec(), vec(),
            _resident((D_MODEL, LANES), lambda b, t: (0, 0)),
            _resident((LANES, D_MODEL), lambda b, t: (0, 0)),
        ],
        out_specs=[head_spec] * 6 + [
            pl.BlockSpec((tm, D_MODEL), lambda b, t: (b * nt + t, 0)),
            pl.BlockSpec((tm, D_MODEL), lambda b, t: (b * nt + t, 0)),
            pl.BlockSpec((1, 8, D_MODEL), lambda b, t: (b, 0, 0)),
        ],
        out_shape=[head_shape] * 6 + [
            jax.ShapeDtypeStruct((B * L, D_MODEL), BF16),
            jax.ShapeDtypeStruct((B * L, D_MODEL), BF16),
            jax.ShapeDtypeStruct((B, 8, D_MODEL), F32),
        ],
        compiler_params=_params(("parallel", "arbitrary")),
    )(x, x, shift_prev, gn, p['rwkv_mu'], p['rwkv_wr'], p['rwkv_wk'], p['rwkv_wv'],
      p['rwkv_w1'], p['rwkv_w2'], p['rwkv_a1'], p['rwkv_a2'], p['rwkv_g1'], p['rwkv_g2'],
      p['rwkv_w0'][i][None], p['rwkv_a0'][i][None], p['rwkv_k_k'][i][None], p['rwkv_k_a'][i][None],
      p['rwkv_r_k'][i].reshape(1, D_MODEL), p['head_onehot'], p['head_onehot_t'])


DIAG = 16


def _replication_matrix(n):
    src = jnp.arange(n)[:, None]
    dst = jnp.arange(n)[None, :]
    return jnp.concatenate([(src == (dst // DIAG) * DIAG + s) for s in range(DIAG - 1)], axis=1).astype(BF16)


def _unit_lower_inverses(lows, c, rep_ref):
    n = 2 * c
    ti = lax.broadcasted_iota(jnp.int32, (c, n), 0)
    li = lax.broadcasted_iota(jnp.int32, (c, n), 1)
    si = li & (c - 1)
    head1 = li >= c

    def block_diag(x):
        return jnp.concatenate([jnp.where(head1, 0.0, x), jnp.where(head1, x, 0.0)], axis=0)

    if rep_ref is None:
        invs = [jnp.where(ti == si, 1.0, 0.0) + jnp.where(jnp.logical_and(ti == si + 1, (ti & 1) == 1), low, 0.0)
                for low in lows]
        m = 2
    else:
        pt = lax.broadcasted_iota(jnp.int32, (DIAG, n), 0)
        pl_ = lax.broadcasted_iota(jnp.int32, (DIAG, n), 1)
        blk = (pl_ & (c - 1)) // DIAG
        packed = []
        for low in lows:
            d = jnp.zeros((DIAG, n), F32)
            for i in range(c // DIAG):
                d = jnp.where(blk == i, low[i * DIAG:(i + 1) * DIAG], d)
            packed.append(d)
        rep = _dot(jnp.concatenate(packed, axis=0).astype(BF16), rep_ref[...])
        sols = [jnp.where(pt == (pl_ & (DIAG - 1)), 1.0, 0.0) for _ in lows]
        for s in range(DIAG - 1):
            sols = [sol + rep[p * DIAG:(p + 1) * DIAG, s * n:(s + 1) * n] * sol[s:s + 1]
                    for p, sol in enumerate(sols)]
        invs = [jnp.concatenate([jnp.where(blk == i, sol, 0.0) for i in range(c // DIAG)], axis=0) for sol in sols]
        m = DIAG
    while m < c:
        sh = m.bit_length() - 1
        sub = jnp.logical_and(jnp.logical_and((ti >> (sh + 1)) == (si >> (sh + 1)), ((ti >> sh) & 1) == 1),
                              ((si >> sh) & 1) == 0)
        diag = [block_diag(inv).astype(BF16) for inv in invs]
        half = [_dot(inv.astype(BF16), block_diag(jnp.where(sub, low, 0.0)).astype(BF16)).astype(BF16)
                for inv, low in zip(invs, lows)]
        invs = [inv + _dot(h, d) for inv, h, d in zip(invs, half, diag)]
        m *= 2
    return invs


def _scan_chunks(seqs, states, c, rep_ref):
    n = 2 * c
    row = lax.broadcasted_iota(jnp.int32, (c, LANES), 0)
    head1 = lax.broadcasted_iota(jnp.int32, (c, LANES), 1) >= RWKV_HEAD
    ti = lax.broadcasted_iota(jnp.int32, (c, n), 0)
    si = lax.broadcasted_iota(jnp.int32, (c, n), 1) & (c - 1)
    strict = ti > si
    incl = ti >= si
    pi = lax.broadcasted_iota(jnp.int32, (LANES, LANES), 0)
    pj = lax.broadcasted_iota(jnp.int32, (LANES, LANES), 1)
    same_head = (pi >= RWKV_HEAD) == (pj >= RWKV_HEAD)

    def stack(x):
        return jnp.concatenate([jnp.where(head1, 0.0, x), jnp.where(head1, x, 0.0)], axis=0)

    lhs, rhs, v_st, v_bf, bk_end, decay = [], [], [], [], [], []
    for r, lw, k, v, kk, b in seqs:
        cum = lw
        sh = 1
        while sh < c:
            cum = cum + jnp.where(row >= sh, pltpu.roll(cum, sh, 0), 0.0)
            sh *= 2
        tot = cum[c - 1:c]
        grow = jnp.exp(-cum)
        tail = jnp.exp(tot - cum)
        a_t = -kk * jnp.exp(cum - lw)
        r_t = r * jnp.exp(cum)
        lhs.append(jnp.concatenate([a_t, r_t], axis=0).astype(BF16))
        rhs.append(jnp.concatenate([stack(b * grow), stack(k * grow)], axis=0).astype(BF16))
        v_st.append(stack(v).astype(BF16))
        v_bf.append(v.astype(BF16))
        bk_end.append(jnp.concatenate([b * tail, k * tail], axis=0).astype(BF16))
        decay.append(jnp.exp(tot))
    grams = [_dot_nt(x, y) for x, y in zip(lhs, rhs)]
    lows = [jnp.where(strict, g[:c, :n], 0.0) for g in grams]
    m_rb = [jnp.where(incl, g[c:, :n], 0.0).astype(BF16) for g in grams]
    m_v = [jnp.concatenate([jnp.where(strict, g[:c, n:], 0.0), jnp.where(incl, g[c:, n:], 0.0)], axis=0).astype(BF16)
           for g in grams]
    invs = [inv.astype(BF16) for inv in _unit_lower_inverses(lows, c, rep_ref)]
    from_v = [_dot(m, vs) for m, vs in zip(m_v, v_st)]

    ys = []
    states = list(states)
    npair = len(states)
    for j in range(len(seqs) // npair):
        sl = slice(j * npair, (j + 1) * npair)
        from_state = [_dot_nt(x, s.astype(BF16)) for x, s in zip(lhs[sl], states)]
        rhs_sa = [stack(f[:c] + fv[:c]).astype(BF16) for f, fv in zip(from_state, from_v[sl])]
        sas = [_dot(inv, x) for inv, x in zip(invs[sl], rhs_sa)]
        ys += [f[c:] + fv[c:] + _dot(m, stack(sa).astype(BF16))
               for f, fv, m, sa in zip(from_state, from_v[sl], m_rb[sl], sas)]
        upd = [_dot_tn(jnp.concatenate([sa.astype(BF16), vb], axis=0), be)
               for sa, vb, be in zip(sas, v_bf[sl], bk_end[sl])]
        states = [s * d + jnp.where(same_head, u, 0.0) for s, d, u in zip(states, decay[sl], upd)]
    return ys, states


def _rwkv_scan_kernel(r_ref, lw_ref, k_ref, v_ref, kk_ref, b_ref, s0_ref, *rest, c, sub, vpu_diag):
    if vpu_diag:
        rep_ref, y_ref, so_ref, s_ref = rest
    else:
        rep_ref = None
        y_ref, so_ref, s_ref = rest
    ci = pl.program_id(1)

    @pl.when(ci == 0)
    def _():
        s_ref[...] = s0_ref[0]

    seqs = [tuple(ref[0, hp, j * c:(j + 1) * c] for ref in (r_ref, lw_ref, k_ref, v_ref, kk_ref, b_ref))
            for j in range(sub) for hp in range(N_PAIRS)]
    ys, new_states = _scan_chunks(seqs, [s_ref[hp] for hp in range(N_PAIRS)], c, rep_ref)
    for j in range(sub):
        for hp in range(N_PAIRS):
            y_ref[0, hp, j * c:(j + 1) * c] = ys[j * N_PAIRS + hp]
    for hp in range(N_PAIRS):
        s_ref[hp] = new_states[hp]

    @pl.when(ci == pl.num_programs(1) - 1)
    def _():
        so_ref[0] = s_ref[...]


def _rwkv_scan(r, lw, k, v, kk, b, s0, B, L):
    c = min(CHUNK, L)
    nc = L // c
    sub = next(s for s in (4, 2, 1) if nc % s == 0)
    nc //= sub
    seq = pl.BlockSpec((1, N_PAIRS, sub * c, LANES), lambda bi, ci: (bi, 0, ci, 0))
    st = pl.BlockSpec((1, N_PAIRS, LANES, LANES), lambda bi, ci: (bi, 0, 0, 0))
    vpu_diag = 2 * c == LANES
    extra_specs, extra_args = [], []
    if vpu_diag:
        extra_specs = [_resident((LANES, (DIAG - 1) * LANES), lambda bi, ci: (0, 0))]
        extra_args = [_replication_matrix(LANES)]
    return pl.pallas_call(
        functools.partial(_rwkv_scan_kernel, c=c, sub=sub, vpu_diag=vpu_diag),
        grid=(B, nc),
        in_specs=[seq] * 6 + [st] + extra_specs,
        out_specs=[seq, st],
        out_shape=[jax.ShapeDtypeStruct((B, N_PAIRS, L, LANES), F32),
                   jax.ShapeDtypeStruct((B, N_PAIRS, LANES, LANES), F32)],
        scratch_shapes=[pltpu.VMEM((N_PAIRS, LANES, LANES), F32)],
        compiler_params=_params(("parallel", "arbitrary")),
    )(r, lw, k, v, kk, b, s0, *extra_args)


def _rwkv_post_kernel(x_ref, y_ref, g_ref, bonus_ref, lw_ref, lb_ref, e_ref, et_ref, wo_ref, o_ref, *, parts):
    rows = x_ref.shape[0] // parts
    sls = [slice(i * rows, (i + 1) * rows) for i in range(parts)]
    head_sum = lambda t, split: _head_sum(t, e_ref, et_ref, split)
    ys = [jnp.concatenate([y_ref[0, hp, sl] for hp in range(N_PAIRS)], axis=-1) for sl in sls]
    ds = [y - head_sum(y, True) * (1.0 / RWKV_HEAD) for y in ys]
    var = [head_sum(d * d, False) * (1.0 / RWKV_HEAD) for d in ds]
    yn = [d * lax.rsqrt(vr + LNX_EPS) * lw_ref[...] + lb_ref[...] for d, vr in zip(ds, var)]
    gated = [((n + bonus_ref[sl, :].astype(F32)) * g_ref[sl, :].astype(F32)).astype(BF16) for n, sl in zip(yn, sls)]
    for gt, sl in zip(gated, sls):
        o_ref[sl, :] = x_ref[sl, :] + _dot(gt, wo_ref[...])


def _rwkv_post(x, y, g, bonus, p, i, B, L):
    tm = min(512, L)
    nt = L // tm
    vec = lambda: _resident((1, D_MODEL), lambda b, t: (0, 0))
    head_spec = pl.BlockSpec((1, N_PAIRS, tm, LANES), lambda b, t: (b, 0, t, 0))
    row = pl.BlockSpec((tm, D_MODEL), lambda b, t: (b * nt + t, 0))
    return pl.pallas_call(
        functools.partial(_rwkv_post_kernel, parts=2 if tm >= 256 else 1),
        grid=(B, nt),
        in_specs=[row, head_spec, row, row, vec(), vec(),
                  _resident((D_MODEL, LANES), lambda b, t: (0, 0)),
                  _resident((LANES, D_MODEL), lambda b, t: (0, 0)),
                  _resident((None, D_MODEL, D_MODEL), lambda b, t: (i, 0, 0))],
        out_specs=row,
        out_shape=jax.ShapeDtypeStruct((B * L, D_MODEL), F32),
        compiler_params=_params(("parallel", "parallel")),
    )(x, y, g, bonus, p['rwkv_lnx_w'][i][None], p['rwkv_lnx_b'][i][None],
      p['head_onehot'], p['head_onehot_t'], p['rwkv_wo'])


def _pack_state(s):
    B = s.shape[0]
    s = s.reshape(B, N_PAIRS, 2, RWKV_HEAD, RWKV_HEAD)
    z = jnp.zeros_like(s[:, :, 0])
    top = jnp.concatenate([s[:, :, 0], z], axis=-1)
    bot = jnp.concatenate([z, s[:, :, 1]], axis=-1)
    return jnp.concatenate([top, bot], axis=-2)


def _unpack_state(s):
    B = s.shape[0]
    return jnp.stack([s[:, :, :RWKV_HEAD, :RWKV_HEAD], s[:, :, RWKV_HEAD:, RWKV_HEAD:]], axis=2).reshape(
        B, RWKV_H, RWKV_HEAD, RWKV_HEAD)


class _Stream:
    def __init__(self, x, pos0, caches):
        self.B, self.L, _ = x.shape
        self.x = x.reshape(self.B * self.L, D_MODEL)
        self.pos0 = pos0
        self.pool_c, self.k_c, self.v_c, self.shift_c, self.wkv_c = caches
        self.stepping = self.k_c is not None
        if self.stepping:
            self.bkt = _bucket_index(self.L, SWA_ROWS + self.L, SWA_ROWS)
        else:
            self.bkt = _bucket_index(CHUNK, WINDOW + CHUNK, WINDOW)
        self.new = {name: [] for name in ('pool', 'k', 'v', 'shift', 'wkv')}

    def mixer(self, l, p):
        B, L, i = self.B, self.L, l // 2
        gn = p['norm_mix'][l][None]
        if l % 2 == 0:
            if self.stepping:
                hist = jnp.pad(self.pool_c[i], ((0, 0), (POOL_HALO - POOL_HIST, 0), (0, 0)))
                k_cache = self.k_c[i].reshape(B, SWA_ROWS, KV_W)
                v_cache = self.v_c[i].reshape(B, SWA_ROWS, KV_W)
            else:
                hist = jnp.zeros((B, POOL_HALO, C_POOL), F32)
                k_cache = v_cache = jnp.zeros((B, WINDOW, KV_W), F32)
            self.x, u_tail, k_tail, v_tail = _even_layer(
                self.x, gn, p, i, hist, k_cache, v_cache, self.bkt, B, L, self.pos0,
                chunk=L if self.stepping else CHUNK, masked=not self.stepping)
            self.new['pool'].append(u_tail[:, -POOL_HIST:])
            self.new['k'].append(k_tail.reshape(B, SWA_ROWS, N_KV_HEADS, HEAD_DIM))
            self.new['v'].append(v_tail.reshape(B, SWA_ROWS, N_KV_HEADS, HEAD_DIM))
        else:
            if self.stepping:
                shift_prev = self.shift_c[i][:, None, :]
                s0 = _pack_state(self.wkv_c[i])
            else:
                shift_prev = jnp.zeros((B, 1, D_MODEL), F32)
                s0 = jnp.zeros((B, N_PAIRS, LANES, LANES), F32)
            r, lw, k, v, kk, b, g, bonus, hs = _rwkv_pre(self.x, shift_prev, gn, p, i, B, L)
            y, s_new = _rwkv_scan(r, lw, k, v, kk, b, s0, B, L)
            self.x = _rwkv_post(self.x, y, g, bonus, p, i, B, L)
            self.new['shift'].append(hs[:, 7])
            self.new['wkv'].append(_unpack_state(s_new))

    def results(self):
        return (self.x.reshape(self.B, self.L, D_MODEL),) + tuple(
            jnp.stack(self.new[name]) for name in ('pool', 'k', 'v', 'shift', 'wkv'))


def _trunk(long, short, p):
    depth = p['norm_mix'].shape[0]
    stacks = (p['ffn_w_gate'], p['ffn_w_up'], p['ffn_w_down'])
    weights = tuple(w[0, 0].astype(BF16) for w in stacks)
    for l in range(depth):
        for j in range(2):
            n = 2 * l + j
            convert = (stacks, *divmod(n + 1, 2)) if n + 1 < 2 * depth else None
            final_g = p['norm_final'][None] if n == 2 * depth - 1 else None
            long.x, short.x, weights = _ffn(long.x, short.x, p['norm_ffn'][l, j][None], weights, final_g, convert)
            if j == 0:
                long.mixer(l, p)
                short.mixer(l, p)
    return long.results(), short.results()


_MATMUL_WEIGHTS = ('w_in_even', 'pool_w', 'w_out_even', 'rwkv_wr', 'rwkv_wk', 'rwkv_wv', 'rwkv_w1', 'rwkv_w2',
                   'rwkv_a1', 'rwkv_a2', 'rwkv_g1', 'rwkv_g2', 'rwkv_wo')


def _prepare(p):
    p = dict(p)
    for name in _MATMUL_WEIGHTS:
        p[name] = p[name].astype(BF16)
    onehot = (jnp.arange(D_MODEL)[:, None] // RWKV_HEAD == jnp.arange(LANES)[None, :]).astype(BF16)
    p['head_onehot'] = onehot
    p['head_onehot_t'] = onehot.T
    return p


def kernel(x_prompt, x_sample, cache_pool, cache_swa_k, cache_swa_v, state_shift, state_wkv, t5_table, norm_ffn, ffn_w_gate, ffn_w_up, ffn_w_down, norm_mix, w_in_even, pool_w, pool_scale, attn_sinks, w_out_even, rwkv_mu, rwkv_wr, rwkv_wk, rwkv_wv, rwkv_w0, rwkv_w1, rwkv_w2, rwkv_a0, rwkv_a1, rwkv_a2, rwkv_g1, rwkv_g2, rwkv_k_k, rwkv_k_a, rwkv_r_k, rwkv_lnx_w, rwkv_lnx_b, rwkv_wo, norm_final):
    p = _prepare(dict(
        t5_table=t5_table, norm_ffn=norm_ffn, ffn_w_gate=ffn_w_gate, ffn_w_up=ffn_w_up, ffn_w_down=ffn_w_down,
        norm_mix=norm_mix, w_in_even=w_in_even, pool_w=pool_w, pool_scale=pool_scale, attn_sinks=attn_sinks,
        w_out_even=w_out_even, rwkv_mu=rwkv_mu, rwkv_wr=rwkv_wr, rwkv_wk=rwkv_wk, rwkv_wv=rwkv_wv,
        rwkv_w0=rwkv_w0, rwkv_w1=rwkv_w1, rwkv_w2=rwkv_w2, rwkv_a0=rwkv_a0, rwkv_a1=rwkv_a1, rwkv_a2=rwkv_a2,
        rwkv_g1=rwkv_g1, rwkv_g2=rwkv_g2, rwkv_k_k=rwkv_k_k, rwkv_k_a=rwkv_k_a, rwkv_r_k=rwkv_r_k,
        rwkv_lnx_w=rwkv_lnx_w, rwkv_lnx_b=rwkv_lnx_b, rwkv_wo=rwkv_wo, norm_final=norm_final))
    prompt = _Stream(x_prompt, 0, (None, None, None, None, None))
    sample = _Stream(x_sample, PAST_LEN, (cache_pool, cache_swa_k, cache_swa_v, state_shift, state_wkv))
    (y_p, pool_p, k_p, v_p, shift_p, wkv_p), (y_s, pool_s, k_s, v_s, shift_s, wkv_s) = _trunk(prompt, sample, p)
    return (y_p, y_s, pool_p, pool_s, k_p, k_s, v_p, v_s, shift_p, shift_s, wkv_p, wkv_s)
```

```python
import functools
import math

import jax
import jax.numpy as jnp
import numpy as np
from jax import lax
from jax.experimental import pallas as pl
from jax.experimental.pallas import tpu as pltpu

F32 = jnp.float32
BF16 = jnp.bfloat16

D_MODEL = 1024
D_FF = 2816
NORM_EPS = 1e-6
CHUNK = 64
POOL_WINDOWS = (2, 4, 8, 16)
C_POOL = 512
POOL_GC = 128
POOL_HIST = 15
POOL_HALO = 16
HEAD_DIM = 64
N_Q_HEADS = 8
N_KV_HEADS = 2
GQA_GROUP = 4
WINDOW = 128
SWA_ROWS = 128
Q_W = 512
KV_W = 128
IN_EVEN = C_POOL + Q_W + 2 * KV_W
N_BUCKETS = 32
MAX_DISTANCE = 128
RWKV_HEAD = 64
RWKV_H = 16
N_PAIRS = RWKV_H // 2
LANES = 128
LNX_EPS = 64e-5
PAST_LEN = 4096
VMEM_LIMIT_BYTES = 56 * 1024 * 1024


def _params(sem):
    return pltpu.CompilerParams(dimension_semantics=sem, vmem_limit_bytes=VMEM_LIMIT_BYTES)


def _dot(a, b):
    return jnp.dot(a, b, preferred_element_type=F32)


def _dot_nt(a, b):
    return lax.dot_general(a, b, (((1,), (1,)), ((), ())), preferred_element_type=F32)


def _dot_tn(a, b):
    return lax.dot_general(a, b, (((0,), (0,)), ((), ())), preferred_element_type=F32)


def _rms(x, g):
    return x * lax.rsqrt(jnp.mean(x * x, axis=-1, keepdims=True) + NORM_EPS) * g


def _split(x):
    hi = x.astype(BF16)
    lo = (x - hi.astype(F32)).astype(BF16)
    return hi, lo


def _head_sum(x, e_ref, et_ref, split=True):
    if not split:
        return _dot(_dot(x.astype(BF16), e_ref[...]).astype(BF16), et_ref[...])
    hi, lo = _split(x)
    s = _dot(hi, e_ref[...]) + _dot(lo, e_ref[...])
    shi, slo = _split(s)
    return _dot(shi, et_ref[...]) + _dot(slo, et_ref[...])


def _ffn_kernel(x_ref, xs_ref, g_ref, wg_ref, wu_ref, wd_ref, *rest, f_chunk, final, convert):
    rest = list(rest)
    gf_ref = rest.pop(0) if final else None
    src_refs = [rest.pop(0) for _ in range(3)] if convert else []
    o_ref, os_ref = rest.pop(0), rest.pop(0)
    dst_refs = [rest.pop(0) for _ in range(3)] if convert else []
    acc_ref, = rest
    for src, dst in zip(src_refs, dst_refs):
        dst[...] = src[...].astype(BF16)

    def rows(src_ref, dst_ref):
        n = src_ref.shape[0]
        x = src_ref[...]
        h = _rms(x, g_ref[...]).astype(BF16)
        for j in range(D_FF // f_chunk):
            sl = slice(j * f_chunk, (j + 1) * f_chunk)
            gate = _dot(h, wg_ref[:, sl])
            up = _dot(h, wu_ref[:, sl])
            act = (gate * jax.nn.sigmoid(gate) * up).astype(BF16)
            part = _dot(act, wd_ref[sl, :])
            if j == 0:
                acc_ref[:n] = part
            else:
                acc_ref[:n] += part
        y = x + 0.5 * acc_ref[:n]
        if final:
            y = _rms(y, gf_ref[...])
        dst_ref[...] = y

    rows(x_ref, o_ref)

    @pl.when(pl.program_id(0) == pl.num_programs(0) - 1)
    def _():
        rows(xs_ref, os_ref)


def _resident(shape, index_map):
    return pl.BlockSpec(shape, index_map, pipeline_mode=pl.Buffered(1))


BF16_SUBLANES = 16


def _slab_steps(rows, steps):
    return next(k for k in range(steps, 0, -1) if rows % (k * BF16_SUBLANES) == 0)


def _ffn(x, xs, g, weights, final_g=None, convert=None):
    T = x.shape[0]
    tm = min(1024, T)
    steps = T // tm
    Ts = xs.shape[0]
    assert Ts <= tm
    final = final_g is not None
    in_specs = [pl.BlockSpec((tm, D_MODEL), lambda i: (i, 0)), _resident((Ts, D_MODEL), lambda i: (0, 0)),
                _resident((1, D_MODEL), lambda i: (0, 0))]
    in_specs += [_resident(w.shape, lambda i: (0, 0)) for w in weights]
    args = [x, xs, g, *weights]
    out_specs = [pl.BlockSpec((tm, D_MODEL), lambda i: (i, 0)), pl.BlockSpec((Ts, D_MODEL), lambda i: (0, 0))]
    out_shape = [jax.ShapeDtypeStruct((T, D_MODEL), F32), jax.ShapeDtypeStruct((Ts, D_MODEL), F32)]
    if final:
        in_specs.append(_resident((1, D_MODEL), lambda i: (0, 0)))
        args.append(final_g)
    if convert is not None:
        stacks, l, j = convert
        for w in stacks:
            rows, cols = w.shape[2:]
            k = _slab_steps(rows, steps)
            in_specs.append(pl.BlockSpec((None, None, rows // k, cols),
                                         lambda i, k=k: (l, j, jnp.minimum(i, k - 1), 0)))
            out_specs.append(pl.BlockSpec((rows // k, cols), lambda i, k=k: (jnp.minimum(i, k - 1), 0)))
            out_shape.append(jax.ShapeDtypeStruct((rows, cols), BF16))
            args.append(w)
    outs = pl.pallas_call(
        functools.partial(_ffn_kernel, f_chunk=256, final=final, convert=convert is not None),
        grid=(steps,),
        in_specs=in_specs,
        out_specs=out_specs,
        out_shape=out_shape,
        scratch_shapes=[pltpu.VMEM((tm, D_MODEL), F32)],
        compiler_params=_params(("arbitrary",)),
    )(*args)
    return outs[0], outs[1], tuple(outs[2:])


def _build_bias(bkt_ref, tab_ref, bias_ref, lq):
    bkt = bkt_ref[...]
    for h in range(N_Q_HEADS):
        b = jnp.zeros(bkt.shape, F32)
        for n in range(N_BUCKETS):
            b = jnp.where(bkt == n, tab_ref[n, h], b)
        g, i = divmod(h, GQA_GROUP)
        bias_ref[g, :, i * lq:(i + 1) * lq] = b


def _group_queries(q, g):
    return jnp.concatenate([q[:, h * HEAD_DIM:(h + 1) * HEAD_DIM]
                            for h in range(g * GQA_GROUP, (g + 1) * GQA_GROUP)], axis=0) * (HEAD_DIM ** -0.5)


def _sink_rows(sink_ref, lq):
    lane = lax.broadcasted_iota(jnp.int32, (1, GQA_GROUP * lq), 1)
    rows = []
    for g in range(N_KV_HEADS):
        r = jnp.zeros((1, GQA_GROUP * lq), F32)
        for i in range(GQA_GROUP):
            r = jnp.where(lane // lq == i, sink_ref[g * GQA_GROUP + i], r)
        rows.append(r)
    return rows


def _attn_core(qs, ks, vs, biases, sinks, valids):
    ss = [_dot_nt(k, q) + b for q, k, b in zip(qs, ks, biases)]
    ss = [s if ok is None else jnp.where(ok, s, -1e30) for s, ok in zip(ss, valids)]
    ms = [jnp.maximum(jnp.max(s, axis=0, keepdims=True), sk) for s, sk in zip(ss, sinks)]
    ps = [jnp.exp(s - m) for s, m in zip(ss, ms)]
    invs = [1.0 / (jnp.sum(p, axis=0, keepdims=True) + jnp.exp(sk - m)) for p, sk, m in zip(ps, sinks, ms)]
    return [_dot_tn((p * r).astype(BF16), v) for p, r, v in zip(ps, invs, vs)]


def _smem():
    return pl.BlockSpec(memory_space=pltpu.SMEM)


def _even_layer_kernel(x_ref, gn_ref, win_ref, pw_ref, ps_ref, hist_ref, kc_ref, vc_ref, bkt_ref, tab_ref, sink_ref,
                       wop_ref, woa_ref, o_ref, utail_o, ktail_o, vtail_o, halo_ref, kprev_ref, vprev_ref, bias_ref,
                       *, rows, chunk, pos0, masked):
    t = pl.program_id(1)

    @pl.when(jnp.logical_and(pl.program_id(0) == 0, t == 0))
    def _():
        _build_bias(bkt_ref, tab_ref, bias_ref, chunk)

    @pl.when(t == 0)
    def _():
        halo_ref[...] = hist_ref[0]
        kprev_ref[...] = kc_ref[0]
        vprev_ref[...] = vc_ref[0]

    x = x_ref[...]
    z = _dot(_rms(x, gn_ref[...]).astype(BF16), win_ref[...])
    u = z[:, :C_POOL]
    q = z[:, C_POOL:C_POOL + Q_W].astype(BF16)
    k_all = jnp.concatenate([kprev_ref[...], z[:, C_POOL + Q_W:C_POOL + Q_W + KV_W]], axis=0)
    v_all = jnp.concatenate([vprev_ref[...], z[:, C_POOL + Q_W + KV_W:]], axis=0)

    ext = jnp.concatenate([halo_ref[...], u], axis=0)
    pos = pos0 + t * rows + lax.broadcasted_iota(jnp.int32, (rows, POOL_GC), 0)
    pooled = []
    for gi, w in enumerate(POOL_WINDOWS):
        sl = slice(gi * POOL_GC, (gi + 1) * POOL_GC)
        s = ext[:, sl]
        span = 1
        while span < w:
            s = s + pltpu.roll(s, span, 0)
            span *= 2
        cnt = jnp.minimum(w, pos + 1).astype(F32)
        pooled.append(_dot((s[POOL_HALO:] / cnt - u[:, sl]).astype(BF16), pw_ref[gi]))
    pool_out = (jnp.concatenate(pooled, axis=-1) * ps_ref[...]).astype(BF16)

    lk = WINDOW + chunk
    kb = k_all.astype(BF16)
    vb = v_all.astype(BF16)
    sink_rows = _sink_rows(sink_ref, chunk)
    first_pos = t * rows - WINDOW + lax.broadcasted_iota(jnp.int32, (lk, 1), 0)
    nchunk = rows // chunk
    qs, ks, vs, biases, sinks, valids = [], [], [], [], [], []
    for j in range(nchunk):
        qj = q[j * chunk:(j + 1) * chunk]
        for g in range(N_KV_HEADS):
            qs.append(_group_queries(qj, g))
            ks.append(kb[j * chunk:j * chunk + lk, g * HEAD_DIM:(g + 1) * HEAD_DIM])
            vs.append(vb[j * chunk:j * chunk + lk, g * HEAD_DIM:(g + 1) * HEAD_DIM])
            biases.append(bias_ref[g])
            sinks.append(sink_rows[g])
            valids.append(first_pos + j * chunk >= 0 if masked else None)
    outs = _attn_core(qs, ks, vs, biases, sinks, valids)
    att = jnp.concatenate(
        [jnp.concatenate([outs[j * N_KV_HEADS + g][i * chunk:(i + 1) * chunk]
                          for g in range(N_KV_HEADS) for i in range(GQA_GROUP)], axis=-1) for j in range(nchunk)],
        axis=0).astype(BF16)

    o_ref[...] = x + _dot(pool_out, wop_ref[...]) + _dot(att, woa_ref[...])

    halo_ref[...] = ext[rows:]
    kprev_ref[...] = k_all[rows:]
    vprev_ref[...] = v_all[rows:]
    utail_o[0] = ext[rows:]
    ktail_o[0] = k_all[rows:]
    vtail_o[0] = v_all[rows:]


def _even_layer(x, gn, p, i, hist, k_cache, v_cache, bkt, B, L, pos0, chunk, masked):
    rows = min(4 * chunk, L)
    nt = L // rows
    lk = WINDOW + chunk
    row = pl.BlockSpec((rows, D_MODEL), lambda b, t: (b * nt + t, 0))
    per_seq = lambda n, m: pl.BlockSpec((1, n, m), lambda b, t: (b, 0, 0))
    return pl.pallas_call(
        functools.partial(_even_layer_kernel, rows=rows, chunk=chunk, pos0=pos0, masked=masked),
        grid=(B, nt),
        in_specs=[row, _resident((1, D_MODEL), lambda b, t: (0, 0)),
                  _resident((None, D_MODEL, IN_EVEN), lambda b, t: (i, 0, 0)),
                  _resident((None, len(POOL_WINDOWS), POOL_GC, POOL_GC), lambda b, t: (i, 0, 0, 0)),
                  _resident((1, C_POOL), lambda b, t: (0, 0)),
                  per_seq(POOL_HALO, C_POOL), per_seq(WINDOW, KV_W), per_seq(WINDOW, KV_W),
                  _resident((lk, chunk), lambda b, t: (0, 0)), _smem(), _smem(),
                  _resident((None, C_POOL, D_MODEL), lambda b, t: (i, 0, 0)),
                  _resident((None, Q_W, D_MODEL), lambda b, t: (i, 1, 0))],
        out_specs=[row, per_seq(POOL_HALO, C_POOL), per_seq(WINDOW, KV_W), per_seq(WINDOW, KV_W)],
        out_shape=[jax.ShapeDtypeStruct((B * L, D_MODEL), F32), jax.ShapeDtypeStruct((B, POOL_HALO, C_POOL), F32),
                   jax.ShapeDtypeStruct((B, WINDOW, KV_W), F32), jax.ShapeDtypeStruct((B, WINDOW, KV_W), F32)],
        scratch_shapes=[pltpu.VMEM((POOL_HALO, C_POOL), F32), pltpu.VMEM((WINDOW, KV_W), F32),
                        pltpu.VMEM((WINDOW, KV_W), F32), pltpu.VMEM((N_KV_HEADS, lk, GQA_GROUP * chunk), F32)],
        compiler_params=_params(("arbitrary", "arbitrary")),
    )(x, gn, p['w_in_even'], p['pool_w'], p['pool_scale'][i][None], hist, k_cache, v_cache, bkt,
      p['t5_table'], p['attn_sinks'][i], p['w_out_even'], p['w_out_even'])


def _t5_bucket(rel):
    half = N_BUCKETS // 2
    max_exact = half // 2
    side = jnp.where(rel > 0, half, 0)
    n = jnp.abs(rel)
    nf = jnp.maximum(n, max_exact).astype(F32)
    large = max_exact + (jnp.log(nf / max_exact) / math.log(MAX_DISTANCE / max_exact)
                         * (half - max_exact)).astype(jnp.int32)
    large = jnp.minimum(large, half - 1)
    return side + jnp.where(n < max_exact, n, large)


def _bucket_index(lq, lk, offset):
    rel = jnp.arange(lk)[:, None] - offset - jnp.arange(lq)[None, :]
    return _t5_bucket(rel).astype(jnp.int32)


def _rwkv_pre_kernel(x_ref, xp_ref, sh_ref, gn_ref, mu_ref, wr_ref, wk_ref, wv_ref, w1_ref, w2_ref,
                     a1_ref, a2_ref, g1_ref, g2_ref, w0_ref, a0_ref, kk_ref, ka_ref, rk_ref, e_ref, et_ref,
                     r_o, lw_o, k_o, v_o, kk_o, b_o, g_o, bonus_o, hs_o, *, tm):
    t = pl.program_id(1)
    gn = gn_ref[...]
    h = _rms(x_ref[...], gn)
    prev_tile_last = _rms(xp_ref[...], gn)[7:8]
    first_prev = jnp.where(t == 0, sh_ref[0], prev_tile_last)
    row = lax.broadcasted_iota(jnp.int32, (tm, 1), 0)
    h_prev = jnp.where(row == 0, first_prev, pltpu.roll(h, 1, 0))
    xx = h_prev - h

    hb = h.astype(BF16)
    xb = xx.astype(BF16)
    mu = mu_ref[...].astype(BF16)

    def mix(j):
        return hb + xb * mu[j:j + 1]

    zw = w0_ref[...] + _dot(jnp.tanh(_dot(mix(1), w1_ref[...])).astype(BF16), w2_ref[...])
    za = a0_ref[...] + _dot(_dot(mix(4), a1_ref[...]).astype(BF16), a2_ref[...])
    g = _dot(jax.nn.sigmoid(_dot(mix(5), g1_ref[...])).astype(BF16), g2_ref[...])
    r = _dot(mix(0), wr_ref[...])
    lw = -math.exp(-0.5) * jax.nn.sigmoid(zw)
    k = _dot(mix(2), wk_ref[...])
    a = jax.nn.sigmoid(za)
    v = _dot(mix(3), wv_ref[...])
    kk = k * kk_ref[...]
    kk = kk * lax.rsqrt(jnp.maximum(_head_sum(kk * kk, e_ref, et_ref, split=False), 1e-24))
    k = k * (1.0 + (a - 1.0) * ka_ref[...])
    b = kk * a
    for hp in range(N_PAIRS):
        sl = slice(hp * LANES, (hp + 1) * LANES)
        r_o[0, hp] = r[:, sl]
        lw_o[0, hp] = lw[:, sl]
        k_o[0, hp] = k[:, sl]
        v_o[0, hp] = v[:, sl]
        kk_o[0, hp] = kk[:, sl]
        b_o[0, hp] = b[:, sl]
    g_o[...] = g.astype(BF16)
    bonus_o[...] = (_head_sum(r * k * rk_ref[...], e_ref, et_ref, split=False) * v).astype(BF16)
    hs_o[0] = h[tm - 8:]


def _rwkv_pre(x, shift_prev, gn, p, i, B, L):
    tm = min(512, L)
    nt = L // tm
    vec = lambda: _resident((1, D_MODEL), lambda b, t: (0, 0))
    mat = lambda r, c: _resident((None, r, c), lambda b, t: (i, 0, 0))
    lora = p['rwkv_w1'].shape[-1], p['rwkv_a1'].shape[-1], p['rwkv_g1'].shape[-1]
    head_spec = pl.BlockSpec((1, N_PAIRS, tm, LANES), lambda b, t: (b, 0, t, 0))
    head_shape = jax.ShapeDtypeStruct((B, N_PAIRS, L, LANES), F32)
    return pl.pallas_call(
        functools.partial(_rwkv_pre_kernel, tm=tm),
        grid=(B, nt),
        in_specs=[
            pl.BlockSpec((tm, D_MODEL), lambda b, t: (b * nt + t, 0)),
            pl.BlockSpec((8, D_MODEL), lambda b, t: (jnp.maximum((b * nt + t) * (tm // 8) - 1, 0), 0)),
            pl.BlockSpec((1, 1, D_MODEL), lambda b, t: (b, 0, 0)),
            vec(),
            _resident((None, 6, D_MODEL), lambda b, t: (i, 0, 0)),
            mat(D_MODEL, D_MODEL), mat(D_MODEL, D_MODEL), mat(D_MODEL, D_MODEL),
            mat(D_MODEL, lora[0]), mat(lora[0], D_MODEL),
            mat(D_MODEL, lora[1]), mat(lora[1], D_MODEL),
            mat(D_MODEL, lora[2]), mat(lora[2], D_MODEL),
            vec(), vec(), vec(), vec(), vec(),
            _resident((D_MODEL, LANES), lambda b, t: (0, 0)),
            _resident((LANES, D_MODEL), lambda b, t: (0, 0)),
        ],
        out_specs=[head_spec] * 6 + [
            pl.BlockSpec((tm, D_MODEL), lambda b, t: (b * nt + t, 0)),
            pl.BlockSpec((tm, D_MODEL), lambda b, t: (b * nt + t, 0)),
            pl.BlockSpec((1, 8, D_MODEL), lambda b, t: (b, 0, 0)),
        ],
        out_shape=[head_shape] * 6 + [
            jax.ShapeDtypeStruct((B * L, D_MODEL), BF16),
            jax.ShapeDtypeStruct((B * L, D_MODEL), BF16),
            jax.ShapeDtypeStruct((B, 8, D_MODEL), F32),
        ],
        compiler_params=_params(("parallel", "arbitrary")),
    )(x, x, shift_prev, gn, p['rwkv_mu'], p['rwkv_wr'], p['rwkv_wk'], p['rwkv_wv'],
      p['rwkv_w1'], p['rwkv_w2'], p['rwkv_a1'], p['rwkv_a2'], p['rwkv_g1'], p['rwkv_g2'],
      p['rwkv_w0'][i][None], p['rwkv_a0'][i][None], p['rwkv_k_k'][i][None], p['rwkv_k_a'][i][None],
      p['rwkv_r_k'][i].reshape(1, D_MODEL), p['head_onehot'], p['head_onehot_t'])


DIAG = 16


def _replication_matrix(n):
    src = np.arange(n)[:, None]
    dst = np.arange(n)[None, :]
    picks = np.concatenate([(src == (dst // DIAG) * DIAG + s) for s in range(DIAG - 1)], axis=1)
    return jnp.asarray(picks, dtype=BF16)


def _unit_lower_inverses(lows, c, rep_ref):
    n = 2 * c
    ti = lax.broadcasted_iota(jnp.int32, (c, n), 0)
    li = lax.broadcasted_iota(jnp.int32, (c, n), 1)
    si = li & (c - 1)
    head1 = li >= c

    def block_diag(x):
        return jnp.concatenate([jnp.where(head1, 0.0, x), jnp.where(head1, x, 0.0)], axis=0)

    if rep_ref is None:
        invs = [jnp.where(ti == si, 1.0, 0.0) + jnp.where(jnp.logical_and(ti == si + 1, (ti & 1) == 1), low, 0.0)
                for low in lows]
        m = 2
    else:
        pt = lax.broadcasted_iota(jnp.int32, (DIAG, n), 0)
        pl_ = lax.broadcasted_iota(jnp.int32, (DIAG, n), 1)
        blk = (pl_ & (c - 1)) // DIAG
        packed = []
        for low in lows:
            d = jnp.zeros((DIAG, n), F32)
            for i in range(c // DIAG):
                d = jnp.where(blk == i, low[i * DIAG:(i + 1) * DIAG], d)
            packed.append(d)
        rep = _dot(jnp.concatenate(packed, axis=0).astype(BF16), rep_ref[...])
        sols = [jnp.where(pt == (pl_ & (DIAG - 1)), 1.0, 0.0) for _ in lows]
        for s in range(DIAG - 1):
            sols = [sol + rep[p * DIAG:(p + 1) * DIAG, s * n:(s + 1) * n] * sol[s:s + 1]
                    for p, sol in enumerate(sols)]
        invs = [jnp.concatenate([jnp.where(blk == i, sol, 0.0) for i in range(c // DIAG)], axis=0) for sol in sols]
        m = DIAG
    while m < c:
        sh = m.bit_length() - 1
        sub = jnp.logical_and(jnp.logical_and((ti >> (sh + 1)) == (si >> (sh + 1)), ((ti >> sh) & 1) == 1),
                              ((si >> sh) & 1) == 0)
        diag = [block_diag(inv).astype(BF16) for inv in invs]
        half = [_dot(inv.astype(BF16), block_diag(jnp.where(sub, low, 0.0)).astype(BF16)).astype(BF16)
                for inv, low in zip(invs, lows)]
        invs = [inv + _dot(h, d) for inv, h, d in zip(invs, half, diag)]
        m *= 2
    return invs


def _scan_chunks(seqs, states, c, rep_ref):
    n = 2 * c
    row = lax.broadcasted_iota(jnp.int32, (c, LANES), 0)
    head1 = lax.broadcasted_iota(jnp.int32, (c, LANES), 1) >= RWKV_HEAD
    ti = lax.broadcasted_iota(jnp.int32, (c, n), 0)
    si = lax.broadcasted_iota(jnp.int32, (c, n), 1) & (c - 1)
    strict = ti > si
    incl = ti >= si
    pi = lax.broadcasted_iota(jnp.int32, (LANES, LANES), 0)
    pj = lax.broadcasted_iota(jnp.int32, (LANES, LANES), 1)
    same_head = (pi >= RWKV_HEAD) == (pj >= RWKV_HEAD)

    def stack(x):
        return jnp.concatenate([jnp.where(head1, 0.0, x), jnp.where(head1, x, 0.0)], axis=0)

    lhs, rhs, v_st, v_bf, bk_end, decay = [], [], [], [], [], []
    for r, lw, k, v, kk, b in seqs:
        cum = lw
        sh = 1
        while sh < c:
            cum = cum + jnp.where(row >= sh, pltpu.roll(cum, sh, 0), 0.0)
            sh *= 2
        tot = cum[c - 1:c]
        grow = jnp.exp(-cum)
        tail = jnp.exp(tot - cum)
        a_t = -kk * jnp.exp(cum - lw)
        r_t = r * jnp.exp(cum)
        lhs.append(jnp.concatenate([a_t, r_t], axis=0).astype(BF16))
        rhs.append(jnp.concatenate([stack(b * grow), stack(k * grow)], axis=0).astype(BF16))
        v_st.append(stack(v).astype(BF16))
        v_bf.append(v.astype(BF16))
        bk_end.append(jnp.concatenate([b * tail, k * tail], axis=0).astype(BF16))
        decay.append(jnp.exp(tot))
    grams = [_dot_nt(x, y) for x, y in zip(lhs, rhs)]
    lows = [jnp.where(strict, g[:c, :n], 0.0) for g in grams]
    m_rb = [jnp.where(incl, g[c:, :n], 0.0).astype(BF16) for g in grams]
    m_v = [jnp.concatenate([jnp.where(strict, g[:c, n:], 0.0), jnp.where(incl, g[c:, n:], 0.0)], axis=0).astype(BF16)
           for g in grams]
    invs = [inv.astype(BF16) for inv in _unit_lower_inverses(lows, c, rep_ref)]
    from_v = [_dot(m, vs) for m, vs in zip(m_v, v_st)]

    ys = []
    states = list(states)
    npair = len(states)
    for j in range(len(seqs) // npair):
        sl = slice(j * npair, (j + 1) * npair)
        from_state = [_dot_nt(x, s.astype(BF16)) for x, s in zip(lhs[sl], states)]
        rhs_sa = [stack(f[:c] + fv[:c]).astype(BF16) for f, fv in zip(from_state, from_v[sl])]
        sas = [_dot(inv, x) for inv, x in zip(invs[sl], rhs_sa)]
        ys += [f[c:] + fv[c:] + _dot(m, stack(sa).astype(BF16))
               for f, fv, m, sa in zip(from_state, from_v[sl], m_rb[sl], sas)]
        upd = [_dot_tn(jnp.concatenate([sa.astype(BF16), vb], axis=0), be)
               for sa, vb, be in zip(sas, v_bf[sl], bk_end[sl])]
        states = [s * d + jnp.where(same_head, u, 0.0) for s, d, u in zip(states, decay[sl], upd)]
    return ys, states


def _rwkv_scan_kernel(r_ref, lw_ref, k_ref, v_ref, kk_ref, b_ref, s0_ref, *rest, c, sub, vpu_diag):
    if vpu_diag:
        rep_ref, y_ref, so_ref, s_ref = rest
    else:
        rep_ref = None
        y_ref, so_ref, s_ref = rest
    ci = pl.program_id(1)

    @pl.when(ci == 0)
    def _():
        s_ref[...] = s0_ref[0]

    seqs = [tuple(ref[0, hp, j * c:(j + 1) * c] for ref in (r_ref, lw_ref, k_ref, v_ref, kk_ref, b_ref))
            for j in range(sub) for hp in range(N_PAIRS)]
    ys, new_states = _scan_chunks(seqs, [s_ref[hp] for hp in range(N_PAIRS)], c, rep_ref)
    for j in range(sub):
        for hp in range(N_PAIRS):
            y_ref[0, hp, j * c:(j + 1) * c] = ys[j * N_PAIRS + hp]
    for hp in range(N_PAIRS):
        s_ref[hp] = new_states[hp]

    @pl.when(ci == pl.num_programs(1) - 1)
    def _():
        so_ref[0] = s_ref[...]


def _rwkv_scan(r, lw, k, v, kk, b, s0, B, L):
    c = min(CHUNK, L)
    nc = L // c
    sub = next(s for s in (4, 2, 1) if nc % s == 0)
    nc //= sub
    seq = pl.BlockSpec((1, N_PAIRS, sub * c, LANES), lambda bi, ci: (bi, 0, ci, 0))
    st = pl.BlockSpec((1, N_PAIRS, LANES, LANES), lambda bi, ci: (bi, 0, 0, 0))
    vpu_diag = 2 * c == LANES
    extra_specs, extra_args = [], []
    if vpu_diag:
        extra_specs = [_resident((LANES, (DIAG - 1) * LANES), lambda bi, ci: (0, 0))]
        extra_args = [_replication_matrix(LANES)]
    return pl.pallas_call(
        functools.partial(_rwkv_scan_kernel, c=c, sub=sub, vpu_diag=vpu_diag),
        grid=(B, nc),
        in_specs=[seq] * 6 + [st] + extra_specs,
        out_specs=[seq, st],
        out_shape=[jax.ShapeDtypeStruct((B, N_PAIRS, L, LANES), F32),
                   jax.ShapeDtypeStruct((B, N_PAIRS, LANES, LANES), F32)],
        scratch_shapes=[pltpu.VMEM((N_PAIRS, LANES, LANES), F32)],
        compiler_params=_params(("parallel", "arbitrary")),
    )(r, lw, k, v, kk, b, s0, *extra_args)


def _rwkv_post_kernel(x_ref, y_ref, g_ref, bonus_ref, lw_ref, lb_ref, e_ref, et_ref, wo_ref, o_ref, *, parts):
    rows = x_ref.shape[0] // parts
    sls = [slice(i * rows, (i + 1) * rows) for i in range(parts)]
    head_sum = lambda t, split: _head_sum(t, e_ref, et_ref, split)
    ys = [jnp.concatenate([y_ref[0, hp, sl] for hp in range(N_PAIRS)], axis=-1) for sl in sls]
    ds = [y - head_sum(y, True) * (1.0 / RWKV_HEAD) for y in ys]
    var = [head_sum(d * d, False) * (1.0 / RWKV_HEAD) for d in ds]
    yn = [d * lax.rsqrt(vr + LNX_EPS) * lw_ref[...] + lb_ref[...] for d, vr in zip(ds, var)]
    gated = [((n + bonus_ref[sl, :].astype(F32)) * g_ref[sl, :].astype(F32)).astype(BF16) for n, sl in zip(yn, sls)]
    for gt, sl in zip(gated, sls):
        o_ref[sl, :] = x_ref[sl, :] + _dot(gt, wo_ref[...])


def _rwkv_post(x, y, g, bonus, p, i, B, L):
    tm = min(512, L)
    nt = L // tm
    vec = lambda: _resident((1, D_MODEL), lambda b, t: (0, 0))
    head_spec = pl.BlockSpec((1, N_PAIRS, tm, LANES), lambda b, t: (b, 0, t, 0))
    row = pl.BlockSpec((tm, D_MODEL), lambda b, t: (b * nt + t, 0))
    return pl.pallas_call(
        functools.partial(_rwkv_post_kernel, parts=2 if tm >= 256 else 1),
        grid=(B, nt),
        in_specs=[row, head_spec, row, row, vec(), vec(),
                  _resident((D_MODEL, LANES), lambda b, t: (0, 0)),
                  _resident((LANES, D_MODEL), lambda b, t: (0, 0)),
                  _resident((None, D_MODEL, D_MODEL), lambda b, t: (i, 0, 0))],
        out_specs=row,
        out_shape=jax.ShapeDtypeStruct((B * L, D_MODEL), F32),
        compiler_params=_params(("parallel", "parallel")),
    )(x, y, g, bonus, p['rwkv_lnx_w'][i][None], p['rwkv_lnx_b'][i][None],
      p['head_onehot'], p['head_onehot_t'], p['rwkv_wo'])


def _pack_state(s):
    B = s.shape[0]
    s = s.reshape(B, N_PAIRS, 2, RWKV_HEAD, RWKV_HEAD)
    z = jnp.zeros_like(s[:, :, 0])
    top = jnp.concatenate([s[:, :, 0], z], axis=-1)
    bot = jnp.concatenate([z, s[:, :, 1]], axis=-1)
    return jnp.concatenate([top, bot], axis=-2)


def _unpack_state(s):
    B = s.shape[0]
    return jnp.stack([s[:, :, :RWKV_HEAD, :RWKV_HEAD], s[:, :, RWKV_HEAD:, RWKV_HEAD:]], axis=2).reshape(
        B, RWKV_H, RWKV_HEAD, RWKV_HEAD)


class _Stream:
    def __init__(self, x, pos0, caches):
        self.B, self.L, _ = x.shape
        self.x = x.reshape(self.B * self.L, D_MODEL)
        self.pos0 = pos0
        self.pool_c, self.k_c, self.v_c, self.shift_c, self.wkv_c = caches
        self.stepping = self.k_c is not None
        if self.stepping:
            self.bkt = _bucket_index(self.L, SWA_ROWS + self.L, SWA_ROWS)
        else:
            self.bkt = _bucket_index(CHUNK, WINDOW + CHUNK, WINDOW)
        self.new = {name: [] for name in ('pool', 'k', 'v', 'shift', 'wkv')}

    def mixer(self, l, p):
        B, L, i = self.B, self.L, l // 2
        gn = p['norm_mix'][l][None]
        if l % 2 == 0:
            if self.stepping:
                hist = jnp.pad(self.pool_c[i], ((0, 0), (POOL_HALO - POOL_HIST, 0), (0, 0)))
                k_cache = self.k_c[i].reshape(B, SWA_ROWS, KV_W)
                v_cache = self.v_c[i].reshape(B, SWA_ROWS, KV_W)
            else:
                hist = jnp.zeros((B, POOL_HALO, C_POOL), F32)
                k_cache = v_cache = jnp.zeros((B, WINDOW, KV_W), F32)
            self.x, u_tail, k_tail, v_tail = _even_layer(
                self.x, gn, p, i, hist, k_cache, v_cache, self.bkt, B, L, self.pos0,
                chunk=L if self.stepping else CHUNK, masked=not self.stepping)
            self.new['pool'].append(u_tail[:, -POOL_HIST:])
            self.new['k'].append(k_tail.reshape(B, SWA_ROWS, N_KV_HEADS, HEAD_DIM))
            self.new['v'].append(v_tail.reshape(B, SWA_ROWS, N_KV_HEADS, HEAD_DIM))
        else:
            if self.stepping:
                shift_prev = self.shift_c[i][:, None, :]
                s0 = _pack_state(self.wkv_c[i])
            else:
                shift_prev = jnp.zeros((B, 1, D_MODEL), F32)
                s0 = jnp.zeros((B, N_PAIRS, LANES, LANES), F32)
            r, lw, k, v, kk, b, g, bonus, hs = _rwkv_pre(self.x, shift_prev, gn, p, i, B, L)
            y, s_new = _rwkv_scan(r, lw, k, v, kk, b, s0, B, L)
            self.x = _rwkv_post(self.x, y, g, bonus, p, i, B, L)
            self.new['shift'].append(hs[:, 7])
            self.new['wkv'].append(_unpack_state(s_new))

    def results(self):
        return (self.x.reshape(self.B, self.L, D_MODEL),) + tuple(
            jnp.stack(self.new[name]) for name in ('pool', 'k', 'v', 'shift', 'wkv'))


def _trunk(long, short, p):
    depth = p['norm_mix'].shape[0]
    stacks = (p['ffn_w_gate'], p['ffn_w_up'], p['ffn_w_down'])
    weights = tuple(w[0, 0].astype(BF16) for w in stacks)
    for l in range(depth):
        for j in range(2):
            n = 2 * l + j
            convert = (stacks, *divmod(n + 1, 2)) if n + 1 < 2 * depth else None
            final_g = p['norm_final'][None] if n == 2 * depth - 1 else None
            long.x, short.x, weights = _ffn(long.x, short.x, p['norm_ffn'][l, j][None], weights, final_g, convert)
            if j == 0:
                long.mixer(l, p)
                short.mixer(l, p)
    return long.results(), short.results()


_MATMUL_WEIGHTS = ('w_in_even', 'pool_w', 'w_out_even', 'rwkv_wr', 'rwkv_wk', 'rwkv_wv', 'rwkv_w1', 'rwkv_w2',
                   'rwkv_a1', 'rwkv_a2', 'rwkv_g1', 'rwkv_g2', 'rwkv_wo')


def _prepare(p):
    p = dict(p)
    for name in _MATMUL_WEIGHTS:
        p[name] = p[name].astype(BF16)
    onehot = np.arange(D_MODEL)[:, None] // RWKV_HEAD == np.arange(LANES)[None, :]
    p['head_onehot'] = jnp.asarray(onehot, dtype=BF16)
    p['head_onehot_t'] = jnp.asarray(onehot.T, dtype=BF16)
    return p


def kernel(x_prompt, x_sample, cache_pool, cache_swa_k, cache_swa_v, state_shift, state_wkv, t5_table, norm_ffn, ffn_w_gate, ffn_w_up, ffn_w_down, norm_mix, w_in_even, pool_w, pool_scale, attn_sinks, w_out_even, rwkv_mu, rwkv_wr, rwkv_wk, rwkv_wv, rwkv_w0, rwkv_w1, rwkv_w2, rwkv_a0, rwkv_a1, rwkv_a2, rwkv_g1, rwkv_g2, rwkv_k_k, rwkv_k_a, rwkv_r_k, rwkv_lnx_w, rwkv_lnx_b, rwkv_wo, norm_final):
    p = _prepare(dict(
        t5_table=t5_table, norm_ffn=norm_ffn, ffn_w_gate=ffn_w_gate, ffn_w_up=ffn_w_up, ffn_w_down=ffn_w_down,
        norm_mix=norm_mix, w_in_even=w_in_even, pool_w=pool_w, pool_scale=pool_scale, attn_sinks=attn_sinks,
        w_out_even=w_out_even, rwkv_mu=rwkv_mu, rwkv_wr=rwkv_wr, rwkv_wk=rwkv_wk, rwkv_wv=rwkv_wv,
        rwkv_w0=rwkv_w0, rwkv_w1=rwkv_w1, rwkv_w2=rwkv_w2, rwkv_a0=rwkv_a0, rwkv_a1=rwkv_a1, rwkv_a2=rwkv_a2,
        rwkv_g1=rwkv_g1, rwkv_g2=rwkv_g2, rwkv_k_k=rwkv_k_k, rwkv_k_a=rwkv_k_a, rwkv_r_k=rwkv_r_k,
        rwkv_lnx_w=rwkv_lnx_w, rwkv_lnx_b=rwkv_lnx_b, rwkv_wo=rwkv_wo, norm_final=norm_final))
    prompt = _Stream(x_prompt, 0, (None, None, None, None, None))
    sample = _Stream(x_sample, PAST_LEN, (cache_pool, cache_swa_k, cache_swa_v, state_shift, state_wkv))
    (y_p, pool_p, k_p, v_p, shift_p, wkv_p), (y_s, pool_s, k_s, v_s, shift_s, wkv_s) = _trunk(prompt, sample, p)
    return (y_p, y_s, pool_p, pool_s, k_p, k_s, v_p, v_s, shift_p, shift_s, wkv_p, wkv_s)
```

```python
import functools
import math

import jax
import jax.numpy as jnp
import numpy as np
from jax import lax
from jax.experimental import pallas as pl
from jax.experimental.pallas import tpu as pltpu

F32 = jnp.float32
BF16 = jnp.bfloat16

D_MODEL = 1024
D_FF = 2816
NORM_EPS = 1e-6
CHUNK = 64
POOL_WINDOWS = (2, 4, 8, 16)
C_POOL = 512
POOL_GC = 128
POOL_HIST = 15
POOL_HALO = 16
HEAD_DIM = 64
N_Q_HEADS = 8
N_KV_HEADS = 2
GQA_GROUP = 4
WINDOW = 128
SWA_ROWS = 128
Q_W = 512
KV_W = 128
IN_EVEN = C_POOL + Q_W + 2 * KV_W
N_BUCKETS = 32
MAX_DISTANCE = 128
RWKV_HEAD = 64
RWKV_H = 16
N_PAIRS = RWKV_H // 2
LANES = 128
LNX_EPS = 64e-5
PAST_LEN = 4096
VMEM_LIMIT_BYTES = 56 * 1024 * 1024


def _params(sem):
    return pltpu.CompilerParams(dimension_semantics=sem, vmem_limit_bytes=VMEM_LIMIT_BYTES)


def _dot(a, b):
    return jnp.dot(a, b, preferred_element_type=F32)


def _dot_nt(a, b):
    return lax.dot_general(a, b, (((1,), (1,)), ((), ())), preferred_element_type=F32)


def _dot_tn(a, b):
    return lax.dot_general(a, b, (((0,), (0,)), ((), ())), preferred_element_type=F32)


def _rms(x, g):
    return x * lax.rsqrt(jnp.mean(x * x, axis=-1, keepdims=True) + NORM_EPS) * g


def _split(x):
    hi = x.astype(BF16)
    lo = (x - hi.astype(F32)).astype(BF16)
    return hi, lo


def _head_sum(x, e_ref, et_ref, split=True):
    if not split:
        return _dot(_dot(x.astype(BF16), e_ref[...]).astype(BF16), et_ref[...])
    hi, lo = _split(x)
    s = _dot(hi, e_ref[...]) + _dot(lo, e_ref[...])
    shi, slo = _split(s)
    return _dot(shi, et_ref[...]) + _dot(slo, et_ref[...])


def _ffn_kernel(x_ref, xs_ref, g_ref, wg_ref, wu_ref, wd_ref, *rest, f_chunk, final, convert):
    rest = list(rest)
    gf_ref = rest.pop(0) if final else None
    src_refs = [rest.pop(0) for _ in range(3)] if convert else []
    o_ref, os_ref = rest.pop(0), rest.pop(0)
    dst_refs = [rest.pop(0) for _ in range(3)] if convert else []
    acc_ref, = rest
    for src, dst in zip(src_refs, dst_refs):
        dst[...] = src[...].astype(BF16)

    def rows(src_ref, dst_ref):
        n = src_ref.shape[0]
        x = src_ref[...]
        h = _rms(x, g_ref[...]).astype(BF16)
        for j in range(D_FF // f_chunk):
            sl = slice(j * f_chunk, (j + 1) * f_chunk)
            gate = _dot(h, wg_ref[:, sl])
            up = _dot(h, wu_ref[:, sl])
            act = (gate * jax.nn.sigmoid(gate) * up).astype(BF16)
            part = _dot(act, wd_ref[sl, :])
            if j == 0:
                acc_ref[:n] = part
            else:
                acc_ref[:n] += part
        y = x + 0.5 * acc_ref[:n]
        if final:
            y = _rms(y, gf_ref[...])
        dst_ref[...] = y

    rows(x_ref, o_ref)

    @pl.when(pl.program_id(0) == pl.num_programs(0) - 1)
    def _():
        rows(xs_ref, os_ref)


def _resident(shape, index_map):
    return pl.BlockSpec(shape, index_map, pipeline_mode=pl.Buffered(1))


BF16_SUBLANES = 16


def _slab_steps(rows, steps):
    return next(k for k in range(steps, 0, -1) if rows % (k * BF16_SUBLANES) == 0)


def _ffn(x, xs, g, weights, final_g=None, convert=None):
    T = x.shape[0]
    tm = min(1024, T)
    steps = T // tm
    Ts = xs.shape[0]
    assert Ts <= tm
    final = final_g is not None
    in_specs = [pl.BlockSpec((tm, D_MODEL), lambda i: (i, 0)), _resident((Ts, D_MODEL), lambda i: (0, 0)),
                _resident((1, D_MODEL), lambda i: (0, 0))]
    in_specs += [_resident(w.shape, lambda i: (0, 0)) for w in weights]
    args = [x, xs, g, *weights]
    out_specs = [pl.BlockSpec((tm, D_MODEL), lambda i: (i, 0)), pl.BlockSpec((Ts, D_MODEL), lambda i: (0, 0))]
    out_shape = [jax.ShapeDtypeStruct((T, D_MODEL), F32), jax.ShapeDtypeStruct((Ts, D_MODEL), F32)]
    if final:
        in_specs.append(_resident((1, D_MODEL), lambda i: (0, 0)))
        args.append(final_g)
    if convert is not None:
        stacks, l, j = convert
        for w in stacks:
            rows, cols = w.shape[2:]
            k = _slab_steps(rows, steps)
            in_specs.append(pl.BlockSpec((None, None, rows // k, cols),
                                         lambda i, k=k: (l, j, jnp.minimum(i, k - 1), 0)))
            out_specs.append(pl.BlockSpec((rows // k, cols), lambda i, k=k: (jnp.minimum(i, k - 1), 0)))
            out_shape.append(jax.ShapeDtypeStruct((rows, cols), BF16))
            args.append(w)
    outs = pl.pallas_call(
        functools.partial(_ffn_kernel, f_chunk=256, final=final, convert=convert is not None),
        grid=(steps,),
        in_specs=in_specs,
        out_specs=out_specs,
        out_shape=out_shape,
        scratch_shapes=[pltpu.VMEM((tm, D_MODEL), F32)],
        compiler_params=_params(("arbitrary",)),
    )(*args)
    return outs[0], outs[1], tuple(outs[2:])


def _build_bias(bkt_ref, tab_ref, bias_ref, lq):
    bkt = bkt_ref[...]
    for h in range(N_Q_HEADS):
        b = jnp.zeros(bkt.shape, F32)
        for n in range(N_BUCKETS):
            b = jnp.where(bkt == n, tab_ref[n, h], b)
        g, i = divmod(h, GQA_GROUP)
        bias_ref[g, :, i * lq:(i + 1) * lq] = b


def _group_queries(q, g):
    return jnp.concatenate([q[:, h * HEAD_DIM:(h + 1) * HEAD_DIM]
                            for h in range(g * GQA_GROUP, (g + 1) * GQA_GROUP)], axis=0) * (HEAD_DIM ** -0.5)


def _sink_rows(sink_ref, lq):
    lane = lax.broadcasted_iota(jnp.int32, (1, GQA_GROUP * lq), 1)
    rows = []
    for g in range(N_KV_HEADS):
        r = jnp.zeros((1, GQA_GROUP * lq), F32)
        for i in range(GQA_GROUP):
            r = jnp.where(lane // lq == i, sink_ref[g * GQA_GROUP + i], r)
        rows.append(r)
    return rows


def _attn_core(qs, ks, vs, biases, sinks, valids):
    ss = [_dot_nt(k, q) + b for q, k, b in zip(qs, ks, biases)]
    ss = [s if ok is None else jnp.where(ok, s, -1e30) for s, ok in zip(ss, valids)]
    ms = [jnp.maximum(jnp.max(s, axis=0, keepdims=True), sk) for s, sk in zip(ss, sinks)]
    ps = [jnp.exp(s - m) for s, m in zip(ss, ms)]
    invs = [1.0 / (jnp.sum(p, axis=0, keepdims=True) + jnp.exp(sk - m)) for p, sk, m in zip(ps, sinks, ms)]
    return [_dot_tn((p * r).astype(BF16), v) for p, r, v in zip(ps, invs, vs)]


def _smem():
    return pl.BlockSpec(memory_space=pltpu.SMEM)


def _even_layer_kernel(x_ref, gn_ref, win_ref, pw_ref, ps_ref, hist_ref, kc_ref, vc_ref, bkt_ref, tab_ref, sink_ref,
                       wop_ref, woa_ref, o_ref, utail_o, ktail_o, vtail_o, halo_ref, kprev_ref, vprev_ref, bias_ref,
                       *, rows, chunk, pos0, masked):
    t = pl.program_id(1)

    @pl.when(jnp.logical_and(pl.program_id(0) == 0, t == 0))
    def _():
        _build_bias(bkt_ref, tab_ref, bias_ref, chunk)

    @pl.when(t == 0)
    def _():
        halo_ref[...] = hist_ref[0]
        kprev_ref[...] = kc_ref[0]
        vprev_ref[...] = vc_ref[0]

    x = x_ref[...]
    z = _dot(_rms(x, gn_ref[...]).astype(BF16), win_ref[...])
    u = z[:, :C_POOL]
    q = z[:, C_POOL:C_POOL + Q_W].astype(BF16)
    k_all = jnp.concatenate([kprev_ref[...], z[:, C_POOL + Q_W:C_POOL + Q_W + KV_W]], axis=0)
    v_all = jnp.concatenate([vprev_ref[...], z[:, C_POOL + Q_W + KV_W:]], axis=0)

    ext = jnp.concatenate([halo_ref[...], u], axis=0)
    pos = pos0 + t * rows + lax.broadcasted_iota(jnp.int32, (rows, POOL_GC), 0)
    pooled = []
    for gi, w in enumerate(POOL_WINDOWS):
        sl = slice(gi * POOL_GC, (gi + 1) * POOL_GC)
        s = ext[:, sl]
        span = 1
        while span < w:
            s = s + pltpu.roll(s, span, 0)
            span *= 2
        cnt = jnp.minimum(w, pos + 1).astype(F32)
        pooled.append(_dot((s[POOL_HALO:] / cnt - u[:, sl]).astype(BF16), pw_ref[gi]))
    pool_out = (jnp.concatenate(pooled, axis=-1) * ps_ref[...]).astype(BF16)

    lk = WINDOW + chunk
    kb = k_all.astype(BF16)
    vb = v_all.astype(BF16)
    sink_rows = _sink_rows(sink_ref, chunk)
    first_pos = t * rows - WINDOW + lax.broadcasted_iota(jnp.int32, (lk, 1), 0)
    nchunk = rows // chunk
    qs, ks, vs, biases, sinks, valids = [], [], [], [], [], []
    for j in range(nchunk):
        qj = q[j * chunk:(j + 1) * chunk]
        for g in range(N_KV_HEADS):
            qs.append(_group_queries(qj, g))
            ks.append(kb[j * chunk:j * chunk + lk, g * HEAD_DIM:(g + 1) * HEAD_DIM])
            vs.append(vb[j * chunk:j * chunk + lk, g * HEAD_DIM:(g + 1) * HEAD_DIM])
            biases.append(bias_ref[g])
            sinks.append(sink_rows[g])
            valids.append(first_pos + j * chunk >= 0 if masked else None)
    outs = _attn_core(qs, ks, vs, biases, sinks, valids)
    att = jnp.concatenate(
        [jnp.concatenate([outs[j * N_KV_HEADS + g][i * chunk:(i + 1) * chunk]
                          for g in range(N_KV_HEADS) for i in range(GQA_GROUP)], axis=-1) for j in range(nchunk)],
        axis=0).astype(BF16)

    o_ref[...] = x + _dot(pool_out, wop_ref[...]) + _dot(att, woa_ref[...])

    halo_ref[...] = ext[rows:]
    kprev_ref[...] = k_all[rows:]
    vprev_ref[...] = v_all[rows:]
    utail_o[0] = ext[rows:]
    ktail_o[0] = k_all[rows:]
    vtail_o[0] = v_all[rows:]


def _even_layer(x, gn, p, i, hist, k_cache, v_cache, bkt, B, L, pos0, chunk, masked):
    rows = min(8 * chunk, L)
    nt = L // rows
    lk = WINDOW + chunk
    row = pl.BlockSpec((rows, D_MODEL), lambda b, t: (b * nt + t, 0))
    per_seq = lambda n, m: pl.BlockSpec((1, n, m), lambda b, t: (b, 0, 0))
    return pl.pallas_call(
        functools.partial(_even_layer_kernel, rows=rows, chunk=chunk, pos0=pos0, masked=masked),
        grid=(B, nt),
        in_specs=[row, _resident((1, D_MODEL), lambda b, t: (0, 0)),
                  _resident((None, D_MODEL, IN_EVEN), lambda b, t: (i, 0, 0)),
                  _resident((None, len(POOL_WINDOWS), POOL_GC, POOL_GC), lambda b, t: (i, 0, 0, 0)),
                  _resident((1, C_POOL), lambda b, t: (0, 0)),
                  per_seq(POOL_HALO, C_POOL), per_seq(WINDOW, KV_W), per_seq(WINDOW, KV_W),
                  _resident((lk, chunk), lambda b, t: (0, 0)), _smem(), _smem(),
                  _resident((None, C_POOL, D_MODEL), lambda b, t: (i, 0, 0)),
                  _resident((None, Q_W, D_MODEL), lambda b, t: (i, 1, 0))],
        out_specs=[row, per_seq(POOL_HALO, C_POOL), per_seq(WINDOW, KV_W), per_seq(WINDOW, KV_W)],
        out_shape=[jax.ShapeDtypeStruct((B * L, D_MODEL), F32), jax.ShapeDtypeStruct((B, POOL_HALO, C_POOL), F32),
                   jax.ShapeDtypeStruct((B, WINDOW, KV_W), F32), jax.ShapeDtypeStruct((B, WINDOW, KV_W), F32)],
        scratch_shapes=[pltpu.VMEM((POOL_HALO, C_POOL), F32), pltpu.VMEM((WINDOW, KV_W), F32),
                        pltpu.VMEM((WINDOW, KV_W), F32), pltpu.VMEM((N_KV_HEADS, lk, GQA_GROUP * chunk), F32)],
        compiler_params=_params(("arbitrary", "arbitrary")),
    )(x, gn, p['w_in_even'], p['pool_w'], p['pool_scale'][i][None], hist, k_cache, v_cache, bkt,
      p['t5_table'], p['attn_sinks'][i], p['w_out_even'], p['w_out_even'])


def _t5_bucket(rel):
    half = N_BUCKETS // 2
    max_exact = half // 2
    side = jnp.where(rel > 0, half, 0)
    n = jnp.abs(rel)
    nf = jnp.maximum(n, max_exact).astype(F32)
    large = max_exact + (jnp.log(nf / max_exact) / math.log(MAX_DISTANCE / max_exact)
                         * (half - max_exact)).astype(jnp.int32)
    large = jnp.minimum(large, half - 1)
    return side + jnp.where(n < max_exact, n, large)


def _bucket_index(lq, lk, offset):
    rel = jnp.arange(lk)[:, None] - offset - jnp.arange(lq)[None, :]
    return _t5_bucket(rel).astype(jnp.int32)


def _rwkv_pre_kernel(x_ref, xp_ref, sh_ref, gn_ref, mu_ref, wr_ref, wk_ref, wv_ref, w1_ref, w2_ref,
                     a1_ref, a2_ref, g1_ref, g2_ref, w0_ref, a0_ref, kk_ref, ka_ref, rk_ref, e_ref, et_ref,
                     r_o, lw_o, k_o, v_o, kk_o, b_o, g_o, bonus_o, hs_o, *, tm):
    t = pl.program_id(1)
    gn = gn_ref[...]
    h = _rms(x_ref[...], gn)
    prev_tile_last = _rms(xp_ref[...], gn)[7:8]
    first_prev = jnp.where(t == 0, sh_ref[0], prev_tile_last)
    row = lax.broadcasted_iota(jnp.int32, (tm, 1), 0)
    h_prev = jnp.where(row == 0, first_prev, pltpu.roll(h, 1, 0))
    xx = h_prev - h

    hb = h.astype(BF16)
    xb = xx.astype(BF16)
    mu = mu_ref[...].astype(BF16)

    def mix(j):
        return hb + xb * mu[j:j + 1]

    zw = w0_ref[...] + _dot(jnp.tanh(_dot(mix(1), w1_ref[...])).astype(BF16), w2_ref[...])
    za = a0_ref[...] + _dot(_dot(mix(4), a1_ref[...]).astype(BF16), a2_ref[...])
    g = _dot(jax.nn.sigmoid(_dot(mix(5), g1_ref[...])).astype(BF16), g2_ref[...])
    r = _dot(mix(0), wr_ref[...])
    lw = -math.exp(-0.5) * jax.nn.sigmoid(zw)
    k = _dot(mix(2), wk_ref[...])
    a = jax.nn.sigmoid(za)
    v = _dot(mix(3), wv_ref[...])
    kk = k * kk_ref[...]
    kk = kk * lax.rsqrt(jnp.maximum(_head_sum(kk * kk, e_ref, et_ref, split=False), 1e-24))
    k = k * (1.0 + (a - 1.0) * ka_ref[...])
    b = kk * a
    for hp in range(N_PAIRS):
        sl = slice(hp * LANES, (hp + 1) * LANES)
        r_o[0, hp] = r[:, sl]
        lw_o[0, hp] = lw[:, sl]
        k_o[0, hp] = k[:, sl]
        v_o[0, hp] = v[:, sl]
        kk_o[0, hp] = kk[:, sl]
        b_o[0, hp] = b[:, sl]
    g_o[...] = g.astype(BF16)
    bonus_o[...] = (_head_sum(r * k * rk_ref[...], e_ref, et_ref, split=False) * v).astype(BF16)
    hs_o[0] = h[tm - 8:]


def _rwkv_pre(x, shift_prev, gn, p, i, B, L):
    tm = min(512, L)
    nt = L // tm
    vec = lambda: _resident((1, D_MODEL), lambda b, t: (0, 0))
    mat = lambda r, c: _resident((None, r, c), lambda b, t: (i, 0, 0))
    lora = p['rwkv_w1'].shape[-1], p['rwkv_a1'].shape[-1], p['rwkv_g1'].shape[-1]
    head_spec = pl.BlockSpec((1, N_PAIRS, tm, LANES), lambda b, t: (b, 0, t, 0))
    head_shape = jax.ShapeDtypeStruct((B, N_PAIRS, L, LANES), F32)
    return pl.pallas_call(
        functools.partial(_rwkv_pre_kernel, tm=tm),
        grid=(B, nt),
        in_specs=[
            pl.BlockSpec((tm, D_MODEL), lambda b, t: (b * nt + t, 0)),
            pl.BlockSpec((8, D_MODEL), lambda b, t: (jnp.maximum((b * nt + t) * (tm // 8) - 1, 0), 0)),
            pl.BlockSpec((1, 1, D_MODEL), lambda b, t: (b, 0, 0)),
            vec(),
            _resident((None, 6, D_MODEL), lambda b, t: (i, 0, 0)),
            mat(D_MODEL, D_MODEL), mat(D_MODEL, D_MODEL), mat(D_MODEL, D_MODEL),
            mat(D_MODEL, lora[0]), mat(lora[0], D_MODEL),
            mat(D_MODEL, lora[1]), mat(lora[1], D_MODEL),
            mat(D_MODEL, lora[2]), mat(lora[2], D_MODEL),
            vec(), vec(), vec(), vec(), vec(),
            _resident((D_MODEL, LANES), lambda b, t: (0, 0)),
            _resident((LANES, D_MODEL), lambda b, t: (0, 0)),
        ],
        out_specs=[head_spec] * 6 + [
            pl.BlockSpec((tm, D_MODEL), lambda b, t: (b * nt + t, 0)),
            pl.BlockSpec((tm, D_MODEL), lambda b, t: (b * nt + t, 0)),
            pl.BlockSpec((1, 8, D_MODEL), lambda b, t: (b, 0, 0)),
        ],
        out_shape=[head_shape] * 6 + [
            jax.ShapeDtypeStruct((B * L, D_MODEL), BF16),
            jax.ShapeDtypeStruct((B * L, D_MODEL), BF16),
            jax.ShapeDtypeStruct((B, 8, D_MODEL), F32),
        ],
        compiler_params=_params(("parallel", "arbitrary")),
    )(x, x, shift_prev, gn, p['rwkv_mu'], p['rwkv_wr'], p['rwkv_wk'], p['rwkv_wv'],
      p['rwkv_w1'], p['rwkv_w2'], p['rwkv_a1'], p['rwkv_a2'], p['rwkv_g1'], p['rwkv_g2'],
      p['rwkv_w0'][i][None], p['rwkv_a0'][i][None], p['rwkv_k_k'][i][None], p['rwkv_k_a'][i][None],
      p['rwkv_r_k'][i].reshape(1, D_MODEL), p['head_onehot'], p['head_onehot_t'])


DIAG = 16


def _replication_matrix(n):
    src = np.arange(n)[:, None]
    dst = np.arange(n)[None, :]
    picks = np.concatenate([(src == (dst // DIAG) * DIAG + s) for s in range(DIAG - 1)], axis=1)
    return jnp.asarray(picks, dtype=BF16)


def _unit_lower_inverses(lows, c, rep_ref):
    n = 2 * c
    ti = lax.broadcasted_iota(jnp.int32, (c, n), 0)
    li = lax.broadcasted_iota(jnp.int32, (c, n), 1)
    si = li & (c - 1)
    head1 = li >= c

    def block_diag(x):
        return jnp.concatenate([jnp.where(head1, 0.0, x), jnp.where(head1, x, 0.0)], axis=0)

    if rep_ref is None:
        invs = [jnp.where(ti == si, 1.0, 0.0) + jnp.where(jnp.logical_and(ti == si + 1, (ti & 1) == 1), low, 0.0)
                for low in lows]
        m = 2
    else:
        pt = lax.broadcasted_iota(jnp.int32, (DIAG, n), 0)
        pl_ = lax.broadcasted_iota(jnp.int32, (DIAG, n), 1)
        blk = (pl_ & (c - 1)) // DIAG
        packed = []
        for low in lows:
            d = jnp.zeros((DIAG, n), F32)
            for i in range(c // DIAG):
                d = jnp.where(blk == i, low[i * DIAG:(i + 1) * DIAG], d)
            packed.append(d)
        rep = _dot(jnp.concatenate(packed, axis=0).astype(BF16), rep_ref[...])
        sols = [jnp.where(pt == (pl_ & (DIAG - 1)), 1.0, 0.0) for _ in lows]
        for s in range(DIAG - 1):
            sols = [sol + rep[p * DIAG:(p + 1) * DIAG, s * n:(s + 1) * n] * sol[s:s + 1]
                    for p, sol in enumerate(sols)]
        invs = [jnp.concatenate([jnp.where(blk == i, sol, 0.0) for i in range(c // DIAG)], axis=0) for sol in sols]
        m = DIAG
    while m < c:
        sh = m.bit_length() - 1
        sub = jnp.logical_and(jnp.logical_and((ti >> (sh + 1)) == (si >> (sh + 1)), ((ti >> sh) & 1) == 1),
                              ((si >> sh) & 1) == 0)
        diag = [block_diag(inv).astype(BF16) for inv in invs]
        half = [_dot(inv.astype(BF16), block_diag(jnp.where(sub, low, 0.0)).astype(BF16)).astype(BF16)
                for inv, low in zip(invs, lows)]
        invs = [inv + _dot(h, d) for inv, h, d in zip(invs, half, diag)]
        m *= 2
    return invs


def _scan_chunks(seqs, states, c, rep_ref):
    n = 2 * c
    row = lax.broadcasted_iota(jnp.int32, (c, LANES), 0)
    head1 = lax.broadcasted_iota(jnp.int32, (c, LANES), 1) >= RWKV_HEAD
    ti = lax.broadcasted_iota(jnp.int32, (c, n), 0)
    si = lax.broadcasted_iota(jnp.int32, (c, n), 1) & (c - 1)
    strict = ti > si
    incl = ti >= si
    pi = lax.broadcasted_iota(jnp.int32, (LANES, LANES), 0)
    pj = lax.broadcasted_iota(jnp.int32, (LANES, LANES), 1)
    same_head = (pi >= RWKV_HEAD) == (pj >= RWKV_HEAD)

    def stack(x):
        return jnp.concatenate([jnp.where(head1, 0.0, x), jnp.where(head1, x, 0.0)], axis=0)

    lhs, rhs, v_st, v_bf, bk_end, decay = [], [], [], [], [], []
    for r, lw, k, v, kk, b in seqs:
        cum = lw
        sh = 1
        while sh < c:
            cum = cum + jnp.where(row >= sh, pltpu.roll(cum, sh, 0), 0.0)
            sh *= 2
        tot = cum[c - 1:c]
        grow = jnp.exp(-cum)
        tail = jnp.exp(tot - cum)
        a_t = -kk * jnp.exp(cum - lw)
        r_t = r * jnp.exp(cum)
        lhs.append(jnp.concatenate([a_t, r_t], axis=0).astype(BF16))
        rhs.append(jnp.concatenate([stack(b * grow), stack(k * grow)], axis=0).astype(BF16))
        v_st.append(stack(v).astype(BF16))
        v_bf.append(v.astype(BF16))
        bk_end.append(jnp.concatenate([b * tail, k * tail], axis=0).astype(BF16))
        decay.append(jnp.exp(tot))
    grams = [_dot_nt(x, y) for x, y in zip(lhs, rhs)]
    lows = [jnp.where(strict, g[:c, :n], 0.0) for g in grams]
    m_rb = [jnp.where(incl, g[c:, :n], 0.0).astype(BF16) for g in grams]
    m_v = [jnp.concatenate([jnp.where(strict, g[:c, n:], 0.0), jnp.where(incl, g[c:, n:], 0.0)], axis=0).astype(BF16)
           for g in grams]
    invs = [inv.astype(BF16) for inv in _unit_lower_inverses(lows, c, rep_ref)]
    from_v = [_dot(m, vs) for m, vs in zip(m_v, v_st)]

    ys = []
    states = list(states)
    npair = len(states)
    for j in range(len(seqs) // npair):
        sl = slice(j * npair, (j + 1) * npair)
        from_state = [_dot_nt(x, s.astype(BF16)) for x, s in zip(lhs[sl], states)]
        rhs_sa = [stack(f[:c] + fv[:c]).astype(BF16) for f, fv in zip(from_state, from_v[sl])]
        sas = [_dot(inv, x) for inv, x in zip(invs[sl], rhs_sa)]
        ys += [f[c:] + fv[c:] + _dot(m, stack(sa).astype(BF16))
               for f, fv, m, sa in zip(from_state, from_v[sl], m_rb[sl], sas)]
        upd = [_dot_tn(jnp.concatenate([sa.astype(BF16), vb], axis=0), be)
               for sa, vb, be in zip(sas, v_bf[sl], bk_end[sl])]
        states = [s * d + jnp.where(same_head, u, 0.0) for s, d, u in zip(states, decay[sl], upd)]
    return ys, states


def _rwkv_scan_kernel(r_ref, lw_ref, k_ref, v_ref, kk_ref, b_ref, s0_ref, *rest, c, sub, vpu_diag):
    if vpu_diag:
        rep_ref, y_ref, so_ref, s_ref = rest
    else:
        rep_ref = None
        y_ref, so_ref, s_ref = rest
    ci = pl.program_id(1)

    @pl.when(ci == 0)
    def _():
        s_ref[...] = s0_ref[0]

    seqs = [tuple(ref[0, hp, j * c:(j + 1) * c] for ref in (r_ref, lw_ref, k_ref, v_ref, kk_ref, b_ref))
            for j in range(sub) for hp in range(N_PAIRS)]
    ys, new_states = _scan_chunks(seqs, [s_ref[hp] for hp in range(N_PAIRS)], c, rep_ref)
    for j in range(sub):
        for hp in range(N_PAIRS):
            y_ref[0, hp, j * c:(j + 1) * c] = ys[j * N_PAIRS + hp]
    for hp in range(N_PAIRS):
        s_ref[hp] = new_states[hp]

    @pl.when(ci == pl.num_programs(1) - 1)
    def _():
        so_ref[0] = s_ref[...]


def _rwkv_scan(r, lw, k, v, kk, b, s0, B, L):
    c = min(CHUNK, L)
    nc = L // c
    sub = next(s for s in (4, 2, 1) if nc % s == 0)
    nc //= sub
    seq = pl.BlockSpec((1, N_PAIRS, sub * c, LANES), lambda bi, ci: (bi, 0, ci, 0))
    st = pl.BlockSpec((1, N_PAIRS, LANES, LANES), lambda bi, ci: (bi, 0, 0, 0))
    vpu_diag = 2 * c == LANES
    extra_specs, extra_args = [], []
    if vpu_diag:
        extra_specs = [_resident((LANES, (DIAG - 1) * LANES), lambda bi, ci: (0, 0))]
        extra_args = [_replication_matrix(LANES)]
    return pl.pallas_call(
        functools.partial(_rwkv_scan_kernel, c=c, sub=sub, vpu_diag=vpu_diag),
        grid=(B, nc),
        in_specs=[seq] * 6 + [st] + extra_specs,
        out_specs=[seq, st],
        out_shape=[jax.ShapeDtypeStruct((B, N_PAIRS, L, LANES), F32),
                   jax.ShapeDtypeStruct((B, N_PAIRS, LANES, LANES), F32)],
        scratch_shapes=[pltpu.VMEM((N_PAIRS, LANES, LANES), F32)],
        compiler_params=_params(("parallel", "arbitrary")),
    )(r, lw, k, v, kk, b, s0, *extra_args)


def _rwkv_post_kernel(x_ref, y_ref, g_ref, bonus_ref, lw_ref, lb_ref, e_ref, et_ref, wo_ref, o_ref, *, parts):
    rows = x_ref.shape[0] // parts
    sls = [slice(i * rows, (i + 1) * rows) for i in range(parts)]
    head_sum = lambda t, split: _head_sum(t, e_ref, et_ref, split)
    ys = [jnp.concatenate([y_ref[0, hp, sl] for hp in range(N_PAIRS)], axis=-1) for sl in sls]
    ds = [y - head_sum(y, True) * (1.0 / RWKV_HEAD) for y in ys]
    var = [head_sum(d * d, False) * (1.0 / RWKV_HEAD) for d in ds]
    yn = [d * lax.rsqrt(vr + LNX_EPS) * lw_ref[...] + lb_ref[...] for d, vr in zip(ds, var)]
    gated = [((n + bonus_ref[sl, :].astype(F32)) * g_ref[sl, :].astype(F32)).astype(BF16) for n, sl in zip(yn, sls)]
    for gt, sl in zip(gated, sls):
        o_ref[sl, :] = x_ref[sl, :] + _dot(gt, wo_ref[...])


def _rwkv_post(x, y, g, bonus, p, i, B, L):
    tm = min(512, L)
    nt = L // tm
    vec = lambda: _resident((1, D_MODEL), lambda b, t: (0, 0))
    head_spec = pl.BlockSpec((1, N_PAIRS, tm, LANES), lambda b, t: (b, 0, t, 0))
    row = pl.BlockSpec((tm, D_MODEL), lambda b, t: (b * nt + t, 0))
    return pl.pallas_call(
        functools.partial(_rwkv_post_kernel, parts=2 if tm >= 256 else 1),
        grid=(B, nt),
        in_specs=[row, head_spec, row, row, vec(), vec(),
                  _resident((D_MODEL, LANES), lambda b, t: (0, 0)),
                  _resident((LANES, D_MODEL), lambda b, t: (0, 0)),
                  _resident((None, D_MODEL, D_MODEL), lambda b, t: (i, 0, 0))],
        out_specs=row,
        out_shape=jax.ShapeDtypeStruct((B * L, D_MODEL), F32),
        compiler_params=_params(("parallel", "parallel")),
    )(x, y, g, bonus, p['rwkv_lnx_w'][i][None], p['rwkv_lnx_b'][i][None],
      p['head_onehot'], p['head_onehot_t'], p['rwkv_wo'])


def _pack_state(s):
    B = s.shape[0]
    s = s.reshape(B, N_PAIRS, 2, RWKV_HEAD, RWKV_HEAD)
    z = jnp.zeros_like(s[:, :, 0])
    top = jnp.concatenate([s[:, :, 0], z], axis=-1)
    bot = jnp.concatenate([z, s[:, :, 1]], axis=-1)
    return jnp.concatenate([top, bot], axis=-2)


def _unpack_state(s):
    B = s.shape[0]
    return jnp.stack([s[:, :, :RWKV_HEAD, :RWKV_HEAD], s[:, :, RWKV_HEAD:, RWKV_HEAD:]], axis=2).reshape(
        B, RWKV_H, RWKV_HEAD, RWKV_HEAD)


class _Stream:
    def __init__(self, x, pos0, caches):
        self.B, self.L, _ = x.shape
        self.x = x.reshape(self.B * self.L, D_MODEL)
        self.pos0 = pos0
        self.pool_c, self.k_c, self.v_c, self.shift_c, self.wkv_c = caches
        self.stepping = self.k_c is not None
        if self.stepping:
            self.bkt = _bucket_index(self.L, SWA_ROWS + self.L, SWA_ROWS)
        else:
            self.bkt = _bucket_index(CHUNK, WINDOW + CHUNK, WINDOW)
        self.new = {name: [] for name in ('pool', 'k', 'v', 'shift', 'wkv')}

    def mixer(self, l, p):
        B, L, i = self.B, self.L, l // 2
        gn = p['norm_mix'][l][None]
        if l % 2 == 0:
            if self.stepping:
                hist = jnp.pad(self.pool_c[i], ((0, 0), (POOL_HALO - POOL_HIST, 0), (0, 0)))
                k_cache = self.k_c[i].reshape(B, SWA_ROWS, KV_W)
                v_cache = self.v_c[i].reshape(B, SWA_ROWS, KV_W)
            else:
                hist = jnp.zeros((B, POOL_HALO, C_POOL), F32)
                k_cache = v_cache = jnp.zeros((B, WINDOW, KV_W), F32)
            self.x, u_tail, k_tail, v_tail = _even_layer(
                self.x, gn, p, i, hist, k_cache, v_cache, self.bkt, B, L, self.pos0,
                chunk=L if self.stepping else CHUNK, masked=not self.stepping)
            self.new['pool'].append(u_tail[:, -POOL_HIST:])
            self.new['k'].append(k_tail.reshape(B, SWA_ROWS, N_KV_HEADS, HEAD_DIM))
            self.new['v'].append(v_tail.reshape(B, SWA_ROWS, N_KV_HEADS, HEAD_DIM))
        else:
            if self.stepping:
                shift_prev = self.shift_c[i][:, None, :]
                s0 = _pack_state(self.wkv_c[i])
            else:
                shift_prev = jnp.zeros((B, 1, D_MODEL), F32)
                s0 = jnp.zeros((B, N_PAIRS, LANES, LANES), F32)
            r, lw, k, v, kk, b, g, bonus, hs = _rwkv_pre(self.x, shift_prev, gn, p, i, B, L)
            y, s_new = _rwkv_scan(r, lw, k, v, kk, b, s0, B, L)
            self.x = _rwkv_post(self.x, y, g, bonus, p, i, B, L)
            self.new['shift'].append(hs[:, 7])
            self.new['wkv'].append(_unpack_state(s_new))

    def results(self):
        return (self.x.reshape(self.B, self.L, D_MODEL),) + tuple(
            jnp.stack(self.new[name]) for name in ('pool', 'k', 'v', 'shift', 'wkv'))


def _trunk(long, short, p):
    depth = p['norm_mix'].shape[0]
    stacks = (p['ffn_w_gate'], p['ffn_w_up'], p['ffn_w_down'])
    weights = tuple(w[0, 0].astype(BF16) for w in stacks)
    for l in range(depth):
        for j in range(2):
            n = 2 * l + j
            convert = (stacks, *divmod(n + 1, 2)) if n + 1 < 2 * depth else None
            final_g = p['norm_final'][None] if n == 2 * depth - 1 else None
            long.x, short.x, weights = _ffn(long.x, short.x, p['norm_ffn'][l, j][None], weights, final_g, convert)
            if j == 0:
                long.mixer(l, p)
                short.mixer(l, p)
    return long.results(), short.results()


_MATMUL_WEIGHTS = ('w_in_even', 'pool_w', 'w_out_even', 'rwkv_wr', 'rwkv_wk', 'rwkv_wv', 'rwkv_w1', 'rwkv_w2',
                   'rwkv_a1', 'rwkv_a2', 'rwkv_g1', 'rwkv_g2', 'rwkv_wo')


def _prepare(p):
    p = dict(p)
    for name in _MATMUL_WEIGHTS:
        p[name] = p[name].astype(BF16)
    onehot = np.arange(D_MODEL)[:, None] // RWKV_HEAD == np.arange(LANES)[None, :]
    p['head_onehot'] = jnp.asarray(onehot, dtype=BF16)
    p['head_onehot_t'] = jnp.asarray(onehot.T, dtype=BF16)
    return p


def kernel(x_prompt, x_sample, cache_pool, cache_swa_k, cache_swa_v, state_shift, state_wkv, t5_table, norm_ffn, ffn_w_gate, ffn_w_up, ffn_w_down, norm_mix, w_in_even, pool_w, pool_scale, attn_sinks, w_out_even, rwkv_mu, rwkv_wr, rwkv_wk, rwkv_wv, rwkv_w0, rwkv_w1, rwkv_w2, rwkv_a0, rwkv_a1, rwkv_a2, rwkv_g1, rwkv_g2, rwkv_k_k, rwkv_k_a, rwkv_r_k, rwkv_lnx_w, rwkv_lnx_b, rwkv_wo, norm_final):
    p = _prepare(dict(
        t5_table=t5_table, norm_ffn=norm_ffn, ffn_w_gate=ffn_w_gate, ffn_w_up=ffn_w_up, ffn_w_down=ffn_w_down,
        norm_mix=norm_mix, w_in_even=w_in_even, pool_w=pool_w, pool_scale=pool_scale, attn_sinks=attn_sinks,
        w_out_even=w_out_even, rwkv_mu=rwkv_mu, rwkv_wr=rwkv_wr, rwkv_wk=rwkv_wk, rwkv_wv=rwkv_wv,
        rwkv_w0=rwkv_w0, rwkv_w1=rwkv_w1, rwkv_w2=rwkv_w2, rwkv_a0=rwkv_a0, rwkv_a1=rwkv_a1, rwkv_a2=rwkv_a2,
        rwkv_g1=rwkv_g1, rwkv_g2=rwkv_g2, rwkv_k_k=rwkv_k_k, rwkv_k_a=rwkv_k_a, rwkv_r_k=rwkv_r_k,
        rwkv_lnx_w=rwkv_lnx_w, rwkv_lnx_b=rwkv_lnx_b, rwkv_wo=rwkv_wo, norm_final=norm_final))
    prompt = _Stream(x_prompt, 0, (None, None, None, None, None))
    sample = _Stream(x_sample, PAST_LEN, (cache_pool, cache_swa_k, cache_swa_v, state_shift, state_wkv))
    (y_p, pool_p, k_p, v_p, shift_p, wkv_p), (y_s, pool_s, k_s, v_s, shift_s, wkv_s) = _trunk(prompt, sample, p)
    return (y_p, y_s, pool_p, pool_s, k_p, k_s, v_p, v_s, shift_p, shift_s, wkv_p, wkv_s)
```

```python
import functools
import math

import jax
import jax.numpy as jnp
import numpy as np
from jax import lax
from jax.experimental import pallas as pl
from jax.experimental.pallas import tpu as pltpu

F32 = jnp.float32
BF16 = jnp.bfloat16

D_MODEL = 1024
D_FF = 2816
NORM_EPS = 1e-6
CHUNK = 64
POOL_WINDOWS = (2, 4, 8, 16)
C_POOL = 512
POOL_GC = 128
POOL_HIST = 15
POOL_HALO = 16
HEAD_DIM = 64
N_Q_HEADS = 8
N_KV_HEADS = 2
GQA_GROUP = 4
WINDOW = 128
SWA_ROWS = 128
Q_W = 512
KV_W = 128
IN_EVEN = C_POOL + Q_W + 2 * KV_W
N_BUCKETS = 32
MAX_DISTANCE = 128
RWKV_HEAD = 64
RWKV_H = 16
N_PAIRS = RWKV_H // 2
LNX_EPS = 64e-5
PAST_LEN = 4096

LANES = 128
F32_SUBLANES = 8
BF16_SUBLANES = 16
VMEM_LIMIT_BYTES = 56 * 1024 * 1024

FFN_TILE = 1024
FFN_F_CHUNK = 256
TOKEN_TILE = 512
POST_PARTS = 2
EVEN_CHUNKS_PER_STEP = 8
SCAN_CHUNKS_PER_STEP = (4, 2, 1)


def _params(sem):
    return pltpu.CompilerParams(dimension_semantics=sem, vmem_limit_bytes=VMEM_LIMIT_BYTES)


def _dot(a, b):
    return jnp.dot(a, b, preferred_element_type=F32)


def _dot_nt(a, b):
    return lax.dot_general(a, b, (((1,), (1,)), ((), ())), preferred_element_type=F32)


def _dot_tn(a, b):
    return lax.dot_general(a, b, (((0,), (0,)), ((), ())), preferred_element_type=F32)


def _rms(x, g):
    return x * lax.rsqrt(jnp.mean(x * x, axis=-1, keepdims=True) + NORM_EPS) * g


def _split(x):
    hi = x.astype(BF16)
    lo = (x - hi.astype(F32)).astype(BF16)
    return hi, lo


def _head_sum(x, e_ref, et_ref, split=True):
    if not split:
        return _dot(_dot(x.astype(BF16), e_ref[...]).astype(BF16), et_ref[...])
    hi, lo = _split(x)
    s = _dot(hi, e_ref[...]) + _dot(lo, e_ref[...])
    shi, slo = _split(s)
    return _dot(shi, et_ref[...]) + _dot(slo, et_ref[...])


def _ffn_kernel(x_ref, xs_ref, g_ref, wg_ref, wu_ref, wd_ref, *rest, f_chunk, final, convert):
    rest = list(rest)
    gf_ref = rest.pop(0) if final else None
    src_refs = [rest.pop(0) for _ in range(convert)]
    o_ref, os_ref = rest.pop(0), rest.pop(0)
    dst_refs = [rest.pop(0) for _ in range(convert)]
    acc_ref, = rest
    for src, dst in zip(src_refs, dst_refs):
        dst[...] = src[...].astype(BF16)

    def rows(src_ref, dst_ref):
        n = src_ref.shape[0]
        x = src_ref[...]
        h = _rms(x, g_ref[...]).astype(BF16)
        for j in range(D_FF // f_chunk):
            sl = slice(j * f_chunk, (j + 1) * f_chunk)
            gate = _dot(h, wg_ref[:, sl])
            up = _dot(h, wu_ref[:, sl])
            act = (gate * jax.nn.sigmoid(gate) * up).astype(BF16)
            part = _dot(act, wd_ref[sl, :])
            if j == 0:
                acc_ref[:n] = part
            else:
                acc_ref[:n] += part
        y = x + 0.5 * acc_ref[:n]
        if final:
            y = _rms(y, gf_ref[...])
        dst_ref[...] = y

    rows(x_ref, o_ref)

    @pl.when(pl.program_id(0) == pl.num_programs(0) - 1)
    def _():
        rows(xs_ref, os_ref)


def _resident(shape, index_map):
    return pl.BlockSpec(shape, index_map, pipeline_mode=pl.Buffered(1))


def _slab_steps(rows, steps):
    return next(k for k in range(steps, 0, -1) if rows % (k * BF16_SUBLANES) == 0)


def _ffn(x, xs, g, weights, final_g=None, convert=()):
    T = x.shape[0]
    tm = min(FFN_TILE, T)
    steps = T // tm
    Ts = xs.shape[0]
    assert Ts <= tm
    final = final_g is not None
    in_specs = [pl.BlockSpec((tm, D_MODEL), lambda i: (i, 0)), _resident((Ts, D_MODEL), lambda i: (0, 0)),
                _resident((1, D_MODEL), lambda i: (0, 0))]
    in_specs += [_resident(w.shape, lambda i: (0, 0)) for w in weights]
    args = [x, xs, g, *weights]
    out_specs = [pl.BlockSpec((tm, D_MODEL), lambda i: (i, 0)), pl.BlockSpec((Ts, D_MODEL), lambda i: (0, 0))]
    out_shape = [jax.ShapeDtypeStruct((T, D_MODEL), F32), jax.ShapeDtypeStruct((Ts, D_MODEL), F32)]
    if final:
        in_specs.append(_resident((1, D_MODEL), lambda i: (0, 0)))
        args.append(final_g)
    for w, lead in convert:
        rows, cols = w.shape[len(lead):]
        k = _slab_steps(rows, steps)
        in_specs.append(pl.BlockSpec((None,) * len(lead) + (rows // k, cols),
                                     lambda i, k=k, lead=lead: (*lead, jnp.minimum(i, k - 1), 0)))
        out_specs.append(pl.BlockSpec((rows // k, cols), lambda i, k=k: (jnp.minimum(i, k - 1), 0)))
        out_shape.append(jax.ShapeDtypeStruct((rows, cols), BF16))
        args.append(w)
    outs = pl.pallas_call(
        functools.partial(_ffn_kernel, f_chunk=FFN_F_CHUNK, final=final, convert=len(convert)),
        grid=(steps,),
        in_specs=in_specs,
        out_specs=out_specs,
        out_shape=out_shape,
        scratch_shapes=[pltpu.VMEM((tm, D_MODEL), F32)],
        compiler_params=_params(("arbitrary",)),
    )(*args)
    return outs[0], outs[1], tuple(outs[2:])


def _build_bias(bkt_ref, tab_ref, bias_ref, lq):
    bkt = bkt_ref[...]
    for h in range(N_Q_HEADS):
        b = jnp.zeros(bkt.shape, F32)
        for n in range(N_BUCKETS):
            b = jnp.where(bkt == n, tab_ref[n, h], b)
        g, i = divmod(h, GQA_GROUP)
        bias_ref[g, :, i * lq:(i + 1) * lq] = b


def _group_queries(q, g):
    return jnp.concatenate([q[:, h * HEAD_DIM:(h + 1) * HEAD_DIM]
                            for h in range(g * GQA_GROUP, (g + 1) * GQA_GROUP)], axis=0) * (HEAD_DIM ** -0.5)


def _sink_rows(sink_ref, lq):
    lane = lax.broadcasted_iota(jnp.int32, (1, GQA_GROUP * lq), 1)
    rows = []
    for g in range(N_KV_HEADS):
        r = jnp.zeros((1, GQA_GROUP * lq), F32)
        for i in range(GQA_GROUP):
            r = jnp.where(lane // lq == i, sink_ref[g * GQA_GROUP + i], r)
        rows.append(r)
    return rows


def _attn_core(qs, ks, vs, biases, sinks, valids):
    ss = [_dot_nt(k, q) + b for q, k, b in zip(qs, ks, biases)]
    ss = [s if ok is None else jnp.where(ok, s, -1e30) for s, ok in zip(ss, valids)]
    ms = [jnp.maximum(jnp.max(s, axis=0, keepdims=True), sk) for s, sk in zip(ss, sinks)]
    ps = [jnp.exp(s - m) for s, m in zip(ss, ms)]
    invs = [1.0 / (jnp.sum(p, axis=0, keepdims=True) + jnp.exp(sk - m)) for p, sk, m in zip(ps, sinks, ms)]
    return [_dot_tn((p * r).astype(BF16), v) for p, r, v in zip(ps, invs, vs)]


def _smem():
    return pl.BlockSpec(memory_space=pltpu.SMEM)


def _even_layer_kernel(x_ref, gn_ref, win_ref, pw_ref, ps_ref, hist_ref, kc_ref, vc_ref, bkt_ref, tab_ref, sink_ref,
                       wop_ref, woa_ref, o_ref, utail_o, ktail_o, vtail_o, halo_ref, kprev_ref, vprev_ref, bias_ref,
                       *, rows, chunk, pos0, masked):
    t = pl.program_id(1)

    @pl.when(jnp.logical_and(pl.program_id(0) == 0, t == 0))
    def _():
        _build_bias(bkt_ref, tab_ref, bias_ref, chunk)

    @pl.when(t == 0)
    def _():
        halo_ref[...] = hist_ref[0]
        kprev_ref[...] = kc_ref[0]
        vprev_ref[...] = vc_ref[0]

    x = x_ref[...]
    z = _dot(_rms(x, gn_ref[...]).astype(BF16), win_ref[...])
    u = z[:, :C_POOL]
    q = z[:, C_POOL:C_POOL + Q_W].astype(BF16)
    k_all = jnp.concatenate([kprev_ref[...], z[:, C_POOL + Q_W:C_POOL + Q_W + KV_W]], axis=0)
    v_all = jnp.concatenate([vprev_ref[...], z[:, C_POOL + Q_W + KV_W:]], axis=0)

    ext = jnp.concatenate([halo_ref[...], u], axis=0)
    pos = pos0 + t * rows + lax.broadcasted_iota(jnp.int32, (rows, POOL_GC), 0)
    pooled = []
    for gi, w in enumerate(POOL_WINDOWS):
        sl = slice(gi * POOL_GC, (gi + 1) * POOL_GC)
        s = ext[:, sl]
        span = 1
        while span < w:
            s = s + pltpu.roll(s, span, 0)
            span *= 2
        cnt = jnp.minimum(w, pos + 1).astype(F32)
        pooled.append(_dot((s[POOL_HALO:] / cnt - u[:, sl]).astype(BF16), pw_ref[gi]))
    pool_out = (jnp.concatenate(pooled, axis=-1) * ps_ref[...]).astype(BF16)

    lk = WINDOW + chunk
    kb = k_all.astype(BF16)
    vb = v_all.astype(BF16)
    sink_rows = _sink_rows(sink_ref, chunk)
    first_pos = t * rows - WINDOW + lax.broadcasted_iota(jnp.int32, (lk, 1), 0)
    nchunk = rows // chunk
    qs, ks, vs, biases, sinks, valids = [], [], [], [], [], []
    for j in range(nchunk):
        qj = q[j * chunk:(j + 1) * chunk]
        for g in range(N_KV_HEADS):
            qs.append(_group_queries(qj, g))
            ks.append(kb[j * chunk:j * chunk + lk, g * HEAD_DIM:(g + 1) * HEAD_DIM])
            vs.append(vb[j * chunk:j * chunk + lk, g * HEAD_DIM:(g + 1) * HEAD_DIM])
            biases.append(bias_ref[g])
            sinks.append(sink_rows[g])
            valids.append(first_pos + j * chunk >= 0 if masked else None)
    outs = _attn_core(qs, ks, vs, biases, sinks, valids)
    att = jnp.concatenate(
        [jnp.concatenate([outs[j * N_KV_HEADS + g][i * chunk:(i + 1) * chunk]
                          for g in range(N_KV_HEADS) for i in range(GQA_GROUP)], axis=-1) for j in range(nchunk)],
        axis=0).astype(BF16)

    o_ref[...] = x + _dot(pool_out, wop_ref[...]) + _dot(att, woa_ref[...])

    halo_ref[...] = ext[rows:]
    kprev_ref[...] = k_all[rows:]
    vprev_ref[...] = v_all[rows:]
    utail_o[0] = ext[rows:]
    ktail_o[0] = k_all[rows:]
    vtail_o[0] = v_all[rows:]


def _even_layer(x, gn, p, i, hist, k_cache, v_cache, bkt, B, L, pos0, chunk, masked):
    rows = min(EVEN_CHUNKS_PER_STEP * chunk, L)
    nt = L // rows
    lk = WINDOW + chunk
    row = pl.BlockSpec((rows, D_MODEL), lambda b, t: (b * nt + t, 0))
    per_seq = lambda n, m: pl.BlockSpec((1, n, m), lambda b, t: (b, 0, 0))
    return pl.pallas_call(
        functools.partial(_even_layer_kernel, rows=rows, chunk=chunk, pos0=pos0, masked=masked),
        grid=(B, nt),
        in_specs=[row, _resident((1, D_MODEL), lambda b, t: (0, 0)),
                  _resident((D_MODEL, IN_EVEN), lambda b, t: (0, 0)),
                  _resident((None, len(POOL_WINDOWS), POOL_GC, POOL_GC), lambda b, t: (i, 0, 0, 0)),
                  _resident((1, C_POOL), lambda b, t: (0, 0)),
                  per_seq(POOL_HALO, C_POOL), per_seq(WINDOW, KV_W), per_seq(WINDOW, KV_W),
                  _resident((lk, chunk), lambda b, t: (0, 0)), _smem(), _smem(),
                  _resident((C_POOL, D_MODEL), lambda b, t: (0, 0)),
                  _resident((Q_W, D_MODEL), lambda b, t: (1, 0))],
        out_specs=[row, per_seq(POOL_HALO, C_POOL), per_seq(WINDOW, KV_W), per_seq(WINDOW, KV_W)],
        out_shape=[jax.ShapeDtypeStruct((B * L, D_MODEL), F32), jax.ShapeDtypeStruct((B, POOL_HALO, C_POOL), F32),
                   jax.ShapeDtypeStruct((B, WINDOW, KV_W), F32), jax.ShapeDtypeStruct((B, WINDOW, KV_W), F32)],
        scratch_shapes=[pltpu.VMEM((POOL_HALO, C_POOL), F32), pltpu.VMEM((WINDOW, KV_W), F32),
                        pltpu.VMEM((WINDOW, KV_W), F32), pltpu.VMEM((N_KV_HEADS, lk, GQA_GROUP * chunk), F32)],
        compiler_params=_params(("arbitrary", "arbitrary")),
    )(x, gn, p['w_in_even'], p['pool_w'], p['pool_scale'][i][None], hist, k_cache, v_cache, bkt,
      p['t5_table'], p['attn_sinks'][i], p['w_out_even'], p['w_out_even'])


def _t5_bucket(rel):
    half = N_BUCKETS // 2
    max_exact = half // 2
    side = jnp.where(rel > 0, half, 0)
    n = jnp.abs(rel)
    nf = jnp.maximum(n, max_exact).astype(F32)
    large = max_exact + (jnp.log(nf / max_exact) / math.log(MAX_DISTANCE / max_exact)
                         * (half - max_exact)).astype(jnp.int32)
    large = jnp.minimum(large, half - 1)
    return side + jnp.where(n < max_exact, n, large)


def _bucket_index(lq, lk, offset):
    rel = jnp.arange(lk)[:, None] - offset - jnp.arange(lq)[None, :]
    return _t5_bucket(rel).astype(jnp.int32)


def _rwkv_pre_kernel(x_ref, xp_ref, sh_ref, gn_ref, mu_ref, wr_ref, wk_ref, wv_ref, w1_ref, w2_ref,
                     a1_ref, a2_ref, g1_ref, g2_ref, w0_ref, a0_ref, kk_ref, ka_ref, rk_ref, e_ref, et_ref,
                     r_o, lw_o, k_o, v_o, kk_o, b_o, g_o, bonus_o, hs_o, *, tm):
    t = pl.program_id(1)
    gn = gn_ref[...]
    h = _rms(x_ref[...], gn)
    prev_tile_last = _rms(xp_ref[...], gn)[F32_SUBLANES - 1:]
    first_prev = jnp.where(t == 0, sh_ref[0], prev_tile_last)
    row = lax.broadcasted_iota(jnp.int32, (tm, 1), 0)
    h_prev = jnp.where(row == 0, first_prev, pltpu.roll(h, 1, 0))
    xx = h_prev - h

    hb = h.astype(BF16)
    xb = xx.astype(BF16)
    mu = mu_ref[...].astype(BF16)

    def mix(j):
        return hb + xb * mu[j:j + 1]

    zw = w0_ref[...] + _dot(jnp.tanh(_dot(mix(1), w1_ref[...])).astype(BF16), w2_ref[...])
    za = a0_ref[...] + _dot(_dot(mix(4), a1_ref[...]).astype(BF16), a2_ref[...])
    g = _dot(jax.nn.sigmoid(_dot(mix(5), g1_ref[...])).astype(BF16), g2_ref[...])
    r = _dot(mix(0), wr_ref[...])
    lw = -math.exp(-0.5) * jax.nn.sigmoid(zw)
    k = _dot(mix(2), wk_ref[...])
    a = jax.nn.sigmoid(za)
    v = _dot(mix(3), wv_ref[...])
    kk = k * kk_ref[...]
    kk = kk * lax.rsqrt(jnp.maximum(_head_sum(kk * kk, e_ref, et_ref, split=False), 1e-24))
    k = k * (1.0 + (a - 1.0) * ka_ref[...])
    b = kk * a
    for hp in range(N_PAIRS):
        sl = slice(hp * LANES, (hp + 1) * LANES)
        r_o[0, hp] = r[:, sl]
        lw_o[0, hp] = lw[:, sl]
        k_o[0, hp] = k[:, sl]
        v_o[0, hp] = v[:, sl]
        kk_o[0, hp] = kk[:, sl]
        b_o[0, hp] = b[:, sl]
    g_o[...] = g.astype(BF16)
    bonus_o[...] = (_head_sum(r * k * rk_ref[...], e_ref, et_ref, split=False) * v).astype(BF16)
    hs_o[0] = h[tm - F32_SUBLANES:]


def _rwkv_pre(x, shift_prev, gn, p, i, B, L):
    tm = min(TOKEN_TILE, L)
    nt = L // tm
    vec = lambda: _resident((1, D_MODEL), lambda b, t: (0, 0))
    mat = lambda r, c: _resident((r, c), lambda b, t: (0, 0))
    lora = p['rwkv_w1'].shape[-1], p['rwkv_a1'].shape[-1], p['rwkv_g1'].shape[-1]
    head_spec = pl.BlockSpec((1, N_PAIRS, tm, LANES), lambda b, t: (b, 0, t, 0))
    head_shape = jax.ShapeDtypeStruct((B, N_PAIRS, L, LANES), F32)
    return pl.pallas_call(
        functools.partial(_rwkv_pre_kernel, tm=tm),
        grid=(B, nt),
        in_specs=[
            pl.BlockSpec((tm, D_MODEL), lambda b, t: (b * nt + t, 0)),
            pl.BlockSpec((F32_SUBLANES, D_MODEL),
                         lambda b, t: (jnp.maximum((b * nt + t) * (tm // F32_SUBLANES) - 1, 0), 0)),
            pl.BlockSpec((1, 1, D_MODEL), lambda b, t: (b, 0, 0)),
            vec(),
            _resident((None, 6, D_MODEL), lambda b, t: (i, 0, 0)),
            mat(D_MODEL, D_MODEL), mat(D_MODEL, D_MODEL), mat(D_MODEL, D_MODEL),
            mat(D_MODEL, lora[0]), mat(lora[0], D_MODEL),
            mat(D_MODEL, lora[1]), mat(lora[1], D_MODEL),
            mat(D_MODEL, lora[2]), mat(lora[2], D_MODEL),
            vec(), vec(), vec(), vec(), vec(),
            _resident((D_MODEL, LANES), lambda b, t: (0, 0)),
            _resident((LANES, D_MODEL), lambda b, t: (0, 0)),
        ],
        out_specs=[head_spec] * 6 + [
            pl.BlockSpec((tm, D_MODEL), lambda b, t: (b * nt + t, 0)),
            pl.BlockSpec((tm, D_MODEL), lambda b, t: (b * nt + t, 0)),
            pl.BlockSpec((1, F32_SUBLANES, D_MODEL), lambda b, t: (b, 0, 0)),
        ],
        out_shape=[head_shape] * 6 + [
            jax.ShapeDtypeStruct((B * L, D_MODEL), BF16),
            jax.ShapeDtypeStruct((B * L, D_MODEL), BF16),
            jax.ShapeDtypeStruct((B, F32_SUBLANES, D_MODEL), F32),
        ],
        compiler_params=_params(("parallel", "arbitrary")),
    )(x, x, shift_prev, gn, p['rwkv_mu'], p['rwkv_wr'], p['rwkv_wk'], p['rwkv_wv'],
      p['rwkv_w1'], p['rwkv_w2'], p['rwkv_a1'], p['rwkv_a2'], p['rwkv_g1'], p['rwkv_g2'],
      p['rwkv_w0'][i][None], p['rwkv_a0'][i][None], p['rwkv_k_k'][i][None], p['rwkv_k_a'][i][None],
      p['rwkv_r_k'][i].reshape(1, D_MODEL), p['head_onehot'], p['head_onehot_t'])


DIAG = 16


def _replication_matrix(n):
    src = np.arange(n)[:, None]
    dst = np.arange(n)[None, :]
    picks = np.concatenate([(src == (dst // DIAG) * DIAG + s) for s in range(DIAG - 1)], axis=1)
    return jnp.asarray(picks, dtype=BF16)


def _unit_lower_inverses(lows, c, rep_ref):
    n = 2 * c
    ti = lax.broadcasted_iota(jnp.int32, (c, n), 0)
    li = lax.broadcasted_iota(jnp.int32, (c, n), 1)
    si = li & (c - 1)
    head1 = li >= c

    def block_diag(x):
        return jnp.concatenate([jnp.where(head1, 0.0, x), jnp.where(head1, x, 0.0)], axis=0)

    if rep_ref is None:
        invs = [jnp.where(ti == si, 1.0, 0.0) + jnp.where(jnp.logical_and(ti == si + 1, (ti & 1) == 1), low, 0.0)
                for low in lows]
        m = 2
    else:
        pt = lax.broadcasted_iota(jnp.int32, (DIAG, n), 0)
        pl_ = lax.broadcasted_iota(jnp.int32, (DIAG, n), 1)
        blk = (pl_ & (c - 1)) // DIAG
        packed = []
        for low in lows:
            d = jnp.zeros((DIAG, n), F32)
            for i in range(c // DIAG):
                d = jnp.where(blk == i, low[i * DIAG:(i + 1) * DIAG], d)
            packed.append(d)
        rep = _dot(jnp.concatenate(packed, axis=0).astype(BF16), rep_ref[...])
        sols = [jnp.where(pt == (pl_ & (DIAG - 1)), 1.0, 0.0) for _ in lows]
        for s in range(DIAG - 1):
            sols = [sol + rep[p * DIAG:(p + 1) * DIAG, s * n:(s + 1) * n] * sol[s:s + 1]
                    for p, sol in enumerate(sols)]
        invs = [jnp.concatenate([jnp.where(blk == i, sol, 0.0) for i in range(c // DIAG)], axis=0) for sol in sols]
        m = DIAG
    while m < c:
        sh = m.bit_length() - 1
        sub = jnp.logical_and(jnp.logical_and((ti >> (sh + 1)) == (si >> (sh + 1)), ((ti >> sh) & 1) == 1),
                              ((si >> sh) & 1) == 0)
        diag = [block_diag(inv).astype(BF16) for inv in invs]
        half = [_dot(inv.astype(BF16), block_diag(jnp.where(sub, low, 0.0)).astype(BF16)).astype(BF16)
                for inv, low in zip(invs, lows)]
        invs = [inv + _dot(h, d) for inv, h, d in zip(invs, half, diag)]
        m *= 2
    return invs


def _scan_chunks(seqs, states, c, rep_ref):
    n = 2 * c
    row = lax.broadcasted_iota(jnp.int32, (c, LANES), 0)
    head1 = lax.broadcasted_iota(jnp.int32, (c, LANES), 1) >= RWKV_HEAD
    ti = lax.broadcasted_iota(jnp.int32, (c, n), 0)
    si = lax.broadcasted_iota(jnp.int32, (c, n), 1) & (c - 1)
    strict = ti > si
    incl = ti >= si
    pi = lax.broadcasted_iota(jnp.int32, (LANES, LANES), 0)
    pj = lax.broadcasted_iota(jnp.int32, (LANES, LANES), 1)
    same_head = (pi >= RWKV_HEAD) == (pj >= RWKV_HEAD)

    def stack(x):
        return jnp.concatenate([jnp.where(head1, 0.0, x), jnp.where(head1, x, 0.0)], axis=0)

    lhs, rhs, v_st, v_bf, bk_end, decay = [], [], [], [], [], []
    for r, lw, k, v, kk, b in seqs:
        cum = lw
        sh = 1
        while sh < c:
            cum = cum + jnp.where(row >= sh, pltpu.roll(cum, sh, 0), 0.0)
            sh *= 2
        tot = cum[c - 1:c]
        grow = jnp.exp(-cum)
        tail = jnp.exp(tot - cum)
        a_t = -kk * jnp.exp(cum - lw)
        r_t = r * jnp.exp(cum)
        lhs.append(jnp.concatenate([a_t, r_t], axis=0).astype(BF16))
        rhs.append(jnp.concatenate([stack(b * grow), stack(k * grow)], axis=0).astype(BF16))
        v_st.append(stack(v).astype(BF16))
        v_bf.append(v.astype(BF16))
        bk_end.append(jnp.concatenate([b * tail, k * tail], axis=0).astype(BF16))
        decay.append(jnp.exp(tot))
    grams = [_dot_nt(x, y) for x, y in zip(lhs, rhs)]
    lows = [jnp.where(strict, g[:c, :n], 0.0) for g in grams]
    m_rb = [jnp.where(incl, g[c:, :n], 0.0).astype(BF16) for g in grams]
    m_v = [jnp.concatenate([jnp.where(strict, g[:c, n:], 0.0), jnp.where(incl, g[c:, n:], 0.0)], axis=0).astype(BF16)
           for g in grams]
    invs = [inv.astype(BF16) for inv in _unit_lower_inverses(lows, c, rep_ref)]
    from_v = [_dot(m, vs) for m, vs in zip(m_v, v_st)]

    ys = []
    states = list(states)
    npair = len(states)
    for j in range(len(seqs) // npair):
        sl = slice(j * npair, (j + 1) * npair)
        from_state = [_dot_nt(x, s.astype(BF16)) for x, s in zip(lhs[sl], states)]
        rhs_sa = [stack(f[:c] + fv[:c]).astype(BF16) for f, fv in zip(from_state, from_v[sl])]
        sas = [_dot(inv, x) for inv, x in zip(invs[sl], rhs_sa)]
        ys += [f[c:] + fv[c:] + _dot(m, stack(sa).astype(BF16))
               for f, fv, m, sa in zip(from_state, from_v[sl], m_rb[sl], sas)]
        upd = [_dot_tn(jnp.concatenate([sa.astype(BF16), vb], axis=0), be)
               for sa, vb, be in zip(sas, v_bf[sl], bk_end[sl])]
        states = [s * d + jnp.where(same_head, u, 0.0) for s, d, u in zip(states, decay[sl], upd)]
    return ys, states


def _rwkv_scan_kernel(r_ref, lw_ref, k_ref, v_ref, kk_ref, b_ref, s0_ref, *rest, c, sub, vpu_diag):
    if vpu_diag:
        rep_ref, y_ref, so_ref, s_ref = rest
    else:
        rep_ref = None
        y_ref, so_ref, s_ref = rest
    ci = pl.program_id(1)

    @pl.when(ci == 0)
    def _():
        s_ref[...] = s0_ref[0]

    seqs = [tuple(ref[0, hp, j * c:(j + 1) * c] for ref in (r_ref, lw_ref, k_ref, v_ref, kk_ref, b_ref))
            for j in range(sub) for hp in range(N_PAIRS)]
    ys, new_states = _scan_chunks(seqs, [s_ref[hp] for hp in range(N_PAIRS)], c, rep_ref)
    for j in range(sub):
        for hp in range(N_PAIRS):
            y_ref[0, hp, j * c:(j + 1) * c] = ys[j * N_PAIRS + hp]
    for hp in range(N_PAIRS):
        s_ref[hp] = new_states[hp]

    @pl.when(ci == pl.num_programs(1) - 1)
    def _():
        so_ref[0] = s_ref[...]


def _rwkv_scan(r, lw, k, v, kk, b, s0, B, L):
    c = min(CHUNK, L)
    nc = L // c
    sub = next(s for s in SCAN_CHUNKS_PER_STEP if nc % s == 0)
    nc //= sub
    seq = pl.BlockSpec((1, N_PAIRS, sub * c, LANES), lambda bi, ci: (bi, 0, ci, 0))
    st = pl.BlockSpec((1, N_PAIRS, LANES, LANES), lambda bi, ci: (bi, 0, 0, 0))
    vpu_diag = 2 * c == LANES
    extra_specs, extra_args = [], []
    if vpu_diag:
        extra_specs = [_resident((LANES, (DIAG - 1) * LANES), lambda bi, ci: (0, 0))]
        extra_args = [_replication_matrix(LANES)]
    return pl.pallas_call(
        functools.partial(_rwkv_scan_kernel, c=c, sub=sub, vpu_diag=vpu_diag),
        grid=(B, nc),
        in_specs=[seq] * 6 + [st] + extra_specs,
        out_specs=[seq, st],
        out_shape=[jax.ShapeDtypeStruct((B, N_PAIRS, L, LANES), F32),
                   jax.ShapeDtypeStruct((B, N_PAIRS, LANES, LANES), F32)],
        scratch_shapes=[pltpu.VMEM((N_PAIRS, LANES, LANES), F32)],
        compiler_params=_params(("parallel", "arbitrary")),
    )(r, lw, k, v, kk, b, s0, *extra_args)


def _rwkv_post_kernel(x_ref, y_ref, g_ref, bonus_ref, lw_ref, lb_ref, e_ref, et_ref, wo_ref, o_ref, *, parts):
    rows = x_ref.shape[0] // parts
    sls = [slice(i * rows, (i + 1) * rows) for i in range(parts)]
    head_sum = lambda t, split: _head_sum(t, e_ref, et_ref, split)
    ys = [jnp.concatenate([y_ref[0, hp, sl] for hp in range(N_PAIRS)], axis=-1) for sl in sls]
    ds = [y - head_sum(y, True) * (1.0 / RWKV_HEAD) for y in ys]
    var = [head_sum(d * d, False) * (1.0 / RWKV_HEAD) for d in ds]
    yn = [d * lax.rsqrt(vr + LNX_EPS) * lw_ref[...] + lb_ref[...] for d, vr in zip(ds, var)]
    gated = [((n + bonus_ref[sl, :].astype(F32)) * g_ref[sl, :].astype(F32)).astype(BF16) for n, sl in zip(yn, sls)]
    for gt, sl in zip(gated, sls):
        o_ref[sl, :] = x_ref[sl, :] + _dot(gt, wo_ref[...])


def _rwkv_post(x, y, g, bonus, p, i, B, L):
    tm = min(TOKEN_TILE, L)
    nt = L // tm
    vec = lambda: _resident((1, D_MODEL), lambda b, t: (0, 0))
    head_spec = pl.BlockSpec((1, N_PAIRS, tm, LANES), lambda b, t: (b, 0, t, 0))
    row = pl.BlockSpec((tm, D_MODEL), lambda b, t: (b * nt + t, 0))
    return pl.pallas_call(
        functools.partial(_rwkv_post_kernel, parts=POST_PARTS if tm == TOKEN_TILE else 1),
        grid=(B, nt),
        in_specs=[row, head_spec, row, row, vec(), vec(),
                  _resident((D_MODEL, LANES), lambda b, t: (0, 0)),
                  _resident((LANES, D_MODEL), lambda b, t: (0, 0)),
                  _resident((D_MODEL, D_MODEL), lambda b, t: (0, 0))],
        out_specs=row,
        out_shape=jax.ShapeDtypeStruct((B * L, D_MODEL), F32),
        compiler_params=_params(("parallel", "parallel")),
    )(x, y, g, bonus, p['rwkv_lnx_w'][i][None], p['rwkv_lnx_b'][i][None],
      p['head_onehot'], p['head_onehot_t'], p['rwkv_wo'])


def _pack_state(s):
    B = s.shape[0]
    s = s.reshape(B, N_PAIRS, 2, RWKV_HEAD, RWKV_HEAD)
    z = jnp.zeros_like(s[:, :, 0])
    top = jnp.concatenate([s[:, :, 0], z], axis=-1)
    bot = jnp.concatenate([z, s[:, :, 1]], axis=-1)
    return jnp.concatenate([top, bot], axis=-2)


def _unpack_state(s):
    B = s.shape[0]
    return jnp.stack([s[:, :, :RWKV_HEAD, :RWKV_HEAD], s[:, :, RWKV_HEAD:, RWKV_HEAD:]], axis=2).reshape(
        B, RWKV_H, RWKV_HEAD, RWKV_HEAD)


class _Stream:
    def __init__(self, x, pos0, caches):
        self.B, self.L, _ = x.shape
        self.x = x.reshape(self.B * self.L, D_MODEL)
        self.pos0 = pos0
        self.pool_c, self.k_c, self.v_c, self.shift_c, self.wkv_c = caches
        self.stepping = self.k_c is not None
        if self.stepping:
            self.bkt = _bucket_index(self.L, SWA_ROWS + self.L, SWA_ROWS)
        else:
            self.bkt = _bucket_index(CHUNK, WINDOW + CHUNK, WINDOW)
        self.new = {name: [] for name in ('pool', 'k', 'v', 'shift', 'wkv')}

    def mixer(self, l, p):
        B, L, i = self.B, self.L, l // 2
        gn = p['norm_mix'][l][None]
        if l % 2 == 0:
            if self.stepping:
                hist = jnp.pad(self.pool_c[i], ((0, 0), (POOL_HALO - POOL_HIST, 0), (0, 0)))
                k_cache = self.k_c[i].reshape(B, SWA_ROWS, KV_W)
                v_cache = self.v_c[i].reshape(B, SWA_ROWS, KV_W)
            else:
                hist = jnp.zeros((B, POOL_HALO, C_POOL), F32)
                k_cache = v_cache = jnp.zeros((B, WINDOW, KV_W), F32)
            self.x, u_tail, k_tail, v_tail = _even_layer(
                self.x, gn, p, i, hist, k_cache, v_cache, self.bkt, B, L, self.pos0,
                chunk=L if self.stepping else CHUNK, masked=not self.stepping)
            self.new['pool'].append(u_tail[:, -POOL_HIST:])
            self.new['k'].append(k_tail.reshape(B, SWA_ROWS, N_KV_HEADS, HEAD_DIM))
            self.new['v'].append(v_tail.reshape(B, SWA_ROWS, N_KV_HEADS, HEAD_DIM))
        else:
            if self.stepping:
                shift_prev = self.shift_c[i][:, None, :]
                s0 = _pack_state(self.wkv_c[i])
            else:
                shift_prev = jnp.zeros((B, 1, D_MODEL), F32)
                s0 = jnp.zeros((B, N_PAIRS, LANES, LANES), F32)
            r, lw, k, v, kk, b, g, bonus, hs = _rwkv_pre(self.x, shift_prev, gn, p, i, B, L)
            y, s_new = _rwkv_scan(r, lw, k, v, kk, b, s0, B, L)
            self.x = _rwkv_post(self.x, y, g, bonus, p, i, B, L)
            self.new['shift'].append(hs[:, -1])
            self.new['wkv'].append(_unpack_state(s_new))

    def results(self):
        return (self.x.reshape(self.B, self.L, D_MODEL),) + tuple(
            jnp.stack(self.new[name]) for name in ('pool', 'k', 'v', 'shift', 'wkv'))


_FFN_WEIGHTS = ('ffn_w_gate', 'ffn_w_up', 'ffn_w_down')
_EVEN_WEIGHTS = ('w_in_even', 'w_out_even')
_ODD_WEIGHTS = ('rwkv_wr', 'rwkv_wk', 'rwkv_wv', 'rwkv_w1', 'rwkv_w2', 'rwkv_a1', 'rwkv_a2', 'rwkv_g1', 'rwkv_g2',
                'rwkv_wo')


def _trunk(long, short, p):
    depth = p['norm_mix'].shape[0]
    ffn_w = tuple(p[name][0, 0].astype(BF16) for name in _FFN_WEIGHTS)
    for l in range(depth):
        for j in range(2):
            n = 2 * l + j
            jobs = [(p[name], divmod(n + 1, 2)) for name in _FFN_WEIGHTS] if n + 1 < 2 * depth else []
            mixer_names = (_EVEN_WEIGHTS if l % 2 == 0 else _ODD_WEIGHTS) if j == 0 else ()
            jobs += [(p[name], (l // 2,)) for name in mixer_names]
            final_g = p['norm_final'][None] if n == 2 * depth - 1 else None
            long.x, short.x, done = _ffn(long.x, short.x, p['norm_ffn'][l, j][None], ffn_w, final_g, jobs)
            ffn_w = done[:len(_FFN_WEIGHTS)]
            if j == 0:
                mixer_w = dict(p, **dict(zip(mixer_names, done[len(done) - len(mixer_names):])))
                long.mixer(l, mixer_w)
                short.mixer(l, mixer_w)
    return long.results(), short.results()


def _prepare(p):
    p = dict(p)
    p['pool_w'] = p['pool_w'].astype(BF16)
    onehot = np.arange(D_MODEL)[:, None] // RWKV_HEAD == np.arange(LANES)[None, :]
    p['head_onehot'] = jnp.asarray(onehot, dtype=BF16)
    p['head_onehot_t'] = jnp.asarray(onehot.T, dtype=BF16)
    return p


def kernel(x_prompt, x_sample, cache_pool, cache_swa_k, cache_swa_v, state_shift, state_wkv, t5_table, norm_ffn, ffn_w_gate, ffn_w_up, ffn_w_down, norm_mix, w_in_even, pool_w, pool_scale, attn_sinks, w_out_even, rwkv_mu, rwkv_wr, rwkv_wk, rwkv_wv, rwkv_w0, rwkv_w1, rwkv_w2, rwkv_a0, rwkv_a1, rwkv_a2, rwkv_g1, rwkv_g2, rwkv_k_k, rwkv_k_a, rwkv_r_k, rwkv_lnx_w, rwkv_lnx_b, rwkv_wo, norm_final):
    p = _prepare(dict(
        t5_table=t5_table, norm_ffn=norm_ffn, ffn_w_gate=ffn_w_gate, ffn_w_up=ffn_w_up, ffn_w_down=ffn_w_down,
        norm_mix=norm_mix, w_in_even=w_in_even, pool_w=pool_w, pool_scale=pool_scale, attn_sinks=attn_sinks,
        w_out_even=w_out_even, rwkv_mu=rwkv_mu, rwkv_wr=rwkv_wr, rwkv_wk=rwkv_wk, rwkv_wv=rwkv_wv,
        rwkv_w0=rwkv_w0, rwkv_w1=rwkv_w1, rwkv_w2=rwkv_w2, rwkv_a0=rwkv_a0, rwkv_a1=rwkv_a1, rwkv_a2=rwkv_a2,
        rwkv_g1=rwkv_g1, rwkv_g2=rwkv_g2, rwkv_k_k=rwkv_k_k, rwkv_k_a=rwkv_k_a, rwkv_r_k=rwkv_r_k,
        rwkv_lnx_w=rwkv_lnx_w, rwkv_lnx_b=rwkv_lnx_b, rwkv_wo=rwkv_wo, norm_final=norm_final))
    prompt = _Stream(x_prompt, 0, (None, None, None, None, None))
    sample = _Stream(x_sample, PAST_LEN, (cache_pool, cache_swa_k, cache_swa_v, state_shift, state_wkv))
    (y_p, pool_p, k_p, v_p, shift_p, wkv_p), (y_s, pool_s, k_s, v_s, shift_s, wkv_s) = _trunk(prompt, sample, p)
    return (y_p, y_s, pool_p, pool_s, k_p, k_s, v_p, v_s, shift_p, shift_s, wkv_p, wkv_s)
```

```python
import functools
import math

import jax
import jax.numpy as jnp
import numpy as np
from jax import lax
from jax.experimental import pallas as pl
from jax.experimental.pallas import tpu as pltpu

F32 = jnp.float32
BF16 = jnp.bfloat16

D_MODEL = 1024
D_FF = 2816
NORM_EPS = 1e-6
CHUNK = 64
POOL_WINDOWS = (2, 4, 8, 16)
C_POOL = 512
POOL_GC = 128
POOL_HIST = 15
POOL_HALO = 16
HEAD_DIM = 64
N_Q_HEADS = 8
N_KV_HEADS = 2
GQA_GROUP = 4
WINDOW = 128
SWA_ROWS = 128
Q_W = 512
KV_W = 128
IN_EVEN = C_POOL + Q_W + 2 * KV_W
N_BUCKETS = 32
MAX_DISTANCE = 128
RWKV_HEAD = 64
RWKV_H = 16
N_PAIRS = RWKV_H // 2
LNX_EPS = 64e-5
PAST_LEN = 4096

LANES = 128
F32_SUBLANES = 8
BF16_SUBLANES = 16
VMEM_LIMIT_BYTES = 56 * 1024 * 1024

FFN_TILE = 1024
FFN_F_CHUNK = 256
TOKEN_TILE = 512
POST_PARTS = 2
EVEN_CHUNKS_PER_STEP = 8
SCAN_CHUNKS_PER_STEP = (4, 2, 1)


def _params(sem):
    return pltpu.CompilerParams(dimension_semantics=sem, vmem_limit_bytes=VMEM_LIMIT_BYTES)


def _dot(a, b):
    return jnp.dot(a, b, preferred_element_type=F32)


def _dot_nt(a, b):
    return lax.dot_general(a, b, (((1,), (1,)), ((), ())), preferred_element_type=F32)


def _dot_tn(a, b):
    return lax.dot_general(a, b, (((0,), (0,)), ((), ())), preferred_element_type=F32)


def _rms(x, g):
    return x * lax.rsqrt(jnp.mean(x * x, axis=-1, keepdims=True) + NORM_EPS) * g


def _split(x):
    hi = x.astype(BF16)
    lo = (x - hi.astype(F32)).astype(BF16)
    return hi, lo


def _head_sum(x, e_ref, et_ref, split=True):
    if not split:
        return _dot(_dot(x.astype(BF16), e_ref[...]).astype(BF16), et_ref[...])
    hi, lo = _split(x)
    s = _dot(hi, e_ref[...]) + _dot(lo, e_ref[...])
    shi, slo = _split(s)
    return _dot(shi, et_ref[...]) + _dot(slo, et_ref[...])


def _ffn_kernel(x_ref, xs_ref, g_ref, wg_ref, wu_ref, wd_ref, *rest, f_chunk, final, convert):
    rest = list(rest)
    gf_ref = rest.pop(0) if final else None
    src_refs = [rest.pop(0) for _ in range(convert)]
    o_ref, os_ref = rest.pop(0), rest.pop(0)
    dst_refs = [rest.pop(0) for _ in range(convert)]
    acc_ref, = rest
    for src, dst in zip(src_refs, dst_refs):
        dst[...] = src[...].astype(BF16)

    def rows(src_ref, dst_ref):
        n = src_ref.shape[0]
        x = src_ref[...]
        h = _rms(x, g_ref[...]).astype(BF16)
        for j in range(D_FF // f_chunk):
            sl = slice(j * f_chunk, (j + 1) * f_chunk)
            gate = _dot(h, wg_ref[:, sl])
            up = _dot(h, wu_ref[:, sl])
            act = (gate * jax.nn.sigmoid(gate) * up).astype(BF16)
            part = _dot(act, wd_ref[sl, :])
            if j == 0:
                acc_ref[:n] = part
            else:
                acc_ref[:n] += part
        y = x + 0.5 * acc_ref[:n]
        if final:
            y = _rms(y, gf_ref[...])
        dst_ref[...] = y

    rows(x_ref, o_ref)

    @pl.when(pl.program_id(0) == pl.num_programs(0) - 1)
    def _():
        rows(xs_ref, os_ref)


def _resident(shape, index_map):
    return pl.BlockSpec(shape, index_map, pipeline_mode=pl.Buffered(1))


def _slab_steps(rows, steps):
    return next(k for k in range(steps, 0, -1) if rows % (k * BF16_SUBLANES) == 0)


def _ffn(x, xs, g, weights, final_g=None, convert=()):
    T = x.shape[0]
    tm = min(FFN_TILE, T)
    steps = T // tm
    Ts = xs.shape[0]
    assert Ts <= tm
    final = final_g is not None
    in_specs = [pl.BlockSpec((tm, D_MODEL), lambda i: (i, 0)), _resident((Ts, D_MODEL), lambda i: (0, 0)),
                _resident((1, D_MODEL), lambda i: (0, 0))]
    in_specs += [_resident(w.shape, lambda i: (0, 0)) for w in weights]
    args = [x, xs, g, *weights]
    out_specs = [pl.BlockSpec((tm, D_MODEL), lambda i: (i, 0)), pl.BlockSpec((Ts, D_MODEL), lambda i: (0, 0))]
    out_shape = [jax.ShapeDtypeStruct((T, D_MODEL), F32), jax.ShapeDtypeStruct((Ts, D_MODEL), F32)]
    if final:
        in_specs.append(_resident((1, D_MODEL), lambda i: (0, 0)))
        args.append(final_g)
    for w, lead in convert:
        rows, cols = w.shape[len(lead):]
        k = _slab_steps(rows, steps)
        in_specs.append(pl.BlockSpec((None,) * len(lead) + (rows // k, cols),
                                     lambda i, k=k, lead=lead: (*lead, jnp.minimum(i, k - 1), 0)))
        out_specs.append(pl.BlockSpec((rows // k, cols), lambda i, k=k: (jnp.minimum(i, k - 1), 0)))
        out_shape.append(jax.ShapeDtypeStruct((rows, cols), BF16))
        args.append(w)
    outs = pl.pallas_call(
        functools.partial(_ffn_kernel, f_chunk=FFN_F_CHUNK, final=final, convert=len(convert)),
        grid=(steps,),
        in_specs=in_specs,
        out_specs=out_specs,
        out_shape=out_shape,
        scratch_shapes=[pltpu.VMEM((tm, D_MODEL), F32)],
        compiler_params=_params(("arbitrary",)),
    )(*args)
    return outs[0], outs[1], tuple(outs[2:])


def _build_bias(bkt_ref, tab_ref, bias_ref, lq):
    bkt = bkt_ref[...]
    for h in range(N_Q_HEADS):
        b = jnp.zeros(bkt.shape, F32)
        for n in range(N_BUCKETS):
            b = jnp.where(bkt == n, tab_ref[n, h], b)
        g, i = divmod(h, GQA_GROUP)
        bias_ref[g, :, i * lq:(i + 1) * lq] = b


def _group_queries(q, g):
    return jnp.concatenate([q[:, h * HEAD_DIM:(h + 1) * HEAD_DIM]
                            for h in range(g * GQA_GROUP, (g + 1) * GQA_GROUP)], axis=0) * (HEAD_DIM ** -0.5)


def _sink_rows(sink_ref, lq):
    lane = lax.broadcasted_iota(jnp.int32, (1, GQA_GROUP * lq), 1)
    rows = []
    for g in range(N_KV_HEADS):
        r = jnp.zeros((1, GQA_GROUP * lq), F32)
        for i in range(GQA_GROUP):
            r = jnp.where(lane // lq == i, sink_ref[g * GQA_GROUP + i], r)
        rows.append(r)
    return rows


def _attn_core(qs, ks, vs, biases, sinks, valids):
    ss = [_dot_nt(k, q) + b for q, k, b in zip(qs, ks, biases)]
    ss = [s if ok is None else jnp.where(ok, s, -1e30) for s, ok in zip(ss, valids)]
    ms = [jnp.maximum(jnp.max(s, axis=0, keepdims=True), sk) for s, sk in zip(ss, sinks)]
    ps = [jnp.exp(s - m) for s, m in zip(ss, ms)]
    invs = [1.0 / (jnp.sum(p, axis=0, keepdims=True) + jnp.exp(sk - m)) for p, sk, m in zip(ps, sinks, ms)]
    return [_dot_tn((p * r).astype(BF16), v) for p, r, v in zip(ps, invs, vs)]


def _smem():
    return pl.BlockSpec(memory_space=pltpu.SMEM)


def _even_layer_kernel(x_ref, gn_ref, win_ref, pw_ref, ps_ref, hist_ref, kc_ref, vc_ref, bkt_ref, tab_ref, sink_ref,
                       wop_ref, woa_ref, o_ref, utail_o, ktail_o, vtail_o, halo_ref, kprev_ref, vprev_ref, bias_ref,
                       *, rows, chunk, pos0, masked):
    t = pl.program_id(1)

    @pl.when(jnp.logical_and(pl.program_id(0) == 0, t == 0))
    def _():
        _build_bias(bkt_ref, tab_ref, bias_ref, chunk)

    @pl.when(t == 0)
    def _():
        halo_ref[...] = hist_ref[0]
        kprev_ref[...] = kc_ref[0]
        vprev_ref[...] = vc_ref[0]

    x = x_ref[...]
    z = _dot(_rms(x, gn_ref[...]).astype(BF16), win_ref[...])
    u = z[:, :C_POOL]
    q = z[:, C_POOL:C_POOL + Q_W].astype(BF16)
    k_all = jnp.concatenate([kprev_ref[...], z[:, C_POOL + Q_W:C_POOL + Q_W + KV_W]], axis=0)
    v_all = jnp.concatenate([vprev_ref[...], z[:, C_POOL + Q_W + KV_W:]], axis=0)

    ext = jnp.concatenate([halo_ref[...], u], axis=0)
    pos = pos0 + t * rows + lax.broadcasted_iota(jnp.int32, (rows, POOL_GC), 0)
    pooled = []
    for gi, w in enumerate(POOL_WINDOWS):
        sl = slice(gi * POOL_GC, (gi + 1) * POOL_GC)
        s = ext[:, sl]
        span = 1
        while span < w:
            s = s + pltpu.roll(s, span, 0)
            span *= 2
        cnt = jnp.minimum(w, pos + 1).astype(F32)
        pooled.append(_dot((s[POOL_HALO:] / cnt - u[:, sl]).astype(BF16), pw_ref[gi]))
    pool_out = (jnp.concatenate(pooled, axis=-1) * ps_ref[...]).astype(BF16)

    lk = WINDOW + chunk
    kb = k_all.astype(BF16)
    vb = v_all.astype(BF16)
    sink_rows = _sink_rows(sink_ref, chunk)
    first_pos = t * rows - WINDOW + lax.broadcasted_iota(jnp.int32, (lk, 1), 0)
    nchunk = rows // chunk
    qs, ks, vs, biases, sinks, valids = [], [], [], [], [], []
    for j in range(nchunk):
        qj = q[j * chunk:(j + 1) * chunk]
        for g in range(N_KV_HEADS):
            qs.append(_group_queries(qj, g))
            ks.append(kb[j * chunk:j * chunk + lk, g * HEAD_DIM:(g + 1) * HEAD_DIM])
            vs.append(vb[j * chunk:j * chunk + lk, g * HEAD_DIM:(g + 1) * HEAD_DIM])
            biases.append(bias_ref[g])
            sinks.append(sink_rows[g])
            valids.append(first_pos + j * chunk >= 0 if masked else None)
    outs = _attn_core(qs, ks, vs, biases, sinks, valids)
    att = jnp.concatenate(
        [jnp.concatenate([outs[j * N_KV_HEADS + g][i * chunk:(i + 1) * chunk]
                          for g in range(N_KV_HEADS) for i in range(GQA_GROUP)], axis=-1) for j in range(nchunk)],
        axis=0).astype(BF16)

    o_ref[...] = x + _dot(pool_out, wop_ref[...]) + _dot(att, woa_ref[...])

    halo_ref[...] = ext[rows:]
    kprev_ref[...] = k_all[rows:]
    vprev_ref[...] = v_all[rows:]
    utail_o[0] = ext[rows:]
    ktail_o[0] = k_all[rows:]
    vtail_o[0] = v_all[rows:]


def _even_layer(x, gn, p, i, hist, k_cache, v_cache, bkt, B, L, pos0, chunk, masked):
    rows = min(EVEN_CHUNKS_PER_STEP * chunk, L)
    nt = L // rows
    lk = WINDOW + chunk
    row = pl.BlockSpec((rows, D_MODEL), lambda b, t: (b * nt + t, 0))
    per_seq = lambda n, m: pl.BlockSpec((1, n, m), lambda b, t: (b, 0, 0))
    return pl.pallas_call(
        functools.partial(_even_layer_kernel, rows=rows, chunk=chunk, pos0=pos0, masked=masked),
        grid=(B, nt),
        in_specs=[row, _resident((1, D_MODEL), lambda b, t: (0, 0)),
                  _resident((D_MODEL, IN_EVEN), lambda b, t: (0, 0)),
                  _resident((None, len(POOL_WINDOWS), POOL_GC, POOL_GC), lambda b, t: (i, 0, 0, 0)),
                  _resident((1, C_POOL), lambda b, t: (0, 0)),
                  per_seq(POOL_HALO, C_POOL), per_seq(WINDOW, KV_W), per_seq(WINDOW, KV_W),
                  _resident((lk, chunk), lambda b, t: (0, 0)), _smem(), _smem(),
                  _resident((C_POOL, D_MODEL), lambda b, t: (0, 0)),
                  _resident((Q_W, D_MODEL), lambda b, t: (1, 0))],
        out_specs=[row, per_seq(POOL_HALO, C_POOL), per_seq(WINDOW, KV_W), per_seq(WINDOW, KV_W)],
        out_shape=[jax.ShapeDtypeStruct((B * L, D_MODEL), F32), jax.ShapeDtypeStruct((B, POOL_HALO, C_POOL), F32),
                   jax.ShapeDtypeStruct((B, WINDOW, KV_W), F32), jax.ShapeDtypeStruct((B, WINDOW, KV_W), F32)],
        scratch_shapes=[pltpu.VMEM((POOL_HALO, C_POOL), F32), pltpu.VMEM((WINDOW, KV_W), F32),
                        pltpu.VMEM((WINDOW, KV_W), F32), pltpu.VMEM((N_KV_HEADS, lk, GQA_GROUP * chunk), F32)],
        compiler_params=_params(("arbitrary", "arbitrary")),
    )(x, gn, p['w_in_even'], p['pool_w'], p['pool_scale'][i][None], hist, k_cache, v_cache, bkt,
      p['t5_table'], p['attn_sinks'][i], p['w_out_even'], p['w_out_even'])


def _t5_bucket(rel):
    half = N_BUCKETS // 2
    max_exact = half // 2
    side = jnp.where(rel > 0, half, 0)
    n = jnp.abs(rel)
    nf = jnp.maximum(n, max_exact).astype(F32)
    large = max_exact + (jnp.log(nf / max_exact) / math.log(MAX_DISTANCE / max_exact)
                         * (half - max_exact)).astype(jnp.int32)
    large = jnp.minimum(large, half - 1)
    return side + jnp.where(n < max_exact, n, large)


def _bucket_index(lq, lk, offset):
    rel = jnp.arange(lk)[:, None] - offset - jnp.arange(lq)[None, :]
    return _t5_bucket(rel).astype(jnp.int32)


def _rwkv_pre_kernel(x_ref, xp_ref, sh_ref, gn_ref, mu_ref, wr_ref, wk_ref, wv_ref, w1_ref, w2_ref,
                     a1_ref, a2_ref, g1_ref, g2_ref, w0_ref, a0_ref, kk_ref, ka_ref, rk_ref, e_ref, et_ref,
                     r_o, lw_o, k_o, v_o, kk_o, b_o, g_o, bonus_o, hs_o, *, tm, nseq):
    t = pl.program_id(1)
    lt = tm // nseq
    gn = gn_ref[...]
    h = _rms(x_ref[...], gn)
    row = lax.broadcasted_iota(jnp.int32, (tm, 1), 0)
    h_prev = pltpu.roll(h, 1, 0)
    for s in range(nseq):
        first_prev = sh_ref[s]
        if nseq == 1:
            first_prev = jnp.where(t == 0, first_prev, _rms(xp_ref[...], gn)[F32_SUBLANES - 1:])
        h_prev = jnp.where(row == s * lt, first_prev, h_prev)
    xx = h_prev - h

    hb = h.astype(BF16)
    xb = xx.astype(BF16)
    mu = mu_ref[...].astype(BF16)

    def mix(j):
        return hb + xb * mu[j:j + 1]

    zw = w0_ref[...] + _dot(jnp.tanh(_dot(mix(1), w1_ref[...])).astype(BF16), w2_ref[...])
    za = a0_ref[...] + _dot(_dot(mix(4), a1_ref[...]).astype(BF16), a2_ref[...])
    g = _dot(jax.nn.sigmoid(_dot(mix(5), g1_ref[...])).astype(BF16), g2_ref[...])
    r = _dot(mix(0), wr_ref[...])
    lw = -math.exp(-0.5) * jax.nn.sigmoid(zw)
    k = _dot(mix(2), wk_ref[...])
    a = jax.nn.sigmoid(za)
    v = _dot(mix(3), wv_ref[...])
    kk = k * kk_ref[...]
    kk = kk * lax.rsqrt(jnp.maximum(_head_sum(kk * kk, e_ref, et_ref, split=False), 1e-24))
    k = k * (1.0 + (a - 1.0) * ka_ref[...])
    b = kk * a
    for s in range(nseq):
        rows = slice(s * lt, (s + 1) * lt)
        for hp in range(N_PAIRS):
            sl = slice(hp * LANES, (hp + 1) * LANES)
            for out, val in ((r_o, r), (lw_o, lw), (k_o, k), (v_o, v), (kk_o, kk), (b_o, b)):
                out[s, hp] = val[rows, sl]
        hs_o[s] = h[(s + 1) * lt - F32_SUBLANES:(s + 1) * lt]
    g_o[...] = g.astype(BF16)
    bonus_o[...] = (_head_sum(r * k * rk_ref[...], e_ref, et_ref, split=False) * v).astype(BF16)


def _rwkv_pre(x, shift_prev, gn, p, i, B, L):
    nseq = max(1, min(B, TOKEN_TILE // L))
    assert B % nseq == 0
    tm = min(TOKEN_TILE, L) * nseq
    lt = tm // nseq
    nt = L // lt
    vec = lambda: _resident((1, D_MODEL), lambda b, t: (0, 0))
    mat = lambda r, c: _resident((r, c), lambda b, t: (0, 0))
    lora = p['rwkv_w1'].shape[-1], p['rwkv_a1'].shape[-1], p['rwkv_g1'].shape[-1]
    head_spec = pl.BlockSpec((nseq, N_PAIRS, lt, LANES), lambda b, t: (b, 0, t, 0))
    head_shape = jax.ShapeDtypeStruct((B, N_PAIRS, L, LANES), F32)
    return pl.pallas_call(
        functools.partial(_rwkv_pre_kernel, tm=tm, nseq=nseq),
        grid=(B // nseq, nt),
        in_specs=[
            pl.BlockSpec((tm, D_MODEL), lambda b, t: (b * nt + t, 0)),
            pl.BlockSpec((F32_SUBLANES, D_MODEL),
                         lambda b, t: (jnp.maximum((b * nt + t) * (tm // F32_SUBLANES) - 1, 0), 0)),
            pl.BlockSpec((nseq, 1, D_MODEL), lambda b, t: (b, 0, 0)),
            vec(),
            _resident((None, 6, D_MODEL), lambda b, t: (i, 0, 0)),
            mat(D_MODEL, D_MODEL), mat(D_MODEL, D_MODEL), mat(D_MODEL, D_MODEL),
            mat(D_MODEL, lora[0]), mat(lora[0], D_MODEL),
            mat(D_MODEL, lora[1]), mat(lora[1], D_MODEL),
            mat(D_MODEL, lora[2]), mat(lora[2], D_MODEL),
            vec(), vec(), vec(), vec(), vec(),
            _resident((D_MODEL, LANES), lambda b, t: (0, 0)),
            _resident((LANES, D_MODEL), lambda b, t: (0, 0)),
        ],
        out_specs=[head_spec] * 6 + [
            pl.BlockSpec((tm, D_MODEL), lambda b, t: (b * nt + t, 0)),
            pl.BlockSpec((tm, D_MODEL), lambda b, t: (b * nt + t, 0)),
            pl.BlockSpec((nseq, F32_SUBLANES, D_MODEL), lambda b, t: (b, 0, 0)),
        ],
        out_shape=[head_shape] * 6 + [
            jax.ShapeDtypeStruct((B * L, D_MODEL), BF16),
            jax.ShapeDtypeStruct((B * L, D_MODEL), BF16),
            jax.ShapeDtypeStruct((B, F32_SUBLANES, D_MODEL), F32),
        ],
        compiler_params=_params(("parallel", "arbitrary")),
    )(x, x, shift_prev, gn, p['rwkv_mu'], p['rwkv_wr'], p['rwkv_wk'], p['rwkv_wv'],
      p['rwkv_w1'], p['rwkv_w2'], p['rwkv_a1'], p['rwkv_a2'], p['rwkv_g1'], p['rwkv_g2'],
      p['rwkv_w0'][i][None], p['rwkv_a0'][i][None], p['rwkv_k_k'][i][None], p['rwkv_k_a'][i][None],
      p['rwkv_r_k'][i].reshape(1, D_MODEL), p['head_onehot'], p['head_onehot_t'])


DIAG = 16


def _replication_matrix(n):
    src = np.arange(n)[:, None]
    dst = np.arange(n)[None, :]
    picks = np.concatenate([(src == (dst // DIAG) * DIAG + s) for s in range(DIAG - 1)], axis=1)
    return jnp.asarray(picks, dtype=BF16)


def _unit_lower_inverses(lows, c, rep_ref):
    n = 2 * c
    ti = lax.broadcasted_iota(jnp.int32, (c, n), 0)
    li = lax.broadcasted_iota(jnp.int32, (c, n), 1)
    si = li & (c - 1)
    head1 = li >= c

    def block_diag(x):
        return jnp.concatenate([jnp.where(head1, 0.0, x), jnp.where(head1, x, 0.0)], axis=0)

    if rep_ref is None:
        invs = [jnp.where(ti == si, 1.0, 0.0) + jnp.where(jnp.logical_and(ti == si + 1, (ti & 1) == 1), low, 0.0)
                for low in lows]
        m = 2
    else:
        pt = lax.broadcasted_iota(jnp.int32, (DIAG, n), 0)
        pl_ = lax.broadcasted_iota(jnp.int32, (DIAG, n), 1)
        blk = (pl_ & (c - 1)) // DIAG
        packed = []
        for low in lows:
            d = jnp.zeros((DIAG, n), F32)
            for i in range(c // DIAG):
                d = jnp.where(blk == i, low[i * DIAG:(i + 1) * DIAG], d)
            packed.append(d)
        rep = _dot(jnp.concatenate(packed, axis=0).astype(BF16), rep_ref[...])
        sols = [jnp.where(pt == (pl_ & (DIAG - 1)), 1.0, 0.0) for _ in lows]
        for s in range(DIAG - 1):
            sols = [sol + rep[p * DIAG:(p + 1) * DIAG, s * n:(s + 1) * n] * sol[s:s + 1]
                    for p, sol in enumerate(sols)]
        invs = [jnp.concatenate([jnp.where(blk == i, sol, 0.0) for i in range(c // DIAG)], axis=0) for sol in sols]
        m = DIAG
    while m < c:
        sh = m.bit_length() - 1
        sub = jnp.logical_and(jnp.logical_and((ti >> (sh + 1)) == (si >> (sh + 1)), ((ti >> sh) & 1) == 1),
                              ((si >> sh) & 1) == 0)
        diag = [block_diag(inv).astype(BF16) for inv in invs]
        half = [_dot(inv.astype(BF16), block_diag(jnp.where(sub, low, 0.0)).astype(BF16)).astype(BF16)
                for inv, low in zip(invs, lows)]
        invs = [inv + _dot(h, d) for inv, h, d in zip(invs, half, diag)]
        m *= 2
    return invs


def _scan_chunks(seqs, states, c, rep_ref):
    n = 2 * c
    row = lax.broadcasted_iota(jnp.int32, (c, LANES), 0)
    head1 = lax.broadcasted_iota(jnp.int32, (c, LANES), 1) >= RWKV_HEAD
    ti = lax.broadcasted_iota(jnp.int32, (c, n), 0)
    si = lax.broadcasted_iota(jnp.int32, (c, n), 1) & (c - 1)
    strict = ti > si
    incl = ti >= si
    pi = lax.broadcasted_iota(jnp.int32, (LANES, LANES), 0)
    pj = lax.broadcasted_iota(jnp.int32, (LANES, LANES), 1)
    same_head = (pi >= RWKV_HEAD) == (pj >= RWKV_HEAD)

    def stack(x):
        return jnp.concatenate([jnp.where(head1, 0.0, x), jnp.where(head1, x, 0.0)], axis=0)

    lhs, rhs, v_st, v_bf, bk_end, decay = [], [], [], [], [], []
    for r, lw, k, v, kk, b in seqs:
        cum = lw
        sh = 1
        while sh < c:
            cum = cum + jnp.where(row >= sh, pltpu.roll(cum, sh, 0), 0.0)
            sh *= 2
        tot = cum[c - 1:c]
        grow = jnp.exp(-cum)
        tail = jnp.exp(tot - cum)
        a_t = -kk * jnp.exp(cum - lw)
        r_t = r * jnp.exp(cum)
        lhs.append(jnp.concatenate([a_t, r_t], axis=0).astype(BF16))
        rhs.append(jnp.concatenate([stack(b * grow), stack(k * grow)], axis=0).astype(BF16))
        v_st.append(stack(v).astype(BF16))
        v_bf.append(v.astype(BF16))
        bk_end.append(jnp.concatenate([b * tail, k * tail], axis=0).astype(BF16))
        decay.append(jnp.exp(tot))
    grams = [_dot_nt(x, y) for x, y in zip(lhs, rhs)]
    lows = [jnp.where(strict, g[:c, :n], 0.0) for g in grams]
    m_rb = [jnp.where(incl, g[c:, :n], 0.0).astype(BF16) for g in grams]
    m_v = [jnp.concatenate([jnp.where(strict, g[:c, n:], 0.0), jnp.where(incl, g[c:, n:], 0.0)], axis=0).astype(BF16)
           for g in grams]
    invs = [inv.astype(BF16) for inv in _unit_lower_inverses(lows, c, rep_ref)]
    from_v = [_dot(m, vs) for m, vs in zip(m_v, v_st)]

    ys = []
    states = list(states)
    npair = len(states)
    for j in range(len(seqs) // npair):
        sl = slice(j * npair, (j + 1) * npair)
        from_state = [_dot_nt(x, s.astype(BF16)) for x, s in zip(lhs[sl], states)]
        rhs_sa = [stack(f[:c] + fv[:c]).astype(BF16) for f, fv in zip(from_state, from_v[sl])]
        sas = [_dot(inv, x) for inv, x in zip(invs[sl], rhs_sa)]
        ys += [f[c:] + fv[c:] + _dot(m, stack(sa).astype(BF16))
               for f, fv, m, sa in zip(from_state, from_v[sl], m_rb[sl], sas)]
        upd = [_dot_tn(jnp.concatenate([sa.astype(BF16), vb], axis=0), be)
               for sa, vb, be in zip(sas, v_bf[sl], bk_end[sl])]
        states = [s * d + jnp.where(same_head, u, 0.0) for s, d, u in zip(states, decay[sl], upd)]
    return ys, states


def _rwkv_scan_kernel(r_ref, lw_ref, k_ref, v_ref, kk_ref, b_ref, s0_ref, *rest, c, sub, nseq, vpu_diag):
    if vpu_diag:
        rep_ref, y_ref, so_ref, s_ref = rest
    else:
        rep_ref = None
        y_ref, so_ref, s_ref = rest
    ci = pl.program_id(1)
    chains = [(s, hp) for s in range(nseq) for hp in range(N_PAIRS)]

    @pl.when(ci == 0)
    def _():
        s_ref[...] = s0_ref[...]

    seqs = [tuple(ref[s, hp, j * c:(j + 1) * c] for ref in (r_ref, lw_ref, k_ref, v_ref, kk_ref, b_ref))
            for j in range(sub) for s, hp in chains]
    ys, new_states = _scan_chunks(seqs, [s_ref[s, hp] for s, hp in chains], c, rep_ref)
    for j in range(sub):
        for n, (s, hp) in enumerate(chains):
            y_ref[s, hp, j * c:(j + 1) * c] = ys[j * len(chains) + n]
    for n, (s, hp) in enumerate(chains):
        s_ref[s, hp] = new_states[n]

    @pl.when(ci == pl.num_programs(1) - 1)
    def _():
        so_ref[...] = s_ref[...]


def _rwkv_scan(r, lw, k, v, kk, b, s0, B, L):
    c = min(CHUNK, L)
    nc = L // c
    sub = next(s for s in SCAN_CHUNKS_PER_STEP if nc % s == 0)
    nc //= sub
    nseq = next(s for s in SCAN_CHUNKS_PER_STEP if B % s == 0) if nc * sub == 1 else 1
    seq = pl.BlockSpec((nseq, N_PAIRS, sub * c, LANES), lambda bi, ci: (bi, 0, ci, 0))
    st = pl.BlockSpec((nseq, N_PAIRS, LANES, LANES), lambda bi, ci: (bi, 0, 0, 0))
    vpu_diag = 2 * c == LANES
    extra_specs, extra_args = [], []
    if vpu_diag:
        extra_specs = [_resident((LANES, (DIAG - 1) * LANES), lambda bi, ci: (0, 0))]
        extra_args = [_replication_matrix(LANES)]
    return pl.pallas_call(
        functools.partial(_rwkv_scan_kernel, c=c, sub=sub, nseq=nseq, vpu_diag=vpu_diag),
        grid=(B // nseq, nc),
        in_specs=[seq] * 6 + [st] + extra_specs,
        out_specs=[seq, st],
        out_shape=[jax.ShapeDtypeStruct((B, N_PAIRS, L, LANES), F32),
                   jax.ShapeDtypeStruct((B, N_PAIRS, LANES, LANES), F32)],
        scratch_shapes=[pltpu.VMEM((nseq, N_PAIRS, LANES, LANES), F32)],
        compiler_params=_params(("parallel", "arbitrary")),
    )(r, lw, k, v, kk, b, s0, *extra_args)


def _rwkv_post_kernel(x_ref, y_ref, g_ref, bonus_ref, lw_ref, lb_ref, e_ref, et_ref, wo_ref, o_ref, *, parts, nseq):
    rows = x_ref.shape[0] // parts
    sls = [slice(i * rows, (i + 1) * rows) for i in range(parts)]
    head_sum = lambda t, split: _head_sum(t, e_ref, et_ref, split)
    if nseq == 1:
        ys = [jnp.concatenate([y_ref[0, hp, sl] for hp in range(N_PAIRS)], axis=-1) for sl in sls]
    else:
        ys = [jnp.concatenate([jnp.concatenate([y_ref[s, hp] for hp in range(N_PAIRS)], axis=-1)
                               for s in range(nseq)], axis=0)]
    ds = [y - head_sum(y, True) * (1.0 / RWKV_HEAD) for y in ys]
    var = [head_sum(d * d, False) * (1.0 / RWKV_HEAD) for d in ds]
    yn = [d * lax.rsqrt(vr + LNX_EPS) * lw_ref[...] + lb_ref[...] for d, vr in zip(ds, var)]
    gated = [((n + bonus_ref[sl, :].astype(F32)) * g_ref[sl, :].astype(F32)).astype(BF16) for n, sl in zip(yn, sls)]
    for gt, sl in zip(gated, sls):
        o_ref[sl, :] = x_ref[sl, :] + _dot(gt, wo_ref[...])


def _rwkv_post(x, y, g, bonus, p, i, B, L):
    nseq = max(1, min(B, TOKEN_TILE // L))
    assert B % nseq == 0
    lt = min(TOKEN_TILE, L)
    tm = lt * nseq
    nt = L // lt
    vec = lambda: _resident((1, D_MODEL), lambda b, t: (0, 0))
    head_spec = pl.BlockSpec((nseq, N_PAIRS, lt, LANES), lambda b, t: (b, 0, t, 0))
    row = pl.BlockSpec((tm, D_MODEL), lambda b, t: (b * nt + t, 0))
    return pl.pallas_call(
        functools.partial(_rwkv_post_kernel, parts=POST_PARTS if lt == TOKEN_TILE else 1, nseq=nseq),
        grid=(B // nseq, nt),
        in_specs=[row, head_spec, row, row, vec(), vec(),
                  _resident((D_MODEL, LANES), lambda b, t: (0, 0)),
                  _resident((LANES, D_MODEL), lambda b, t: (0, 0)),
                  _resident((D_MODEL, D_MODEL), lambda b, t: (0, 0))],
        out_specs=row,
        out_shape=jax.ShapeDtypeStruct((B * L, D_MODEL), F32),
        compiler_params=_params(("parallel", "parallel")),
    )(x, y, g, bonus, p['rwkv_lnx_w'][i][None], p['rwkv_lnx_b'][i][None],
      p['head_onehot'], p['head_onehot_t'], p['rwkv_wo'])


def _pack_state(s):
    B = s.shape[0]
    s = s.reshape(B, N_PAIRS, 2, RWKV_HEAD, RWKV_HEAD)
    z = jnp.zeros_like(s[:, :, 0])
    top = jnp.concatenate([s[:, :, 0], z], axis=-1)
    bot = jnp.concatenate([z, s[:, :, 1]], axis=-1)
    return jnp.concatenate([top, bot], axis=-2)


def _unpack_state(s):
    B = s.shape[0]
    return jnp.stack([s[:, :, :RWKV_HEAD, :RWKV_HEAD], s[:, :, RWKV_HEAD:, RWKV_HEAD:]], axis=2).reshape(
        B, RWKV_H, RWKV_HEAD, RWKV_HEAD)


class _Stream:
    def __init__(self, x, pos0, caches):
        self.B, self.L, _ = x.shape
        self.x = x.reshape(self.B * self.L, D_MODEL)
        self.pos0 = pos0
        self.pool_c, self.k_c, self.v_c, self.shift_c, self.wkv_c = caches
        self.stepping = self.k_c is not None
        if self.stepping:
            self.bkt = _bucket_index(self.L, SWA_ROWS + self.L, SWA_ROWS)
        else:
            self.bkt = _bucket_index(CHUNK, WINDOW + CHUNK, WINDOW)
        self.new = {name: [] for name in ('pool', 'k', 'v', 'shift', 'wkv')}

    def mixer(self, l, p):
        B, L, i = self.B, self.L, l // 2
        gn = p['norm_mix'][l][None]
        if l % 2 == 0:
            if self.stepping:
                hist = jnp.pad(self.pool_c[i], ((0, 0), (POOL_HALO - POOL_HIST, 0), (0, 0)))
                k_cache = self.k_c[i].reshape(B, SWA_ROWS, KV_W)
                v_cache = self.v_c[i].reshape(B, SWA_ROWS, KV_W)
            else:
                hist = jnp.zeros((B, POOL_HALO, C_POOL), F32)
                k_cache = v_cache = jnp.zeros((B, WINDOW, KV_W), F32)
            self.x, u_tail, k_tail, v_tail = _even_layer(
                self.x, gn, p, i, hist, k_cache, v_cache, self.bkt, B, L, self.pos0,
                chunk=L if self.stepping else CHUNK, masked=not self.stepping)
            self.new['pool'].append(u_tail[:, -POOL_HIST:])
            self.new['k'].append(k_tail.reshape(B, SWA_ROWS, N_KV_HEADS, HEAD_DIM))
            self.new['v'].append(v_tail.reshape(B, SWA_ROWS, N_KV_HEADS, HEAD_DIM))
        else:
            if self.stepping:
                shift_prev = self.shift_c[i][:, None, :]
                s0 = _pack_state(self.wkv_c[i])
            else:
                shift_prev = jnp.zeros((B, 1, D_MODEL), F32)
                s0 = jnp.zeros((B, N_PAIRS, LANES, LANES), F32)
            r, lw, k, v, kk, b, g, bonus, hs = _rwkv_pre(self.x, shift_prev, gn, p, i, B, L)
            y, s_new = _rwkv_scan(r, lw, k, v, kk, b, s0, B, L)
            self.x = _rwkv_post(self.x, y, g, bonus, p, i, B, L)
            self.new['shift'].append(hs[:, -1])
            self.new['wkv'].append(_unpack_state(s_new))

    def results(self):
        return (self.x.reshape(self.B, self.L, D_MODEL),) + tuple(
            jnp.stack(self.new[name]) for name in ('pool', 'k', 'v', 'shift', 'wkv'))


_FFN_WEIGHTS = ('ffn_w_gate', 'ffn_w_up', 'ffn_w_down')
_EVEN_WEIGHTS = ('w_in_even', 'w_out_even')
_ODD_WEIGHTS = ('rwkv_wr', 'rwkv_wk', 'rwkv_wv', 'rwkv_w1', 'rwkv_w2', 'rwkv_a1', 'rwkv_a2', 'rwkv_g1', 'rwkv_g2',
                'rwkv_wo')


def _trunk(long, short, p):
    depth = p['norm_mix'].shape[0]
    ffn_w = tuple(p[name][0, 0].astype(BF16) for name in _FFN_WEIGHTS)
    for l in range(depth):
        for j in range(2):
            n = 2 * l + j
            jobs = [(p[name], divmod(n + 1, 2)) for name in _FFN_WEIGHTS] if n + 1 < 2 * depth else []
            mixer_names = (_EVEN_WEIGHTS if l % 2 == 0 else _ODD_WEIGHTS) if j == 0 else ()
            jobs += [(p[name], (l // 2,)) for name in mixer_names]
            final_g = p['norm_final'][None] if n == 2 * depth - 1 else None
            long.x, short.x, done = _ffn(long.x, short.x, p['norm_ffn'][l, j][None], ffn_w, final_g, jobs)
            ffn_w = done[:len(_FFN_WEIGHTS)]
            if j == 0:
                mixer_w = dict(p, **dict(zip(mixer_names, done[len(done) - len(mixer_names):])))
                long.mixer(l, mixer_w)
                short.mixer(l, mixer_w)
    return long.results(), short.results()


def _prepare(p):
    p = dict(p)
    p['pool_w'] = p['pool_w'].astype(BF16)
    onehot = np.arange(D_MODEL)[:, None] // RWKV_HEAD == np.arange(LANES)[None, :]
    p['head_onehot'] = jnp.asarray(onehot, dtype=BF16)
    p['head_onehot_t'] = jnp.asarray(onehot.T, dtype=BF16)
    return p


def kernel(x_prompt, x_sample, cache_pool, cache_swa_k, cache_swa_v, state_shift, state_wkv, t5_table, norm_ffn, ffn_w_gate, ffn_w_up, ffn_w_down, norm_mix, w_in_even, pool_w, pool_scale, attn_sinks, w_out_even, rwkv_mu, rwkv_wr, rwkv_wk, rwkv_wv, rwkv_w0, rwkv_w1, rwkv_w2, rwkv_a0, rwkv_a1, rwkv_a2, rwkv_g1, rwkv_g2, rwkv_k_k, rwkv_k_a, rwkv_r_k, rwkv_lnx_w, rwkv_lnx_b, rwkv_wo, norm_final):
    p = _prepare(dict(
        t5_table=t5_table, norm_ffn=norm_ffn, ffn_w_gate=ffn_w_gate, ffn_w_up=ffn_w_up, ffn_w_down=ffn_w_down,
        norm_mix=norm_mix, w_in_even=w_in_even, pool_w=pool_w, pool_scale=pool_scale, attn_sinks=attn_sinks,
        w_out_even=w_out_even, rwkv_mu=rwkv_mu, rwkv_wr=rwkv_wr, rwkv_wk=rwkv_wk, rwkv_wv=rwkv_wv,
        rwkv_w0=rwkv_w0, rwkv_w1=rwkv_w1, rwkv_w2=rwkv_w2, rwkv_a0=rwkv_a0, rwkv_a1=rwkv_a1, rwkv_a2=rwkv_a2,
        rwkv_g1=rwkv_g1, rwkv_g2=rwkv_g2, rwkv_k_k=rwkv_k_k, rwkv_k_a=rwkv_k_a, rwkv_r_k=rwkv_r_k,
        rwkv_lnx_w=rwkv_lnx_w, rwkv_lnx_b=rwkv_lnx_b, rwkv_wo=rwkv_wo, norm_final=norm_final))
    prompt = _Stream(x_prompt, 0, (None, None, None, None, None))
    sample = _Stream(x_sample, PAST_LEN, (cache_pool, cache_swa_k, cache_swa_v, state_shift, state_wkv))
    (y_p, pool_p, k_p, v_p, shift_p, wkv_p), (y_s, pool_s, k_s, v_s, shift_s, wkv_s) = _trunk(prompt, sample, p)
    return (y_p, y_s, pool_p, pool_s, k_p, k_s, v_p, v_s, shift_p, shift_s, wkv_p, wkv_s)
```

```python
import functools
import math

import jax
import jax.numpy as jnp
import numpy as np
from jax import lax
from jax.experimental import pallas as pl
from jax.experimental.pallas import tpu as pltpu

F32 = jnp.float32
BF16 = jnp.bfloat16

D_MODEL = 1024
D_FF = 2816
NORM_EPS = 1e-6
CHUNK = 64
POOL_WINDOWS = (2, 4, 8, 16)
C_POOL = 512
POOL_GC = 128
POOL_HIST = 15
POOL_HALO = 16
HEAD_DIM = 64
N_Q_HEADS = 8
N_KV_HEADS = 2
GQA_GROUP = 4
WINDOW = 128
SWA_ROWS = 128
Q_W = 512
KV_W = 128
IN_EVEN = C_POOL + Q_W + 2 * KV_W
N_BUCKETS = 32
MAX_DISTANCE = 128
RWKV_HEAD = 64
RWKV_H = 16
N_PAIRS = RWKV_H // 2
LNX_EPS = 64e-5
PAST_LEN = 4096

LANES = 128
F32_SUBLANES = 8
BF16_SUBLANES = 16
VMEM_LIMIT_BYTES = 56 * 1024 * 1024

FFN_TILE = 1024
FFN_F_CHUNK = 256
TOKEN_TILE = 512
POST_PARTS = 2
EVEN_CHUNKS_PER_STEP = 16
SCAN_CHUNKS_PER_STEP = (8, 4, 2, 1)


def _params(sem):
    return pltpu.CompilerParams(dimension_semantics=sem, vmem_limit_bytes=VMEM_LIMIT_BYTES)


def _dot(a, b):
    return jnp.dot(a, b, preferred_element_type=F32)


def _dot_nt(a, b):
    return lax.dot_general(a, b, (((1,), (1,)), ((), ())), preferred_element_type=F32)


def _dot_tn(a, b):
    return lax.dot_general(a, b, (((0,), (0,)), ((), ())), preferred_element_type=F32)


def _rms(x, g):
    return x * lax.rsqrt(jnp.mean(x * x, axis=-1, keepdims=True) + NORM_EPS) * g


def _split(x):
    hi = x.astype(BF16)
    lo = (x - hi.astype(F32)).astype(BF16)
    return hi, lo


def _head_sum(x, e_ref, et_ref, split=True):
    if not split:
        return _dot(_dot(x.astype(BF16), e_ref[...]).astype(BF16), et_ref[...])
    hi, lo = _split(x)
    s = _dot(hi, e_ref[...]) + _dot(lo, e_ref[...])
    shi, slo = _split(s)
    return _dot(shi, et_ref[...]) + _dot(slo, et_ref[...])


def _ffn_kernel(x_ref, xs_ref, g_ref, wg_ref, wu_ref, wd_ref, *rest, f_chunk, final, convert):
    rest = list(rest)
    gf_ref = rest.pop(0) if final else None
    src_refs = [rest.pop(0) for _ in range(convert)]
    o_ref, os_ref = rest.pop(0), rest.pop(0)
    dst_refs = [rest.pop(0) for _ in range(convert)]
    acc_ref, = rest
    for src, dst in zip(src_refs, dst_refs):
        dst[...] = src[...].astype(BF16)

    def rows(src_ref, dst_ref):
        n = src_ref.shape[0]
        x = src_ref[...]
        h = _rms(x, g_ref[...]).astype(BF16)
        for j in range(D_FF // f_chunk):
            sl = slice(j * f_chunk, (j + 1) * f_chunk)
            gate = _dot(h, wg_ref[:, sl])
            up = _dot(h, wu_ref[:, sl])
            act = (gate * jax.nn.sigmoid(gate) * up).astype(BF16)
            part = _dot(act, wd_ref[sl, :])
            if j == 0:
                acc_ref[:n] = part
            else:
                acc_ref[:n] += part
        y = x + 0.5 * acc_ref[:n]
        if final:
            y = _rms(y, gf_ref[...])
        dst_ref[...] = y

    rows(x_ref, o_ref)

    @pl.when(pl.program_id(0) == pl.num_programs(0) - 1)
    def _():
        rows(xs_ref, os_ref)


def _resident(shape, index_map):
    return pl.BlockSpec(shape, index_map, pipeline_mode=pl.Buffered(1))


def _slab_steps(rows, steps):
    return next(k for k in range(steps, 0, -1) if rows % (k * BF16_SUBLANES) == 0)


def _ffn(x, xs, g, weights, final_g=None, convert=()):
    T = x.shape[0]
    tm = min(FFN_TILE, T)
    steps = T // tm
    Ts = xs.shape[0]
    assert Ts <= tm
    final = final_g is not None
    in_specs = [pl.BlockSpec((tm, D_MODEL), lambda i: (i, 0)), _resident((Ts, D_MODEL), lambda i: (0, 0)),
                _resident((1, D_MODEL), lambda i: (0, 0))]
    in_specs += [_resident(w.shape, lambda i: (0, 0)) for w in weights]
    args = [x, xs, g, *weights]
    out_specs = [pl.BlockSpec((tm, D_MODEL), lambda i: (i, 0)), pl.BlockSpec((Ts, D_MODEL), lambda i: (0, 0))]
    out_shape = [jax.ShapeDtypeStruct((T, D_MODEL), F32), jax.ShapeDtypeStruct((Ts, D_MODEL), F32)]
    if final:
        in_specs.append(_resident((1, D_MODEL), lambda i: (0, 0)))
        args.append(final_g)
    for w, lead in convert:
        rows, cols = w.shape[len(lead):]
        k = _slab_steps(rows, steps)
        in_specs.append(pl.BlockSpec((None,) * len(lead) + (rows // k, cols),
                                     lambda i, k=k, lead=lead: (*lead, jnp.minimum(i, k - 1), 0)))
        out_specs.append(pl.BlockSpec((rows // k, cols), lambda i, k=k: (jnp.minimum(i, k - 1), 0)))
        out_shape.append(jax.ShapeDtypeStruct((rows, cols), BF16))
        args.append(w)
    outs = pl.pallas_call(
        functools.partial(_ffn_kernel, f_chunk=FFN_F_CHUNK, final=final, convert=len(convert)),
        grid=(steps,),
        in_specs=in_specs,
        out_specs=out_specs,
        out_shape=out_shape,
        scratch_shapes=[pltpu.VMEM((tm, D_MODEL), F32)],
        compiler_params=_params(("arbitrary",)),
    )(*args)
    return outs[0], outs[1], tuple(outs[2:])


def _build_bias(bkt_ref, tab_ref, bias_ref, lq):
    bkt = bkt_ref[...]
    for h in range(N_Q_HEADS):
        b = jnp.zeros(bkt.shape, F32)
        for n in range(N_BUCKETS):
            b = jnp.where(bkt == n, tab_ref[n, h], b)
        g, i = divmod(h, GQA_GROUP)
        bias_ref[g, :, i * lq:(i + 1) * lq] = b


def _group_queries(q, g):
    return jnp.concatenate([q[:, h * HEAD_DIM:(h + 1) * HEAD_DIM]
                            for h in range(g * GQA_GROUP, (g + 1) * GQA_GROUP)], axis=0) * (HEAD_DIM ** -0.5)


def _sink_rows(sink_ref, lq):
    lane = lax.broadcasted_iota(jnp.int32, (1, GQA_GROUP * lq), 1)
    rows = []
    for g in range(N_KV_HEADS):
        r = jnp.zeros((1, GQA_GROUP * lq), F32)
        for i in range(GQA_GROUP):
            r = jnp.where(lane // lq == i, sink_ref[g * GQA_GROUP + i], r)
        rows.append(r)
    return rows


def _attn_core(qs, ks, vs, biases, sinks, valids):
    ss = [_dot_nt(k, q) + b for q, k, b in zip(qs, ks, biases)]
    ss = [s if ok is None else jnp.where(ok, s, -1e30) for s, ok in zip(ss, valids)]
    ms = [jnp.maximum(jnp.max(s, axis=0, keepdims=True), sk) for s, sk in zip(ss, sinks)]
    ps = [jnp.exp(s - m) for s, m in zip(ss, ms)]
    invs = [1.0 / (jnp.sum(p, axis=0, keepdims=True) + jnp.exp(sk - m)) for p, sk, m in zip(ps, sinks, ms)]
    return [_dot_tn((p * r).astype(BF16), v) for p, r, v in zip(ps, invs, vs)]


def _smem():
    return pl.BlockSpec(memory_space=pltpu.SMEM)


def _even_layer_kernel(x_ref, gn_ref, win_ref, pw_ref, ps_ref, hist_ref, kc_ref, vc_ref, bkt_ref, tab_ref, sink_ref,
                       wop_ref, woa_ref, o_ref, utail_o, ktail_o, vtail_o, halo_ref, kprev_ref, vprev_ref, bias_ref,
                       *, rows, chunk, pos0, masked):
    t = pl.program_id(1)

    @pl.when(jnp.logical_and(pl.program_id(0) == 0, t == 0))
    def _():
        _build_bias(bkt_ref, tab_ref, bias_ref, chunk)

    @pl.when(t == 0)
    def _():
        halo_ref[...] = hist_ref[0]
        kprev_ref[...] = kc_ref[0]
        vprev_ref[...] = vc_ref[0]

    x = x_ref[...]
    z = _dot(_rms(x, gn_ref[...]).astype(BF16), win_ref[...])
    u = z[:, :C_POOL]
    q = z[:, C_POOL:C_POOL + Q_W].astype(BF16)
    k_all = jnp.concatenate([kprev_ref[...], z[:, C_POOL + Q_W:C_POOL + Q_W + KV_W]], axis=0)
    v_all = jnp.concatenate([vprev_ref[...], z[:, C_POOL + Q_W + KV_W:]], axis=0)

    ext = jnp.concatenate([halo_ref[...], u], axis=0)
    pos = pos0 + t * rows + lax.broadcasted_iota(jnp.int32, (rows, POOL_GC), 0)
    pooled = []
    for gi, w in enumerate(POOL_WINDOWS):
        sl = slice(gi * POOL_GC, (gi + 1) * POOL_GC)
        s = ext[:, sl]
        span = 1
        while span < w:
            s = s + pltpu.roll(s, span, 0)
            span *= 2
        cnt = jnp.minimum(w, pos + 1).astype(F32)
        pooled.append(_dot((s[POOL_HALO:] / cnt - u[:, sl]).astype(BF16), pw_ref[gi]))
    pool_out = (jnp.concatenate(pooled, axis=-1) * ps_ref[...]).astype(BF16)

    lk = WINDOW + chunk
    kb = k_all.astype(BF16)
    vb = v_all.astype(BF16)
    sink_rows = _sink_rows(sink_ref, chunk)
    first_pos = t * rows - WINDOW + lax.broadcasted_iota(jnp.int32, (lk, 1), 0)
    nchunk = rows // chunk
    qs, ks, vs, biases, sinks, valids = [], [], [], [], [], []
    for j in range(nchunk):
        qj = q[j * chunk:(j + 1) * chunk]
        for g in range(N_KV_HEADS):
            qs.append(_group_queries(qj, g))
            ks.append(kb[j * chunk:j * chunk + lk, g * HEAD_DIM:(g + 1) * HEAD_DIM])
            vs.append(vb[j * chunk:j * chunk + lk, g * HEAD_DIM:(g + 1) * HEAD_DIM])
            biases.append(bias_ref[g])
            sinks.append(sink_rows[g])
            valids.append(first_pos + j * chunk >= 0 if masked else None)
    outs = _attn_core(qs, ks, vs, biases, sinks, valids)
    att = jnp.concatenate(
        [jnp.concatenate([outs[j * N_KV_HEADS + g][i * chunk:(i + 1) * chunk]
                          for g in range(N_KV_HEADS) for i in range(GQA_GROUP)], axis=-1) for j in range(nchunk)],
        axis=0).astype(BF16)

    o_ref[...] = x + _dot(pool_out, wop_ref[...]) + _dot(att, woa_ref[...])

    halo_ref[...] = ext[rows:]
    kprev_ref[...] = k_all[rows:]
    vprev_ref[...] = v_all[rows:]
    utail_o[0] = ext[rows:]
    ktail_o[0] = k_all[rows:]
    vtail_o[0] = v_all[rows:]


def _even_layer(x, gn, p, i, hist, k_cache, v_cache, bkt, B, L, pos0, chunk, masked):
    rows = min(EVEN_CHUNKS_PER_STEP * chunk, L)
    nt = L // rows
    lk = WINDOW + chunk
    row = pl.BlockSpec((rows, D_MODEL), lambda b, t: (b * nt + t, 0))
    per_seq = lambda n, m: pl.BlockSpec((1, n, m), lambda b, t: (b, 0, 0))
    return pl.pallas_call(
        functools.partial(_even_layer_kernel, rows=rows, chunk=chunk, pos0=pos0, masked=masked),
        grid=(B, nt),
        in_specs=[row, _resident((1, D_MODEL), lambda b, t: (0, 0)),
                  _resident((D_MODEL, IN_EVEN), lambda b, t: (0, 0)),
                  _resident((None, len(POOL_WINDOWS), POOL_GC, POOL_GC), lambda b, t: (i, 0, 0, 0)),
                  _resident((1, C_POOL), lambda b, t: (0, 0)),
                  per_seq(POOL_HALO, C_POOL), per_seq(WINDOW, KV_W), per_seq(WINDOW, KV_W),
                  _resident((lk, chunk), lambda b, t: (0, 0)), _smem(), _smem(),
                  _resident((C_POOL, D_MODEL), lambda b, t: (0, 0)),
                  _resident((Q_W, D_MODEL), lambda b, t: (1, 0))],
        out_specs=[row, per_seq(POOL_HALO, C_POOL), per_seq(WINDOW, KV_W), per_seq(WINDOW, KV_W)],
        out_shape=[jax.ShapeDtypeStruct((B * L, D_MODEL), F32), jax.ShapeDtypeStruct((B, POOL_HALO, C_POOL), F32),
                   jax.ShapeDtypeStruct((B, WINDOW, KV_W), F32), jax.ShapeDtypeStruct((B, WINDOW, KV_W), F32)],
        scratch_shapes=[pltpu.VMEM((POOL_HALO, C_POOL), F32), pltpu.VMEM((WINDOW, KV_W), F32),
                        pltpu.VMEM((WINDOW, KV_W), F32), pltpu.VMEM((N_KV_HEADS, lk, GQA_GROUP * chunk), F32)],
        compiler_params=_params(("arbitrary", "arbitrary")),
    )(x, gn, p['w_in_even'], p['pool_w'], p['pool_scale'][i][None], hist, k_cache, v_cache, bkt,
      p['t5_table'], p['attn_sinks'][i], p['w_out_even'], p['w_out_even'])


def _t5_bucket(rel):
    half = N_BUCKETS // 2
    max_exact = half // 2
    side = jnp.where(rel > 0, half, 0)
    n = jnp.abs(rel)
    nf = jnp.maximum(n, max_exact).astype(F32)
    large = max_exact + (jnp.log(nf / max_exact) / math.log(MAX_DISTANCE / max_exact)
                         * (half - max_exact)).astype(jnp.int32)
    large = jnp.minimum(large, half - 1)
    return side + jnp.where(n < max_exact, n, large)


def _bucket_index(lq, lk, offset):
    rel = jnp.arange(lk)[:, None] - offset - jnp.arange(lq)[None, :]
    return _t5_bucket(rel).astype(jnp.int32)


def _rwkv_pre_kernel(x_ref, xp_ref, sh_ref, gn_ref, mu_ref, wr_ref, wk_ref, wv_ref, w1_ref, w2_ref,
                     a1_ref, a2_ref, g1_ref, g2_ref, w0_ref, a0_ref, kk_ref, ka_ref, rk_ref, e_ref, et_ref,
                     r_o, lw_o, k_o, v_o, kk_o, b_o, g_o, bonus_o, hs_o, *, tm, nseq):
    t = pl.program_id(1)
    lt = tm // nseq
    gn = gn_ref[...]
    h = _rms(x_ref[...], gn)
    row = lax.broadcasted_iota(jnp.int32, (tm, 1), 0)
    h_prev = pltpu.roll(h, 1, 0)
    for s in range(nseq):
        first_prev = sh_ref[s]
        if nseq == 1:
            first_prev = jnp.where(t == 0, first_prev, _rms(xp_ref[...], gn)[F32_SUBLANES - 1:])
        h_prev = jnp.where(row == s * lt, first_prev, h_prev)
    xx = h_prev - h

    hb = h.astype(BF16)
    xb = xx.astype(BF16)
    mu = mu_ref[...].astype(BF16)

    def mix(j):
        return hb + xb * mu[j:j + 1]

    zw = w0_ref[...] + _dot(jnp.tanh(_dot(mix(1), w1_ref[...])).astype(BF16), w2_ref[...])
    za = a0_ref[...] + _dot(_dot(mix(4), a1_ref[...]).astype(BF16), a2_ref[...])
    g = _dot(jax.nn.sigmoid(_dot(mix(5), g1_ref[...])).astype(BF16), g2_ref[...])
    r = _dot(mix(0), wr_ref[...])
    lw = -math.exp(-0.5) * jax.nn.sigmoid(zw)
    k = _dot(mix(2), wk_ref[...])
    a = jax.nn.sigmoid(za)
    v = _dot(mix(3), wv_ref[...])
    kk = k * kk_ref[...]
    kk = kk * lax.rsqrt(jnp.maximum(_head_sum(kk * kk, e_ref, et_ref, split=False), 1e-24))
    k = k * (1.0 + (a - 1.0) * ka_ref[...])
    b = kk * a
    for s in range(nseq):
        rows = slice(s * lt, (s + 1) * lt)
        for hp in range(N_PAIRS):
            sl = slice(hp * LANES, (hp + 1) * LANES)
            for out, val in ((r_o, r), (lw_o, lw), (k_o, k), (v_o, v), (kk_o, kk), (b_o, b)):
                out[s, hp] = val[rows, sl]
        hs_o[s] = h[(s + 1) * lt - F32_SUBLANES:(s + 1) * lt]
    g_o[...] = g.astype(BF16)
    bonus_o[...] = (_head_sum(r * k * rk_ref[...], e_ref, et_ref, split=False) * v).astype(BF16)


def _rwkv_pre(x, shift_prev, gn, p, i, B, L):
    nseq = max(1, min(B, TOKEN_TILE // L))
    assert B % nseq == 0
    tm = min(TOKEN_TILE, L) * nseq
    lt = tm // nseq
    nt = L // lt
    vec = lambda: _resident((1, D_MODEL), lambda b, t: (0, 0))
    mat = lambda r, c: _resident((r, c), lambda b, t: (0, 0))
    lora = p['rwkv_w1'].shape[-1], p['rwkv_a1'].shape[-1], p['rwkv_g1'].shape[-1]
    head_spec = pl.BlockSpec((nseq, N_PAIRS, lt, LANES), lambda b, t: (b, 0, t, 0))
    head_shape = jax.ShapeDtypeStruct((B, N_PAIRS, L, LANES), F32)
    return pl.pallas_call(
        functools.partial(_rwkv_pre_kernel, tm=tm, nseq=nseq),
        grid=(B // nseq, nt),
        in_specs=[
            pl.BlockSpec((tm, D_MODEL), lambda b, t: (b * nt + t, 0)),
            pl.BlockSpec((F32_SUBLANES, D_MODEL),
                         lambda b, t: (jnp.maximum((b * nt + t) * (tm // F32_SUBLANES) - 1, 0), 0)),
            pl.BlockSpec((nseq, 1, D_MODEL), lambda b, t: (b, 0, 0)),
            vec(),
            _resident((None, 6, D_MODEL), lambda b, t: (i, 0, 0)),
            mat(D_MODEL, D_MODEL), mat(D_MODEL, D_MODEL), mat(D_MODEL, D_MODEL),
            mat(D_MODEL, lora[0]), mat(lora[0], D_MODEL),
            mat(D_MODEL, lora[1]), mat(lora[1], D_MODEL),
            mat(D_MODEL, lora[2]), mat(lora[2], D_MODEL),
            vec(), vec(), vec(), vec(), vec(),
            _resident((D_MODEL, LANES), lambda b, t: (0, 0)),
            _resident((LANES, D_MODEL), lambda b, t: (0, 0)),
        ],
        out_specs=[head_spec] * 6 + [
            pl.BlockSpec((tm, D_MODEL), lambda b, t: (b * nt + t, 0)),
            pl.BlockSpec((tm, D_MODEL), lambda b, t: (b * nt + t, 0)),
            pl.BlockSpec((nseq, F32_SUBLANES, D_MODEL), lambda b, t: (b, 0, 0)),
        ],
        out_shape=[head_shape] * 6 + [
            jax.ShapeDtypeStruct((B * L, D_MODEL), BF16),
            jax.ShapeDtypeStruct((B * L, D_MODEL), BF16),
            jax.ShapeDtypeStruct((B, F32_SUBLANES, D_MODEL), F32),
        ],
        compiler_params=_params(("parallel", "arbitrary")),
    )(x, x, shift_prev, gn, p['rwkv_mu'], p['rwkv_wr'], p['rwkv_wk'], p['rwkv_wv'],
      p['rwkv_w1'], p['rwkv_w2'], p['rwkv_a1'], p['rwkv_a2'], p['rwkv_g1'], p['rwkv_g2'],
      p['rwkv_w0'][i][None], p['rwkv_a0'][i][None], p['rwkv_k_k'][i][None], p['rwkv_k_a'][i][None],
      p['rwkv_r_k'][i].reshape(1, D_MODEL), p['head_onehot'], p['head_onehot_t'])


DIAG = 16


def _replication_matrix(n):
    src = np.arange(n)[:, None]
    dst = np.arange(n)[None, :]
    picks = np.concatenate([(src == (dst // DIAG) * DIAG + s) for s in range(DIAG - 1)], axis=1)
    return jnp.asarray(picks, dtype=BF16)


def _unit_lower_inverses(lows, c, rep_ref):
    n = 2 * c
    ti = lax.broadcasted_iota(jnp.int32, (c, n), 0)
    li = lax.broadcasted_iota(jnp.int32, (c, n), 1)
    si = li & (c - 1)
    head1 = li >= c

    def block_diag(x):
        return jnp.concatenate([jnp.where(head1, 0.0, x), jnp.where(head1, x, 0.0)], axis=0)

    if rep_ref is None:
        invs = [jnp.where(ti == si, 1.0, 0.0) + jnp.where(jnp.logical_and(ti == si + 1, (ti & 1) == 1), low, 0.0)
                for low in lows]
        m = 2
    else:
        pt = lax.broadcasted_iota(jnp.int32, (DIAG, n), 0)
        pl_ = lax.broadcasted_iota(jnp.int32, (DIAG, n), 1)
        blk = (pl_ & (c - 1)) // DIAG
        packed = []
        for low in lows:
            d = jnp.zeros((DIAG, n), F32)
            for i in range(c // DIAG):
                d = jnp.where(blk == i, low[i * DIAG:(i + 1) * DIAG], d)
            packed.append(d)
        rep = _dot(jnp.concatenate(packed, axis=0).astype(BF16), rep_ref[...])
        sols = [jnp.where(pt == (pl_ & (DIAG - 1)), 1.0, 0.0) for _ in lows]
        for s in range(DIAG - 1):
            sols = [sol + rep[p * DIAG:(p + 1) * DIAG, s * n:(s + 1) * n] * sol[s:s + 1]
                    for p, sol in enumerate(sols)]
        invs = [jnp.concatenate([jnp.where(blk == i, sol, 0.0) for i in range(c // DIAG)], axis=0) for sol in sols]
        m = DIAG
    while m < c:
        sh = m.bit_length() - 1
        sub = jnp.logical_and(jnp.logical_and((ti >> (sh + 1)) == (si >> (sh + 1)), ((ti >> sh) & 1) == 1),
                              ((si >> sh) & 1) == 0)
        diag = [block_diag(inv).astype(BF16) for inv in invs]
        half = [_dot(inv.astype(BF16), block_diag(jnp.where(sub, low, 0.0)).astype(BF16)).astype(BF16)
                for inv, low in zip(invs, lows)]
        invs = [inv + _dot(h, d) for inv, h, d in zip(invs, half, diag)]
        m *= 2
    return invs


def _scan_chunks(seqs, states, c, rep_ref):
    n = 2 * c
    row = lax.broadcasted_iota(jnp.int32, (c, LANES), 0)
    head1 = lax.broadcasted_iota(jnp.int32, (c, LANES), 1) >= RWKV_HEAD
    ti = lax.broadcasted_iota(jnp.int32, (c, n), 0)
    si = lax.broadcasted_iota(jnp.int32, (c, n), 1) & (c - 1)
    strict = ti > si
    incl = ti >= si
    pi = lax.broadcasted_iota(jnp.int32, (LANES, LANES), 0)
    pj = lax.broadcasted_iota(jnp.int32, (LANES, LANES), 1)
    same_head = (pi >= RWKV_HEAD) == (pj >= RWKV_HEAD)

    def stack(x):
        return jnp.concatenate([jnp.where(head1, 0.0, x), jnp.where(head1, x, 0.0)], axis=0)

    lhs, rhs, v_st, v_bf, bk_end, decay = [], [], [], [], [], []
    for r, lw, k, v, kk, b in seqs:
        cum = lw
        sh = 1
        while sh < c:
            cum = cum + jnp.where(row >= sh, pltpu.roll(cum, sh, 0), 0.0)
            sh *= 2
        tot = cum[c - 1:c]
        grow = jnp.exp(-cum)
        tail = jnp.exp(tot - cum)
        a_t = -kk * jnp.exp(cum - lw)
        r_t = r * jnp.exp(cum)
        lhs.append(jnp.concatenate([a_t, r_t], axis=0).astype(BF16))
        rhs.append(jnp.concatenate([stack(b * grow), stack(k * grow)], axis=0).astype(BF16))
        v_st.append(stack(v).astype(BF16))
        v_bf.append(v.astype(BF16))
        bk_end.append(jnp.concatenate([b * tail, k * tail], axis=0).astype(BF16))
        decay.append(jnp.exp(tot))
    grams = [_dot_nt(x, y) for x, y in zip(lhs, rhs)]
    lows = [jnp.where(strict, g[:c, :n], 0.0) for g in grams]
    m_rb = [jnp.where(incl, g[c:, :n], 0.0).astype(BF16) for g in grams]
    m_v = [jnp.concatenate([jnp.where(strict, g[:c, n:], 0.0), jnp.where(incl, g[c:, n:], 0.0)], axis=0).astype(BF16)
           for g in grams]
    invs = [inv.astype(BF16) for inv in _unit_lower_inverses(lows, c, rep_ref)]
    from_v = [_dot(m, vs) for m, vs in zip(m_v, v_st)]

    ys = []
    states = list(states)
    npair = len(states)
    for j in range(len(seqs) // npair):
        sl = slice(j * npair, (j + 1) * npair)
        from_state = [_dot_nt(x, s.astype(BF16)) for x, s in zip(lhs[sl], states)]
        rhs_sa = [stack(f[:c] + fv[:c]).astype(BF16) for f, fv in zip(from_state, from_v[sl])]
        sas = [_dot(inv, x) for inv, x in zip(invs[sl], rhs_sa)]
        ys += [f[c:] + fv[c:] + _dot(m, stack(sa).astype(BF16))
               for f, fv, m, sa in zip(from_state, from_v[sl], m_rb[sl], sas)]
        upd = [_dot_tn(jnp.concatenate([sa.astype(BF16), vb], axis=0), be)
               for sa, vb, be in zip(sas, v_bf[sl], bk_end[sl])]
        states = [s * d + jnp.where(same_head, u, 0.0) for s, d, u in zip(states, decay[sl], upd)]
    return ys, states


def _rwkv_scan_kernel(r_ref, lw_ref, k_ref, v_ref, kk_ref, b_ref, s0_ref, *rest, c, sub, nseq, vpu_diag):
    if vpu_diag:
        rep_ref, y_ref, so_ref, s_ref = rest
    else:
        rep_ref = None
        y_ref, so_ref, s_ref = rest
    ci = pl.program_id(1)
    chains = [(s, hp) for s in range(nseq) for hp in range(N_PAIRS)]

    @pl.when(ci == 0)
    def _():
        s_ref[...] = s0_ref[...]

    seqs = [tuple(ref[s, hp, j * c:(j + 1) * c] for ref in (r_ref, lw_ref, k_ref, v_ref, kk_ref, b_ref))
            for j in range(sub) for s, hp in chains]
    ys, new_states = _scan_chunks(seqs, [s_ref[s, hp] for s, hp in chains], c, rep_ref)
    for j in range(sub):
        for n, (s, hp) in enumerate(chains):
            y_ref[s, hp, j * c:(j + 1) * c] = ys[j * len(chains) + n]
    for n, (s, hp) in enumerate(chains):
        s_ref[s, hp] = new_states[n]

    @pl.when(ci == pl.num_programs(1) - 1)
    def _():
        so_ref[...] = s_ref[...]


def _rwkv_scan(r, lw, k, v, kk, b, s0, B, L):
    c = min(CHUNK, L)
    nc = L // c
    sub = next(s for s in SCAN_CHUNKS_PER_STEP if nc % s == 0)
    nc //= sub
    nseq = next(s for s in SCAN_CHUNKS_PER_STEP if B % s == 0) if nc * sub == 1 else 1
    seq = pl.BlockSpec((nseq, N_PAIRS, sub * c, LANES), lambda bi, ci: (bi, 0, ci, 0))
    st = pl.BlockSpec((nseq, N_PAIRS, LANES, LANES), lambda bi, ci: (bi, 0, 0, 0))
    vpu_diag = 2 * c == LANES
    extra_specs, extra_args = [], []
    if vpu_diag:
        extra_specs = [_resident((LANES, (DIAG - 1) * LANES), lambda bi, ci: (0, 0))]
        extra_args = [_replication_matrix(LANES)]
    return pl.pallas_call(
        functools.partial(_rwkv_scan_kernel, c=c, sub=sub, nseq=nseq, vpu_diag=vpu_diag),
        grid=(B // nseq, nc),
        in_specs=[seq] * 6 + [st] + extra_specs,
        out_specs=[seq, st],
        out_shape=[jax.ShapeDtypeStruct((B, N_PAIRS, L, LANES), F32),
                   jax.ShapeDtypeStruct((B, N_PAIRS, LANES, LANES), F32)],
        scratch_shapes=[pltpu.VMEM((nseq, N_PAIRS, LANES, LANES), F32)],
        compiler_params=_params(("parallel", "arbitrary")),
    )(r, lw, k, v, kk, b, s0, *extra_args)


def _rwkv_post_kernel(x_ref, y_ref, g_ref, bonus_ref, lw_ref, lb_ref, e_ref, et_ref, wo_ref, o_ref, *, parts, nseq):
    rows = x_ref.shape[0] // parts
    sls = [slice(i * rows, (i + 1) * rows) for i in range(parts)]
    head_sum = lambda t, split: _head_sum(t, e_ref, et_ref, split)
    if nseq == 1:
        ys = [jnp.concatenate([y_ref[0, hp, sl] for hp in range(N_PAIRS)], axis=-1) for sl in sls]
    else:
        ys = [jnp.concatenate([jnp.concatenate([y_ref[s, hp] for hp in range(N_PAIRS)], axis=-1)
                               for s in range(nseq)], axis=0)]
    ds = [y - head_sum(y, True) * (1.0 / RWKV_HEAD) for y in ys]
    var = [head_sum(d * d, False) * (1.0 / RWKV_HEAD) for d in ds]
    yn = [d * lax.rsqrt(vr + LNX_EPS) * lw_ref[...] + lb_ref[...] for d, vr in zip(ds, var)]
    gated = [((n + bonus_ref[sl, :].astype(F32)) * g_ref[sl, :].astype(F32)).astype(BF16) for n, sl in zip(yn, sls)]
    for gt, sl in zip(gated, sls):
        o_ref[sl, :] = x_ref[sl, :] + _dot(gt, wo_ref[...])


def _rwkv_post(x, y, g, bonus, p, i, B, L):
    nseq = max(1, min(B, TOKEN_TILE // L))
    assert B % nseq == 0
    lt = min(TOKEN_TILE, L)
    tm = lt * nseq
    nt = L // lt
    vec = lambda: _resident((1, D_MODEL), lambda b, t: (0, 0))
    head_spec = pl.BlockSpec((nseq, N_PAIRS, lt, LANES), lambda b, t: (b, 0, t, 0))
    row = pl.BlockSpec((tm, D_MODEL), lambda b, t: (b * nt + t, 0))
    return pl.pallas_call(
        functools.partial(_rwkv_post_kernel, parts=POST_PARTS if lt == TOKEN_TILE else 1, nseq=nseq),
        grid=(B // nseq, nt),
        in_specs=[row, head_spec, row, row, vec(), vec(),
                  _resident((D_MODEL, LANES), lambda b, t: (0, 0)),
                  _resident((LANES, D_MODEL), lambda b, t: (0, 0)),
                  _resident((D_MODEL, D_MODEL), lambda b, t: (0, 0))],
        out_specs=row,
        out_shape=jax.ShapeDtypeStruct((B * L, D_MODEL), F32),
        compiler_params=_params(("parallel", "parallel")),
    )(x, y, g, bonus, p['rwkv_lnx_w'][i][None], p['rwkv_lnx_b'][i][None],
      p['head_onehot'], p['head_onehot_t'], p['rwkv_wo'])


def _pack_state(s):
    B = s.shape[0]
    s = s.reshape(B, N_PAIRS, 2, RWKV_HEAD, RWKV_HEAD)
    z = jnp.zeros_like(s[:, :, 0])
    top = jnp.concatenate([s[:, :, 0], z], axis=-1)
    bot = jnp.concatenate([z, s[:, :, 1]], axis=-1)
    return jnp.concatenate([top, bot], axis=-2)


def _unpack_state(s):
    B = s.shape[0]
    return jnp.stack([s[:, :, :RWKV_HEAD, :RWKV_HEAD], s[:, :, RWKV_HEAD:, RWKV_HEAD:]], axis=2).reshape(
        B, RWKV_H, RWKV_HEAD, RWKV_HEAD)


class _Stream:
    def __init__(self, x, pos0, caches):
        self.B, self.L, _ = x.shape
        self.x = x.reshape(self.B * self.L, D_MODEL)
        self.pos0 = pos0
        self.pool_c, self.k_c, self.v_c, self.shift_c, self.wkv_c = caches
        self.stepping = self.k_c is not None
        if self.stepping:
            self.bkt = _bucket_index(self.L, SWA_ROWS + self.L, SWA_ROWS)
        else:
            self.bkt = _bucket_index(CHUNK, WINDOW + CHUNK, WINDOW)
        self.new = {name: [] for name in ('pool', 'k', 'v', 'shift', 'wkv')}

    def mixer(self, l, p):
        B, L, i = self.B, self.L, l // 2
        gn = p['norm_mix'][l][None]
        if l % 2 == 0:
            if self.stepping:
                hist = jnp.pad(self.pool_c[i], ((0, 0), (POOL_HALO - POOL_HIST, 0), (0, 0)))
                k_cache = self.k_c[i].reshape(B, SWA_ROWS, KV_W)
                v_cache = self.v_c[i].reshape(B, SWA_ROWS, KV_W)
            else:
                hist = jnp.zeros((B, POOL_HALO, C_POOL), F32)
                k_cache = v_cache = jnp.zeros((B, WINDOW, KV_W), F32)
            self.x, u_tail, k_tail, v_tail = _even_layer(
                self.x, gn, p, i, hist, k_cache, v_cache, self.bkt, B, L, self.pos0,
                chunk=L if self.stepping else CHUNK, masked=not self.stepping)
            self.new['pool'].append(u_tail[:, -POOL_HIST:])
            self.new['k'].append(k_tail.reshape(B, SWA_ROWS, N_KV_HEADS, HEAD_DIM))
            self.new['v'].append(v_tail.reshape(B, SWA_ROWS, N_KV_HEADS, HEAD_DIM))
        else:
            if self.stepping:
                shift_prev = self.shift_c[i][:, None, :]
                s0 = _pack_state(self.wkv_c[i])
            else:
                shift_prev = jnp.zeros((B, 1, D_MODEL), F32)
                s0 = jnp.zeros((B, N_PAIRS, LANES, LANES), F32)
            r, lw, k, v, kk, b, g, bonus, hs = _rwkv_pre(self.x, shift_prev, gn, p, i, B, L)
            y, s_new = _rwkv_scan(r, lw, k, v, kk, b, s0, B, L)
            self.x = _rwkv_post(self.x, y, g, bonus, p, i, B, L)
            self.new['shift'].append(hs[:, -1])
            self.new['wkv'].append(_unpack_state(s_new))

    def results(self):
        return (self.x.reshape(self.B, self.L, D_MODEL),) + tuple(
            jnp.stack(self.new[name]) for name in ('pool', 'k', 'v', 'shift', 'wkv'))


_FFN_WEIGHTS = ('ffn_w_gate', 'ffn_w_up', 'ffn_w_down')
_EVEN_WEIGHTS = ('w_in_even', 'w_out_even')
_ODD_WEIGHTS = ('rwkv_wr', 'rwkv_wk', 'rwkv_wv', 'rwkv_w1', 'rwkv_w2', 'rwkv_a1', 'rwkv_a2', 'rwkv_g1', 'rwkv_g2',
                'rwkv_wo')


def _trunk(long, short, p):
    depth = p['norm_mix'].shape[0]
    ffn_w = tuple(p[name][0, 0].astype(BF16) for name in _FFN_WEIGHTS)
    for l in range(depth):
        for j in range(2):
            n = 2 * l + j
            jobs = [(p[name], divmod(n + 1, 2)) for name in _FFN_WEIGHTS] if n + 1 < 2 * depth else []
            mixer_names = (_EVEN_WEIGHTS if l % 2 == 0 else _ODD_WEIGHTS) if j == 0 else ()
            jobs += [(p[name], (l // 2,)) for name in mixer_names]
            final_g = p['norm_final'][None] if n == 2 * depth - 1 else None
            long.x, short.x, done = _ffn(long.x, short.x, p['norm_ffn'][l, j][None], ffn_w, final_g, jobs)
            ffn_w = done[:len(_FFN_WEIGHTS)]
            if j == 0:
                mixer_w = dict(p, **dict(zip(mixer_names, done[len(done) - len(mixer_names):])))
                long.mixer(l, mixer_w)
                short.mixer(l, mixer_w)
    return long.results(), short.results()


def _prepare(p):
    p = dict(p)
    p['pool_w'] = p['pool_w'].astype(BF16)
    onehot = np.arange(D_MODEL)[:, None] // RWKV_HEAD == np.arange(LANES)[None, :]
    p['head_onehot'] = jnp.asarray(onehot, dtype=BF16)
    p['head_onehot_t'] = jnp.asarray(onehot.T, dtype=BF16)
    return p


def kernel(x_prompt, x_sample, cache_pool, cache_swa_k, cache_swa_v, state_shift, state_wkv, t5_table, norm_ffn, ffn_w_gate, ffn_w_up, ffn_w_down, norm_mix, w_in_even, pool_w, pool_scale, attn_sinks, w_out_even, rwkv_mu, rwkv_wr, rwkv_wk, rwkv_wv, rwkv_w0, rwkv_w1, rwkv_w2, rwkv_a0, rwkv_a1, rwkv_a2, rwkv_g1, rwkv_g2, rwkv_k_k, rwkv_k_a, rwkv_r_k, rwkv_lnx_w, rwkv_lnx_b, rwkv_wo, norm_final):
    p = _prepare(dict(
        t5_table=t5_table, norm_ffn=norm_ffn, ffn_w_gate=ffn_w_gate, ffn_w_up=ffn_w_up, ffn_w_down=ffn_w_down,
        norm_mix=norm_mix, w_in_even=w_in_even, pool_w=pool_w, pool_scale=pool_scale, attn_sinks=attn_sinks,
        w_out_even=w_out_even, rwkv_mu=rwkv_mu, rwkv_wr=rwkv_wr, rwkv_wk=rwkv_wk, rwkv_wv=rwkv_wv,
        rwkv_w0=rwkv_w0, rwkv_w1=rwkv_w1, rwkv_w2=rwkv_w2, rwkv_a0=rwkv_a0, rwkv_a1=rwkv_a1, rwkv_a2=rwkv_a2,
        rwkv_g1=rwkv_g1, rwkv_g2=rwkv_g2, rwkv_k_k=rwkv_k_k, rwkv_k_a=rwkv_k_a, rwkv_r_k=rwkv_r_k,
        rwkv_lnx_w=rwkv_lnx_w, rwkv_lnx_b=rwkv_lnx_b, rwkv_wo=rwkv_wo, norm_final=norm_final))
    prompt = _Stream(x_prompt, 0, (None, None, None, None, None))
    sample = _Stream(x_sample, PAST_LEN, (cache_pool, cache_swa_k, cache_swa_v, state_shift, state_wkv))
    (y_p, pool_p, k_p, v_p, shift_p, wkv_p), (y_s, pool_s, k_s, v_s, shift_s, wkv_s) = _trunk(prompt, sample, p)
    return (y_p, y_s, pool_p, pool_s, k_p, k_s, v_p, v_s, shift_p, shift_s, wkv_p, wkv_s)
```

```python
import functools
import math

import jax
import jax.numpy as jnp
import numpy as np
from jax import lax
from jax.experimental import pallas as pl
from jax.experimental.pallas import tpu as pltpu

F32 = jnp.float32
BF16 = jnp.bfloat16

D_MODEL = 1024
D_FF = 2816
NORM_EPS = 1e-6
CHUNK = 64
POOL_WINDOWS = (2, 4, 8, 16)
C_POOL = 512
POOL_GC = 128
POOL_HIST = 15
POOL_HALO = 16
HEAD_DIM = 64
N_Q_HEADS = 8
N_KV_HEADS = 2
GQA_GROUP = 4
WINDOW = 128
SWA_ROWS = 128
Q_W = 512
KV_W = 128
IN_EVEN = C_POOL + Q_W + 2 * KV_W
N_BUCKETS = 32
MAX_DISTANCE = 128
RWKV_HEAD = 64
RWKV_H = 16
N_PAIRS = RWKV_H // 2
LNX_EPS = 64e-5
PAST_LEN = 4096

LANES = 128
F32_SUBLANES = 8
BF16_SUBLANES = 16
VMEM_LIMIT_BYTES = 56 * 1024 * 1024

FFN_TILE = 1024
FFN_F_CHUNK = 256
TOKEN_TILE = 512
POST_PARTS = 2
EVEN_CHUNKS_PER_STEP = 16
SCAN_CHUNKS_PER_STEP = (4, 2, 1)


def _params(sem):
    return pltpu.CompilerParams(dimension_semantics=sem, vmem_limit_bytes=VMEM_LIMIT_BYTES)


def _dot(a, b):
    return jnp.dot(a, b, preferred_element_type=F32)


def _dot_nt(a, b):
    return lax.dot_general(a, b, (((1,), (1,)), ((), ())), preferred_element_type=F32)


def _dot_tn(a, b):
    return lax.dot_general(a, b, (((0,), (0,)), ((), ())), preferred_element_type=F32)


def _rms(x, g):
    return x * lax.rsqrt(jnp.mean(x * x, axis=-1, keepdims=True) + NORM_EPS) * g


def _split(x):
    hi = x.astype(BF16)
    lo = (x - hi.astype(F32)).astype(BF16)
    return hi, lo


def _head_sum(x, e_ref, et_ref, split=True):
    if not split:
        return _dot(_dot(x.astype(BF16), e_ref[...]).astype(BF16), et_ref[...])
    hi, lo = _split(x)
    s = _dot(hi, e_ref[...]) + _dot(lo, e_ref[...])
    shi, slo = _split(s)
    return _dot(shi, et_ref[...]) + _dot(slo, et_ref[...])


def _ffn_kernel(x_ref, xs_ref, g_ref, wg_ref, wu_ref, wd_ref, *rest, f_chunk, final, convert):
    rest = list(rest)
    gf_ref = rest.pop(0) if final else None
    src_refs = [rest.pop(0) for _ in range(convert)]
    o_ref, os_ref = rest.pop(0), rest.pop(0)
    dst_refs = [rest.pop(0) for _ in range(convert)]
    acc_ref, = rest
    for src, dst in zip(src_refs, dst_refs):
        dst[...] = src[...].astype(BF16)

    def rows(src_ref, dst_ref):
        n = src_ref.shape[0]
        x = src_ref[...]
        h = _rms(x, g_ref[...]).astype(BF16)
        for j in range(D_FF // f_chunk):
            sl = slice(j * f_chunk, (j + 1) * f_chunk)
            gate = _dot(h, wg_ref[:, sl])
            up = _dot(h, wu_ref[:, sl])
            act = (gate * jax.nn.sigmoid(gate) * up).astype(BF16)
            part = _dot(act, wd_ref[sl, :])
            if j == 0:
                acc_ref[:n] = part
            else:
                acc_ref[:n] += part
        y = x + 0.5 * acc_ref[:n]
        if final:
            y = _rms(y, gf_ref[...])
        dst_ref[...] = y

    rows(x_ref, o_ref)

    @pl.when(pl.program_id(0) == pl.num_programs(0) - 1)
    def _():
        rows(xs_ref, os_ref)


def _resident(shape, index_map):
    return pl.BlockSpec(shape, index_map, pipeline_mode=pl.Buffered(1))


def _slab_steps(rows, steps):
    return next(k for k in range(steps, 0, -1) if rows % (k * BF16_SUBLANES) == 0)


def _ffn(x, xs, g, weights, final_g=None, convert=()):
    T = x.shape[0]
    tm = min(FFN_TILE, T)
    steps = T // tm
    Ts = xs.shape[0]
    assert Ts <= tm
    final = final_g is not None
    in_specs = [pl.BlockSpec((tm, D_MODEL), lambda i: (i, 0)), _resident((Ts, D_MODEL), lambda i: (0, 0)),
                _resident((1, D_MODEL), lambda i: (0, 0))]
    in_specs += [_resident(w.shape, lambda i: (0, 0)) for w in weights]
    args = [x, xs, g, *weights]
    out_specs = [pl.BlockSpec((tm, D_MODEL), lambda i: (i, 0)), pl.BlockSpec((Ts, D_MODEL), lambda i: (0, 0))]
    out_shape = [jax.ShapeDtypeStruct((T, D_MODEL), F32), jax.ShapeDtypeStruct((Ts, D_MODEL), F32)]
    if final:
        in_specs.append(_resident((1, D_MODEL), lambda i: (0, 0)))
        args.append(final_g)
    for w, lead in convert:
        rows, cols = w.shape[len(lead):]
        k = _slab_steps(rows, steps)
        in_specs.append(pl.BlockSpec((None,) * len(lead) + (rows // k, cols),
                                     lambda i, k=k, lead=lead: (*lead, jnp.minimum(i, k - 1), 0)))
        out_specs.append(pl.BlockSpec((rows // k, cols), lambda i, k=k: (jnp.minimum(i, k - 1), 0)))
        out_shape.append(jax.ShapeDtypeStruct((rows, cols), BF16))
        args.append(w)
    outs = pl.pallas_call(
        functools.partial(_ffn_kernel, f_chunk=FFN_F_CHUNK, final=final, convert=len(convert)),
        grid=(steps,),
        in_specs=in_specs,
        out_specs=out_specs,
        out_shape=out_shape,
        scratch_shapes=[pltpu.VMEM((tm, D_MODEL), F32)],
        compiler_params=_params(("arbitrary",)),
    )(*args)
    return outs[0], outs[1], tuple(outs[2:])


def _build_bias(bkt_ref, tab_ref, bias_ref, lq):
    bkt = bkt_ref[...]
    for h in range(N_Q_HEADS):
        b = jnp.zeros(bkt.shape, F32)
        for n in range(N_BUCKETS):
            b = jnp.where(bkt == n, tab_ref[n, h], b)
        g, i = divmod(h, GQA_GROUP)
        bias_ref[g, :, i * lq:(i + 1) * lq] = b


def _group_queries(q, g):
    return jnp.concatenate([q[:, h * HEAD_DIM:(h + 1) * HEAD_DIM]
                            for h in range(g * GQA_GROUP, (g + 1) * GQA_GROUP)], axis=0) * (HEAD_DIM ** -0.5)


def _sink_rows(sink_ref, lq):
    lane = lax.broadcasted_iota(jnp.int32, (1, GQA_GROUP * lq), 1)
    rows = []
    for g in range(N_KV_HEADS):
        r = jnp.zeros((1, GQA_GROUP * lq), F32)
        for i in range(GQA_GROUP):
            r = jnp.where(lane // lq == i, sink_ref[g * GQA_GROUP + i], r)
        rows.append(r)
    return rows


def _attn_core(qs, ks, vs, biases, sinks, valids):
    ss = [_dot_nt(k, q) + b for q, k, b in zip(qs, ks, biases)]
    ss = [s if ok is None else jnp.where(ok, s, -1e30) for s, ok in zip(ss, valids)]
    ms = [jnp.maximum(jnp.max(s, axis=0, keepdims=True), sk) for s, sk in zip(ss, sinks)]
    ps = [jnp.exp(s - m) for s, m in zip(ss, ms)]
    invs = [1.0 / (jnp.sum(p, axis=0, keepdims=True) + jnp.exp(sk - m)) for p, sk, m in zip(ps, sinks, ms)]
    return [_dot_tn((p * r).astype(BF16), v) for p, r, v in zip(ps, invs, vs)]


def _smem():
    return pl.BlockSpec(memory_space=pltpu.SMEM)


def _even_layer_kernel(x_ref, gn_ref, win_ref, pw_ref, ps_ref, hist_ref, kc_ref, vc_ref, bkt_ref, tab_ref, sink_ref,
                       wop_ref, woa_ref, o_ref, utail_o, ktail_o, vtail_o, halo_ref, kprev_ref, vprev_ref, bias_ref,
                       *, rows, chunk, pos0, masked):
    t = pl.program_id(1)

    @pl.when(jnp.logical_and(pl.program_id(0) == 0, t == 0))
    def _():
        _build_bias(bkt_ref, tab_ref, bias_ref, chunk)

    @pl.when(t == 0)
    def _():
        halo_ref[...] = hist_ref[0]
        kprev_ref[...] = kc_ref[0]
        vprev_ref[...] = vc_ref[0]

    x = x_ref[...]
    z = _dot(_rms(x, gn_ref[...]).astype(BF16), win_ref[...])
    u = z[:, :C_POOL]
    q = z[:, C_POOL:C_POOL + Q_W].astype(BF16)
    k_all = jnp.concatenate([kprev_ref[...], z[:, C_POOL + Q_W:C_POOL + Q_W + KV_W]], axis=0)
    v_all = jnp.concatenate([vprev_ref[...], z[:, C_POOL + Q_W + KV_W:]], axis=0)

    ext = jnp.concatenate([halo_ref[...], u], axis=0)
    pos = pos0 + t * rows + lax.broadcasted_iota(jnp.int32, (rows, POOL_GC), 0)
    pooled = []
    for gi, w in enumerate(POOL_WINDOWS):
        sl = slice(gi * POOL_GC, (gi + 1) * POOL_GC)
        s = ext[:, sl]
        span = 1
        while span < w:
            s = s + pltpu.roll(s, span, 0)
            span *= 2
        cnt = jnp.minimum(w, pos + 1).astype(F32)
        pooled.append(_dot((s[POOL_HALO:] / cnt - u[:, sl]).astype(BF16), pw_ref[gi]))
    pool_out = (jnp.concatenate(pooled, axis=-1) * ps_ref[...]).astype(BF16)

    lk = WINDOW + chunk
    kb = k_all.astype(BF16)
    vb = v_all.astype(BF16)
    sink_rows = _sink_rows(sink_ref, chunk)
    first_pos = t * rows - WINDOW + lax.broadcasted_iota(jnp.int32, (lk, 1), 0)
    nchunk = rows // chunk
    qs, ks, vs, biases, sinks, valids = [], [], [], [], [], []
    for j in range(nchunk):
        qj = q[j * chunk:(j + 1) * chunk]
        for g in range(N_KV_HEADS):
            qs.append(_group_queries(qj, g))
            ks.append(kb[j * chunk:j * chunk + lk, g * HEAD_DIM:(g + 1) * HEAD_DIM])
            vs.append(vb[j * chunk:j * chunk + lk, g * HEAD_DIM:(g + 1) * HEAD_DIM])
            biases.append(bias_ref[g])
            sinks.append(sink_rows[g])
            valids.append(first_pos + j * chunk >= 0 if masked else None)
    outs = _attn_core(qs, ks, vs, biases, sinks, valids)
    att = jnp.concatenate(
        [jnp.concatenate([outs[j * N_KV_HEADS + g][i * chunk:(i + 1) * chunk]
                          for g in range(N_KV_HEADS) for i in range(GQA_GROUP)], axis=-1) for j in range(nchunk)],
        axis=0).astype(BF16)

    o_ref[...] = x + _dot(pool_out, wop_ref[...]) + _dot(att, woa_ref[...])

    halo_ref[...] = ext[rows:]
    kprev_ref[...] = k_all[rows:]
    vprev_ref[...] = v_all[rows:]
    utail_o[0] = ext[rows:]
    ktail_o[0] = k_all[rows:]
    vtail_o[0] = v_all[rows:]


def _even_layer(x, gn, p, i, hist, k_cache, v_cache, bkt, B, L, pos0, chunk, masked):
    rows = min(EVEN_CHUNKS_PER_STEP * chunk, L)
    nt = L // rows
    lk = WINDOW + chunk
    row = pl.BlockSpec((rows, D_MODEL), lambda b, t: (b * nt + t, 0))
    per_seq = lambda n, m: pl.BlockSpec((1, n, m), lambda b, t: (b, 0, 0))
    return pl.pallas_call(
        functools.partial(_even_layer_kernel, rows=rows, chunk=chunk, pos0=pos0, masked=masked),
        grid=(B, nt),
        in_specs=[row, _resident((1, D_MODEL), lambda b, t: (0, 0)),
                  _resident((D_MODEL, IN_EVEN), lambda b, t: (0, 0)),
                  _resident((None, len(POOL_WINDOWS), POOL_GC, POOL_GC), lambda b, t: (i, 0, 0, 0)),
                  _resident((1, C_POOL), lambda b, t: (0, 0)),
                  per_seq(POOL_HALO, C_POOL), per_seq(WINDOW, KV_W), per_seq(WINDOW, KV_W),
                  _resident((lk, chunk), lambda b, t: (0, 0)), _smem(), _smem(),
                  _resident((C_POOL, D_MODEL), lambda b, t: (0, 0)),
                  _resident((Q_W, D_MODEL), lambda b, t: (1, 0))],
        out_specs=[row, per_seq(POOL_HALO, C_POOL), per_seq(WINDOW, KV_W), per_seq(WINDOW, KV_W)],
        out_shape=[jax.ShapeDtypeStruct((B * L, D_MODEL), F32), jax.ShapeDtypeStruct((B, POOL_HALO, C_POOL), F32),
                   jax.ShapeDtypeStruct((B, WINDOW, KV_W), F32), jax.ShapeDtypeStruct((B, WINDOW, KV_W), F32)],
        scratch_shapes=[pltpu.VMEM((POOL_HALO, C_POOL), F32), pltpu.VMEM((WINDOW, KV_W), F32),
                        pltpu.VMEM((WINDOW, KV_W), F32), pltpu.VMEM((N_KV_HEADS, lk, GQA_GROUP * chunk), F32)],
        compiler_params=_params(("arbitrary", "arbitrary")),
    )(x, gn, p['w_in_even'], p['pool_w'], p['pool_scale'][i][None], hist, k_cache, v_cache, bkt,
      p['t5_table'], p['attn_sinks'][i], p['w_out_even'], p['w_out_even'])


def _t5_bucket(rel):
    half = N_BUCKETS // 2
    max_exact = half // 2
    side = jnp.where(rel > 0, half, 0)
    n = jnp.abs(rel)
    nf = jnp.maximum(n, max_exact).astype(F32)
    large = max_exact + (jnp.log(nf / max_exact) / math.log(MAX_DISTANCE / max_exact)
                         * (half - max_exact)).astype(jnp.int32)
    large = jnp.minimum(large, half - 1)
    return side + jnp.where(n < max_exact, n, large)


def _bucket_index(lq, lk, offset):
    rel = jnp.arange(lk)[:, None] - offset - jnp.arange(lq)[None, :]
    return _t5_bucket(rel).astype(jnp.int32)


def _rwkv_pre_kernel(x_ref, xp_ref, sh_ref, gn_ref, mu_ref, wr_ref, wk_ref, wv_ref, w1_ref, w2_ref,
                     a1_ref, a2_ref, g1_ref, g2_ref, w0_ref, a0_ref, kk_ref, ka_ref, rk_ref, e_ref, et_ref,
                     r_o, lw_o, k_o, v_o, kk_o, b_o, g_o, bonus_o, hs_o, *, tm, nseq):
    t = pl.program_id(1)
    lt = tm // nseq
    gn = gn_ref[...]
    h = _rms(x_ref[...], gn)
    row = lax.broadcasted_iota(jnp.int32, (tm, 1), 0)
    h_prev = pltpu.roll(h, 1, 0)
    for s in range(nseq):
        first_prev = sh_ref[s]
        if nseq == 1:
            first_prev = jnp.where(t == 0, first_prev, _rms(xp_ref[...], gn)[F32_SUBLANES - 1:])
        h_prev = jnp.where(row == s * lt, first_prev, h_prev)
    xx = h_prev - h

    hb = h.astype(BF16)
    xb = xx.astype(BF16)
    mu = mu_ref[...].astype(BF16)

    def mix(j):
        return hb + xb * mu[j:j + 1]

    zw = w0_ref[...] + _dot(jnp.tanh(_dot(mix(1), w1_ref[...])).astype(BF16), w2_ref[...])
    za = a0_ref[...] + _dot(_dot(mix(4), a1_ref[...]).astype(BF16), a2_ref[...])
    g = _dot(jax.nn.sigmoid(_dot(mix(5), g1_ref[...])).astype(BF16), g2_ref[...])
    r = _dot(mix(0), wr_ref[...])
    lw = -math.exp(-0.5) * jax.nn.sigmoid(zw)
    k = _dot(mix(2), wk_ref[...])
    a = jax.nn.sigmoid(za)
    v = _dot(mix(3), wv_ref[...])
    kk = k * kk_ref[...]
    kk = kk * lax.rsqrt(jnp.maximum(_head_sum(kk * kk, e_ref, et_ref, split=False), 1e-24))
    k = k * (1.0 + (a - 1.0) * ka_ref[...])
    b = kk * a
    for s in range(nseq):
        rows = slice(s * lt, (s + 1) * lt)
        for hp in range(N_PAIRS):
            sl = slice(hp * LANES, (hp + 1) * LANES)
            for out, val in ((r_o, r), (lw_o, lw), (k_o, k), (v_o, v), (kk_o, kk), (b_o, b)):
                out[s, hp] = val[rows, sl]
        hs_o[s] = h[(s + 1) * lt - F32_SUBLANES:(s + 1) * lt]
    g_o[...] = g.astype(BF16)
    bonus_o[...] = (_head_sum(r * k * rk_ref[...], e_ref, et_ref, split=False) * v).astype(BF16)


def _rwkv_pre(x, shift_prev, gn, p, i, B, L):
    nseq = max(1, min(B, TOKEN_TILE // L))
    assert B % nseq == 0
    tm = min(TOKEN_TILE, L) * nseq
    lt = tm // nseq
    nt = L // lt
    vec = lambda: _resident((1, D_MODEL), lambda b, t: (0, 0))
    mat = lambda r, c: _resident((r, c), lambda b, t: (0, 0))
    lora = p['rwkv_w1'].shape[-1], p['rwkv_a1'].shape[-1], p['rwkv_g1'].shape[-1]
    head_spec = pl.BlockSpec((nseq, N_PAIRS, lt, LANES), lambda b, t: (b, 0, t, 0))
    head_shape = jax.ShapeDtypeStruct((B, N_PAIRS, L, LANES), F32)
    return pl.pallas_call(
        functools.partial(_rwkv_pre_kernel, tm=tm, nseq=nseq),
        grid=(B // nseq, nt),
        in_specs=[
            pl.BlockSpec((tm, D_MODEL), lambda b, t: (b * nt + t, 0)),
            pl.BlockSpec((F32_SUBLANES, D_MODEL),
                         lambda b, t: (jnp.maximum((b * nt + t) * (tm // F32_SUBLANES) - 1, 0), 0)),
            pl.BlockSpec((nseq, 1, D_MODEL), lambda b, t: (b, 0, 0)),
            vec(),
            _resident((None, 6, D_MODEL), lambda b, t: (i, 0, 0)),
            mat(D_MODEL, D_MODEL), mat(D_MODEL, D_MODEL), mat(D_MODEL, D_MODEL),
            mat(D_MODEL, lora[0]), mat(lora[0], D_MODEL),
            mat(D_MODEL, lora[1]), mat(lora[1], D_MODEL),
            mat(D_MODEL, lora[2]), mat(lora[2], D_MODEL),
            vec(), vec(), vec(), vec(), vec(),
            _resident((D_MODEL, LANES), lambda b, t: (0, 0)),
            _resident((LANES, D_MODEL), lambda b, t: (0, 0)),
        ],
        out_specs=[head_spec] * 6 + [
            pl.BlockSpec((tm, D_MODEL), lambda b, t: (b * nt + t, 0)),
            pl.BlockSpec((tm, D_MODEL), lambda b, t: (b * nt + t, 0)),
            pl.BlockSpec((nseq, F32_SUBLANES, D_MODEL), lambda b, t: (b, 0, 0)),
        ],
        out_shape=[head_shape] * 6 + [
            jax.ShapeDtypeStruct((B * L, D_MODEL), BF16),
            jax.ShapeDtypeStruct((B * L, D_MODEL), BF16),
            jax.ShapeDtypeStruct((B, F32_SUBLANES, D_MODEL), F32),
        ],
        compiler_params=_params(("parallel", "arbitrary")),
    )(x, x, shift_prev, gn, p['rwkv_mu'], p['rwkv_wr'], p['rwkv_wk'], p['rwkv_wv'],
      p['rwkv_w1'], p['rwkv_w2'], p['rwkv_a1'], p['rwkv_a2'], p['rwkv_g1'], p['rwkv_g2'],
      p['rwkv_w0'][i][None], p['rwkv_a0'][i][None], p['rwkv_k_k'][i][None], p['rwkv_k_a'][i][None],
      p['rwkv_r_k'][i].reshape(1, D_MODEL), p['head_onehot'], p['head_onehot_t'])


DIAG = 16


def _replication_matrix(n):
    src = np.arange(n)[:, None]
    dst = np.arange(n)[None, :]
    picks = np.concatenate([(src == (dst // DIAG) * DIAG + s) for s in range(DIAG - 1)], axis=1)
    return jnp.asarray(picks, dtype=BF16)


def _unit_lower_inverses(lows, c, rep_ref):
    n = 2 * c
    ti = lax.broadcasted_iota(jnp.int32, (c, n), 0)
    li = lax.broadcasted_iota(jnp.int32, (c, n), 1)
    si = li & (c - 1)
    head1 = li >= c

    def block_diag(x):
        return jnp.concatenate([jnp.where(head1, 0.0, x), jnp.where(head1, x, 0.0)], axis=0)

    if rep_ref is None:
        invs = [jnp.where(ti == si, 1.0, 0.0) + jnp.where(jnp.logical_and(ti == si + 1, (ti & 1) == 1), low, 0.0)
                for low in lows]
        m = 2
    else:
        pt = lax.broadcasted_iota(jnp.int32, (DIAG, n), 0)
        pl_ = lax.broadcasted_iota(jnp.int32, (DIAG, n), 1)
        blk = (pl_ & (c - 1)) // DIAG
        packed = []
        for low in lows:
            d = jnp.zeros((DIAG, n), F32)
            for i in range(c // DIAG):
                d = jnp.where(blk == i, low[i * DIAG:(i + 1) * DIAG], d)
            packed.append(d)
        rep = _dot(jnp.concatenate(packed, axis=0).astype(BF16), rep_ref[...])
        sols = [jnp.where(pt == (pl_ & (DIAG - 1)), 1.0, 0.0) for _ in lows]
        for s in range(DIAG - 1):
            sols = [sol + rep[p * DIAG:(p + 1) * DIAG, s * n:(s + 1) * n] * sol[s:s + 1]
                    for p, sol in enumerate(sols)]
        invs = [jnp.concatenate([jnp.where(blk == i, sol, 0.0) for i in range(c // DIAG)], axis=0) for sol in sols]
        m = DIAG
    while m < c:
        sh = m.bit_length() - 1
        sub = jnp.logical_and(jnp.logical_and((ti >> (sh + 1)) == (si >> (sh + 1)), ((ti >> sh) & 1) == 1),
                              ((si >> sh) & 1) == 0)
        diag = [block_diag(inv).astype(BF16) for inv in invs]
        half = [_dot(inv.astype(BF16), block_diag(jnp.where(sub, low, 0.0)).astype(BF16)).astype(BF16)
                for inv, low in zip(invs, lows)]
        invs = [inv + _dot(h, d) for inv, h, d in zip(invs, half, diag)]
        m *= 2
    return invs


def _scan_chunks(seqs, states, c, rep_ref):
    n = 2 * c
    row = lax.broadcasted_iota(jnp.int32, (c, LANES), 0)
    head1 = lax.broadcasted_iota(jnp.int32, (c, LANES), 1) >= RWKV_HEAD
    ti = lax.broadcasted_iota(jnp.int32, (c, n), 0)
    si = lax.broadcasted_iota(jnp.int32, (c, n), 1) & (c - 1)
    strict = ti > si
    incl = ti >= si
    pi = lax.broadcasted_iota(jnp.int32, (LANES, LANES), 0)
    pj = lax.broadcasted_iota(jnp.int32, (LANES, LANES), 1)
    same_head = (pi >= RWKV_HEAD) == (pj >= RWKV_HEAD)

    def stack(x):
        return jnp.concatenate([jnp.where(head1, 0.0, x), jnp.where(head1, x, 0.0)], axis=0)

    lhs, rhs, v_st, v_bf, bk_end, decay = [], [], [], [], [], []
    for r, lw, k, v, kk, b in seqs:
        cum = lw
        sh = 1
        while sh < c:
            cum = cum + jnp.where(row >= sh, pltpu.roll(cum, sh, 0), 0.0)
            sh *= 2
        tot = cum[c - 1:c]
        grow = jnp.exp(-cum)
        tail = jnp.exp(tot - cum)
        a_t = -kk * jnp.exp(cum - lw)
        r_t = r * jnp.exp(cum)
        lhs.append(jnp.concatenate([a_t, r_t], axis=0).astype(BF16))
        rhs.append(jnp.concatenate([stack(b * grow), stack(k * grow)], axis=0).astype(BF16))
        v_st.append(stack(v).astype(BF16))
        v_bf.append(v.astype(BF16))
        bk_end.append(jnp.concatenate([b * tail, k * tail], axis=0).astype(BF16))
        decay.append(jnp.exp(tot))
    grams = [_dot_nt(x, y) for x, y in zip(lhs, rhs)]
    lows = [jnp.where(strict, g[:c, :n], 0.0) for g in grams]
    m_rb = [jnp.where(incl, g[c:, :n], 0.0).astype(BF16) for g in grams]
    m_v = [jnp.concatenate([jnp.where(strict, g[:c, n:], 0.0), jnp.where(incl, g[c:, n:], 0.0)], axis=0).astype(BF16)
           for g in grams]
    invs = [inv.astype(BF16) for inv in _unit_lower_inverses(lows, c, rep_ref)]
    from_v = [_dot(m, vs) for m, vs in zip(m_v, v_st)]

    ys = []
    states = list(states)
    npair = len(states)
    for j in range(len(seqs) // npair):
        sl = slice(j * npair, (j + 1) * npair)
        from_state = [_dot_nt(x, s.astype(BF16)) for x, s in zip(lhs[sl], states)]
        rhs_sa = [stack(f[:c] + fv[:c]).astype(BF16) for f, fv in zip(from_state, from_v[sl])]
        sas = [_dot(inv, x) for inv, x in zip(invs[sl], rhs_sa)]
        ys += [f[c:] + fv[c:] + _dot(m, stack(sa).astype(BF16))
               for f, fv, m, sa in zip(from_state, from_v[sl], m_rb[sl], sas)]
        upd = [_dot_tn(jnp.concatenate([sa.astype(BF16), vb], axis=0), be)
               for sa, vb, be in zip(sas, v_bf[sl], bk_end[sl])]
        states = [s * d + jnp.where(same_head, u, 0.0) for s, d, u in zip(states, decay[sl], upd)]
    return ys, states


def _rwkv_scan_kernel(r_ref, lw_ref, k_ref, v_ref, kk_ref, b_ref, s0_ref, *rest, c, sub, nseq, vpu_diag):
    if vpu_diag:
        rep_ref, y_ref, so_ref, s_ref = rest
    else:
        rep_ref = None
        y_ref, so_ref, s_ref = rest
    ci = pl.program_id(1)
    chains = [(s, hp) for s in range(nseq) for hp in range(N_PAIRS)]

    @pl.when(ci == 0)
    def _():
        s_ref[...] = s0_ref[...]

    seqs = [tuple(ref[s, hp, j * c:(j + 1) * c] for ref in (r_ref, lw_ref, k_ref, v_ref, kk_ref, b_ref))
            for j in range(sub) for s, hp in chains]
    ys, new_states = _scan_chunks(seqs, [s_ref[s, hp] for s, hp in chains], c, rep_ref)
    for j in range(sub):
        for n, (s, hp) in enumerate(chains):
            y_ref[s, hp, j * c:(j + 1) * c] = ys[j * len(chains) + n]
    for n, (s, hp) in enumerate(chains):
        s_ref[s, hp] = new_states[n]

    @pl.when(ci == pl.num_programs(1) - 1)
    def _():
        so_ref[...] = s_ref[...]


def _rwkv_scan(r, lw, k, v, kk, b, s0, B, L):
    c = min(CHUNK, L)
    nc = L // c
    sub = next(s for s in SCAN_CHUNKS_PER_STEP if nc % s == 0)
    nc //= sub
    nseq = next(s for s in SCAN_CHUNKS_PER_STEP if B % s == 0) if nc * sub == 1 else 1
    seq = pl.BlockSpec((nseq, N_PAIRS, sub * c, LANES), lambda bi, ci: (bi, 0, ci, 0))
    st = pl.BlockSpec((nseq, N_PAIRS, LANES, LANES), lambda bi, ci: (bi, 0, 0, 0))
    vpu_diag = 2 * c == LANES
    extra_specs, extra_args = [], []
    if vpu_diag:
        extra_specs = [_resident((LANES, (DIAG - 1) * LANES), lambda bi, ci: (0, 0))]
        extra_args = [_replication_matrix(LANES)]
    return pl.pallas_call(
        functools.partial(_rwkv_scan_kernel, c=c, sub=sub, nseq=nseq, vpu_diag=vpu_diag),
        grid=(B // nseq, nc),
        in_specs=[seq] * 6 + [st] + extra_specs,
        out_specs=[seq, st],
        out_shape=[jax.ShapeDtypeStruct((B, N_PAIRS, L, LANES), F32),
                   jax.ShapeDtypeStruct((B, N_PAIRS, LANES, LANES), F32)],
        scratch_shapes=[pltpu.VMEM((nseq, N_PAIRS, LANES, LANES), F32)],
        compiler_params=_params(("parallel", "arbitrary")),
    )(r, lw, k, v, kk, b, s0, *extra_args)


def _rwkv_post_kernel(x_ref, y_ref, g_ref, bonus_ref, lw_ref, lb_ref, e_ref, et_ref, wo_ref, o_ref, *, parts, nseq):
    rows = x_ref.shape[0] // parts
    sls = [slice(i * rows, (i + 1) * rows) for i in range(parts)]
    head_sum = lambda t, split: _head_sum(t, e_ref, et_ref, split)
    if nseq == 1:
        ys = [jnp.concatenate([y_ref[0, hp, sl] for hp in range(N_PAIRS)], axis=-1) for sl in sls]
    else:
        ys = [jnp.concatenate([jnp.concatenate([y_ref[s, hp] for hp in range(N_PAIRS)], axis=-1)
                               for s in range(nseq)], axis=0)]
    ds = [y - head_sum(y, True) * (1.0 / RWKV_HEAD) for y in ys]
    var = [head_sum(d * d, False) * (1.0 / RWKV_HEAD) for d in ds]
    yn = [d * lax.rsqrt(vr + LNX_EPS) * lw_ref[...] + lb_ref[...] for d, vr in zip(ds, var)]
    gated = [((n + bonus_ref[sl, :].astype(F32)) * g_ref[sl, :].astype(F32)).astype(BF16) for n, sl in zip(yn, sls)]
    for gt, sl in zip(gated, sls):
        o_ref[sl, :] = x_ref[sl, :] + _dot(gt, wo_ref[...])


def _rwkv_post(x, y, g, bonus, p, i, B, L):
    nseq = max(1, min(B, TOKEN_TILE // L))
    assert B % nseq == 0
    lt = min(TOKEN_TILE, L)
    tm = lt * nseq
    nt = L // lt
    vec = lambda: _resident((1, D_MODEL), lambda b, t: (0, 0))
    head_spec = pl.BlockSpec((nseq, N_PAIRS, lt, LANES), lambda b, t: (b, 0, t, 0))
    row = pl.BlockSpec((tm, D_MODEL), lambda b, t: (b * nt + t, 0))
    return pl.pallas_call(
        functools.partial(_rwkv_post_kernel, parts=POST_PARTS if lt == TOKEN_TILE else 1, nseq=nseq),
        grid=(B // nseq, nt),
        in_specs=[row, head_spec, row, row, vec(), vec(),
                  _resident((D_MODEL, LANES), lambda b, t: (0, 0)),
                  _resident((LANES, D_MODEL), lambda b, t: (0, 0)),
                  _resident((D_MODEL, D_MODEL), lambda b, t: (0, 0))],
        out_specs=row,
        out_shape=jax.ShapeDtypeStruct((B * L, D_MODEL), F32),
        compiler_params=_params(("parallel", "parallel")),
    )(x, y, g, bonus, p['rwkv_lnx_w'][i][None], p['rwkv_lnx_b'][i][None],
      p['head_onehot'], p['head_onehot_t'], p['rwkv_wo'])


def _pack_state(s):
    B = s.shape[0]
    s = s.reshape(B, N_PAIRS, 2, RWKV_HEAD, RWKV_HEAD)
    z = jnp.zeros_like(s[:, :, 0])
    top = jnp.concatenate([s[:, :, 0], z], axis=-1)
    bot = jnp.concatenate([z, s[:, :, 1]], axis=-1)
    return jnp.concatenate([top, bot], axis=-2)


def _unpack_state(s):
    B = s.shape[0]
    return jnp.stack([s[:, :, :RWKV_HEAD, :RWKV_HEAD], s[:, :, RWKV_HEAD:, RWKV_HEAD:]], axis=2).reshape(
        B, RWKV_H, RWKV_HEAD, RWKV_HEAD)


class _Stream:
    def __init__(self, x, pos0, caches):
        self.B, self.L, _ = x.shape
        self.x = x.reshape(self.B * self.L, D_MODEL)
        self.pos0 = pos0
        self.pool_c, self.k_c, self.v_c, self.shift_c, self.wkv_c = caches
        self.stepping = self.k_c is not None
        if self.stepping:
            self.bkt = _bucket_index(self.L, SWA_ROWS + self.L, SWA_ROWS)
        else:
            self.bkt = _bucket_index(CHUNK, WINDOW + CHUNK, WINDOW)
        self.new = {name: [] for name in ('pool', 'k', 'v', 'shift', 'wkv')}

    def mixer(self, l, p):
        B, L, i = self.B, self.L, l // 2
        gn = p['norm_mix'][l][None]
        if l % 2 == 0:
            if self.stepping:
                hist = jnp.pad(self.pool_c[i], ((0, 0), (POOL_HALO - POOL_HIST, 0), (0, 0)))
                k_cache = self.k_c[i].reshape(B, SWA_ROWS, KV_W)
                v_cache = self.v_c[i].reshape(B, SWA_ROWS, KV_W)
            else:
                hist = jnp.zeros((B, POOL_HALO, C_POOL), F32)
                k_cache = v_cache = jnp.zeros((B, WINDOW, KV_W), F32)
            self.x, u_tail, k_tail, v_tail = _even_layer(
                self.x, gn, p, i, hist, k_cache, v_cache, self.bkt, B, L, self.pos0,
                chunk=L if self.stepping else CHUNK, masked=not self.stepping)
            self.new['pool'].append(u_tail[:, -POOL_HIST:])
            self.new['k'].append(k_tail.reshape(B, SWA_ROWS, N_KV_HEADS, HEAD_DIM))
            self.new['v'].append(v_tail.reshape(B, SWA_ROWS, N_KV_HEADS, HEAD_DIM))
        else:
            if self.stepping:
                shift_prev = self.shift_c[i][:, None, :]
                s0 = _pack_state(self.wkv_c[i])
            else:
                shift_prev = jnp.zeros((B, 1, D_MODEL), F32)
                s0 = jnp.zeros((B, N_PAIRS, LANES, LANES), F32)
            r, lw, k, v, kk, b, g, bonus, hs = _rwkv_pre(self.x, shift_prev, gn, p, i, B, L)
            y, s_new = _rwkv_scan(r, lw, k, v, kk, b, s0, B, L)
            self.x = _rwkv_post(self.x, y, g, bonus, p, i, B, L)
            self.new['shift'].append(hs[:, -1])
            self.new['wkv'].append(_unpack_state(s_new))

    def results(self):
        return (self.x.reshape(self.B, self.L, D_MODEL),) + tuple(
            jnp.stack(self.new[name]) for name in ('pool', 'k', 'v', 'shift', 'wkv'))


_FFN_WEIGHTS = ('ffn_w_gate', 'ffn_w_up', 'ffn_w_down')
_EVEN_WEIGHTS = ('w_in_even', 'w_out_even')
_ODD_WEIGHTS = ('rwkv_wr', 'rwkv_wk', 'rwkv_wv', 'rwkv_w1', 'rwkv_w2', 'rwkv_a1', 'rwkv_a2', 'rwkv_g1', 'rwkv_g2',
                'rwkv_wo')


def _trunk(long, short, p):
    depth = p['norm_mix'].shape[0]
    ffn_w = tuple(p[name][0, 0].astype(BF16) for name in _FFN_WEIGHTS)
    for l in range(depth):
        for j in range(2):
            n = 2 * l + j
            jobs = [(p[name], divmod(n + 1, 2)) for name in _FFN_WEIGHTS] if n + 1 < 2 * depth else []
            mixer_names = (_EVEN_WEIGHTS if l % 2 == 0 else _ODD_WEIGHTS) if j == 0 else ()
            jobs += [(p[name], (l // 2,)) for name in mixer_names]
            final_g = p['norm_final'][None] if n == 2 * depth - 1 else None
            long.x, short.x, done = _ffn(long.x, short.x, p['norm_ffn'][l, j][None], ffn_w, final_g, jobs)
            ffn_w = done[:len(_FFN_WEIGHTS)]
            if j == 0:
                mixer_w = dict(p, **dict(zip(mixer_names, done[len(done) - len(mixer_names):])))
                long.mixer(l, mixer_w)
                short.mixer(l, mixer_w)
    return long.results(), short.results()


def _prepare(p):
    p = dict(p)
    p['pool_w'] = p['pool_w'].astype(BF16)
    onehot = np.arange(D_MODEL)[:, None] // RWKV_HEAD == np.arange(LANES)[None, :]
    p['head_onehot'] = jnp.asarray(onehot, dtype=BF16)
    p['head_onehot_t'] = jnp.asarray(onehot.T, dtype=BF16)
    return p


def kernel(x_prompt, x_sample, cache_pool, cache_swa_k, cache_swa_v, state_shift, state_wkv, t5_table, norm_ffn, ffn_w_gate, ffn_w_up, ffn_w_down, norm_mix, w_in_even, pool_w, pool_scale, attn_sinks, w_out_even, rwkv_mu, rwkv_wr, rwkv_wk, rwkv_wv, rwkv_w0, rwkv_w1, rwkv_w2, rwkv_a0, rwkv_a1, rwkv_a2, rwkv_g1, rwkv_g2, rwkv_k_k, rwkv_k_a, rwkv_r_k, rwkv_lnx_w, rwkv_lnx_b, rwkv_wo, norm_final):
    p = _prepare(dict(
        t5_table=t5_table, norm_ffn=norm_ffn, ffn_w_gate=ffn_w_gate, ffn_w_up=ffn_w_up, ffn_w_down=ffn_w_down,
        norm_mix=norm_mix, w_in_even=w_in_even, pool_w=pool_w, pool_scale=pool_scale, attn_sinks=attn_sinks,
        w_out_even=w_out_even, rwkv_mu=rwkv_mu, rwkv_wr=rwkv_wr, rwkv_wk=rwkv_wk, rwkv_wv=rwkv_wv,
        rwkv_w0=rwkv_w0, rwkv_w1=rwkv_w1, rwkv_w2=rwkv_w2, rwkv_a0=rwkv_a0, rwkv_a1=rwkv_a1, rwkv_a2=rwkv_a2,
        rwkv_g1=rwkv_g1, rwkv_g2=rwkv_g2, rwkv_k_k=rwkv_k_k, rwkv_k_a=rwkv_k_a, rwkv_r_k=rwkv_r_k,
        rwkv_lnx_w=rwkv_lnx_w, rwkv_lnx_b=rwkv_lnx_b, rwkv_wo=rwkv_wo, norm_final=norm_final))
    prompt = _Stream(x_prompt, 0, (None, None, None, None, None))
    sample = _Stream(x_sample, PAST_LEN, (cache_pool, cache_swa_k, cache_swa_v, state_shift, state_wkv))
    (y_p, pool_p, k_p, v_p, shift_p, wkv_p), (y_s, pool_s, k_s, v_s, shift_s, wkv_s) = _trunk(prompt, sample, p)
    return (y_p, y_s, pool_p, pool_s, k_p, k_s, v_p, v_s, shift_p, shift_s, wkv_p, wkv_s)
```

```python
import functools
import math

import jax
import jax.numpy as jnp
import numpy as np
from jax import lax
from jax.experimental import pallas as pl
from jax.experimental.pallas import tpu as pltpu

F32 = jnp.float32
BF16 = jnp.bfloat16

D_MODEL = 1024
D_FF = 2816
NORM_EPS = 1e-6
CHUNK = 64
POOL_WINDOWS = (2, 4, 8, 16)
C_POOL = 512
POOL_GC = 128
POOL_HIST = 15
POOL_HALO = 16
HEAD_DIM = 64
N_Q_HEADS = 8
N_KV_HEADS = 2
GQA_GROUP = 4
WINDOW = 128
SWA_ROWS = 128
Q_W = 512
KV_W = 128
IN_EVEN = C_POOL + Q_W + 2 * KV_W
N_BUCKETS = 32
MAX_DISTANCE = 128
RWKV_HEAD = 64
RWKV_H = 16
N_PAIRS = RWKV_H // 2
LNX_EPS = 64e-5
PAST_LEN = 4096

LANES = 128
F32_SUBLANES = 8
BF16_SUBLANES = 16
VMEM_LIMIT_BYTES = 56 * 1024 * 1024

FFN_TILE = 1024
FFN_F_CHUNK = 256
TOKEN_TILE = 512
POST_PARTS = 2
EVEN_CHUNKS_PER_STEP = 16
SCAN_CHUNKS_PER_STEP = (4, 2, 1)


def _params(sem):
    return pltpu.CompilerParams(dimension_semantics=sem, vmem_limit_bytes=VMEM_LIMIT_BYTES)


def _dot(a, b):
    return jnp.dot(a, b, preferred_element_type=F32)


def _dot_nt(a, b):
    return lax.dot_general(a, b, (((1,), (1,)), ((), ())), preferred_element_type=F32)


def _dot_tn(a, b):
    return lax.dot_general(a, b, (((0,), (0,)), ((), ())), preferred_element_type=F32)


def _rms(x, g):
    return x * lax.rsqrt(jnp.mean(x * x, axis=-1, keepdims=True) + NORM_EPS) * g


def _split(x):
    hi = x.astype(BF16)
    lo = (x - hi.astype(F32)).astype(BF16)
    return hi, lo


def _head_sum(x, e_ref, et_ref, split=True):
    if not split:
        return _dot(_dot(x.astype(BF16), e_ref[...]).astype(BF16), et_ref[...])
    hi, lo = _split(x)
    s = _dot(hi, e_ref[...]) + _dot(lo, e_ref[...])
    shi, slo = _split(s)
    return _dot(shi, et_ref[...]) + _dot(slo, et_ref[...])


def _ffn_kernel(x_ref, xs_ref, g_ref, wg_ref, wu_ref, wd_ref, *rest, f_chunk, final, convert):
    rest = list(rest)
    gf_ref = rest.pop(0) if final else None
    src_refs = [rest.pop(0) for _ in range(convert)]
    o_ref, os_ref = rest.pop(0), rest.pop(0)
    dst_refs = [rest.pop(0) for _ in range(convert)]
    acc_ref, = rest
    for src, dst in zip(src_refs, dst_refs):
        dst[...] = src[...].astype(BF16)

    def rows(src_ref, dst_ref):
        n = src_ref.shape[0]
        x = src_ref[...]
        h = _rms(x, g_ref[...]).astype(BF16)
        for j in range(D_FF // f_chunk):
            sl = slice(j * f_chunk, (j + 1) * f_chunk)
            gate = _dot(h, wg_ref[:, sl])
            up = _dot(h, wu_ref[:, sl])
            act = (gate * jax.nn.sigmoid(gate) * up).astype(BF16)
            part = _dot(act, wd_ref[sl, :])
            if j == 0:
                acc_ref[:n] = part
            else:
                acc_ref[:n] += part
        y = x + 0.5 * acc_ref[:n]
        if final:
            y = _rms(y, gf_ref[...])
        dst_ref[...] = y

    rows(x_ref, o_ref)

    @pl.when(pl.program_id(0) == pl.num_programs(0) - 1)
    def _():
        rows(xs_ref, os_ref)


def _resident(shape, index_map):
    return pl.BlockSpec(shape, index_map, pipeline_mode=pl.Buffered(1))


def _slab_steps(rows, steps):
    return next(k for k in range(steps, 0, -1) if rows % (k * BF16_SUBLANES) == 0)


def _ffn(x, xs, g, weights, final_g=None, convert=()):
    T = x.shape[0]
    tm = min(FFN_TILE, T)
    steps = T // tm
    Ts = xs.shape[0]
    assert Ts <= tm
    final = final_g is not None
    in_specs = [pl.BlockSpec((tm, D_MODEL), lambda i: (i, 0)), _resident((Ts, D_MODEL), lambda i: (0, 0)),
                _resident((1, D_MODEL), lambda i: (0, 0))]
    in_specs += [_resident(w.shape, lambda i: (0, 0)) for w in weights]
    args = [x, xs, g, *weights]
    out_specs = [pl.BlockSpec((tm, D_MODEL), lambda i: (i, 0)), pl.BlockSpec((Ts, D_MODEL), lambda i: (0, 0))]
    out_shape = [jax.ShapeDtypeStruct((T, D_MODEL), F32), jax.ShapeDtypeStruct((Ts, D_MODEL), F32)]
    if final:
        in_specs.append(_resident((1, D_MODEL), lambda i: (0, 0)))
        args.append(final_g)
    for w, lead in convert:
        rows, cols = w.shape[len(lead):]
        k = _slab_steps(rows, steps)
        in_specs.append(pl.BlockSpec((None,) * len(lead) + (rows // k, cols),
                                     lambda i, k=k, lead=lead: (*lead, jnp.minimum(i, k - 1), 0)))
        out_specs.append(pl.BlockSpec((rows // k, cols), lambda i, k=k: (jnp.minimum(i, k - 1), 0)))
        out_shape.append(jax.ShapeDtypeStruct((rows, cols), BF16))
        args.append(w)
    outs = pl.pallas_call(
        functools.partial(_ffn_kernel, f_chunk=FFN_F_CHUNK, final=final, convert=len(convert)),
        grid=(steps,),
        in_specs=in_specs,
        out_specs=out_specs,
        out_shape=out_shape,
        scratch_shapes=[pltpu.VMEM((tm, D_MODEL), F32)],
        compiler_params=_params(("arbitrary",)),
    )(*args)
    return outs[0], outs[1], tuple(outs[2:])


def _build_bias(bkt_ref, tab_ref, bias_ref, lq):
    bkt = bkt_ref[...]
    for h in range(N_Q_HEADS):
        b = jnp.zeros(bkt.shape, F32)
        for n in range(N_BUCKETS):
            b = jnp.where(bkt == n, tab_ref[n, h], b)
        g, i = divmod(h, GQA_GROUP)
        bias_ref[g, :, i * lq:(i + 1) * lq] = b


def _group_queries(q, g):
    return jnp.concatenate([q[:, h * HEAD_DIM:(h + 1) * HEAD_DIM]
                            for h in range(g * GQA_GROUP, (g + 1) * GQA_GROUP)], axis=0) * (HEAD_DIM ** -0.5)


def _sink_rows(sink_ref, lq):
    lane = lax.broadcasted_iota(jnp.int32, (1, GQA_GROUP * lq), 1)
    rows = []
    for g in range(N_KV_HEADS):
        r = jnp.zeros((1, GQA_GROUP * lq), F32)
        for i in range(GQA_GROUP):
            r = jnp.where(lane // lq == i, sink_ref[g * GQA_GROUP + i], r)
        rows.append(r)
    return rows


def _attn_core(qs, ks, vs, biases, sinks, valids):
    ss = [_dot_nt(k, q) + b for q, k, b in zip(qs, ks, biases)]
    ss = [s if ok is None else jnp.where(ok, s, -1e30) for s, ok in zip(ss, valids)]
    ms = [jnp.maximum(jnp.max(s, axis=0, keepdims=True), sk) for s, sk in zip(ss, sinks)]
    ps = [jnp.exp(s - m) for s, m in zip(ss, ms)]
    invs = [1.0 / (jnp.sum(p, axis=0, keepdims=True) + jnp.exp(sk - m)) for p, sk, m in zip(ps, sinks, ms)]
    return [_dot_tn((p * r).astype(BF16), v) for p, r, v in zip(ps, invs, vs)]


def _smem():
    return pl.BlockSpec(memory_space=pltpu.SMEM)


def _even_layer_kernel(x_ref, gn_ref, win_ref, pw_ref, ps_ref, hist_ref, kc_ref, vc_ref, bkt_ref, tab_ref, sink_ref,
                       wop_ref, woa_ref, o_ref, utail_o, ktail_o, vtail_o, halo_ref, kprev_ref, vprev_ref, bias_ref,
                       *, rows, chunk, pos0, masked):
    t = pl.program_id(1)

    @pl.when(jnp.logical_and(pl.program_id(0) == 0, t == 0))
    def _():
        _build_bias(bkt_ref, tab_ref, bias_ref, chunk)

    @pl.when(t == 0)
    def _():
        halo_ref[...] = hist_ref[0]
        kprev_ref[...] = kc_ref[0]
        vprev_ref[...] = vc_ref[0]

    x = x_ref[...]
    z = _dot(_rms(x, gn_ref[...]).astype(BF16), win_ref[...])
    u = z[:, :C_POOL]
    q = z[:, C_POOL:C_POOL + Q_W].astype(BF16)
    k_all = jnp.concatenate([kprev_ref[...], z[:, C_POOL + Q_W:C_POOL + Q_W + KV_W]], axis=0)
    v_all = jnp.concatenate([vprev_ref[...], z[:, C_POOL + Q_W + KV_W:]], axis=0)

    ext = jnp.concatenate([halo_ref[...], u], axis=0)
    pos = pos0 + t * rows + lax.broadcasted_iota(jnp.int32, (rows, POOL_GC), 0)
    pooled = []
    for gi, w in enumerate(POOL_WINDOWS):
        sl = slice(gi * POOL_GC, (gi + 1) * POOL_GC)
        s = ext[:, sl]
        span = 1
        while span < w:
            s = s + pltpu.roll(s, span, 0)
            span *= 2
        cnt = jnp.minimum(w, pos + 1).astype(F32)
        pooled.append(_dot((s[POOL_HALO:] / cnt - u[:, sl]).astype(BF16), pw_ref[gi]))
    pool_out = (jnp.concatenate(pooled, axis=-1) * ps_ref[...]).astype(BF16)

    lk = WINDOW + chunk
    kb = k_all.astype(BF16)
    vb = v_all.astype(BF16)
    sink_rows = _sink_rows(sink_ref, chunk)
    first_pos = t * rows - WINDOW + lax.broadcasted_iota(jnp.int32, (lk, 1), 0)
    nchunk = rows // chunk
    qs, ks, vs, biases, sinks, valids = [], [], [], [], [], []
    for j in range(nchunk):
        qj = q[j * chunk:(j + 1) * chunk]
        for g in range(N_KV_HEADS):
            qs.append(_group_queries(qj, g))
            ks.append(kb[j * chunk:j * chunk + lk, g * HEAD_DIM:(g + 1) * HEAD_DIM])
            vs.append(vb[j * chunk:j * chunk + lk, g * HEAD_DIM:(g + 1) * HEAD_DIM])
            biases.append(bias_ref[g])
            sinks.append(sink_rows[g])
            valids.append(first_pos + j * chunk >= 0 if masked else None)
    outs = _attn_core(qs, ks, vs, biases, sinks, valids)
    att = jnp.concatenate(
        [jnp.concatenate([outs[j * N_KV_HEADS + g][i * chunk:(i + 1) * chunk]
                          for g in range(N_KV_HEADS) for i in range(GQA_GROUP)], axis=-1) for j in range(nchunk)],
        axis=0).astype(BF16)

    o_ref[...] = x + _dot(pool_out, wop_ref[...]) + _dot(att, woa_ref[...])

    halo_ref[...] = ext[rows:]
    kprev_ref[...] = k_all[rows:]
    vprev_ref[...] = v_all[rows:]
    utail_o[0] = ext[rows:]
    ktail_o[0] = k_all[rows:]
    vtail_o[0] = v_all[rows:]


def _even_layer(x, gn, p, i, hist, k_cache, v_cache, bkt, B, L, pos0, chunk, masked):
    rows = min(EVEN_CHUNKS_PER_STEP * chunk, L)
    nt = L // rows
    lk = WINDOW + chunk
    row = pl.BlockSpec((rows, D_MODEL), lambda b, t: (b * nt + t, 0))
    per_seq = lambda n, m: pl.BlockSpec((1, n, m), lambda b, t: (b, 0, 0))
    return pl.pallas_call(
        functools.partial(_even_layer_kernel, rows=rows, chunk=chunk, pos0=pos0, masked=masked),
        grid=(B, nt),
        in_specs=[row, _resident((1, D_MODEL), lambda b, t: (0, 0)),
                  _resident((D_MODEL, IN_EVEN), lambda b, t: (0, 0)),
                  _resident((None, len(POOL_WINDOWS), POOL_GC, POOL_GC), lambda b, t: (i, 0, 0, 0)),
                  _resident((1, C_POOL), lambda b, t: (0, 0)),
                  per_seq(POOL_HALO, C_POOL), per_seq(WINDOW, KV_W), per_seq(WINDOW, KV_W),
                  _resident((lk, chunk), lambda b, t: (0, 0)), _smem(), _smem(),
                  _resident((C_POOL, D_MODEL), lambda b, t: (0, 0)),
                  _resident((Q_W, D_MODEL), lambda b, t: (1, 0))],
        out_specs=[row, per_seq(POOL_HALO, C_POOL), per_seq(WINDOW, KV_W), per_seq(WINDOW, KV_W)],
        out_shape=[jax.ShapeDtypeStruct((B * L, D_MODEL), F32), jax.ShapeDtypeStruct((B, POOL_HALO, C_POOL), F32),
                   jax.ShapeDtypeStruct((B, WINDOW, KV_W), F32), jax.ShapeDtypeStruct((B, WINDOW, KV_W), F32)],
        scratch_shapes=[pltpu.VMEM((POOL_HALO, C_POOL), F32), pltpu.VMEM((WINDOW, KV_W), F32),
                        pltpu.VMEM((WINDOW, KV_W), F32), pltpu.VMEM((N_KV_HEADS, lk, GQA_GROUP * chunk), F32)],
        compiler_params=_params(("arbitrary", "arbitrary")),
    )(x, gn, p['w_in_even'], p['pool_w'], p['pool_scale'][i][None], hist, k_cache, v_cache, bkt,
      p['t5_table'], p['attn_sinks'][i], p['w_out_even'], p['w_out_even'])


def _t5_bucket(rel):
    half = N_BUCKETS // 2
    max_exact = half // 2
    side = jnp.where(rel > 0, half, 0)
    n = jnp.abs(rel)
    nf = jnp.maximum(n, max_exact).astype(F32)
    large = max_exact + (jnp.log(nf / max_exact) / math.log(MAX_DISTANCE / max_exact)
                         * (half - max_exact)).astype(jnp.int32)
    large = jnp.minimum(large, half - 1)
    return side + jnp.where(n < max_exact, n, large)


def _bucket_index(lq, lk, offset):
    rel = jnp.arange(lk)[:, None] - offset - jnp.arange(lq)[None, :]
    return _t5_bucket(rel).astype(jnp.int32)


def _rwkv_pre_kernel(x_ref, xp_ref, sh_ref, gn_ref, mu_ref, wr_ref, wk_ref, wv_ref, w1_ref, w2_ref,
                     a1_ref, a2_ref, g1_ref, g2_ref, w0_ref, a0_ref, kk_ref, ka_ref,
                     r_o, lw_o, k_o, v_o, kk_o, a_o, g_o, hs_o, *, tm, nseq):
    t = pl.program_id(1)
    lt = tm // nseq
    gn = gn_ref[...]
    h = _rms(x_ref[...], gn)
    row = lax.broadcasted_iota(jnp.int32, (tm, 1), 0)
    h_prev = pltpu.roll(h, 1, 0)
    for s in range(nseq):
        first_prev = sh_ref[s]
        if nseq == 1:
            first_prev = jnp.where(t == 0, first_prev, _rms(xp_ref[...], gn)[F32_SUBLANES - 1:])
        h_prev = jnp.where(row == s * lt, first_prev, h_prev)
    xx = h_prev - h

    hb = h.astype(BF16)
    xb = xx.astype(BF16)
    mu = mu_ref[...].astype(BF16)

    def mix(j):
        return hb + xb * mu[j:j + 1]

    zw = w0_ref[...] + _dot(jnp.tanh(_dot(mix(1), w1_ref[...])).astype(BF16), w2_ref[...])
    za = a0_ref[...] + _dot(_dot(mix(4), a1_ref[...]).astype(BF16), a2_ref[...])
    g = _dot(jax.nn.sigmoid(_dot(mix(5), g1_ref[...])).astype(BF16), g2_ref[...])
    r = _dot(mix(0), wr_ref[...])
    lw = -math.exp(-0.5) * jax.nn.sigmoid(zw)
    k = _dot(mix(2), wk_ref[...])
    a = jax.nn.sigmoid(za)
    v = _dot(mix(3), wv_ref[...])
    kk = k * kk_ref[...]
    k = k * (1.0 + (a - 1.0) * ka_ref[...])
    for s in range(nseq):
        rows = slice(s * lt, (s + 1) * lt)
        for hp in range(N_PAIRS):
            sl = slice(hp * LANES, (hp + 1) * LANES)
            for out, val in ((r_o, r), (lw_o, lw), (k_o, k), (v_o, v), (kk_o, kk), (a_o, a)):
                out[s, hp] = val[rows, sl]
        hs_o[s] = h[(s + 1) * lt - F32_SUBLANES:(s + 1) * lt]
    g_o[...] = g.astype(BF16)


def _rwkv_pre(x, shift_prev, gn, p, i, B, L):
    nseq = max(1, min(B, TOKEN_TILE // L))
    assert B % nseq == 0
    tm = min(TOKEN_TILE, L) * nseq
    lt = tm // nseq
    nt = L // lt
    vec = lambda: _resident((1, D_MODEL), lambda b, t: (0, 0))
    mat = lambda r, c: _resident((r, c), lambda b, t: (0, 0))
    lora = p['rwkv_w1'].shape[-1], p['rwkv_a1'].shape[-1], p['rwkv_g1'].shape[-1]
    head_spec = pl.BlockSpec((nseq, N_PAIRS, lt, LANES), lambda b, t: (b, 0, t, 0))
    head_shape = jax.ShapeDtypeStruct((B, N_PAIRS, L, LANES), F32)
    return pl.pallas_call(
        functools.partial(_rwkv_pre_kernel, tm=tm, nseq=nseq),
        grid=(B // nseq, nt),
        in_specs=[
            pl.BlockSpec((tm, D_MODEL), lambda b, t: (b * nt + t, 0)),
            pl.BlockSpec((F32_SUBLANES, D_MODEL),
                         lambda b, t: (jnp.maximum((b * nt + t) * (tm // F32_SUBLANES) - 1, 0), 0)),
            pl.BlockSpec((nseq, 1, D_MODEL), lambda b, t: (b, 0, 0)),
            vec(),
            _resident((None, 6, D_MODEL), lambda b, t: (i, 0, 0)),
            mat(D_MODEL, D_MODEL), mat(D_MODEL, D_MODEL), mat(D_MODEL, D_MODEL),
            mat(D_MODEL, lora[0]), mat(lora[0], D_MODEL),
            mat(D_MODEL, lora[1]), mat(lora[1], D_MODEL),
            mat(D_MODEL, lora[2]), mat(lora[2], D_MODEL),
            vec(), vec(), vec(), vec(),
        ],
        out_specs=[head_spec] * 6 + [
            pl.BlockSpec((tm, D_MODEL), lambda b, t: (b * nt + t, 0)),
            pl.BlockSpec((nseq, F32_SUBLANES, D_MODEL), lambda b, t: (b, 0, 0)),
        ],
        out_shape=[head_shape] * 6 + [
            jax.ShapeDtypeStruct((B * L, D_MODEL), BF16),
            jax.ShapeDtypeStruct((B, F32_SUBLANES, D_MODEL), F32),
        ],
        compiler_params=_params(("parallel", "arbitrary")),
    )(x, x, shift_prev, gn, p['rwkv_mu'], p['rwkv_wr'], p['rwkv_wk'], p['rwkv_wv'],
      p['rwkv_w1'], p['rwkv_w2'], p['rwkv_a1'], p['rwkv_a2'], p['rwkv_g1'], p['rwkv_g2'],
      p['rwkv_w0'][i][None], p['rwkv_a0'][i][None], p['rwkv_k_k'][i][None], p['rwkv_k_a'][i][None])


DIAG = 16


def _replication_matrix(n):
    src = np.arange(n)[:, None]
    dst = np.arange(n)[None, :]
    picks = np.concatenate([(src == (dst // DIAG) * DIAG + s) for s in range(DIAG - 1)], axis=1)
    return jnp.asarray(picks, dtype=BF16)


def _unit_lower_inverses(lows, c, rep_ref):
    n = 2 * c
    ti = lax.broadcasted_iota(jnp.int32, (c, n), 0)
    li = lax.broadcasted_iota(jnp.int32, (c, n), 1)
    si = li & (c - 1)
    head1 = li >= c

    def block_diag(x):
        return jnp.concatenate([jnp.where(head1, 0.0, x), jnp.where(head1, x, 0.0)], axis=0)

    if rep_ref is None:
        invs = [jnp.where(ti == si, 1.0, 0.0) + jnp.where(jnp.logical_and(ti == si + 1, (ti & 1) == 1), low, 0.0)
                for low in lows]
        m = 2
    else:
        pt = lax.broadcasted_iota(jnp.int32, (DIAG, n), 0)
        pl_ = lax.broadcasted_iota(jnp.int32, (DIAG, n), 1)
        blk = (pl_ & (c - 1)) // DIAG
        packed = []
        for low in lows:
            d = jnp.zeros((DIAG, n), F32)
            for i in range(c // DIAG):
                d = jnp.where(blk == i, low[i * DIAG:(i + 1) * DIAG], d)
            packed.append(d)
        rep = _dot(jnp.concatenate(packed, axis=0).astype(BF16), rep_ref[...])
        sols = [jnp.where(pt == (pl_ & (DIAG - 1)), 1.0, 0.0) for _ in lows]
        for s in range(DIAG - 1):
            sols = [sol + rep[p * DIAG:(p + 1) * DIAG, s * n:(s + 1) * n] * sol[s:s + 1]
                    for p, sol in enumerate(sols)]
        invs = [jnp.concatenate([jnp.where(blk == i, sol, 0.0) for i in range(c // DIAG)], axis=0) for sol in sols]
        m = DIAG
    while m < c:
        sh = m.bit_length() - 1
        sub = jnp.logical_and(jnp.logical_and((ti >> (sh + 1)) == (si >> (sh + 1)), ((ti >> sh) & 1) == 1),
                              ((si >> sh) & 1) == 0)
        diag = [block_diag(inv).astype(BF16) for inv in invs]
        half = [_dot(inv.astype(BF16), block_diag(jnp.where(sub, low, 0.0)).astype(BF16)).astype(BF16)
                for inv, low in zip(invs, lows)]
        invs = [inv + _dot(h, d) for inv, h, d in zip(invs, half, diag)]
        m *= 2
    return invs


def _scan_chunks(seqs, states, c, rep_ref):
    n = 2 * c
    row = lax.broadcasted_iota(jnp.int32, (c, LANES), 0)
    head1 = lax.broadcasted_iota(jnp.int32, (c, LANES), 1) >= RWKV_HEAD
    ti = lax.broadcasted_iota(jnp.int32, (c, n), 0)
    si = lax.broadcasted_iota(jnp.int32, (c, n), 1) & (c - 1)
    strict = ti > si
    incl = ti >= si
    pi = lax.broadcasted_iota(jnp.int32, (LANES, LANES), 0)
    pj = lax.broadcasted_iota(jnp.int32, (LANES, LANES), 1)
    same_head = (pi >= RWKV_HEAD) == (pj >= RWKV_HEAD)

    def stack(x):
        return jnp.concatenate([jnp.where(head1, 0.0, x), jnp.where(head1, x, 0.0)], axis=0)

    def head_sum(x):
        return jnp.where(head1, jnp.sum(jnp.where(head1, x, 0.0), axis=-1, keepdims=True),
                         jnp.sum(jnp.where(head1, 0.0, x), axis=-1, keepdims=True))

    lhs, rhs, v_st, v_bf, bk_end, decay, bonus = [], [], [], [], [], [], []
    for r, lw, k, v, kk, a, rk in seqs:
        kk = kk * lax.rsqrt(jnp.maximum(head_sum(kk * kk), 1e-24))
        b = kk * a
        bonus.append(head_sum(r * k * rk) * v)
        cum = lw
        sh = 1
        while sh < c:
            cum = cum + jnp.where(row >= sh, pltpu.roll(cum, sh, 0), 0.0)
            sh *= 2
        tot = cum[c - 1:c]
        grow = jnp.exp(-cum)
        tail = jnp.exp(tot - cum)
        a_t = -kk * jnp.exp(cum - lw)
        r_t = r * jnp.exp(cum)
        lhs.append(jnp.concatenate([a_t, r_t], axis=0).astype(BF16))
        rhs.append(jnp.concatenate([stack(b * grow), stack(k * grow)], axis=0).astype(BF16))
        v_st.append(stack(v).astype(BF16))
        v_bf.append(v.astype(BF16))
        bk_end.append(jnp.concatenate([b * tail, k * tail], axis=0).astype(BF16))
        decay.append(jnp.exp(tot))
    grams = [_dot_nt(x, y) for x, y in zip(lhs, rhs)]
    lows = [jnp.where(strict, g[:c, :n], 0.0) for g in grams]
    m_rb = [jnp.where(incl, g[c:, :n], 0.0).astype(BF16) for g in grams]
    m_v = [jnp.concatenate([jnp.where(strict, g[:c, n:], 0.0), jnp.where(incl, g[c:, n:], 0.0)], axis=0).astype(BF16)
           for g in grams]
    invs = [inv.astype(BF16) for inv in _unit_lower_inverses(lows, c, rep_ref)]
    from_v = [_dot(m, vs) for m, vs in zip(m_v, v_st)]

    ys = []
    states = list(states)
    npair = len(states)
    for j in range(len(seqs) // npair):
        sl = slice(j * npair, (j + 1) * npair)
        from_state = [_dot_nt(x, s.astype(BF16)) for x, s in zip(lhs[sl], states)]
        rhs_sa = [stack(f[:c] + fv[:c]).astype(BF16) for f, fv in zip(from_state, from_v[sl])]
        sas = [_dot(inv, x) for inv, x in zip(invs[sl], rhs_sa)]
        ys += [f[c:] + fv[c:] + _dot(m, stack(sa).astype(BF16))
               for f, fv, m, sa in zip(from_state, from_v[sl], m_rb[sl], sas)]
        upd = [_dot_tn(jnp.concatenate([sa.astype(BF16), vb], axis=0), be)
               for sa, vb, be in zip(sas, v_bf[sl], bk_end[sl])]
        states = [s * d + jnp.where(same_head, u, 0.0) for s, d, u in zip(states, decay[sl], upd)]
    return ys, bonus, states


def _rwkv_scan_kernel(r_ref, lw_ref, k_ref, v_ref, kk_ref, a_ref, rk_ref, s0_ref, *rest, c, sub, nseq, vpu_diag):
    if vpu_diag:
        rep_ref, y_ref, bonus_ref, so_ref, s_ref = rest
    else:
        rep_ref = None
        y_ref, bonus_ref, so_ref, s_ref = rest
    ci = pl.program_id(1)
    chains = [(s, hp) for s in range(nseq) for hp in range(N_PAIRS)]

    @pl.when(ci == 0)
    def _():
        s_ref[...] = s0_ref[...]

    seqs = [tuple(ref[s, hp, j * c:(j + 1) * c] for ref in (r_ref, lw_ref, k_ref, v_ref, kk_ref, a_ref)) + (rk_ref[hp],)
            for j in range(sub) for s, hp in chains]
    ys, bonus, new_states = _scan_chunks(seqs, [s_ref[s, hp] for s, hp in chains], c, rep_ref)
    for j in range(sub):
        for n, (s, hp) in enumerate(chains):
            y_ref[s, hp, j * c:(j + 1) * c] = ys[j * len(chains) + n]
            bonus_ref[s, hp, j * c:(j + 1) * c] = bonus[j * len(chains) + n].astype(BF16)
    for n, (s, hp) in enumerate(chains):
        s_ref[s, hp] = new_states[n]

    @pl.when(ci == pl.num_programs(1) - 1)
    def _():
        so_ref[...] = s_ref[...]


def _rwkv_scan(r, lw, k, v, kk, a, r_k, s0, B, L):
    c = min(CHUNK, L)
    nc = L // c
    sub = next(s for s in SCAN_CHUNKS_PER_STEP if nc % s == 0)
    nc //= sub
    nseq = next(s for s in SCAN_CHUNKS_PER_STEP if B % s == 0) if nc * sub == 1 else 1
    seq = pl.BlockSpec((nseq, N_PAIRS, sub * c, LANES), lambda bi, ci: (bi, 0, ci, 0))
    st = pl.BlockSpec((nseq, N_PAIRS, LANES, LANES), lambda bi, ci: (bi, 0, 0, 0))
    vpu_diag = 2 * c == LANES
    extra_specs, extra_args = [], []
    if vpu_diag:
        extra_specs = [_resident((LANES, (DIAG - 1) * LANES), lambda bi, ci: (0, 0))]
        extra_args = [_replication_matrix(LANES)]
    return pl.pallas_call(
        functools.partial(_rwkv_scan_kernel, c=c, sub=sub, nseq=nseq, vpu_diag=vpu_diag),
        grid=(B // nseq, nc),
        in_specs=[seq] * 6 + [_resident((N_PAIRS, 1, LANES), lambda bi, ci: (0, 0, 0)), st] + extra_specs,
        out_specs=[seq, seq, st],
        out_shape=[jax.ShapeDtypeStruct((B, N_PAIRS, L, LANES), F32),
                   jax.ShapeDtypeStruct((B, N_PAIRS, L, LANES), BF16),
                   jax.ShapeDtypeStruct((B, N_PAIRS, LANES, LANES), F32)],
        scratch_shapes=[pltpu.VMEM((nseq, N_PAIRS, LANES, LANES), F32)],
        compiler_params=_params(("parallel", "arbitrary")),
    )(r, lw, k, v, kk, a, r_k, s0, *extra_args)


def _rwkv_post_kernel(x_ref, y_ref, g_ref, bonus_ref, lw_ref, lb_ref, e_ref, et_ref, wo_ref, o_ref, *, parts, nseq):
    rows = x_ref.shape[0] // parts
    sls = [slice(i * rows, (i + 1) * rows) for i in range(parts)]
    head_sum = lambda t, split: _head_sum(t, e_ref, et_ref, split)

    def token_major(ref):
        if nseq == 1:
            return [jnp.concatenate([ref[0, hp, sl] for hp in range(N_PAIRS)], axis=-1) for sl in sls]
        return [jnp.concatenate([jnp.concatenate([ref[s, hp] for hp in range(N_PAIRS)], axis=-1)
                                 for s in range(nseq)], axis=0)]

    ys = token_major(y_ref)
    ds = [y - head_sum(y, True) * (1.0 / RWKV_HEAD) for y in ys]
    var = [head_sum(d * d, False) * (1.0 / RWKV_HEAD) for d in ds]
    yn = [d * lax.rsqrt(vr + LNX_EPS) * lw_ref[...] + lb_ref[...] for d, vr in zip(ds, var)]
    gated = [((n + bo.astype(F32)) * g_ref[sl, :].astype(F32)).astype(BF16)
             for n, bo, sl in zip(yn, token_major(bonus_ref), sls)]
    for gt, sl in zip(gated, sls):
        o_ref[sl, :] = x_ref[sl, :] + _dot(gt, wo_ref[...])


def _rwkv_post(x, y, g, bonus, p, i, B, L):
    nseq = max(1, min(B, TOKEN_TILE // L))
    assert B % nseq == 0
    lt = min(TOKEN_TILE, L)
    tm = lt * nseq
    nt = L // lt
    vec = lambda: _resident((1, D_MODEL), lambda b, t: (0, 0))
    head_spec = pl.BlockSpec((nseq, N_PAIRS, lt, LANES), lambda b, t: (b, 0, t, 0))
    row = pl.BlockSpec((tm, D_MODEL), lambda b, t: (b * nt + t, 0))
    return pl.pallas_call(
        functools.partial(_rwkv_post_kernel, parts=POST_PARTS if lt == TOKEN_TILE else 1, nseq=nseq),
        grid=(B // nseq, nt),
        in_specs=[row, head_spec, row, head_spec, vec(), vec(),
                  _resident((D_MODEL, LANES), lambda b, t: (0, 0)),
                  _resident((LANES, D_MODEL), lambda b, t: (0, 0)),
                  _resident((D_MODEL, D_MODEL), lambda b, t: (0, 0))],
        out_specs=row,
        out_shape=jax.ShapeDtypeStruct((B * L, D_MODEL), F32),
        compiler_params=_params(("parallel", "parallel")),
    )(x, y, g, bonus, p['rwkv_lnx_w'][i][None], p['rwkv_lnx_b'][i][None],
      p['head_onehot'], p['head_onehot_t'], p['rwkv_wo'])


def _pack_state(s):
    B = s.shape[0]
    s = s.reshape(B, N_PAIRS, 2, RWKV_HEAD, RWKV_HEAD)
    z = jnp.zeros_like(s[:, :, 0])
    top = jnp.concatenate([s[:, :, 0], z], axis=-1)
    bot = jnp.concatenate([z, s[:, :, 1]], axis=-1)
    return jnp.concatenate([top, bot], axis=-2)


def _unpack_state(s):
    B = s.shape[0]
    return jnp.stack([s[:, :, :RWKV_HEAD, :RWKV_HEAD], s[:, :, RWKV_HEAD:, RWKV_HEAD:]], axis=2).reshape(
        B, RWKV_H, RWKV_HEAD, RWKV_HEAD)


class _Stream:
    def __init__(self, x, pos0, caches):
        self.B, self.L, _ = x.shape
        self.x = x.reshape(self.B * self.L, D_MODEL)
        self.pos0 = pos0
        self.pool_c, self.k_c, self.v_c, self.shift_c, self.wkv_c = caches
        self.stepping = self.k_c is not None
        if self.stepping:
            self.bkt = _bucket_index(self.L, SWA_ROWS + self.L, SWA_ROWS)
        else:
            self.bkt = _bucket_index(CHUNK, WINDOW + CHUNK, WINDOW)
        self.new = {name: [] for name in ('pool', 'k', 'v', 'shift', 'wkv')}

    def mixer(self, l, p):
        B, L, i = self.B, self.L, l // 2
        gn = p['norm_mix'][l][None]
        if l % 2 == 0:
            if self.stepping:
                hist = jnp.pad(self.pool_c[i], ((0, 0), (POOL_HALO - POOL_HIST, 0), (0, 0)))
                k_cache = self.k_c[i].reshape(B, SWA_ROWS, KV_W)
                v_cache = self.v_c[i].reshape(B, SWA_ROWS, KV_W)
            else:
                hist = jnp.zeros((B, POOL_HALO, C_POOL), F32)
                k_cache = v_cache = jnp.zeros((B, WINDOW, KV_W), F32)
            self.x, u_tail, k_tail, v_tail = _even_layer(
                self.x, gn, p, i, hist, k_cache, v_cache, self.bkt, B, L, self.pos0,
                chunk=L if self.stepping else CHUNK, masked=not self.stepping)
            self.new['pool'].append(u_tail[:, -POOL_HIST:])
            self.new['k'].append(k_tail.reshape(B, SWA_ROWS, N_KV_HEADS, HEAD_DIM))
            self.new['v'].append(v_tail.reshape(B, SWA_ROWS, N_KV_HEADS, HEAD_DIM))
        else:
            if self.stepping:
                shift_prev = self.shift_c[i][:, None, :]
                s0 = _pack_state(self.wkv_c[i])
            else:
                shift_prev = jnp.zeros((B, 1, D_MODEL), F32)
                s0 = jnp.zeros((B, N_PAIRS, LANES, LANES), F32)
            r, lw, k, v, kk, a, g, hs = _rwkv_pre(self.x, shift_prev, gn, p, i, B, L)
            r_k = p['rwkv_r_k'][i].reshape(N_PAIRS, 1, LANES)
            y, bonus, s_new = _rwkv_scan(r, lw, k, v, kk, a, r_k, s0, B, L)
            self.x = _rwkv_post(self.x, y, g, bonus, p, i, B, L)
            self.new['shift'].append(hs[:, -1])
            self.new['wkv'].append(_unpack_state(s_new))

    def results(self):
        return (self.x.reshape(self.B, self.L, D_MODEL),) + tuple(
            jnp.stack(self.new[name]) for name in ('pool', 'k', 'v', 'shift', 'wkv'))


_FFN_WEIGHTS = ('ffn_w_gate', 'ffn_w_up', 'ffn_w_down')
_EVEN_WEIGHTS = ('w_in_even', 'w_out_even')
_ODD_WEIGHTS = ('rwkv_wr', 'rwkv_wk', 'rwkv_wv', 'rwkv_w1', 'rwkv_w2', 'rwkv_a1', 'rwkv_a2', 'rwkv_g1', 'rwkv_g2',
                'rwkv_wo')


def _trunk(long, short, p):
    depth = p['norm_mix'].shape[0]
    ffn_w = tuple(p[name][0, 0].astype(BF16) for name in _FFN_WEIGHTS)
    for l in range(depth):
        for j in range(2):
            n = 2 * l + j
            jobs = [(p[name], divmod(n + 1, 2)) for name in _FFN_WEIGHTS] if n + 1 < 2 * depth else []
            mixer_names = (_EVEN_WEIGHTS if l % 2 == 0 else _ODD_WEIGHTS) if j == 0 else ()
            jobs += [(p[name], (l // 2,)) for name in mixer_names]
            final_g = p['norm_final'][None] if n == 2 * depth - 1 else None
            long.x, short.x, done = _ffn(long.x, short.x, p['norm_ffn'][l, j][None], ffn_w, final_g, jobs)
            ffn_w = done[:len(_FFN_WEIGHTS)]
            if j == 0:
                mixer_w = dict(p, **dict(zip(mixer_names, done[len(done) - len(mixer_names):])))
                long.mixer(l, mixer_w)
                short.mixer(l, mixer_w)
    return long.results(), short.results()


def _prepare(p):
    p = dict(p)
    p['pool_w'] = p['pool_w'].astype(BF16)
    onehot = np.arange(D_MODEL)[:, None] // RWKV_HEAD == np.arange(LANES)[None, :]
    p['head_onehot'] = jnp.asarray(onehot, dtype=BF16)
    p['head_onehot_t'] = jnp.asarray(onehot.T, dtype=BF16)
    return p


def kernel(x_prompt, x_sample, cache_pool, cache_swa_k, cache_swa_v, state_shift, state_wkv, t5_table, norm_ffn, ffn_w_gate, ffn_w_up, ffn_w_down, norm_mix, w_in_even, pool_w, pool_scale, attn_sinks, w_out_even, rwkv_mu, rwkv_wr, rwkv_wk, rwkv_wv, rwkv_w0, rwkv_w1, rwkv_w2, rwkv_a0, rwkv_a1, rwkv_a2, rwkv_g1, rwkv_g2, rwkv_k_k, rwkv_k_a, rwkv_r_k, rwkv_lnx_w, rwkv_lnx_b, rwkv_wo, norm_final):
    p = _prepare(dict(
        t5_table=t5_table, norm_ffn=norm_ffn, ffn_w_gate=ffn_w_gate, ffn_w_up=ffn_w_up, ffn_w_down=ffn_w_down,
        norm_mix=norm_mix, w_in_even=w_in_even, pool_w=pool_w, pool_scale=pool_scale, attn_sinks=attn_sinks,
        w_out_even=w_out_even, rwkv_mu=rwkv_mu, rwkv_wr=rwkv_wr, rwkv_wk=rwkv_wk, rwkv_wv=rwkv_wv,
        rwkv_w0=rwkv_w0, rwkv_w1=rwkv_w1, rwkv_w2=rwkv_w2, rwkv_a0=rwkv_a0, rwkv_a1=rwkv_a1, rwkv_a2=rwkv_a2,
        rwkv_g1=rwkv_g1, rwkv_g2=rwkv_g2, rwkv_k_k=rwkv_k_k, rwkv_k_a=rwkv_k_a, rwkv_r_k=rwkv_r_k,
        rwkv_lnx_w=rwkv_lnx_w, rwkv_lnx_b=rwkv_lnx_b, rwkv_wo=rwkv_wo, norm_final=norm_final))
    prompt = _Stream(x_prompt, 0, (None, None, None, None, None))
    sample = _Stream(x_sample, PAST_LEN, (cache_pool, cache_swa_k, cache_swa_v, state_shift, state_wkv))
    (y_p, pool_p, k_p, v_p, shift_p, wkv_p), (y_s, pool_s, k_s, v_s, shift_s, wkv_s) = _trunk(prompt, sample, p)
    return (y_p, y_s, pool_p, pool_s, k_p, k_s, v_p, v_s, shift_p, shift_s, wkv_p, wkv_s)
```

```python
import functools
import math

import jax
import jax.numpy as jnp
import numpy as np
from jax import lax
from jax.experimental import pallas as pl
from jax.experimental.pallas import tpu as pltpu

F32 = jnp.float32
BF16 = jnp.bfloat16

D_MODEL = 1024
D_FF = 2816
NORM_EPS = 1e-6
CHUNK = 64
POOL_WINDOWS = (2, 4, 8, 16)
C_POOL = 512
POOL_GC = 128
POOL_HIST = 15
POOL_HALO = 16
HEAD_DIM = 64
N_Q_HEADS = 8
N_KV_HEADS = 2
GQA_GROUP = 4
WINDOW = 128
SWA_ROWS = 128
Q_W = 512
KV_W = 128
IN_EVEN = C_POOL + Q_W + 2 * KV_W
N_BUCKETS = 32
MAX_DISTANCE = 128
RWKV_HEAD = 64
RWKV_H = 16
N_PAIRS = RWKV_H // 2
LNX_EPS = 64e-5
PAST_LEN = 4096

LANES = 128
F32_SUBLANES = 8
BF16_SUBLANES = 16
VMEM_LIMIT_BYTES = 56 * 1024 * 1024

FFN_TILE = 1024
FFN_F_CHUNK = 256
TOKEN_TILE = 512
EVEN_CHUNKS_PER_STEP = 16
SCAN_CHUNKS_PER_STEP = (4, 2, 1)


def _params(sem):
    return pltpu.CompilerParams(dimension_semantics=sem, vmem_limit_bytes=VMEM_LIMIT_BYTES)


def _dot(a, b):
    return jnp.dot(a, b, preferred_element_type=F32)


def _dot_nt(a, b):
    return lax.dot_general(a, b, (((1,), (1,)), ((), ())), preferred_element_type=F32)


def _dot_tn(a, b):
    return lax.dot_general(a, b, (((0,), (0,)), ((), ())), preferred_element_type=F32)


def _rms(x, g):
    return x * lax.rsqrt(jnp.mean(x * x, axis=-1, keepdims=True) + NORM_EPS) * g


def _ffn_kernel(x_ref, xs_ref, g_ref, wg_ref, wu_ref, wd_ref, *rest, f_chunk, final, convert):
    rest = list(rest)
    gf_ref = rest.pop(0) if final else None
    src_refs = [rest.pop(0) for _ in range(convert)]
    o_ref, os_ref = rest.pop(0), rest.pop(0)
    dst_refs = [rest.pop(0) for _ in range(convert)]
    acc_ref, = rest
    for src, dst in zip(src_refs, dst_refs):
        dst[...] = src[...].astype(BF16)

    def rows(src_ref, dst_ref):
        n = src_ref.shape[0]
        x = src_ref[...]
        h = _rms(x, g_ref[...]).astype(BF16)
        for j in range(D_FF // f_chunk):
            sl = slice(j * f_chunk, (j + 1) * f_chunk)
            gate = _dot(h, wg_ref[:, sl])
            up = _dot(h, wu_ref[:, sl])
            act = (gate * jax.nn.sigmoid(gate) * up).astype(BF16)
            part = _dot(act, wd_ref[sl, :])
            if j == 0:
                acc_ref[:n] = part
            else:
                acc_ref[:n] += part
        y = x + 0.5 * acc_ref[:n]
        if final:
            y = _rms(y, gf_ref[...])
        dst_ref[...] = y

    rows(x_ref, o_ref)

    @pl.when(pl.program_id(0) == pl.num_programs(0) - 1)
    def _():
        rows(xs_ref, os_ref)


def _resident(shape, index_map):
    return pl.BlockSpec(shape, index_map, pipeline_mode=pl.Buffered(1))


def _slab_steps(rows, steps):
    return next(k for k in range(steps, 0, -1) if rows % (k * BF16_SUBLANES) == 0)


def _ffn(x, xs, g, weights, final_g=None, convert=()):
    T = x.shape[0]
    tm = min(FFN_TILE, T)
    steps = T // tm
    Ts = xs.shape[0]
    assert Ts <= tm
    final = final_g is not None
    in_specs = [pl.BlockSpec((tm, D_MODEL), lambda i: (i, 0)), _resident((Ts, D_MODEL), lambda i: (0, 0)),
                _resident((1, D_MODEL), lambda i: (0, 0))]
    in_specs += [_resident(w.shape, lambda i: (0, 0)) for w in weights]
    args = [x, xs, g, *weights]
    out_specs = [pl.BlockSpec((tm, D_MODEL), lambda i: (i, 0)), pl.BlockSpec((Ts, D_MODEL), lambda i: (0, 0))]
    out_shape = [jax.ShapeDtypeStruct((T, D_MODEL), F32), jax.ShapeDtypeStruct((Ts, D_MODEL), F32)]
    if final:
        in_specs.append(_resident((1, D_MODEL), lambda i: (0, 0)))
        args.append(final_g)
    for w, lead in convert:
        rows, cols = w.shape[len(lead):]
        k = _slab_steps(rows, steps)
        in_specs.append(pl.BlockSpec((None,) * len(lead) + (rows // k, cols),
                                     lambda i, k=k, lead=lead: (*lead, jnp.minimum(i, k - 1), 0)))
        out_specs.append(pl.BlockSpec((rows // k, cols), lambda i, k=k: (jnp.minimum(i, k - 1), 0)))
        out_shape.append(jax.ShapeDtypeStruct((rows, cols), BF16))
        args.append(w)
    outs = pl.pallas_call(
        functools.partial(_ffn_kernel, f_chunk=FFN_F_CHUNK, final=final, convert=len(convert)),
        grid=(steps,),
        in_specs=in_specs,
        out_specs=out_specs,
        out_shape=out_shape,
        scratch_shapes=[pltpu.VMEM((tm, D_MODEL), F32)],
        compiler_params=_params(("arbitrary",)),
    )(*args)
    return outs[0], outs[1], tuple(outs[2:])


def _build_bias(bkt_ref, tab_ref, bias_ref, lq):
    bkt = bkt_ref[...]
    for h in range(N_Q_HEADS):
        b = jnp.zeros(bkt.shape, F32)
        for n in range(N_BUCKETS):
            b = jnp.where(bkt == n, tab_ref[n, h], b)
        g, i = divmod(h, GQA_GROUP)
        bias_ref[g, :, i * lq:(i + 1) * lq] = b


def _group_queries(q, g):
    return jnp.concatenate([q[:, h * HEAD_DIM:(h + 1) * HEAD_DIM]
                            for h in range(g * GQA_GROUP, (g + 1) * GQA_GROUP)], axis=0) * (HEAD_DIM ** -0.5)


def _sink_rows(sink_ref, lq):
    lane = lax.broadcasted_iota(jnp.int32, (1, GQA_GROUP * lq), 1)
    rows = []
    for g in range(N_KV_HEADS):
        r = jnp.zeros((1, GQA_GROUP * lq), F32)
        for i in range(GQA_GROUP):
            r = jnp.where(lane // lq == i, sink_ref[g * GQA_GROUP + i], r)
        rows.append(r)
    return rows


def _attn_core(qs, ks, vs, biases, sinks, valids):
    ss = [_dot_nt(k, q) + b for q, k, b in zip(qs, ks, biases)]
    ss = [s if ok is None else jnp.where(ok, s, -1e30) for s, ok in zip(ss, valids)]
    ms = [jnp.maximum(jnp.max(s, axis=0, keepdims=True), sk) for s, sk in zip(ss, sinks)]
    ps = [jnp.exp(s - m) for s, m in zip(ss, ms)]
    invs = [1.0 / (jnp.sum(p, axis=0, keepdims=True) + jnp.exp(sk - m)) for p, sk, m in zip(ps, sinks, ms)]
    return [_dot_tn((p * r).astype(BF16), v) for p, r, v in zip(ps, invs, vs)]


def _smem():
    return pl.BlockSpec(memory_space=pltpu.SMEM)


def _even_layer_kernel(x_ref, gn_ref, win_ref, pw_ref, ps_ref, hist_ref, kc_ref, vc_ref, bkt_ref, tab_ref, sink_ref,
                       wop_ref, woa_ref, o_ref, utail_o, ktail_o, vtail_o, halo_ref, kprev_ref, vprev_ref, bias_ref,
                       *, rows, chunk, pos0, masked):
    t = pl.program_id(1)

    @pl.when(jnp.logical_and(pl.program_id(0) == 0, t == 0))
    def _():
        _build_bias(bkt_ref, tab_ref, bias_ref, chunk)

    @pl.when(t == 0)
    def _():
        halo_ref[...] = hist_ref[0]
        kprev_ref[...] = kc_ref[0]
        vprev_ref[...] = vc_ref[0]

    x = x_ref[...]
    z = _dot(_rms(x, gn_ref[...]).astype(BF16), win_ref[...])
    u = z[:, :C_POOL]
    q = z[:, C_POOL:C_POOL + Q_W].astype(BF16)
    k_all = jnp.concatenate([kprev_ref[...], z[:, C_POOL + Q_W:C_POOL + Q_W + KV_W]], axis=0)
    v_all = jnp.concatenate([vprev_ref[...], z[:, C_POOL + Q_W + KV_W:]], axis=0)

    ext = jnp.concatenate([halo_ref[...], u], axis=0)
    pos = pos0 + t * rows + lax.broadcasted_iota(jnp.int32, (rows, POOL_GC), 0)
    pooled = []
    for gi, w in enumerate(POOL_WINDOWS):
        sl = slice(gi * POOL_GC, (gi + 1) * POOL_GC)
        s = ext[:, sl]
        span = 1
        while span < w:
            s = s + pltpu.roll(s, span, 0)
            span *= 2
        cnt = jnp.minimum(w, pos + 1).astype(F32)
        pooled.append(_dot((s[POOL_HALO:] / cnt - u[:, sl]).astype(BF16), pw_ref[gi]))
    pool_out = (jnp.concatenate(pooled, axis=-1) * ps_ref[...]).astype(BF16)

    lk = WINDOW + chunk
    kb = k_all.astype(BF16)
    vb = v_all.astype(BF16)
    sink_rows = _sink_rows(sink_ref, chunk)
    first_pos = t * rows - WINDOW + lax.broadcasted_iota(jnp.int32, (lk, 1), 0)
    nchunk = rows // chunk
    qs, ks, vs, biases, sinks, valids = [], [], [], [], [], []
    for j in range(nchunk):
        qj = q[j * chunk:(j + 1) * chunk]
        for g in range(N_KV_HEADS):
            qs.append(_group_queries(qj, g))
            ks.append(kb[j * chunk:j * chunk + lk, g * HEAD_DIM:(g + 1) * HEAD_DIM])
            vs.append(vb[j * chunk:j * chunk + lk, g * HEAD_DIM:(g + 1) * HEAD_DIM])
            biases.append(bias_ref[g])
            sinks.append(sink_rows[g])
            valids.append(first_pos + j * chunk >= 0 if masked else None)
    outs = _attn_core(qs, ks, vs, biases, sinks, valids)
    att = jnp.concatenate(
        [jnp.concatenate([outs[j * N_KV_HEADS + g][i * chunk:(i + 1) * chunk]
                          for g in range(N_KV_HEADS) for i in range(GQA_GROUP)], axis=-1) for j in range(nchunk)],
        axis=0).astype(BF16)

    o_ref[...] = x + _dot(pool_out, wop_ref[...]) + _dot(att, woa_ref[...])

    halo_ref[...] = ext[rows:]
    kprev_ref[...] = k_all[rows:]
    vprev_ref[...] = v_all[rows:]
    utail_o[0] = ext[rows:]
    ktail_o[0] = k_all[rows:]
    vtail_o[0] = v_all[rows:]


def _even_layer(x, gn, p, i, hist, k_cache, v_cache, bkt, B, L, pos0, chunk, masked):
    rows = min(EVEN_CHUNKS_PER_STEP * chunk, L)
    nt = L // rows
    lk = WINDOW + chunk
    row = pl.BlockSpec((rows, D_MODEL), lambda b, t: (b * nt + t, 0))
    per_seq = lambda n, m: pl.BlockSpec((1, n, m), lambda b, t: (b, 0, 0))
    return pl.pallas_call(
        functools.partial(_even_layer_kernel, rows=rows, chunk=chunk, pos0=pos0, masked=masked),
        grid=(B, nt),
        in_specs=[row, _resident((1, D_MODEL), lambda b, t: (0, 0)),
                  _resident((D_MODEL, IN_EVEN), lambda b, t: (0, 0)),
                  _resident((None, len(POOL_WINDOWS), POOL_GC, POOL_GC), lambda b, t: (i, 0, 0, 0)),
                  _resident((1, C_POOL), lambda b, t: (0, 0)),
                  per_seq(POOL_HALO, C_POOL), per_seq(WINDOW, KV_W), per_seq(WINDOW, KV_W),
                  _resident((lk, chunk), lambda b, t: (0, 0)), _smem(), _smem(),
                  _resident((C_POOL, D_MODEL), lambda b, t: (0, 0)),
                  _resident((Q_W, D_MODEL), lambda b, t: (1, 0))],
        out_specs=[row, per_seq(POOL_HALO, C_POOL), per_seq(WINDOW, KV_W), per_seq(WINDOW, KV_W)],
        out_shape=[jax.ShapeDtypeStruct((B * L, D_MODEL), F32), jax.ShapeDtypeStruct((B, POOL_HALO, C_POOL), F32),
                   jax.ShapeDtypeStruct((B, WINDOW, KV_W), F32), jax.ShapeDtypeStruct((B, WINDOW, KV_W), F32)],
        scratch_shapes=[pltpu.VMEM((POOL_HALO, C_POOL), F32), pltpu.VMEM((WINDOW, KV_W), F32),
                        pltpu.VMEM((WINDOW, KV_W), F32), pltpu.VMEM((N_KV_HEADS, lk, GQA_GROUP * chunk), F32)],
        compiler_params=_params(("arbitrary", "arbitrary")),
    )(x, gn, p['w_in_even'], p['pool_w'], p['pool_scale'][i][None], hist, k_cache, v_cache, bkt,
      p['t5_table'], p['attn_sinks'][i], p['w_out_even'], p['w_out_even'])


def _t5_bucket(rel):
    half = N_BUCKETS // 2
    max_exact = half // 2
    side = jnp.where(rel > 0, half, 0)
    n = jnp.abs(rel)
    nf = jnp.maximum(n, max_exact).astype(F32)
    large = max_exact + (jnp.log(nf / max_exact) / math.log(MAX_DISTANCE / max_exact)
                         * (half - max_exact)).astype(jnp.int32)
    large = jnp.minimum(large, half - 1)
    return side + jnp.where(n < max_exact, n, large)


def _bucket_index(lq, lk, offset):
    rel = jnp.arange(lk)[:, None] - offset - jnp.arange(lq)[None, :]
    return _t5_bucket(rel).astype(jnp.int32)


def _rwkv_pre_kernel(x_ref, xp_ref, sh_ref, gn_ref, mu_ref, wr_ref, wk_ref, wv_ref, w1_ref, w2_ref,
                     a1_ref, a2_ref, g1_ref, g2_ref, w0_ref, a0_ref, kk_ref, ka_ref,
                     r_o, lw_o, k_o, v_o, kk_o, a_o, g_o, hs_o, *, tm, nseq):
    t = pl.program_id(1)
    lt = tm // nseq
    gn = gn_ref[...]
    h = _rms(x_ref[...], gn)
    row = lax.broadcasted_iota(jnp.int32, (tm, 1), 0)
    h_prev = pltpu.roll(h, 1, 0)
    for s in range(nseq):
        first_prev = sh_ref[s]
        if nseq == 1:
            first_prev = jnp.where(t == 0, first_prev, _rms(xp_ref[...], gn)[F32_SUBLANES - 1:])
        h_prev = jnp.where(row == s * lt, first_prev, h_prev)
    xx = h_prev - h

    hb = h.astype(BF16)
    xb = xx.astype(BF16)
    mu = mu_ref[...].astype(BF16)

    def mix(j):
        return hb + xb * mu[j:j + 1]

    zw = w0_ref[...] + _dot(jnp.tanh(_dot(mix(1), w1_ref[...])).astype(BF16), w2_ref[...])
    za = a0_ref[...] + _dot(_dot(mix(4), a1_ref[...]).astype(BF16), a2_ref[...])
    g = _dot(jax.nn.sigmoid(_dot(mix(5), g1_ref[...])).astype(BF16), g2_ref[...])
    r = _dot(mix(0), wr_ref[...])
    lw = -math.exp(-0.5) * jax.nn.sigmoid(zw)
    k = _dot(mix(2), wk_ref[...])
    a = jax.nn.sigmoid(za)
    v = _dot(mix(3), wv_ref[...])
    kk = k * kk_ref[...]
    k = k * (1.0 + (a - 1.0) * ka_ref[...])
    for s in range(nseq):
        rows = slice(s * lt, (s + 1) * lt)
        for hp in range(N_PAIRS):
            sl = slice(hp * LANES, (hp + 1) * LANES)
            for out, val in ((r_o, r), (lw_o, lw), (k_o, k), (v_o, v), (kk_o, kk), (a_o, a)):
                out[s, hp] = val[rows, sl]
        hs_o[s] = h[(s + 1) * lt - F32_SUBLANES:(s + 1) * lt]
    g_o[...] = g.astype(BF16)


def _rwkv_pre(x, shift_prev, gn, p, i, B, L):
    nseq = max(1, min(B, TOKEN_TILE // L))
    assert B % nseq == 0
    tm = min(TOKEN_TILE, L) * nseq
    lt = tm // nseq
    nt = L // lt
    vec = lambda: _resident((1, D_MODEL), lambda b, t: (0, 0))
    mat = lambda r, c: _resident((r, c), lambda b, t: (0, 0))
    lora = p['rwkv_w1'].shape[-1], p['rwkv_a1'].shape[-1], p['rwkv_g1'].shape[-1]
    head_spec = pl.BlockSpec((nseq, N_PAIRS, lt, LANES), lambda b, t: (b, 0, t, 0))
    head_shape = jax.ShapeDtypeStruct((B, N_PAIRS, L, LANES), F32)
    return pl.pallas_call(
        functools.partial(_rwkv_pre_kernel, tm=tm, nseq=nseq),
        grid=(B // nseq, nt),
        in_specs=[
            pl.BlockSpec((tm, D_MODEL), lambda b, t: (b * nt + t, 0)),
            pl.BlockSpec((F32_SUBLANES, D_MODEL),
                         lambda b, t: (jnp.maximum((b * nt + t) * (tm // F32_SUBLANES) - 1, 0), 0)),
            pl.BlockSpec((nseq, 1, D_MODEL), lambda b, t: (b, 0, 0)),
            vec(),
            _resident((None, 6, D_MODEL), lambda b, t: (i, 0, 0)),
            mat(D_MODEL, D_MODEL), mat(D_MODEL, D_MODEL), mat(D_MODEL, D_MODEL),
            mat(D_MODEL, lora[0]), mat(lora[0], D_MODEL),
            mat(D_MODEL, lora[1]), mat(lora[1], D_MODEL),
            mat(D_MODEL, lora[2]), mat(lora[2], D_MODEL),
            vec(), vec(), vec(), vec(),
        ],
        out_specs=[head_spec] * 6 + [
            pl.BlockSpec((tm, D_MODEL), lambda b, t: (b * nt + t, 0)),
            pl.BlockSpec((nseq, F32_SUBLANES, D_MODEL), lambda b, t: (b, 0, 0)),
        ],
        out_shape=[head_shape] * 6 + [
            jax.ShapeDtypeStruct((B * L, D_MODEL), BF16),
            jax.ShapeDtypeStruct((B, F32_SUBLANES, D_MODEL), F32),
        ],
        compiler_params=_params(("parallel", "arbitrary")),
    )(x, x, shift_prev, gn, p['rwkv_mu'], p['rwkv_wr'], p['rwkv_wk'], p['rwkv_wv'],
      p['rwkv_w1'], p['rwkv_w2'], p['rwkv_a1'], p['rwkv_a2'], p['rwkv_g1'], p['rwkv_g2'],
      p['rwkv_w0'][i][None], p['rwkv_a0'][i][None], p['rwkv_k_k'][i][None], p['rwkv_k_a'][i][None])


DIAG = 16


def _replication_matrix(n):
    src = np.arange(n)[:, None]
    dst = np.arange(n)[None, :]
    picks = np.concatenate([(src == (dst // DIAG) * DIAG + s) for s in range(DIAG - 1)], axis=1)
    return jnp.asarray(picks, dtype=BF16)


def _unit_lower_inverses(lows, c, rep_ref):
    n = 2 * c
    ti = lax.broadcasted_iota(jnp.int32, (c, n), 0)
    li = lax.broadcasted_iota(jnp.int32, (c, n), 1)
    si = li & (c - 1)
    head1 = li >= c

    def block_diag(x):
        return jnp.concatenate([jnp.where(head1, 0.0, x), jnp.where(head1, x, 0.0)], axis=0)

    if rep_ref is None:
        invs = [jnp.where(ti == si, 1.0, 0.0) + jnp.where(jnp.logical_and(ti == si + 1, (ti & 1) == 1), low, 0.0)
                for low in lows]
        m = 2
    else:
        pt = lax.broadcasted_iota(jnp.int32, (DIAG, n), 0)
        pl_ = lax.broadcasted_iota(jnp.int32, (DIAG, n), 1)
        blk = (pl_ & (c - 1)) // DIAG
        packed = []
        for low in lows:
            d = jnp.zeros((DIAG, n), F32)
            for i in range(c // DIAG):
                d = jnp.where(blk == i, low[i * DIAG:(i + 1) * DIAG], d)
            packed.append(d)
        rep = _dot(jnp.concatenate(packed, axis=0).astype(BF16), rep_ref[...])
        sols = [jnp.where(pt == (pl_ & (DIAG - 1)), 1.0, 0.0) for _ in lows]
        for s in range(DIAG - 1):
            sols = [sol + rep[p * DIAG:(p + 1) * DIAG, s * n:(s + 1) * n] * sol[s:s + 1]
                    for p, sol in enumerate(sols)]
        invs = [jnp.concatenate([jnp.where(blk == i, sol, 0.0) for i in range(c // DIAG)], axis=0) for sol in sols]
        m = DIAG
    while m < c:
        sh = m.bit_length() - 1
        sub = jnp.logical_and(jnp.logical_and((ti >> (sh + 1)) == (si >> (sh + 1)), ((ti >> sh) & 1) == 1),
                              ((si >> sh) & 1) == 0)
        diag = [block_diag(inv).astype(BF16) for inv in invs]
        half = [_dot(inv.astype(BF16), block_diag(jnp.where(sub, low, 0.0)).astype(BF16)).astype(BF16)
                for inv, low in zip(invs, lows)]
        invs = [inv + _dot(h, d) for inv, h, d in zip(invs, half, diag)]
        m *= 2
    return invs


def _scan_chunks(seqs, states, c, rep_ref):
    n = 2 * c
    row = lax.broadcasted_iota(jnp.int32, (c, LANES), 0)
    head1 = lax.broadcasted_iota(jnp.int32, (c, LANES), 1) >= RWKV_HEAD
    ti = lax.broadcasted_iota(jnp.int32, (c, n), 0)
    si = lax.broadcasted_iota(jnp.int32, (c, n), 1) & (c - 1)
    strict = ti > si
    incl = ti >= si
    pi = lax.broadcasted_iota(jnp.int32, (LANES, LANES), 0)
    pj = lax.broadcasted_iota(jnp.int32, (LANES, LANES), 1)
    same_head = (pi >= RWKV_HEAD) == (pj >= RWKV_HEAD)

    def stack(x):
        return jnp.concatenate([jnp.where(head1, 0.0, x), jnp.where(head1, x, 0.0)], axis=0)

    def head_sum(x):
        return jnp.where(head1, jnp.sum(jnp.where(head1, x, 0.0), axis=-1, keepdims=True),
                         jnp.sum(jnp.where(head1, 0.0, x), axis=-1, keepdims=True))

    lhs, rhs, v_st, v_bf, bk_end, decay, bonus = [], [], [], [], [], [], []
    for r, lw, k, v, kk, a, rk, _, _ in seqs:
        kk = kk * lax.rsqrt(jnp.maximum(head_sum(kk * kk), 1e-24))
        b = kk * a
        bonus.append(head_sum(r * k * rk) * v)
        cum = lw
        sh = 1
        while sh < c:
            cum = cum + jnp.where(row >= sh, pltpu.roll(cum, sh, 0), 0.0)
            sh *= 2
        tot = cum[c - 1:c]
        grow = jnp.exp(-cum)
        tail = jnp.exp(tot - cum)
        a_t = -kk * jnp.exp(cum - lw)
        r_t = r * jnp.exp(cum)
        lhs.append(jnp.concatenate([a_t, r_t], axis=0).astype(BF16))
        rhs.append(jnp.concatenate([stack(b * grow), stack(k * grow)], axis=0).astype(BF16))
        v_st.append(stack(v).astype(BF16))
        v_bf.append(v.astype(BF16))
        bk_end.append(jnp.concatenate([b * tail, k * tail], axis=0).astype(BF16))
        decay.append(jnp.exp(tot))
    grams = [_dot_nt(x, y) for x, y in zip(lhs, rhs)]
    lows = [jnp.where(strict, g[:c, :n], 0.0) for g in grams]
    m_rb = [jnp.where(incl, g[c:, :n], 0.0).astype(BF16) for g in grams]
    m_v = [jnp.concatenate([jnp.where(strict, g[:c, n:], 0.0), jnp.where(incl, g[c:, n:], 0.0)], axis=0).astype(BF16)
           for g in grams]
    invs = [inv.astype(BF16) for inv in _unit_lower_inverses(lows, c, rep_ref)]
    from_v = [_dot(m, vs) for m, vs in zip(m_v, v_st)]

    ys = []
    states = list(states)
    npair = len(states)
    for j in range(len(seqs) // npair):
        sl = slice(j * npair, (j + 1) * npair)
        from_state = [_dot_nt(x, s.astype(BF16)) for x, s in zip(lhs[sl], states)]
        rhs_sa = [stack(f[:c] + fv[:c]).astype(BF16) for f, fv in zip(from_state, from_v[sl])]
        sas = [_dot(inv, x) for inv, x in zip(invs[sl], rhs_sa)]
        ys += [f[c:] + fv[c:] + _dot(m, stack(sa).astype(BF16))
               for f, fv, m, sa in zip(from_state, from_v[sl], m_rb[sl], sas)]
        upd = [_dot_tn(jnp.concatenate([sa.astype(BF16), vb], axis=0), be)
               for sa, vb, be in zip(sas, v_bf[sl], bk_end[sl])]
        states = [s * d + jnp.where(same_head, u, 0.0) for s, d, u in zip(states, decay[sl], upd)]
    ds = [y - head_sum(y) * (1.0 / RWKV_HEAD) for y in ys]
    var = [head_sum(d * d) * (1.0 / RWKV_HEAD) for d in ds]
    zs = [d * lax.rsqrt(vr + LNX_EPS) * seq[7] + seq[8] + bo for d, vr, seq, bo in zip(ds, var, seqs, bonus)]
    return zs, states


def _rwkv_scan_kernel(r_ref, lw_ref, k_ref, v_ref, kk_ref, a_ref, rk_ref, lnw_ref, lnb_ref, s0_ref, *rest,
                      c, sub, nseq, vpu_diag):
    if vpu_diag:
        rep_ref, z_ref, so_ref, s_ref = rest
    else:
        rep_ref = None
        z_ref, so_ref, s_ref = rest
    ci = pl.program_id(1)
    chains = [(s, hp) for s in range(nseq) for hp in range(N_PAIRS)]

    @pl.when(ci == 0)
    def _():
        s_ref[...] = s0_ref[...]

    seqs = [tuple(ref[s, hp, j * c:(j + 1) * c] for ref in (r_ref, lw_ref, k_ref, v_ref, kk_ref, a_ref))
            + (rk_ref[hp], lnw_ref[hp], lnb_ref[hp]) for j in range(sub) for s, hp in chains]
    zs, new_states = _scan_chunks(seqs, [s_ref[s, hp] for s, hp in chains], c, rep_ref)
    for j in range(sub):
        for n, (s, hp) in enumerate(chains):
            z_ref[s, hp, j * c:(j + 1) * c] = zs[j * len(chains) + n].astype(BF16)
    for n, (s, hp) in enumerate(chains):
        s_ref[s, hp] = new_states[n]

    @pl.when(ci == pl.num_programs(1) - 1)
    def _():
        so_ref[...] = s_ref[...]


def _rwkv_scan(r, lw, k, v, kk, a, r_k, lnx_w, lnx_b, s0, B, L):
    c = min(CHUNK, L)
    nc = L // c
    sub = next(s for s in SCAN_CHUNKS_PER_STEP if nc % s == 0)
    nc //= sub
    nseq = next(s for s in SCAN_CHUNKS_PER_STEP if B % s == 0) if nc * sub == 1 else 1
    seq = pl.BlockSpec((nseq, N_PAIRS, sub * c, LANES), lambda bi, ci: (bi, 0, ci, 0))
    st = pl.BlockSpec((nseq, N_PAIRS, LANES, LANES), lambda bi, ci: (bi, 0, 0, 0))
    vpu_diag = 2 * c == LANES
    extra_specs, extra_args = [], []
    if vpu_diag:
        extra_specs = [_resident((LANES, (DIAG - 1) * LANES), lambda bi, ci: (0, 0))]
        extra_args = [_replication_matrix(LANES)]
    return pl.pallas_call(
        functools.partial(_rwkv_scan_kernel, c=c, sub=sub, nseq=nseq, vpu_diag=vpu_diag),
        grid=(B // nseq, nc),
        in_specs=[seq] * 6 + [_resident((N_PAIRS, 1, LANES), lambda bi, ci: (0, 0, 0))] * 3 + [st] + extra_specs,
        out_specs=[seq, st],
        out_shape=[jax.ShapeDtypeStruct((B, N_PAIRS, L, LANES), BF16),
                   jax.ShapeDtypeStruct((B, N_PAIRS, LANES, LANES), F32)],
        scratch_shapes=[pltpu.VMEM((nseq, N_PAIRS, LANES, LANES), F32)],
        compiler_params=_params(("parallel", "arbitrary")),
    )(r, lw, k, v, kk, a, r_k, lnx_w, lnx_b, s0, *extra_args)


def _rwkv_post_kernel(x_ref, z_ref, g_ref, wo_ref, o_ref, *, nseq):
    z = jnp.concatenate([jnp.concatenate([z_ref[s, hp] for hp in range(N_PAIRS)], axis=-1) for s in range(nseq)],
                        axis=0)
    o_ref[...] = x_ref[...] + _dot(z * g_ref[...], wo_ref[...])


def _rwkv_post(x, z, g, p, B, L):
    nseq = max(1, min(B, TOKEN_TILE // L))
    assert B % nseq == 0
    lt = min(TOKEN_TILE, L)
    tm = lt * nseq
    nt = L // lt
    row = pl.BlockSpec((tm, D_MODEL), lambda b, t: (b * nt + t, 0))
    return pl.pallas_call(
        functools.partial(_rwkv_post_kernel, nseq=nseq),
        grid=(B // nseq, nt),
        in_specs=[row, pl.BlockSpec((nseq, N_PAIRS, lt, LANES), lambda b, t: (b, 0, t, 0)), row,
                  _resident((D_MODEL, D_MODEL), lambda b, t: (0, 0))],
        out_specs=row,
        out_shape=jax.ShapeDtypeStruct((B * L, D_MODEL), F32),
        compiler_params=_params(("parallel", "parallel")),
    )(x, z, g, p['rwkv_wo'])


def _pack_state(s):
    B = s.shape[0]
    s = s.reshape(B, N_PAIRS, 2, RWKV_HEAD, RWKV_HEAD)
    z = jnp.zeros_like(s[:, :, 0])
    top = jnp.concatenate([s[:, :, 0], z], axis=-1)
    bot = jnp.concatenate([z, s[:, :, 1]], axis=-1)
    return jnp.concatenate([top, bot], axis=-2)


def _unpack_state(s):
    B = s.shape[0]
    return jnp.stack([s[:, :, :RWKV_HEAD, :RWKV_HEAD], s[:, :, RWKV_HEAD:, RWKV_HEAD:]], axis=2).reshape(
        B, RWKV_H, RWKV_HEAD, RWKV_HEAD)


class _Stream:
    def __init__(self, x, pos0, caches):
        self.B, self.L, _ = x.shape
        self.x = x.reshape(self.B * self.L, D_MODEL)
        self.pos0 = pos0
        self.pool_c, self.k_c, self.v_c, self.shift_c, self.wkv_c = caches
        self.stepping = self.k_c is not None
        if self.stepping:
            self.bkt = _bucket_index(self.L, SWA_ROWS + self.L, SWA_ROWS)
        else:
            self.bkt = _bucket_index(CHUNK, WINDOW + CHUNK, WINDOW)
        self.new = {name: [] for name in ('pool', 'k', 'v', 'shift', 'wkv')}

    def mixer(self, l, p):
        B, L, i = self.B, self.L, l // 2
        gn = p['norm_mix'][l][None]
        if l % 2 == 0:
            if self.stepping:
                hist = jnp.pad(self.pool_c[i], ((0, 0), (POOL_HALO - POOL_HIST, 0), (0, 0)))
                k_cache = self.k_c[i].reshape(B, SWA_ROWS, KV_W)
                v_cache = self.v_c[i].reshape(B, SWA_ROWS, KV_W)
            else:
                hist = jnp.zeros((B, POOL_HALO, C_POOL), F32)
                k_cache = v_cache = jnp.zeros((B, WINDOW, KV_W), F32)
            self.x, u_tail, k_tail, v_tail = _even_layer(
                self.x, gn, p, i, hist, k_cache, v_cache, self.bkt, B, L, self.pos0,
                chunk=L if self.stepping else CHUNK, masked=not self.stepping)
            self.new['pool'].append(u_tail[:, -POOL_HIST:])
            self.new['k'].append(k_tail.reshape(B, SWA_ROWS, N_KV_HEADS, HEAD_DIM))
            self.new['v'].append(v_tail.reshape(B, SWA_ROWS, N_KV_HEADS, HEAD_DIM))
        else:
            if self.stepping:
                shift_prev = self.shift_c[i][:, None, :]
                s0 = _pack_state(self.wkv_c[i])
            else:
                shift_prev = jnp.zeros((B, 1, D_MODEL), F32)
                s0 = jnp.zeros((B, N_PAIRS, LANES, LANES), F32)
            r, lw, k, v, kk, a, g, hs = _rwkv_pre(self.x, shift_prev, gn, p, i, B, L)
            per_pair = lambda name: p[name][i].reshape(N_PAIRS, 1, LANES)
            z, s_new = _rwkv_scan(r, lw, k, v, kk, a, per_pair('rwkv_r_k'), per_pair('rwkv_lnx_w'),
                                  per_pair('rwkv_lnx_b'), s0, B, L)
            self.x = _rwkv_post(self.x, z, g, p, B, L)
            self.new['shift'].append(hs[:, -1])
            self.new['wkv'].append(_unpack_state(s_new))

    def results(self):
        return (self.x.reshape(self.B, self.L, D_MODEL),) + tuple(
            jnp.stack(self.new[name]) for name in ('pool', 'k', 'v', 'shift', 'wkv'))


_FFN_WEIGHTS = ('ffn_w_gate', 'ffn_w_up', 'ffn_w_down')
_EVEN_WEIGHTS = ('w_in_even', 'w_out_even')
_ODD_WEIGHTS = ('rwkv_wr', 'rwkv_wk', 'rwkv_wv', 'rwkv_w1', 'rwkv_w2', 'rwkv_a1', 'rwkv_a2', 'rwkv_g1', 'rwkv_g2',
                'rwkv_wo')


def _trunk(long, short, p):
    depth = p['norm_mix'].shape[0]
    ffn_w = tuple(p[name][0, 0].astype(BF16) for name in _FFN_WEIGHTS)
    for l in range(depth):
        for j in range(2):
            n = 2 * l + j
            jobs = [(p[name], divmod(n + 1, 2)) for name in _FFN_WEIGHTS] if n + 1 < 2 * depth else []
            mixer_names = (_EVEN_WEIGHTS if l % 2 == 0 else _ODD_WEIGHTS) if j == 0 else ()
            jobs += [(p[name], (l // 2,)) for name in mixer_names]
            final_g = p['norm_final'][None] if n == 2 * depth - 1 else None
            long.x, short.x, done = _ffn(long.x, short.x, p['norm_ffn'][l, j][None], ffn_w, final_g, jobs)
            ffn_w = done[:len(_FFN_WEIGHTS)]
            if j == 0:
                mixer_w = dict(p, **dict(zip(mixer_names, done[len(done) - len(mixer_names):])))
                long.mixer(l, mixer_w)
                short.mixer(l, mixer_w)
    return long.results(), short.results()


def _prepare(p):
    p = dict(p)
    p['pool_w'] = p['pool_w'].astype(BF16)
    return p


def kernel(x_prompt, x_sample, cache_pool, cache_swa_k, cache_swa_v, state_shift, state_wkv, t5_table, norm_ffn, ffn_w_gate, ffn_w_up, ffn_w_down, norm_mix, w_in_even, pool_w, pool_scale, attn_sinks, w_out_even, rwkv_mu, rwkv_wr, rwkv_wk, rwkv_wv, rwkv_w0, rwkv_w1, rwkv_w2, rwkv_a0, rwkv_a1, rwkv_a2, rwkv_g1, rwkv_g2, rwkv_k_k, rwkv_k_a, rwkv_r_k, rwkv_lnx_w, rwkv_lnx_b, rwkv_wo, norm_final):
    p = _prepare(dict(
        t5_table=t5_table, norm_ffn=norm_ffn, ffn_w_gate=ffn_w_gate, ffn_w_up=ffn_w_up, ffn_w_down=ffn_w_down,
        norm_mix=norm_mix, w_in_even=w_in_even, pool_w=pool_w, pool_scale=pool_scale, attn_sinks=attn_sinks,
        w_out_even=w_out_even, rwkv_mu=rwkv_mu, rwkv_wr=rwkv_wr, rwkv_wk=rwkv_wk, rwkv_wv=rwkv_wv,
        rwkv_w0=rwkv_w0, rwkv_w1=rwkv_w1, rwkv_w2=rwkv_w2, rwkv_a0=rwkv_a0, rwkv_a1=rwkv_a1, rwkv_a2=rwkv_a2,
        rwkv_g1=rwkv_g1, rwkv_g2=rwkv_g2, rwkv_k_k=rwkv_k_k, rwkv_k_a=rwkv_k_a, rwkv_r_k=rwkv_r_k,
        rwkv_lnx_w=rwkv_lnx_w, rwkv_lnx_b=rwkv_lnx_b, rwkv_wo=rwkv_wo, norm_final=norm_final))
    prompt = _Stream(x_prompt, 0, (None, None, None, None, None))
    sample = _Stream(x_sample, PAST_LEN, (cache_pool, cache_swa_k, cache_swa_v, state_shift, state_wkv))
    (y_p, pool_p, k_p, v_p, shift_p, wkv_p), (y_s, pool_s, k_s, v_s, shift_s, wkv_s) = _trunk(prompt, sample, p)
    return (y_p, y_s, pool_p, pool_s, k_p, k_s, v_p, v_s, shift_p, shift_s, wkv_p, wkv_s)
```

```python
import functools
import math

import jax
import jax.numpy as jnp
import numpy as np
from jax import lax
from jax.experimental import pallas as pl
from jax.experimental.pallas import tpu as pltpu

F32 = jnp.float32
BF16 = jnp.bfloat16

D_MODEL = 1024
D_FF = 2816
NORM_EPS = 1e-6
CHUNK = 64
POOL_WINDOWS = (2, 4, 8, 16)
C_POOL = 512
POOL_GC = 128
POOL_HIST = 15
POOL_HALO = 16
HEAD_DIM = 64
N_Q_HEADS = 8
N_KV_HEADS = 2
GQA_GROUP = 4
WINDOW = 128
SWA_ROWS = 128
Q_W = 512
KV_W = 128
IN_EVEN = C_POOL + Q_W + 2 * KV_W
N_BUCKETS = 32
MAX_DISTANCE = 128
RWKV_HEAD = 64
RWKV_H = 16
N_PAIRS = RWKV_H // 2
LNX_EPS = 64e-5
PAST_LEN = 4096

LANES = 128
F32_SUBLANES = 8
BF16_SUBLANES = 16
VMEM_LIMIT_BYTES = 56 * 1024 * 1024

FFN_TILE = 1024
FFN_F_CHUNK = 256
TOKEN_TILE = 512
EVEN_CHUNKS_PER_STEP = 16
SCAN_CHUNKS_PER_STEP = (4, 2, 1)


def _params(sem):
    return pltpu.CompilerParams(dimension_semantics=sem, vmem_limit_bytes=VMEM_LIMIT_BYTES)


def _dot(a, b):
    return jnp.dot(a, b, preferred_element_type=F32)


def _dot_nt(a, b):
    return lax.dot_general(a, b, (((1,), (1,)), ((), ())), preferred_element_type=F32)


def _dot_tn(a, b):
    return lax.dot_general(a, b, (((0,), (0,)), ((), ())), preferred_element_type=F32)


def _rms(x, g):
    return x * lax.rsqrt(jnp.mean(x * x, axis=-1, keepdims=True) + NORM_EPS) * g


def _ffn_kernel(x_ref, xs_ref, g_ref, wg_ref, wu_ref, wd_ref, *rest, f_chunk, final, convert, mixer_out):
    rest = list(rest)
    gf_ref = rest.pop(0) if final else None
    z_ref, gate_ref, wo_ref = (rest.pop(0), rest.pop(0), rest.pop(0)) if mixer_out else (None,) * 3
    src_refs = [rest.pop(0) for _ in range(convert)]
    o_ref, os_ref = rest.pop(0), rest.pop(0)
    dst_refs = [rest.pop(0) for _ in range(convert)]
    acc_ref, = rest
    for src, dst in zip(src_refs, dst_refs):
        dst[...] = src[...].astype(BF16)

    def rows(x, dst_ref):
        n = x.shape[0]
        h = _rms(x, g_ref[...]).astype(BF16)
        for j in range(D_FF // f_chunk):
            sl = slice(j * f_chunk, (j + 1) * f_chunk)
            gate = _dot(h, wg_ref[:, sl])
            up = _dot(h, wu_ref[:, sl])
            act = (gate * jax.nn.sigmoid(gate) * up).astype(BF16)
            part = _dot(act, wd_ref[sl, :])
            if j == 0:
                acc_ref[:n] = part
            else:
                acc_ref[:n] += part
        y = x + 0.5 * acc_ref[:n]
        if final:
            y = _rms(y, gf_ref[...])
        dst_ref[...] = y

    x = x_ref[...]
    if mixer_out:
        z = jnp.concatenate([z_ref[0, hp] for hp in range(N_PAIRS)], axis=-1)
        x = x + _dot(z * gate_ref[...], wo_ref[...])
    rows(x, o_ref)

    @pl.when(pl.program_id(0) == pl.num_programs(0) - 1)
    def _():
        rows(xs_ref[...], os_ref)


def _resident(shape, index_map):
    return pl.BlockSpec(shape, index_map, pipeline_mode=pl.Buffered(1))


def _slab_steps(rows, steps):
    return next(k for k in range(steps, 0, -1) if rows % (k * BF16_SUBLANES) == 0)


def _ffn(x, xs, g, weights, final_g=None, convert=(), mixer_out=None):
    T = x.shape[0]
    tm = min(FFN_TILE if mixer_out is None else TOKEN_TILE, T)
    steps = T // tm
    Ts = xs.shape[0]
    assert Ts <= tm
    final = final_g is not None
    in_specs = [pl.BlockSpec((tm, D_MODEL), lambda i: (i, 0)), _resident((Ts, D_MODEL), lambda i: (0, 0)),
                _resident((1, D_MODEL), lambda i: (0, 0))]
    in_specs += [_resident(w.shape, lambda i: (0, 0)) for w in weights]
    args = [x, xs, g, *weights]
    out_specs = [pl.BlockSpec((tm, D_MODEL), lambda i: (i, 0)), pl.BlockSpec((Ts, D_MODEL), lambda i: (0, 0))]
    out_shape = [jax.ShapeDtypeStruct((T, D_MODEL), F32), jax.ShapeDtypeStruct((Ts, D_MODEL), F32)]
    if final:
        in_specs.append(_resident((1, D_MODEL), lambda i: (0, 0)))
        args.append(final_g)
    if mixer_out is not None:
        z, gate, wo, L = mixer_out
        nt = L // tm
        in_specs += [pl.BlockSpec((1, N_PAIRS, tm, LANES), lambda i: (i // nt, 0, i % nt, 0)),
                     pl.BlockSpec((tm, D_MODEL), lambda i: (i, 0)), _resident(wo.shape, lambda i: (0, 0))]
        args += [z, gate, wo]
    for w, lead in convert:
        rows, cols = w.shape[len(lead):]
        k = _slab_steps(rows, steps)
        in_specs.append(pl.BlockSpec((None,) * len(lead) + (rows // k, cols),
                                     lambda i, k=k, lead=lead: (*lead, jnp.minimum(i, k - 1), 0)))
        out_specs.append(pl.BlockSpec((rows // k, cols), lambda i, k=k: (jnp.minimum(i, k - 1), 0)))
        out_shape.append(jax.ShapeDtypeStruct((rows, cols), BF16))
        args.append(w)
    outs = pl.pallas_call(
        functools.partial(_ffn_kernel, f_chunk=FFN_F_CHUNK, final=final, convert=len(convert),
                          mixer_out=mixer_out is not None),
        grid=(steps,),
        in_specs=in_specs,
        out_specs=out_specs,
        out_shape=out_shape,
        scratch_shapes=[pltpu.VMEM((tm, D_MODEL), F32)],
        compiler_params=_params(("arbitrary",)),
    )(*args)
    return outs[0], outs[1], tuple(outs[2:])


def _build_bias(bkt_ref, tab_ref, bias_ref, lq):
    bkt = bkt_ref[...]
    for h in range(N_Q_HEADS):
        b = jnp.zeros(bkt.shape, F32)
        for n in range(N_BUCKETS):
            b = jnp.where(bkt == n, tab_ref[n, h], b)
        g, i = divmod(h, GQA_GROUP)
        bias_ref[g, :, i * lq:(i + 1) * lq] = b


def _group_queries(q, g):
    return jnp.concatenate([q[:, h * HEAD_DIM:(h + 1) * HEAD_DIM]
                            for h in range(g * GQA_GROUP, (g + 1) * GQA_GROUP)], axis=0) * (HEAD_DIM ** -0.5)


def _sink_rows(sink_ref, lq):
    lane = lax.broadcasted_iota(jnp.int32, (1, GQA_GROUP * lq), 1)
    rows = []
    for g in range(N_KV_HEADS):
        r = jnp.zeros((1, GQA_GROUP * lq), F32)
        for i in range(GQA_GROUP):
            r = jnp.where(lane // lq == i, sink_ref[g * GQA_GROUP + i], r)
        rows.append(r)
    return rows


def _attn_core(qs, ks, vs, biases, sinks, valids):
    ss = [_dot_nt(k, q) + b for q, k, b in zip(qs, ks, biases)]
    ss = [s if ok is None else jnp.where(ok, s, -1e30) for s, ok in zip(ss, valids)]
    ms = [jnp.maximum(jnp.max(s, axis=0, keepdims=True), sk) for s, sk in zip(ss, sinks)]
    ps = [jnp.exp(s - m) for s, m in zip(ss, ms)]
    invs = [1.0 / (jnp.sum(p, axis=0, keepdims=True) + jnp.exp(sk - m)) for p, sk, m in zip(ps, sinks, ms)]
    return [_dot_tn((p * r).astype(BF16), v) for p, r, v in zip(ps, invs, vs)]


def _smem():
    return pl.BlockSpec(memory_space=pltpu.SMEM)


def _even_layer_kernel(x_ref, gn_ref, win_ref, pw_ref, ps_ref, hist_ref, kc_ref, vc_ref, bkt_ref, tab_ref, sink_ref,
                       wop_ref, woa_ref, o_ref, utail_o, ktail_o, vtail_o, halo_ref, kprev_ref, vprev_ref, bias_ref,
                       *, rows, chunk, pos0, masked):
    t = pl.program_id(1)

    @pl.when(jnp.logical_and(pl.program_id(0) == 0, t == 0))
    def _():
        _build_bias(bkt_ref, tab_ref, bias_ref, chunk)

    @pl.when(t == 0)
    def _():
        halo_ref[...] = hist_ref[0]
        kprev_ref[...] = kc_ref[0]
        vprev_ref[...] = vc_ref[0]

    x = x_ref[...]
    z = _dot(_rms(x, gn_ref[...]).astype(BF16), win_ref[...])
    u = z[:, :C_POOL]
    q = z[:, C_POOL:C_POOL + Q_W].astype(BF16)
    k_all = jnp.concatenate([kprev_ref[...], z[:, C_POOL + Q_W:C_POOL + Q_W + KV_W]], axis=0)
    v_all = jnp.concatenate([vprev_ref[...], z[:, C_POOL + Q_W + KV_W:]], axis=0)

    ext = jnp.concatenate([halo_ref[...], u], axis=0)
    pos = pos0 + t * rows + lax.broadcasted_iota(jnp.int32, (rows, POOL_GC), 0)
    pooled = []
    for gi, w in enumerate(POOL_WINDOWS):
        sl = slice(gi * POOL_GC, (gi + 1) * POOL_GC)
        s = ext[:, sl]
        span = 1
        while span < w:
            s = s + pltpu.roll(s, span, 0)
            span *= 2
        cnt = jnp.minimum(w, pos + 1).astype(F32)
        pooled.append(_dot((s[POOL_HALO:] / cnt - u[:, sl]).astype(BF16), pw_ref[gi]))
    pool_out = (jnp.concatenate(pooled, axis=-1) * ps_ref[...]).astype(BF16)

    lk = WINDOW + chunk
    kb = k_all.astype(BF16)
    vb = v_all.astype(BF16)
    sink_rows = _sink_rows(sink_ref, chunk)
    first_pos = t * rows - WINDOW + lax.broadcasted_iota(jnp.int32, (lk, 1), 0)
    nchunk = rows // chunk
    qs, ks, vs, biases, sinks, valids = [], [], [], [], [], []
    for j in range(nchunk):
        qj = q[j * chunk:(j + 1) * chunk]
        for g in range(N_KV_HEADS):
            qs.append(_group_queries(qj, g))
            ks.append(kb[j * chunk:j * chunk + lk, g * HEAD_DIM:(g + 1) * HEAD_DIM])
            vs.append(vb[j * chunk:j * chunk + lk, g * HEAD_DIM:(g + 1) * HEAD_DIM])
            biases.append(bias_ref[g])
            sinks.append(sink_rows[g])
            valids.append(first_pos + j * chunk >= 0 if masked else None)
    outs = _attn_core(qs, ks, vs, biases, sinks, valids)
    att = jnp.concatenate(
        [jnp.concatenate([outs[j * N_KV_HEADS + g][i * chunk:(i + 1) * chunk]
                          for g in range(N_KV_HEADS) for i in range(GQA_GROUP)], axis=-1) for j in range(nchunk)],
        axis=0).astype(BF16)

    o_ref[...] = x + _dot(pool_out, wop_ref[...]) + _dot(att, woa_ref[...])

    halo_ref[...] = ext[rows:]
    kprev_ref[...] = k_all[rows:]
    vprev_ref[...] = v_all[rows:]
    utail_o[0] = ext[rows:]
    ktail_o[0] = k_all[rows:]
    vtail_o[0] = v_all[rows:]


def _even_layer(x, gn, p, i, hist, k_cache, v_cache, bkt, B, L, pos0, chunk, masked):
    rows = min(EVEN_CHUNKS_PER_STEP * chunk, L)
    nt = L // rows
    lk = WINDOW + chunk
    row = pl.BlockSpec((rows, D_MODEL), lambda b, t: (b * nt + t, 0))
    per_seq = lambda n, m: pl.BlockSpec((1, n, m), lambda b, t: (b, 0, 0))
    return pl.pallas_call(
        functools.partial(_even_layer_kernel, rows=rows, chunk=chunk, pos0=pos0, masked=masked),
        grid=(B, nt),
        in_specs=[row, _resident((1, D_MODEL), lambda b, t: (0, 0)),
                  _resident((D_MODEL, IN_EVEN), lambda b, t: (0, 0)),
                  _resident((None, len(POOL_WINDOWS), POOL_GC, POOL_GC), lambda b, t: (i, 0, 0, 0)),
                  _resident((1, C_POOL), lambda b, t: (0, 0)),
                  per_seq(POOL_HALO, C_POOL), per_seq(WINDOW, KV_W), per_seq(WINDOW, KV_W),
                  _resident((lk, chunk), lambda b, t: (0, 0)), _smem(), _smem(),
                  _resident((C_POOL, D_MODEL), lambda b, t: (0, 0)),
                  _resident((Q_W, D_MODEL), lambda b, t: (1, 0))],
        out_specs=[row, per_seq(POOL_HALO, C_POOL), per_seq(WINDOW, KV_W), per_seq(WINDOW, KV_W)],
        out_shape=[jax.ShapeDtypeStruct((B * L, D_MODEL), F32), jax.ShapeDtypeStruct((B, POOL_HALO, C_POOL), F32),
                   jax.ShapeDtypeStruct((B, WINDOW, KV_W), F32), jax.ShapeDtypeStruct((B, WINDOW, KV_W), F32)],
        scratch_shapes=[pltpu.VMEM((POOL_HALO, C_POOL), F32), pltpu.VMEM((WINDOW, KV_W), F32),
                        pltpu.VMEM((WINDOW, KV_W), F32), pltpu.VMEM((N_KV_HEADS, lk, GQA_GROUP * chunk), F32)],
        compiler_params=_params(("arbitrary", "arbitrary")),
    )(x, gn, p['w_in_even'], p['pool_w'], p['pool_scale'][i][None], hist, k_cache, v_cache, bkt,
      p['t5_table'], p['attn_sinks'][i], p['w_out_even'], p['w_out_even'])


def _t5_bucket(rel):
    half = N_BUCKETS // 2
    max_exact = half // 2
    side = jnp.where(rel > 0, half, 0)
    n = jnp.abs(rel)
    nf = jnp.maximum(n, max_exact).astype(F32)
    large = max_exact + (jnp.log(nf / max_exact) / math.log(MAX_DISTANCE / max_exact)
                         * (half - max_exact)).astype(jnp.int32)
    large = jnp.minimum(large, half - 1)
    return side + jnp.where(n < max_exact, n, large)


def _bucket_index(lq, lk, offset):
    rel = jnp.arange(lk)[:, None] - offset - jnp.arange(lq)[None, :]
    return _t5_bucket(rel).astype(jnp.int32)


def _rwkv_pre_kernel(x_ref, xp_ref, sh_ref, gn_ref, mu_ref, wr_ref, wk_ref, wv_ref, w1_ref, w2_ref,
                     a1_ref, a2_ref, g1_ref, g2_ref, w0_ref, a0_ref, kk_ref, ka_ref,
                     r_o, lw_o, k_o, v_o, kk_o, a_o, g_o, hs_o, *, tm, nseq):
    t = pl.program_id(1)
    lt = tm // nseq
    gn = gn_ref[...]
    h = _rms(x_ref[...], gn)
    row = lax.broadcasted_iota(jnp.int32, (tm, 1), 0)
    h_prev = pltpu.roll(h, 1, 0)
    for s in range(nseq):
        first_prev = sh_ref[s]
        if nseq == 1:
            first_prev = jnp.where(t == 0, first_prev, _rms(xp_ref[...], gn)[F32_SUBLANES - 1:])
        h_prev = jnp.where(row == s * lt, first_prev, h_prev)
    xx = h_prev - h

    hb = h.astype(BF16)
    xb = xx.astype(BF16)
    mu = mu_ref[...].astype(BF16)

    def mix(j):
        return hb + xb * mu[j:j + 1]

    zw = w0_ref[...] + _dot(jnp.tanh(_dot(mix(1), w1_ref[...])).astype(BF16), w2_ref[...])
    za = a0_ref[...] + _dot(_dot(mix(4), a1_ref[...]).astype(BF16), a2_ref[...])
    g = _dot(jax.nn.sigmoid(_dot(mix(5), g1_ref[...])).astype(BF16), g2_ref[...])
    r = _dot(mix(0), wr_ref[...])
    lw = -math.exp(-0.5) * jax.nn.sigmoid(zw)
    k = _dot(mix(2), wk_ref[...])
    a = jax.nn.sigmoid(za)
    v = _dot(mix(3), wv_ref[...])
    kk = k * kk_ref[...]
    k = k * (1.0 + (a - 1.0) * ka_ref[...])
    for s in range(nseq):
        rows = slice(s * lt, (s + 1) * lt)
        for hp in range(N_PAIRS):
            sl = slice(hp * LANES, (hp + 1) * LANES)
            for out, val in ((r_o, r), (lw_o, lw), (k_o, k), (v_o, v), (kk_o, kk), (a_o, a)):
                out[s, hp] = val[rows, sl]
        hs_o[s] = h[(s + 1) * lt - F32_SUBLANES:(s + 1) * lt]
    g_o[...] = g.astype(BF16)


def _rwkv_pre(x, shift_prev, gn, p, i, B, L):
    nseq = max(1, min(B, TOKEN_TILE // L))
    assert B % nseq == 0
    tm = min(TOKEN_TILE, L) * nseq
    lt = tm // nseq
    nt = L // lt
    vec = lambda: _resident((1, D_MODEL), lambda b, t: (0, 0))
    mat = lambda r, c: _resident((r, c), lambda b, t: (0, 0))
    lora = p['rwkv_w1'].shape[-1], p['rwkv_a1'].shape[-1], p['rwkv_g1'].shape[-1]
    head_spec = pl.BlockSpec((nseq, N_PAIRS, lt, LANES), lambda b, t: (b, 0, t, 0))
    head_shape = jax.ShapeDtypeStruct((B, N_PAIRS, L, LANES), F32)
    return pl.pallas_call(
        functools.partial(_rwkv_pre_kernel, tm=tm, nseq=nseq),
        grid=(B // nseq, nt),
        in_specs=[
            pl.BlockSpec((tm, D_MODEL), lambda b, t: (b * nt + t, 0)),
            pl.BlockSpec((F32_SUBLANES, D_MODEL),
                         lambda b, t: (jnp.maximum((b * nt + t) * (tm // F32_SUBLANES) - 1, 0), 0)),
            pl.BlockSpec((nseq, 1, D_MODEL), lambda b, t: (b, 0, 0)),
            vec(),
            _resident((None, 6, D_MODEL), lambda b, t: (i, 0, 0)),
            mat(D_MODEL, D_MODEL), mat(D_MODEL, D_MODEL), mat(D_MODEL, D_MODEL),
            mat(D_MODEL, lora[0]), mat(lora[0], D_MODEL),
            mat(D_MODEL, lora[1]), mat(lora[1], D_MODEL),
            mat(D_MODEL, lora[2]), mat(lora[2], D_MODEL),
            vec(), vec(), vec(), vec(),
        ],
        out_specs=[head_spec] * 6 + [
            pl.BlockSpec((tm, D_MODEL), lambda b, t: (b * nt + t, 0)),
            pl.BlockSpec((nseq, F32_SUBLANES, D_MODEL), lambda b, t: (b, 0, 0)),
        ],
        out_shape=[head_shape] * 6 + [
            jax.ShapeDtypeStruct((B * L, D_MODEL), BF16),
            jax.ShapeDtypeStruct((B, F32_SUBLANES, D_MODEL), F32),
        ],
        compiler_params=_params(("parallel", "arbitrary")),
    )(x, x, shift_prev, gn, p['rwkv_mu'], p['rwkv_wr'], p['rwkv_wk'], p['rwkv_wv'],
      p['rwkv_w1'], p['rwkv_w2'], p['rwkv_a1'], p['rwkv_a2'], p['rwkv_g1'], p['rwkv_g2'],
      p['rwkv_w0'][i][None], p['rwkv_a0'][i][None], p['rwkv_k_k'][i][None], p['rwkv_k_a'][i][None])


DIAG = 16


def _replication_matrix(n):
    src = np.arange(n)[:, None]
    dst = np.arange(n)[None, :]
    picks = np.concatenate([(src == (dst // DIAG) * DIAG + s) for s in range(DIAG - 1)], axis=1)
    return jnp.asarray(picks, dtype=BF16)


def _unit_lower_inverses(lows, c, rep_ref):
    n = 2 * c
    ti = lax.broadcasted_iota(jnp.int32, (c, n), 0)
    li = lax.broadcasted_iota(jnp.int32, (c, n), 1)
    si = li & (c - 1)
    head1 = li >= c

    def block_diag(x):
        return jnp.concatenate([jnp.where(head1, 0.0, x), jnp.where(head1, x, 0.0)], axis=0)

    if rep_ref is None:
        invs = [jnp.where(ti == si, 1.0, 0.0) + jnp.where(jnp.logical_and(ti == si + 1, (ti & 1) == 1), low, 0.0)
                for low in lows]
        m = 2
    else:
        pt = lax.broadcasted_iota(jnp.int32, (DIAG, n), 0)
        pl_ = lax.broadcasted_iota(jnp.int32, (DIAG, n), 1)
        blk = (pl_ & (c - 1)) // DIAG
        packed = []
        for low in lows:
            d = jnp.zeros((DIAG, n), F32)
            for i in range(c // DIAG):
                d = jnp.where(blk == i, low[i * DIAG:(i + 1) * DIAG], d)
            packed.append(d)
        rep = _dot(jnp.concatenate(packed, axis=0).astype(BF16), rep_ref[...])
        sols = [jnp.where(pt == (pl_ & (DIAG - 1)), 1.0, 0.0) for _ in lows]
        for s in range(DIAG - 1):
            sols = [sol + rep[p * DIAG:(p + 1) * DIAG, s * n:(s + 1) * n] * sol[s:s + 1]
                    for p, sol in enumerate(sols)]
        invs = [jnp.concatenate([jnp.where(blk == i, sol, 0.0) for i in range(c // DIAG)], axis=0) for sol in sols]
        m = DIAG
    while m < c:
        sh = m.bit_length() - 1
        sub = jnp.logical_and(jnp.logical_and((ti >> (sh + 1)) == (si >> (sh + 1)), ((ti >> sh) & 1) == 1),
                              ((si >> sh) & 1) == 0)
        diag = [block_diag(inv).astype(BF16) for inv in invs]
        half = [_dot(inv.astype(BF16), block_diag(jnp.where(sub, low, 0.0)).astype(BF16)).astype(BF16)
                for inv, low in zip(invs, lows)]
        invs = [inv + _dot(h, d) for inv, h, d in zip(invs, half, diag)]
        m *= 2
    return invs


def _scan_chunks(seqs, states, c, rep_ref):
    n = 2 * c
    row = lax.broadcasted_iota(jnp.int32, (c, LANES), 0)
    head1 = lax.broadcasted_iota(jnp.int32, (c, LANES), 1) >= RWKV_HEAD
    ti = lax.broadcasted_iota(jnp.int32, (c, n), 0)
    si = lax.broadcasted_iota(jnp.int32, (c, n), 1) & (c - 1)
    strict = ti > si
    incl = ti >= si
    pi = lax.broadcasted_iota(jnp.int32, (LANES, LANES), 0)
    pj = lax.broadcasted_iota(jnp.int32, (LANES, LANES), 1)
    same_head = (pi >= RWKV_HEAD) == (pj >= RWKV_HEAD)

    def stack(x):
        return jnp.concatenate([jnp.where(head1, 0.0, x), jnp.where(head1, x, 0.0)], axis=0)

    def head_sum(x):
        return jnp.where(head1, jnp.sum(jnp.where(head1, x, 0.0), axis=-1, keepdims=True),
                         jnp.sum(jnp.where(head1, 0.0, x), axis=-1, keepdims=True))

    lhs, rhs, v_st, v_bf, bk_end, decay, bonus = [], [], [], [], [], [], []
    for r, lw, k, v, kk, a, rk, _, _ in seqs:
        kk = kk * lax.rsqrt(jnp.maximum(head_sum(kk * kk), 1e-24))
        b = kk * a
        bonus.append(head_sum(r * k * rk) * v)
        cum = lw
        sh = 1
        while sh < c:
            cum = cum + jnp.where(row >= sh, pltpu.roll(cum, sh, 0), 0.0)
            sh *= 2
        tot = cum[c - 1:c]
        grow = jnp.exp(-cum)
        tail = jnp.exp(tot - cum)
        a_t = -kk * jnp.exp(cum - lw)
        r_t = r * jnp.exp(cum)
        lhs.append(jnp.concatenate([a_t, r_t], axis=0).astype(BF16))
        rhs.append(jnp.concatenate([stack(b * grow), stack(k * grow)], axis=0).astype(BF16))
        v_st.append(stack(v).astype(BF16))
        v_bf.append(v.astype(BF16))
        bk_end.append(jnp.concatenate([b * tail, k * tail], axis=0).astype(BF16))
        decay.append(jnp.exp(tot))
    grams = [_dot_nt(x, y) for x, y in zip(lhs, rhs)]
    lows = [jnp.where(strict, g[:c, :n], 0.0) for g in grams]
    m_rb = [jnp.where(incl, g[c:, :n], 0.0).astype(BF16) for g in grams]
    m_v = [jnp.concatenate([jnp.where(strict, g[:c, n:], 0.0), jnp.where(incl, g[c:, n:], 0.0)], axis=0).astype(BF16)
           for g in grams]
    invs = [inv.astype(BF16) for inv in _unit_lower_inverses(lows, c, rep_ref)]
    from_v = [_dot(m, vs) for m, vs in zip(m_v, v_st)]

    ys = []
    states = list(states)
    npair = len(states)
    for j in range(len(seqs) // npair):
        sl = slice(j * npair, (j + 1) * npair)
        from_state = [_dot_nt(x, s.astype(BF16)) for x, s in zip(lhs[sl], states)]
        rhs_sa = [stack(f[:c] + fv[:c]).astype(BF16) for f, fv in zip(from_state, from_v[sl])]
        sas = [_dot(inv, x) for inv, x in zip(invs[sl], rhs_sa)]
        ys += [f[c:] + fv[c:] + _dot(m, stack(sa).astype(BF16))
               for f, fv, m, sa in zip(from_state, from_v[sl], m_rb[sl], sas)]
        upd = [_dot_tn(jnp.concatenate([sa.astype(BF16), vb], axis=0), be)
               for sa, vb, be in zip(sas, v_bf[sl], bk_end[sl])]
        states = [s * d + jnp.where(same_head, u, 0.0) for s, d, u in zip(states, decay[sl], upd)]
    ds = [y - head_sum(y) * (1.0 / RWKV_HEAD) for y in ys]
    var = [head_sum(d * d) * (1.0 / RWKV_HEAD) for d in ds]
    zs = [d * lax.rsqrt(vr + LNX_EPS) * seq[7] + seq[8] + bo for d, vr, seq, bo in zip(ds, var, seqs, bonus)]
    return zs, states


def _rwkv_scan_kernel(r_ref, lw_ref, k_ref, v_ref, kk_ref, a_ref, rk_ref, lnw_ref, lnb_ref, s0_ref, *rest,
                      c, sub, nseq, vpu_diag):
    if vpu_diag:
        rep_ref, z_ref, so_ref, s_ref = rest
    else:
        rep_ref = None
        z_ref, so_ref, s_ref = rest
    ci = pl.program_id(1)
    chains = [(s, hp) for s in range(nseq) for hp in range(N_PAIRS)]

    @pl.when(ci == 0)
    def _():
        s_ref[...] = s0_ref[...]

    seqs = [tuple(ref[s, hp, j * c:(j + 1) * c] for ref in (r_ref, lw_ref, k_ref, v_ref, kk_ref, a_ref))
            + (rk_ref[hp], lnw_ref[hp], lnb_ref[hp]) for j in range(sub) for s, hp in chains]
    zs, new_states = _scan_chunks(seqs, [s_ref[s, hp] for s, hp in chains], c, rep_ref)
    for j in range(sub):
        for n, (s, hp) in enumerate(chains):
            z_ref[s, hp, j * c:(j + 1) * c] = zs[j * len(chains) + n].astype(BF16)
    for n, (s, hp) in enumerate(chains):
        s_ref[s, hp] = new_states[n]

    @pl.when(ci == pl.num_programs(1) - 1)
    def _():
        so_ref[...] = s_ref[...]


def _rwkv_scan(r, lw, k, v, kk, a, r_k, lnx_w, lnx_b, s0, B, L):
    c = min(CHUNK, L)
    nc = L // c
    sub = next(s for s in SCAN_CHUNKS_PER_STEP if nc % s == 0)
    nc //= sub
    nseq = next(s for s in SCAN_CHUNKS_PER_STEP if B % s == 0) if nc * sub == 1 else 1
    seq = pl.BlockSpec((nseq, N_PAIRS, sub * c, LANES), lambda bi, ci: (bi, 0, ci, 0))
    st = pl.BlockSpec((nseq, N_PAIRS, LANES, LANES), lambda bi, ci: (bi, 0, 0, 0))
    vpu_diag = 2 * c == LANES
    extra_specs, extra_args = [], []
    if vpu_diag:
        extra_specs = [_resident((LANES, (DIAG - 1) * LANES), lambda bi, ci: (0, 0))]
        extra_args = [_replication_matrix(LANES)]
    return pl.pallas_call(
        functools.partial(_rwkv_scan_kernel, c=c, sub=sub, nseq=nseq, vpu_diag=vpu_diag),
        grid=(B // nseq, nc),
        in_specs=[seq] * 6 + [_resident((N_PAIRS, 1, LANES), lambda bi, ci: (0, 0, 0))] * 3 + [st] + extra_specs,
        out_specs=[seq, st],
        out_shape=[jax.ShapeDtypeStruct((B, N_PAIRS, L, LANES), BF16),
                   jax.ShapeDtypeStruct((B, N_PAIRS, LANES, LANES), F32)],
        scratch_shapes=[pltpu.VMEM((nseq, N_PAIRS, LANES, LANES), F32)],
        compiler_params=_params(("parallel", "arbitrary")),
    )(r, lw, k, v, kk, a, r_k, lnx_w, lnx_b, s0, *extra_args)


def _rwkv_post_kernel(x_ref, z_ref, g_ref, wo_ref, o_ref, *, nseq):
    z = jnp.concatenate([jnp.concatenate([z_ref[s, hp] for hp in range(N_PAIRS)], axis=-1) for s in range(nseq)],
                        axis=0)
    o_ref[...] = x_ref[...] + _dot(z * g_ref[...], wo_ref[...])


def _rwkv_post(x, z, g, p, B, L):
    nseq = max(1, min(B, TOKEN_TILE // L))
    assert B % nseq == 0
    lt = min(TOKEN_TILE, L)
    tm = lt * nseq
    nt = L // lt
    row = pl.BlockSpec((tm, D_MODEL), lambda b, t: (b * nt + t, 0))
    return pl.pallas_call(
        functools.partial(_rwkv_post_kernel, nseq=nseq),
        grid=(B // nseq, nt),
        in_specs=[row, pl.BlockSpec((nseq, N_PAIRS, lt, LANES), lambda b, t: (b, 0, t, 0)), row,
                  _resident((D_MODEL, D_MODEL), lambda b, t: (0, 0))],
        out_specs=row,
        out_shape=jax.ShapeDtypeStruct((B * L, D_MODEL), F32),
        compiler_params=_params(("parallel", "parallel")),
    )(x, z, g, p['rwkv_wo'])


def _pack_state(s):
    B = s.shape[0]
    s = s.reshape(B, N_PAIRS, 2, RWKV_HEAD, RWKV_HEAD)
    z = jnp.zeros_like(s[:, :, 0])
    top = jnp.concatenate([s[:, :, 0], z], axis=-1)
    bot = jnp.concatenate([z, s[:, :, 1]], axis=-1)
    return jnp.concatenate([top, bot], axis=-2)


def _unpack_state(s):
    B = s.shape[0]
    return jnp.stack([s[:, :, :RWKV_HEAD, :RWKV_HEAD], s[:, :, RWKV_HEAD:, RWKV_HEAD:]], axis=2).reshape(
        B, RWKV_H, RWKV_HEAD, RWKV_HEAD)


class _Stream:
    def __init__(self, x, pos0, caches):
        self.B, self.L, _ = x.shape
        self.x = x.reshape(self.B * self.L, D_MODEL)
        self.pos0 = pos0
        self.pool_c, self.k_c, self.v_c, self.shift_c, self.wkv_c = caches
        self.stepping = self.k_c is not None
        if self.stepping:
            self.bkt = _bucket_index(self.L, SWA_ROWS + self.L, SWA_ROWS)
        else:
            self.bkt = _bucket_index(CHUNK, WINDOW + CHUNK, WINDOW)
        self.new = {name: [] for name in ('pool', 'k', 'v', 'shift', 'wkv')}
        self.pending = None

    def mixer(self, l, p):
        B, L, i = self.B, self.L, l // 2
        gn = p['norm_mix'][l][None]
        if l % 2 == 0:
            if self.stepping:
                hist = jnp.pad(self.pool_c[i], ((0, 0), (POOL_HALO - POOL_HIST, 0), (0, 0)))
                k_cache = self.k_c[i].reshape(B, SWA_ROWS, KV_W)
                v_cache = self.v_c[i].reshape(B, SWA_ROWS, KV_W)
            else:
                hist = jnp.zeros((B, POOL_HALO, C_POOL), F32)
                k_cache = v_cache = jnp.zeros((B, WINDOW, KV_W), F32)
            self.x, u_tail, k_tail, v_tail = _even_layer(
                self.x, gn, p, i, hist, k_cache, v_cache, self.bkt, B, L, self.pos0,
                chunk=L if self.stepping else CHUNK, masked=not self.stepping)
            self.new['pool'].append(u_tail[:, -POOL_HIST:])
            self.new['k'].append(k_tail.reshape(B, SWA_ROWS, N_KV_HEADS, HEAD_DIM))
            self.new['v'].append(v_tail.reshape(B, SWA_ROWS, N_KV_HEADS, HEAD_DIM))
        else:
            if self.stepping:
                shift_prev = self.shift_c[i][:, None, :]
                s0 = _pack_state(self.wkv_c[i])
            else:
                shift_prev = jnp.zeros((B, 1, D_MODEL), F32)
                s0 = jnp.zeros((B, N_PAIRS, LANES, LANES), F32)
            r, lw, k, v, kk, a, g, hs = _rwkv_pre(self.x, shift_prev, gn, p, i, B, L)
            per_pair = lambda name: p[name][i].reshape(N_PAIRS, 1, LANES)
            z, s_new = _rwkv_scan(r, lw, k, v, kk, a, per_pair('rwkv_r_k'), per_pair('rwkv_lnx_w'),
                                  per_pair('rwkv_lnx_b'), s0, B, L)
            if self.L % TOKEN_TILE == 0:
                self.pending = (z, g, p['rwkv_wo'], L)
            else:
                self.x = _rwkv_post(self.x, z, g, p, B, L)
            self.new['shift'].append(hs[:, -1])
            self.new['wkv'].append(_unpack_state(s_new))

    def results(self):
        return (self.x.reshape(self.B, self.L, D_MODEL),) + tuple(
            jnp.stack(self.new[name]) for name in ('pool', 'k', 'v', 'shift', 'wkv'))


_FFN_WEIGHTS = ('ffn_w_gate', 'ffn_w_up', 'ffn_w_down')
_EVEN_WEIGHTS = ('w_in_even', 'w_out_even')
_ODD_WEIGHTS = ('rwkv_wr', 'rwkv_wk', 'rwkv_wv', 'rwkv_w1', 'rwkv_w2', 'rwkv_a1', 'rwkv_a2', 'rwkv_g1', 'rwkv_g2',
                'rwkv_wo')


def _trunk(long, short, p):
    depth = p['norm_mix'].shape[0]
    ffn_w = tuple(p[name][0, 0].astype(BF16) for name in _FFN_WEIGHTS)
    for l in range(depth):
        for j in range(2):
            n = 2 * l + j
            jobs = [(p[name], divmod(n + 1, 2)) for name in _FFN_WEIGHTS] if n + 1 < 2 * depth else []
            mixer_names = (_EVEN_WEIGHTS if l % 2 == 0 else _ODD_WEIGHTS) if j == 0 else ()
            jobs += [(p[name], (l // 2,)) for name in mixer_names]
            final_g = p['norm_final'][None] if n == 2 * depth - 1 else None
            long.x, short.x, done = _ffn(long.x, short.x, p['norm_ffn'][l, j][None], ffn_w, final_g, jobs,
                                         long.pending)
            long.pending = None
            ffn_w = done[:len(_FFN_WEIGHTS)]
            if j == 0:
                mixer_w = dict(p, **dict(zip(mixer_names, done[len(done) - len(mixer_names):])))
                long.mixer(l, mixer_w)
                short.mixer(l, mixer_w)
    return long.results(), short.results()


def _prepare(p):
    p = dict(p)
    p['pool_w'] = p['pool_w'].astype(BF16)
    return p


def kernel(x_prompt, x_sample, cache_pool, cache_swa_k, cache_swa_v, state_shift, state_wkv, t5_table, norm_ffn, ffn_w_gate, ffn_w_up, ffn_w_down, norm_mix, w_in_even, pool_w, pool_scale, attn_sinks, w_out_even, rwkv_mu, rwkv_wr, rwkv_wk, rwkv_wv, rwkv_w0, rwkv_w1, rwkv_w2, rwkv_a0, rwkv_a1, rwkv_a2, rwkv_g1, rwkv_g2, rwkv_k_k, rwkv_k_a, rwkv_r_k, rwkv_lnx_w, rwkv_lnx_b, rwkv_wo, norm_final):
    p = _prepare(dict(
        t5_table=t5_table, norm_ffn=norm_ffn, ffn_w_gate=ffn_w_gate, ffn_w_up=ffn_w_up, ffn_w_down=ffn_w_down,
        norm_mix=norm_mix, w_in_even=w_in_even, pool_w=pool_w, pool_scale=pool_scale, attn_sinks=attn_sinks,
        w_out_even=w_out_even, rwkv_mu=rwkv_mu, rwkv_wr=rwkv_wr, rwkv_wk=rwkv_wk, rwkv_wv=rwkv_wv,
        rwkv_w0=rwkv_w0, rwkv_w1=rwkv_w1, rwkv_w2=rwkv_w2, rwkv_a0=rwkv_a0, rwkv_a1=rwkv_a1, rwkv_a2=rwkv_a2,
        rwkv_g1=rwkv_g1, rwkv_g2=rwkv_g2, rwkv_k_k=rwkv_k_k, rwkv_k_a=rwkv_k_a, rwkv_r_k=rwkv_r_k,
        rwkv_lnx_w=rwkv_lnx_w, rwkv_lnx_b=rwkv_lnx_b, rwkv_wo=rwkv_wo, norm_final=norm_final))
    prompt = _Stream(x_prompt, 0, (None, None, None, None, None))
    sample = _Stream(x_sample, PAST_LEN, (cache_pool, cache_swa_k, cache_swa_v, state_shift, state_wkv))
    (y_p, pool_p, k_p, v_p, shift_p, wkv_p), (y_s, pool_s, k_s, v_s, shift_s, wkv_s) = _trunk(prompt, sample, p)
    return (y_p, y_s, pool_p, pool_s, k_p, k_s, v_p, v_s, shift_p, shift_s, wkv_p, wkv_s)
```

```python
import functools
import math

import jax
import jax.numpy as jnp
import numpy as np
from jax import lax
from jax.experimental import pallas as pl
from jax.experimental.pallas import tpu as pltpu

F32 = jnp.float32
BF16 = jnp.bfloat16

D_MODEL = 1024
D_FF = 2816
NORM_EPS = 1e-6
CHUNK = 64
POOL_WINDOWS = (2, 4, 8, 16)
C_POOL = 512
POOL_GC = 128
POOL_HIST = 15
POOL_HALO = 16
HEAD_DIM = 64
N_Q_HEADS = 8
N_KV_HEADS = 2
GQA_GROUP = 4
WINDOW = 128
SWA_ROWS = 128
Q_W = 512
KV_W = 128
IN_EVEN = C_POOL + Q_W + 2 * KV_W
N_BUCKETS = 32
MAX_DISTANCE = 128
RWKV_HEAD = 64
RWKV_H = 16
N_PAIRS = RWKV_H // 2
LNX_EPS = 64e-5
PAST_LEN = 4096

LANES = 128
F32_SUBLANES = 8
BF16_SUBLANES = 16
VMEM_LIMIT_BYTES = 56 * 1024 * 1024

FFN_TILE = 1024
FFN_F_CHUNK = 256
TOKEN_TILE = 512
EVEN_CHUNKS_PER_STEP = 16
SCAN_CHUNKS_PER_STEP = (4, 2, 1)


def _params(sem):
    return pltpu.CompilerParams(dimension_semantics=sem, vmem_limit_bytes=VMEM_LIMIT_BYTES)


def _dot(a, b):
    return jnp.dot(a, b, preferred_element_type=F32)


def _dot_nt(a, b):
    return lax.dot_general(a, b, (((1,), (1,)), ((), ())), preferred_element_type=F32)


def _dot_tn(a, b):
    return lax.dot_general(a, b, (((0,), (0,)), ((), ())), preferred_element_type=F32)


def _rms(x, g):
    return x * lax.rsqrt(jnp.mean(x * x, axis=-1, keepdims=True) + NORM_EPS) * g


def _ffn_kernel(x_ref, xs_ref, g_ref, wg_ref, wu_ref, wd_ref, *rest, f_chunk, final, convert, mixer_out):
    rest = list(rest)
    gf_ref = rest.pop(0) if final else None
    z_ref, gate_ref, wo_ref = (rest.pop(0), rest.pop(0), rest.pop(0)) if mixer_out else (None,) * 3
    src_refs = [rest.pop(0) for _ in range(convert)]
    o_ref, os_ref = rest.pop(0), rest.pop(0)
    dst_refs = [rest.pop(0) for _ in range(convert)]
    acc_ref, = rest
    for src, dst in zip(src_refs, dst_refs):
        dst[...] = src[...].astype(BF16)

    def rows(x, dst_ref):
        n = x.shape[0]
        h = _rms(x, g_ref[...]).astype(BF16)
        for j in range(D_FF // f_chunk):
            sl = slice(j * f_chunk, (j + 1) * f_chunk)
            gate = _dot(h, wg_ref[:, sl])
            up = _dot(h, wu_ref[:, sl])
            act = (gate * jax.nn.sigmoid(gate) * up).astype(BF16)
            part = _dot(act, wd_ref[sl, :])
            if j == 0:
                acc_ref[:n] = part
            else:
                acc_ref[:n] += part
        y = x + 0.5 * acc_ref[:n]
        if final:
            y = _rms(y, gf_ref[...])
        dst_ref[...] = y

    x = x_ref[...]
    if mixer_out:
        z = jnp.concatenate([z_ref[0, hp] for hp in range(N_PAIRS)], axis=-1)
        x = x + _dot(z * gate_ref[...], wo_ref[...])
    rows(x, o_ref)

    @pl.when(pl.program_id(0) == pl.num_programs(0) - 1)
    def _():
        rows(xs_ref[...], os_ref)


def _resident(shape, index_map):
    return pl.BlockSpec(shape, index_map, pipeline_mode=pl.Buffered(1))


def _slab_steps(rows, steps):
    return next(k for k in range(steps, 0, -1) if rows % (k * BF16_SUBLANES) == 0)


def _ffn(x, xs, g, weights, final_g=None, convert=(), mixer_out=None):
    T = x.shape[0]
    tm = min(FFN_TILE if mixer_out is None else TOKEN_TILE, T)
    steps = T // tm
    Ts = xs.shape[0]
    assert Ts <= tm
    final = final_g is not None
    in_specs = [pl.BlockSpec((tm, D_MODEL), lambda i: (i, 0)), _resident((Ts, D_MODEL), lambda i: (0, 0)),
                _resident((1, D_MODEL), lambda i: (0, 0))]
    in_specs += [_resident(w.shape, lambda i: (0, 0)) for w in weights]
    args = [x, xs, g, *weights]
    out_specs = [pl.BlockSpec((tm, D_MODEL), lambda i: (i, 0)), pl.BlockSpec((Ts, D_MODEL), lambda i: (0, 0))]
    out_shape = [jax.ShapeDtypeStruct((T, D_MODEL), F32), jax.ShapeDtypeStruct((Ts, D_MODEL), F32)]
    if final:
        in_specs.append(_resident((1, D_MODEL), lambda i: (0, 0)))
        args.append(final_g)
    if mixer_out is not None:
        z, gate, wo, L = mixer_out
        nt = L // tm
        in_specs += [pl.BlockSpec((1, N_PAIRS, tm, LANES), lambda i: (i // nt, 0, i % nt, 0)),
                     pl.BlockSpec((tm, D_MODEL), lambda i: (i, 0)), _resident(wo.shape, lambda i: (0, 0))]
        args += [z, gate, wo]
    for w, lead in convert:
        rows, cols = w.shape[len(lead):]
        k = _slab_steps(rows, steps)
        in_specs.append(pl.BlockSpec((None,) * len(lead) + (rows // k, cols),
                                     lambda i, k=k, lead=lead: (*lead, jnp.minimum(i, k - 1), 0)))
        out_specs.append(pl.BlockSpec((rows // k, cols), lambda i, k=k: (jnp.minimum(i, k - 1), 0)))
        out_shape.append(jax.ShapeDtypeStruct((rows, cols), BF16))
        args.append(w)
    outs = pl.pallas_call(
        functools.partial(_ffn_kernel, f_chunk=FFN_F_CHUNK, final=final, convert=len(convert),
                          mixer_out=mixer_out is not None),
        grid=(steps,),
        in_specs=in_specs,
        out_specs=out_specs,
        out_shape=out_shape,
        scratch_shapes=[pltpu.VMEM((tm, D_MODEL), F32)],
        compiler_params=_params(("arbitrary",)),
    )(*args)
    return outs[0], outs[1], tuple(outs[2:])


def _build_bias(bkt_ref, tab_ref, bias_ref, lq):
    bkt = bkt_ref[...]
    for h in range(N_Q_HEADS):
        b = jnp.zeros(bkt.shape, F32)
        for n in range(N_BUCKETS):
            b = jnp.where(bkt == n, tab_ref[n, h], b)
        g, i = divmod(h, GQA_GROUP)
        bias_ref[g, :, i * lq:(i + 1) * lq] = b


def _group_queries(q, g):
    return jnp.concatenate([q[:, h * HEAD_DIM:(h + 1) * HEAD_DIM]
                            for h in range(g * GQA_GROUP, (g + 1) * GQA_GROUP)], axis=0) * (HEAD_DIM ** -0.5)


def _sink_rows(sink_ref, lq):
    lane = lax.broadcasted_iota(jnp.int32, (1, GQA_GROUP * lq), 1)
    rows = []
    for g in range(N_KV_HEADS):
        r = jnp.zeros((1, GQA_GROUP * lq), F32)
        for i in range(GQA_GROUP):
            r = jnp.where(lane // lq == i, sink_ref[g * GQA_GROUP + i], r)
        rows.append(r)
    return rows


def _attn_core(qs, ks, vs, biases, sinks, valids):
    ss = [_dot_nt(k, q) + b for q, k, b in zip(qs, ks, biases)]
    ss = [s if ok is None else jnp.where(ok, s, -1e30) for s, ok in zip(ss, valids)]
    ms = [jnp.maximum(jnp.max(s, axis=0, keepdims=True), sk) for s, sk in zip(ss, sinks)]
    ps = [jnp.exp(s - m) for s, m in zip(ss, ms)]
    invs = [1.0 / (jnp.sum(p, axis=0, keepdims=True) + jnp.exp(sk - m)) for p, sk, m in zip(ps, sinks, ms)]
    return [_dot_tn((p * r).astype(BF16), v) for p, r, v in zip(ps, invs, vs)]


def _smem():
    return pl.BlockSpec(memory_space=pltpu.SMEM)


def _even_layer_kernel(x_ref, gn_ref, win_ref, pw_ref, ps_ref, hist_ref, kc_ref, vc_ref, bkt_ref, tab_ref, sink_ref,
                       wop_ref, woa_ref, o_ref, utail_o, ktail_o, vtail_o, halo_ref, kprev_ref, vprev_ref, bias_ref,
                       *, rows, chunk, pos0, masked):
    t = pl.program_id(1)

    @pl.when(jnp.logical_and(pl.program_id(0) == 0, t == 0))
    def _():
        _build_bias(bkt_ref, tab_ref, bias_ref, chunk)

    @pl.when(t == 0)
    def _():
        halo_ref[...] = hist_ref[0]
        kprev_ref[...] = kc_ref[0]
        vprev_ref[...] = vc_ref[0]

    x = x_ref[...]
    z = _dot(_rms(x, gn_ref[...]).astype(BF16), win_ref[...])
    u = z[:, :C_POOL]
    q = z[:, C_POOL:C_POOL + Q_W].astype(BF16)
    k_all = jnp.concatenate([kprev_ref[...], z[:, C_POOL + Q_W:C_POOL + Q_W + KV_W]], axis=0)
    v_all = jnp.concatenate([vprev_ref[...], z[:, C_POOL + Q_W + KV_W:]], axis=0)

    ext = jnp.concatenate([halo_ref[...], u], axis=0)
    pos = pos0 + t * rows + lax.broadcasted_iota(jnp.int32, (rows, POOL_GC), 0)
    pooled = []
    for gi, w in enumerate(POOL_WINDOWS):
        sl = slice(gi * POOL_GC, (gi + 1) * POOL_GC)
        s = ext[:, sl]
        span = 1
        while span < w:
            s = s + pltpu.roll(s, span, 0)
            span *= 2
        cnt = jnp.minimum(w, pos + 1).astype(F32)
        pooled.append(_dot((s[POOL_HALO:] / cnt - u[:, sl]).astype(BF16), pw_ref[gi]))
    pool_out = (jnp.concatenate(pooled, axis=-1) * ps_ref[...]).astype(BF16)

    lk = WINDOW + chunk
    kb = k_all.astype(BF16)
    vb = v_all.astype(BF16)
    sink_rows = _sink_rows(sink_ref, chunk)
    first_pos = t * rows - WINDOW + lax.broadcasted_iota(jnp.int32, (lk, 1), 0)
    nchunk = rows // chunk
    qs, ks, vs, biases, sinks, valids = [], [], [], [], [], []
    for j in range(nchunk):
        qj = q[j * chunk:(j + 1) * chunk]
        for g in range(N_KV_HEADS):
            qs.append(_group_queries(qj, g))
            ks.append(kb[j * chunk:j * chunk + lk, g * HEAD_DIM:(g + 1) * HEAD_DIM])
            vs.append(vb[j * chunk:j * chunk + lk, g * HEAD_DIM:(g + 1) * HEAD_DIM])
            biases.append(bias_ref[g])
            sinks.append(sink_rows[g])
            valids.append(first_pos + j * chunk >= 0 if masked else None)
    outs = _attn_core(qs, ks, vs, biases, sinks, valids)
    att = jnp.concatenate(
        [jnp.concatenate([outs[j * N_KV_HEADS + g][i * chunk:(i + 1) * chunk]
                          for g in range(N_KV_HEADS) for i in range(GQA_GROUP)], axis=-1) for j in range(nchunk)],
        axis=0).astype(BF16)

    o_ref[...] = x + _dot(pool_out, wop_ref[...]) + _dot(att, woa_ref[...])

    halo_ref[...] = ext[rows:]
    kprev_ref[...] = k_all[rows:]
    vprev_ref[...] = v_all[rows:]
    utail_o[0] = ext[rows:]
    ktail_o[0] = k_all[rows:]
    vtail_o[0] = v_all[rows:]


def _even_layer(x, gn, p, i, hist, k_cache, v_cache, bkt, B, L, pos0, chunk, masked):
    rows = min(EVEN_CHUNKS_PER_STEP * chunk, L)
    nt = L // rows
    lk = WINDOW + chunk
    row = pl.BlockSpec((rows, D_MODEL), lambda b, t: (b * nt + t, 0))
    per_seq = lambda n, m: pl.BlockSpec((1, n, m), lambda b, t: (b, 0, 0))
    return pl.pallas_call(
        functools.partial(_even_layer_kernel, rows=rows, chunk=chunk, pos0=pos0, masked=masked),
        grid=(B, nt),
        in_specs=[row, _resident((1, D_MODEL), lambda b, t: (0, 0)),
                  _resident((D_MODEL, IN_EVEN), lambda b, t: (0, 0)),
                  _resident((None, len(POOL_WINDOWS), POOL_GC, POOL_GC), lambda b, t: (i, 0, 0, 0)),
                  _resident((1, C_POOL), lambda b, t: (0, 0)),
                  per_seq(POOL_HALO, C_POOL), per_seq(WINDOW, KV_W), per_seq(WINDOW, KV_W),
                  _resident((lk, chunk), lambda b, t: (0, 0)), _smem(), _smem(),
                  _resident((C_POOL, D_MODEL), lambda b, t: (0, 0)),
                  _resident((Q_W, D_MODEL), lambda b, t: (1, 0))],
        out_specs=[row, per_seq(POOL_HALO, C_POOL), per_seq(WINDOW, KV_W), per_seq(WINDOW, KV_W)],
        out_shape=[jax.ShapeDtypeStruct((B * L, D_MODEL), F32), jax.ShapeDtypeStruct((B, POOL_HALO, C_POOL), F32),
                   jax.ShapeDtypeStruct((B, WINDOW, KV_W), F32), jax.ShapeDtypeStruct((B, WINDOW, KV_W), F32)],
        scratch_shapes=[pltpu.VMEM((POOL_HALO, C_POOL), F32), pltpu.VMEM((WINDOW, KV_W), F32),
                        pltpu.VMEM((WINDOW, KV_W), F32), pltpu.VMEM((N_KV_HEADS, lk, GQA_GROUP * chunk), F32)],
        compiler_params=_params(("arbitrary", "arbitrary")),
    )(x, gn, p['w_in_even'], p['pool_w'], p['pool_scale'][i][None], hist, k_cache, v_cache, bkt,
      p['t5_table'], p['attn_sinks'][i], p['w_out_even'], p['w_out_even'])


def _t5_bucket(rel):
    half = N_BUCKETS // 2
    max_exact = half // 2
    side = jnp.where(rel > 0, half, 0)
    n = jnp.abs(rel)
    nf = jnp.maximum(n, max_exact).astype(F32)
    large = max_exact + (jnp.log(nf / max_exact) / math.log(MAX_DISTANCE / max_exact)
                         * (half - max_exact)).astype(jnp.int32)
    large = jnp.minimum(large, half - 1)
    return side + jnp.where(n < max_exact, n, large)


def _bucket_index(lq, lk, offset):
    rel = jnp.arange(lk)[:, None] - offset - jnp.arange(lq)[None, :]
    return _t5_bucket(rel).astype(jnp.int32)


def _rwkv_pre_kernel(x_ref, xp_ref, sh_ref, gn_ref, mu_ref, wr_ref, wk_ref, wv_ref, w1_ref, w2_ref,
                     a1_ref, a2_ref, g1_ref, g2_ref, w0_ref, a0_ref, kk_ref, ka_ref,
                     r_o, lw_o, k_o, v_o, kk_o, a_o, g_o, hs_o, *, tm, nseq):
    t = pl.program_id(1)
    lt = tm // nseq
    gn = gn_ref[...]
    h = _rms(x_ref[...], gn)
    row = lax.broadcasted_iota(jnp.int32, (tm, 1), 0)
    h_prev = pltpu.roll(h, 1, 0)
    for s in range(nseq):
        first_prev = sh_ref[s]
        if nseq == 1:
            first_prev = jnp.where(t == 0, first_prev, _rms(xp_ref[...], gn)[F32_SUBLANES - 1:])
        h_prev = jnp.where(row == s * lt, first_prev, h_prev)
    xx = h_prev - h

    hb = h.astype(BF16)
    xb = xx.astype(BF16)
    mu = mu_ref[...].astype(BF16)

    def mix(j):
        return hb + xb * mu[j:j + 1]

    zw = w0_ref[...] + _dot(jnp.tanh(_dot(mix(1), w1_ref[...])).astype(BF16), w2_ref[...])
    za = a0_ref[...] + _dot(_dot(mix(4), a1_ref[...]).astype(BF16), a2_ref[...])
    g = _dot(jax.nn.sigmoid(_dot(mix(5), g1_ref[...])).astype(BF16), g2_ref[...])
    r = _dot(mix(0), wr_ref[...])
    lw = -math.exp(-0.5) * jax.nn.sigmoid(zw)
    k = _dot(mix(2), wk_ref[...])
    a = jax.nn.sigmoid(za)
    v = _dot(mix(3), wv_ref[...])
    kk = k * kk_ref[...]
    k = k * (1.0 + (a - 1.0) * ka_ref[...])
    for s in range(nseq):
        rows = slice(s * lt, (s + 1) * lt)
        for hp in range(N_PAIRS):
            sl = slice(hp * LANES, (hp + 1) * LANES)
            for out, val in ((r_o, r), (lw_o, lw), (k_o, k), (v_o, v), (kk_o, kk), (a_o, a)):
                out[s, hp] = val[rows, sl]
        hs_o[s] = h[(s + 1) * lt - F32_SUBLANES:(s + 1) * lt]
    g_o[...] = g.astype(BF16)


def _rwkv_pre(x, shift_prev, gn, p, i, B, L):
    nseq = max(1, min(B, TOKEN_TILE // L))
    assert B % nseq == 0
    tm = min(TOKEN_TILE, L) * nseq
    lt = tm // nseq
    nt = L // lt
    vec = lambda: _resident((1, D_MODEL), lambda b, t: (0, 0))
    mat = lambda r, c: _resident((r, c), lambda b, t: (0, 0))
    lora = p['rwkv_w1'].shape[-1], p['rwkv_a1'].shape[-1], p['rwkv_g1'].shape[-1]
    head_spec = pl.BlockSpec((nseq, N_PAIRS, lt, LANES), lambda b, t: (b, 0, t, 0))
    head_shape = jax.ShapeDtypeStruct((B, N_PAIRS, L, LANES), F32)
    return pl.pallas_call(
        functools.partial(_rwkv_pre_kernel, tm=tm, nseq=nseq),
        grid=(B // nseq, nt),
        in_specs=[
            pl.BlockSpec((tm, D_MODEL), lambda b, t: (b * nt + t, 0)),
            pl.BlockSpec((F32_SUBLANES, D_MODEL),
                         lambda b, t: (jnp.maximum((b * nt + t) * (tm // F32_SUBLANES) - 1, 0), 0)),
            pl.BlockSpec((nseq, 1, D_MODEL), lambda b, t: (b, 0, 0)),
            vec(),
            _resident((None, 6, D_MODEL), lambda b, t: (i, 0, 0)),
            mat(D_MODEL, D_MODEL), mat(D_MODEL, D_MODEL), mat(D_MODEL, D_MODEL),
            mat(D_MODEL, lora[0]), mat(lora[0], D_MODEL),
            mat(D_MODEL, lora[1]), mat(lora[1], D_MODEL),
            mat(D_MODEL, lora[2]), mat(lora[2], D_MODEL),
            vec(), vec(), vec(), vec(),
        ],
        out_specs=[head_spec] * 6 + [
            pl.BlockSpec((tm, D_MODEL), lambda b, t: (b * nt + t, 0)),
            pl.BlockSpec((nseq, F32_SUBLANES, D_MODEL), lambda b, t: (b, 0, 0)),
        ],
        out_shape=[head_shape] * 6 + [
            jax.ShapeDtypeStruct((B * L, D_MODEL), BF16),
            jax.ShapeDtypeStruct((B, F32_SUBLANES, D_MODEL), F32),
        ],
        compiler_params=_params(("parallel", "arbitrary")),
    )(x, x, shift_prev, gn, p['rwkv_mu'], p['rwkv_wr'], p['rwkv_wk'], p['rwkv_wv'],
      p['rwkv_w1'], p['rwkv_w2'], p['rwkv_a1'], p['rwkv_a2'], p['rwkv_g1'], p['rwkv_g2'],
      p['rwkv_w0'][i][None], p['rwkv_a0'][i][None], p['rwkv_k_k'][i][None], p['rwkv_k_a'][i][None])


DIAG = 16


def _replication_matrix(n):
    src = np.arange(n)[:, None]
    dst = np.arange(n)[None, :]
    picks = np.concatenate([(src == (dst // DIAG) * DIAG + s) for s in range(DIAG - 1)], axis=1)
    return jnp.asarray(picks, dtype=BF16)


def _unit_lower_inverses(lows, c, rep_ref):
    n = 2 * c
    ti = lax.broadcasted_iota(jnp.int32, (c, n), 0)
    li = lax.broadcasted_iota(jnp.int32, (c, n), 1)
    si = li & (c - 1)
    head1 = li >= c

    def block_diag(x):
        return jnp.concatenate([jnp.where(head1, 0.0, x), jnp.where(head1, x, 0.0)], axis=0)

    if rep_ref is None:
        invs = [jnp.where(ti == si, 1.0, 0.0) + jnp.where(jnp.logical_and(ti == si + 1, (ti & 1) == 1), low, 0.0)
                for low in lows]
        m = 2
    else:
        pt = lax.broadcasted_iota(jnp.int32, (DIAG, n), 0)
        pl_ = lax.broadcasted_iota(jnp.int32, (DIAG, n), 1)
        blk = (pl_ & (c - 1)) // DIAG
        packed = []
        for low in lows:
            d = jnp.zeros((DIAG, n), F32)
            for i in range(c // DIAG):
                d = jnp.where(blk == i, low[i * DIAG:(i + 1) * DIAG], d)
            packed.append(d)
        rep = _dot(jnp.concatenate(packed, axis=0).astype(BF16), rep_ref[...])
        sols = [jnp.where(pt == (pl_ & (DIAG - 1)), 1.0, 0.0) for _ in lows]
        for s in range(DIAG - 1):
            sols = [sol + rep[p * DIAG:(p + 1) * DIAG, s * n:(s + 1) * n] * sol[s:s + 1]
                    for p, sol in enumerate(sols)]
        invs = [jnp.concatenate([jnp.where(blk == i, sol, 0.0) for i in range(c // DIAG)], axis=0) for sol in sols]
        m = DIAG
    while m < c:
        sh = m.bit_length() - 1
        sub = jnp.logical_and(jnp.logical_and((ti >> (sh + 1)) == (si >> (sh + 1)), ((ti >> sh) & 1) == 1),
                              ((si >> sh) & 1) == 0)
        diag = [block_diag(inv).astype(BF16) for inv in invs]
        half = [_dot(inv.astype(BF16), block_diag(jnp.where(sub, low, 0.0)).astype(BF16)).astype(BF16)
                for inv, low in zip(invs, lows)]
        invs = [inv + _dot(h, d) for inv, h, d in zip(invs, half, diag)]
        m *= 2
    return invs


def _scan_chunks(seqs, states, c, rep_ref):
    n = 2 * c
    row = lax.broadcasted_iota(jnp.int32, (c, LANES), 0)
    head1 = lax.broadcasted_iota(jnp.int32, (c, LANES), 1) >= RWKV_HEAD
    ti = lax.broadcasted_iota(jnp.int32, (c, n), 0)
    si = lax.broadcasted_iota(jnp.int32, (c, n), 1) & (c - 1)
    strict = ti > si
    incl = ti >= si
    pi = lax.broadcasted_iota(jnp.int32, (LANES, LANES), 0)
    pj = lax.broadcasted_iota(jnp.int32, (LANES, LANES), 1)
    same_head = (pi >= RWKV_HEAD) == (pj >= RWKV_HEAD)

    def stack(x):
        return jnp.concatenate([jnp.where(head1, 0.0, x), jnp.where(head1, x, 0.0)], axis=0)

    def head_sum(x):
        return jnp.where(head1, jnp.sum(jnp.where(head1, x, 0.0), axis=-1, keepdims=True),
                         jnp.sum(jnp.where(head1, 0.0, x), axis=-1, keepdims=True))

    lhs, rhs, v_st, v_bf, bk_end, decay, bonus = [], [], [], [], [], [], []
    for r, lw, k, v, kk, a, rk, _, _ in seqs:
        kk = kk * lax.rsqrt(jnp.maximum(head_sum(kk * kk), 1e-24))
        b = kk * a
        bonus.append(head_sum(r * k * rk) * v)
        cum = lw
        sh = 1
        while sh < c:
            cum = cum + jnp.where(row >= sh, pltpu.roll(cum, sh, 0), 0.0)
            sh *= 2
        tot = cum[c - 1:c]
        grow = jnp.exp(-cum)
        tail = jnp.exp(tot - cum)
        a_t = -kk * jnp.exp(cum - lw)
        r_t = r * jnp.exp(cum)
        lhs.append(jnp.concatenate([a_t, r_t], axis=0).astype(BF16))
        rhs.append(jnp.concatenate([stack(b * grow), stack(k * grow)], axis=0).astype(BF16))
        v_st.append(stack(v).astype(BF16))
        v_bf.append(v.astype(BF16))
        bk_end.append(jnp.concatenate([b * tail, k * tail], axis=0).astype(BF16))
        decay.append(jnp.exp(tot))
    grams = [_dot_nt(x, y) for x, y in zip(lhs, rhs)]
    lows = [jnp.where(strict, g[:c, :n], 0.0) for g in grams]
    m_rb = [jnp.where(incl, g[c:, :n], 0.0).astype(BF16) for g in grams]
    m_v = [jnp.concatenate([jnp.where(strict, g[:c, n:], 0.0), jnp.where(incl, g[c:, n:], 0.0)], axis=0).astype(BF16)
           for g in grams]
    invs = [inv.astype(BF16) for inv in _unit_lower_inverses(lows, c, rep_ref)]
    from_v = [_dot(m, vs) for m, vs in zip(m_v, v_st)]

    ys = []
    states = list(states)
    npair = len(states)
    for j in range(len(seqs) // npair):
        sl = slice(j * npair, (j + 1) * npair)
        from_state = [_dot_nt(x, s.astype(BF16)) for x, s in zip(lhs[sl], states)]
        rhs_sa = [stack(f[:c] + fv[:c]).astype(BF16) for f, fv in zip(from_state, from_v[sl])]
        sas = [_dot(inv, x) for inv, x in zip(invs[sl], rhs_sa)]
        ys += [f[c:] + fv[c:] + _dot(m, stack(sa).astype(BF16))
               for f, fv, m, sa in zip(from_state, from_v[sl], m_rb[sl], sas)]
        upd = [_dot_tn(jnp.concatenate([sa.astype(BF16), vb], axis=0), be)
               for sa, vb, be in zip(sas, v_bf[sl], bk_end[sl])]
        states = [s * d + jnp.where(same_head, u, 0.0) for s, d, u in zip(states, decay[sl], upd)]
    ds = [y - head_sum(y) * (1.0 / RWKV_HEAD) for y in ys]
    var = [head_sum(d * d) * (1.0 / RWKV_HEAD) for d in ds]
    zs = [d * lax.rsqrt(vr + LNX_EPS) * seq[7] + seq[8] + bo for d, vr, seq, bo in zip(ds, var, seqs, bonus)]
    return zs, states


def _rwkv_scan_kernel(r_ref, lw_ref, k_ref, v_ref, kk_ref, a_ref, rk_ref, lnw_ref, lnb_ref, s0_ref, *rest,
                      c, sub, nseq, vpu_diag):
    if vpu_diag:
        rep_ref, z_ref, so_ref, s_ref = rest
    else:
        rep_ref = None
        z_ref, so_ref, s_ref = rest
    ci = pl.program_id(1)
    chains = [(s, hp) for s in range(nseq) for hp in range(N_PAIRS)]

    @pl.when(ci == 0)
    def _():
        s_ref[...] = s0_ref[...]

    seqs = [tuple(ref[s, hp, j * c:(j + 1) * c] for ref in (r_ref, lw_ref, k_ref, v_ref, kk_ref, a_ref))
            + (rk_ref[hp], lnw_ref[hp], lnb_ref[hp]) for j in range(sub) for s, hp in chains]
    zs, new_states = _scan_chunks(seqs, [s_ref[s, hp] for s, hp in chains], c, rep_ref)
    for j in range(sub):
        for n, (s, hp) in enumerate(chains):
            z_ref[s, hp, j * c:(j + 1) * c] = zs[j * len(chains) + n].astype(BF16)
    for n, (s, hp) in enumerate(chains):
        s_ref[s, hp] = new_states[n]

    @pl.when(ci == pl.num_programs(1) - 1)
    def _():
        so_ref[...] = s_ref[...]


def _rwkv_scan(r, lw, k, v, kk, a, r_k, lnx_w, lnx_b, s0, B, L):
    c = min(CHUNK, L)
    nc = L // c
    sub = next(s for s in SCAN_CHUNKS_PER_STEP if nc % s == 0)
    nc //= sub
    nseq = next(s for s in SCAN_CHUNKS_PER_STEP if B % s == 0) if nc * sub == 1 else 1
    if nseq == 1 and sub == 4 and B % 2 == 0:
        nseq, sub, nc = 2, 2, nc * 2
    seq = pl.BlockSpec((nseq, N_PAIRS, sub * c, LANES), lambda bi, ci: (bi, 0, ci, 0))
    st = pl.BlockSpec((nseq, N_PAIRS, LANES, LANES), lambda bi, ci: (bi, 0, 0, 0))
    vpu_diag = 2 * c == LANES
    extra_specs, extra_args = [], []
    if vpu_diag:
        extra_specs = [_resident((LANES, (DIAG - 1) * LANES), lambda bi, ci: (0, 0))]
        extra_args = [_replication_matrix(LANES)]
    return pl.pallas_call(
        functools.partial(_rwkv_scan_kernel, c=c, sub=sub, nseq=nseq, vpu_diag=vpu_diag),
        grid=(B // nseq, nc),
        in_specs=[seq] * 6 + [_resident((N_PAIRS, 1, LANES), lambda bi, ci: (0, 0, 0))] * 3 + [st] + extra_specs,
        out_specs=[seq, st],
        out_shape=[jax.ShapeDtypeStruct((B, N_PAIRS, L, LANES), BF16),
                   jax.ShapeDtypeStruct((B, N_PAIRS, LANES, LANES), F32)],
        scratch_shapes=[pltpu.VMEM((nseq, N_PAIRS, LANES, LANES), F32)],
        compiler_params=_params(("parallel", "arbitrary")),
    )(r, lw, k, v, kk, a, r_k, lnx_w, lnx_b, s0, *extra_args)


def _rwkv_post_kernel(x_ref, z_ref, g_ref, wo_ref, o_ref, *, nseq):
    z = jnp.concatenate([jnp.concatenate([z_ref[s, hp] for hp in range(N_PAIRS)], axis=-1) for s in range(nseq)],
                        axis=0)
    o_ref[...] = x_ref[...] + _dot(z * g_ref[...], wo_ref[...])


def _rwkv_post(x, z, g, p, B, L):
    nseq = max(1, min(B, TOKEN_TILE // L))
    assert B % nseq == 0
    lt = min(TOKEN_TILE, L)
    tm = lt * nseq
    nt = L // lt
    row = pl.BlockSpec((tm, D_MODEL), lambda b, t: (b * nt + t, 0))
    return pl.pallas_call(
        functools.partial(_rwkv_post_kernel, nseq=nseq),
        grid=(B // nseq, nt),
        in_specs=[row, pl.BlockSpec((nseq, N_PAIRS, lt, LANES), lambda b, t: (b, 0, t, 0)), row,
                  _resident((D_MODEL, D_MODEL), lambda b, t: (0, 0))],
        out_specs=row,
        out_shape=jax.ShapeDtypeStruct((B * L, D_MODEL), F32),
        compiler_params=_params(("parallel", "parallel")),
    )(x, z, g, p['rwkv_wo'])


def _pack_state(s):
    B = s.shape[0]
    s = s.reshape(B, N_PAIRS, 2, RWKV_HEAD, RWKV_HEAD)
    z = jnp.zeros_like(s[:, :, 0])
    top = jnp.concatenate([s[:, :, 0], z], axis=-1)
    bot = jnp.concatenate([z, s[:, :, 1]], axis=-1)
    return jnp.concatenate([top, bot], axis=-2)


def _unpack_state(s):
    B = s.shape[0]
    return jnp.stack([s[:, :, :RWKV_HEAD, :RWKV_HEAD], s[:, :, RWKV_HEAD:, RWKV_HEAD:]], axis=2).reshape(
        B, RWKV_H, RWKV_HEAD, RWKV_HEAD)


class _Stream:
    def __init__(self, x, pos0, caches):
        self.B, self.L, _ = x.shape
        self.x = x.reshape(self.B * self.L, D_MODEL)
        self.pos0 = pos0
        self.pool_c, self.k_c, self.v_c, self.shift_c, self.wkv_c = caches
        self.stepping = self.k_c is not None
        if self.stepping:
            self.bkt = _bucket_index(self.L, SWA_ROWS + self.L, SWA_ROWS)
        else:
            self.bkt = _bucket_index(CHUNK, WINDOW + CHUNK, WINDOW)
        self.new = {name: [] for name in ('pool', 'k', 'v', 'shift', 'wkv')}
        self.pending = None

    def mixer(self, l, p):
        B, L, i = self.B, self.L, l // 2
        gn = p['norm_mix'][l][None]
        if l % 2 == 0:
            if self.stepping:
                hist = jnp.pad(self.pool_c[i], ((0, 0), (POOL_HALO - POOL_HIST, 0), (0, 0)))
                k_cache = self.k_c[i].reshape(B, SWA_ROWS, KV_W)
                v_cache = self.v_c[i].reshape(B, SWA_ROWS, KV_W)
            else:
                hist = jnp.zeros((B, POOL_HALO, C_POOL), F32)
                k_cache = v_cache = jnp.zeros((B, WINDOW, KV_W), F32)
            self.x, u_tail, k_tail, v_tail = _even_layer(
                self.x, gn, p, i, hist, k_cache, v_cache, self.bkt, B, L, self.pos0,
                chunk=L if self.stepping else CHUNK, masked=not self.stepping)
            self.new['pool'].append(u_tail[:, -POOL_HIST:])
            self.new['k'].append(k_tail.reshape(B, SWA_ROWS, N_KV_HEADS, HEAD_DIM))
            self.new['v'].append(v_tail.reshape(B, SWA_ROWS, N_KV_HEADS, HEAD_DIM))
        else:
            if self.stepping:
                shift_prev = self.shift_c[i][:, None, :]
                s0 = _pack_state(self.wkv_c[i])
            else:
                shift_prev = jnp.zeros((B, 1, D_MODEL), F32)
                s0 = jnp.zeros((B, N_PAIRS, LANES, LANES), F32)
            r, lw, k, v, kk, a, g, hs = _rwkv_pre(self.x, shift_prev, gn, p, i, B, L)
            per_pair = lambda name: p[name][i].reshape(N_PAIRS, 1, LANES)
            z, s_new = _rwkv_scan(r, lw, k, v, kk, a, per_pair('rwkv_r_k'), per_pair('rwkv_lnx_w'),
                                  per_pair('rwkv_lnx_b'), s0, B, L)
            if self.L % TOKEN_TILE == 0:
                self.pending = (z, g, p['rwkv_wo'], L)
            else:
                self.x = _rwkv_post(self.x, z, g, p, B, L)
            self.new['shift'].append(hs[:, -1])
            self.new['wkv'].append(_unpack_state(s_new))

    def results(self):
        return (self.x.reshape(self.B, self.L, D_MODEL),) + tuple(
            jnp.stack(self.new[name]) for name in ('pool', 'k', 'v', 'shift', 'wkv'))


_FFN_WEIGHTS = ('ffn_w_gate', 'ffn_w_up', 'ffn_w_down')
_EVEN_WEIGHTS = ('w_in_even', 'w_out_even')
_ODD_WEIGHTS = ('rwkv_wr', 'rwkv_wk', 'rwkv_wv', 'rwkv_w1', 'rwkv_w2', 'rwkv_a1', 'rwkv_a2', 'rwkv_g1', 'rwkv_g2',
                'rwkv_wo')


def _trunk(long, short, p):
    depth = p['norm_mix'].shape[0]
    ffn_w = tuple(p[name][0, 0].astype(BF16) for name in _FFN_WEIGHTS)
    for l in range(depth):
        for j in range(2):
            n = 2 * l + j
            jobs = [(p[name], divmod(n + 1, 2)) for name in _FFN_WEIGHTS] if n + 1 < 2 * depth else []
            mixer_names = (_EVEN_WEIGHTS if l % 2 == 0 else _ODD_WEIGHTS) if j == 0 else ()
            jobs += [(p[name], (l // 2,)) for name in mixer_names]
            final_g = p['norm_final'][None] if n == 2 * depth - 1 else None
            long.x, short.x, done = _ffn(long.x, short.x, p['norm_ffn'][l, j][None], ffn_w, final_g, jobs,
                                         long.pending)
            long.pending = None
            ffn_w = done[:len(_FFN_WEIGHTS)]
            if j == 0:
                mixer_w = dict(p, **dict(zip(mixer_names, done[len(done) - len(mixer_names):])))
                long.mixer(l, mixer_w)
                short.mixer(l, mixer_w)
    return long.results(), short.results()


def _prepare(p):
    p = dict(p)
    p['pool_w'] = p['pool_w'].astype(BF16)
    return p


def kernel(x_prompt, x_sample, cache_pool, cache_swa_k, cache_swa_v, state_shift, state_wkv, t5_table, norm_ffn, ffn_w_gate, ffn_w_up, ffn_w_down, norm_mix, w_in_even, pool_w, pool_scale, attn_sinks, w_out_even, rwkv_mu, rwkv_wr, rwkv_wk, rwkv_wv, rwkv_w0, rwkv_w1, rwkv_w2, rwkv_a0, rwkv_a1, rwkv_a2, rwkv_g1, rwkv_g2, rwkv_k_k, rwkv_k_a, rwkv_r_k, rwkv_lnx_w, rwkv_lnx_b, rwkv_wo, norm_final):
    p = _prepare(dict(
        t5_table=t5_table, norm_ffn=norm_ffn, ffn_w_gate=ffn_w_gate, ffn_w_up=ffn_w_up, ffn_w_down=ffn_w_down,
        norm_mix=norm_mix, w_in_even=w_in_even, pool_w=pool_w, pool_scale=pool_scale, attn_sinks=attn_sinks,
        w_out_even=w_out_even, rwkv_mu=rwkv_mu, rwkv_wr=rwkv_wr, rwkv_wk=rwkv_wk, rwkv_wv=rwkv_wv,
        rwkv_w0=rwkv_w0, rwkv_w1=rwkv_w1, rwkv_w2=rwkv_w2, rwkv_a0=rwkv_a0, rwkv_a1=rwkv_a1, rwkv_a2=rwkv_a2,
        rwkv_g1=rwkv_g1, rwkv_g2=rwkv_g2, rwkv_k_k=rwkv_k_k, rwkv_k_a=rwkv_k_a, rwkv_r_k=rwkv_r_k,
        rwkv_lnx_w=rwkv_lnx_w, rwkv_lnx_b=rwkv_lnx_b, rwkv_wo=rwkv_wo, norm_final=norm_final))
    prompt = _Stream(x_prompt, 0, (None, None, None, None, None))
    sample = _Stream(x_sample, PAST_LEN, (cache_pool, cache_swa_k, cache_swa_v, state_shift, state_wkv))
    (y_p, pool_p, k_p, v_p, shift_p, wkv_p), (y_s, pool_s, k_s, v_s, shift_s, wkv_s) = _trunk(prompt, sample, p)
    return (y_p, y_s, pool_p, pool_s, k_p, k_s, v_p, v_s, shift_p, shift_s, wkv_p, wkv_s)
```
